```python
import jax, jax.numpy as jnp
from jax import lax
import numpy as np


D_MODEL = 2048
BATCH = 4
SEQ = 4096
DEPTH = 1

N_Q_HEADS = 8
N_KV_HEADS = 2
Q_PER_KV = N_Q_HEADS // N_KV_HEADS
HEAD_DIM = 128
WINDOW = 128
WBLK = 128
GLA_HEADS = 4
GLA_DK = 128
GLA_DV = 256
GLA_LOWRANK = 16
GLA_TAU = 16.0
GLA_CHUNK = 64
ATT_Q = N_Q_HEADS * HEAD_DIM
ATT_KV = N_KV_HEADS * HEAD_DIM
GLA_QK = GLA_HEADS * GLA_DK
GLA_V = GLA_HEADS * GLA_DV
MIX_WIDTH = ATT_Q + GLA_V
IN_SIZES = (ATT_Q, ATT_KV, ATT_KV, GLA_QK, GLA_QK, GLA_V, GLA_V, GLA_LOWRANK, GLA_LOWRANK)
IN_COLS = ATT_Q + 2 * ATT_KV + 2 * GLA_QK + 2 * GLA_V + 2 * GLA_LOWRANK
MEM_LEN = 256
X_HEADS = 4
X_HEAD_DIM = D_MODEL // X_HEADS
N_GROUPS = 4
EXPERTS_PER_GROUP = 8
N_EXPERTS = N_GROUPS * EXPERTS_PER_GROUP
TOP_K = 2
D_FF_EXPERT = D_MODEL // 2
MOE_BLOCK = 256
RMS_EPS = 1e-6
NEG_INF = -1e30

kernel_name = 'hymba_swa_alibi_bigla_hmoe_encoder'


def rmsnorm(x, gain):
    xf = x.astype(jnp.float32)
    y = xf * lax.rsqrt(jnp.mean(xf * xf, axis=-1, keepdims=True) + RMS_EPS)
    return (y * gain.astype(jnp.float32)).astype(x.dtype)


def alibi_slopes():
    return 2.0 ** (-8.0 * (jnp.arange(N_Q_HEADS, dtype=jnp.float32) + 1.0) / N_Q_HEADS)


def window_attention(q, k, v, sink_logit):
    B, S = q.shape[:2]
    nb = S // WBLK
    qb = q.reshape(B, nb, WBLK, N_KV_HEADS, Q_PER_KV, HEAD_DIM)
    pad = ((0, 0), (WBLK, WBLK), (0, 0), (0, 0))

    def band(t):
        tp = jnp.pad(t, pad).reshape(B, nb + 2, WBLK, N_KV_HEADS, HEAD_DIM)
        return jnp.concatenate([tp[:, :-2], tp[:, 1:-1], tp[:, 2:]], axis=2)

    kb, vb = band(k), band(v)
    scores = jnp.einsum('bnqhgd,bnkhd->bhgnqk', qb, kb).astype(jnp.float32) * (HEAD_DIM ** -0.5)
    kpos = jnp.arange(3 * WBLK)
    qpos = jnp.arange(WBLK) + WBLK
    dist = jnp.abs(kpos[None, :] - qpos[:, None])
    kabs = jnp.arange(nb)[:, None] * WBLK + kpos[None, :] - WBLK
    valid = (dist <= WINDOW)[None] & ((kabs >= 0) & (kabs < S))[:, None, :]
    slopes = alibi_slopes().reshape(N_KV_HEADS, Q_PER_KV)
    scores = scores - slopes[:, :, None, None, None] * dist.astype(jnp.float32)
    scores = jnp.where(valid, scores, NEG_INF)
    sink = sink_logit.astype(jnp.float32).reshape(N_KV_HEADS, Q_PER_KV)[None, :, :, None, None, None]
    sink = jnp.broadcast_to(sink, scores.shape[:-1] + (1,))
    probs = jax.nn.softmax(jnp.concatenate([scores, sink], axis=-1), axis=-1)[..., :-1]
    out = jnp.einsum('bhgnqk,bnkhd->bnqhgd', probs.astype(v.dtype), vb)
    return out.reshape(B, S, ATT_Q)


def gla_chunked(q, k, v, g):
    B, H, S, DK = q.shape
    DV = v.shape[-1]
    C = GLA_CHUNK
    N = S // C
    q = q.reshape(B, H, N, C, DK)
    k = k.reshape(B, H, N, C, DK)
    g = g.reshape(B, H, N, C, DK)
    v = v.reshape(B, H, N, C, DV)
    b = jnp.cumsum(g, axis=3)
    b_end = b[:, :, :, -1:, :]
    q_dec = q * jnp.exp(b)
    attn = jnp.einsum('bhncd,bhnsd->bhncs', q_dec, k * jnp.exp(-b))
    lower_tri = jnp.tril(jnp.ones((C, C), dtype=bool))
    attn = jnp.where(lower_tri, attn, 0.0)
    o = jnp.einsum('bhncs,bhnsv->bhncv', attn, v)
    chunk_kv = jnp.einsum('bhncd,bhncv->bhndv', k * jnp.exp(b_end - b), v)
    chunk_decay = jnp.exp(b_end[:, :, :, 0, :])

    def step(state, inp):
        dec, kv = inp
        return dec[..., None] * state + kv, state

    init = jnp.zeros((B, H, DK, DV), q.dtype)
    _, states = lax.scan(step, init, (jnp.moveaxis(chunk_decay, 2, 0), jnp.moveaxis(chunk_kv, 2, 0)))
    states = jnp.moveaxis(states, 0, 2)
    o = o + jnp.einsum('bhncd,bhndv->bhncv', q_dec, states)
    return o.reshape(B, H, S, DV)


def parallel_head_mixer(u, w_in, attn_out_norm, sink_logit, w_gla_gf, b_gla_gf, w_gla_gb, b_gla_gb, gla_out_norm, w_out):
    B, S, _ = u.shape
    proj = u @ w_in
    cuts = [sum(IN_SIZES[:i + 1]) for i in range(len(IN_SIZES) - 1)]
    q_a, k_a, v_a, q_g, k_g, v_g, r_g, lr_f, lr_b = jnp.split(proj, cuts, axis=-1)
    o_a = window_attention(q_a.reshape(B, S, N_KV_HEADS, Q_PER_KV, HEAD_DIM),
                           k_a.reshape(B, S, N_KV_HEADS, HEAD_DIM),
                           v_a.reshape(B, S, N_KV_HEADS, HEAD_DIM), sink_logit)
    o_a = rmsnorm(o_a, attn_out_norm)

    def heads(t, d):
        return t.reshape(B, S, GLA_HEADS, d).transpose(0, 2, 1, 3).astype(jnp.float32)

    qg = heads(q_g, GLA_DK) * (GLA_DK ** -0.5)
    kg = heads(k_g, GLA_DK)
    vg = heads(v_g, GLA_DV)
    g_f = heads(jax.nn.log_sigmoid((lr_f @ w_gla_gf + b_gla_gf).astype(jnp.float32)) / GLA_TAU, GLA_DK)
    g_b = heads(jax.nn.log_sigmoid((lr_b @ w_gla_gb + b_gla_gb).astype(jnp.float32)) / GLA_TAU, GLA_DK)
    o_fwd = gla_chunked(qg, kg, vg, g_f)
    flip = lambda t: jnp.flip(t, axis=2)
    o_bwd = flip(gla_chunked(flip(qg), flip(kg), flip(vg), flip(g_b)))
    o_g = rmsnorm(o_fwd + o_bwd, gla_out_norm)
    o_g = o_g.transpose(0, 2, 1, 3).reshape(B, S, GLA_V).astype(u.dtype) * jax.nn.silu(r_g)
    return jnp.concatenate([o_a.astype(u.dtype), o_g], axis=-1) @ w_out


def memory_cross_attention(hn, memn, w_cq, w_ck, w_cv, w_co):
    B, S, _ = hn.shape
    M = memn.shape[1]
    q = (hn @ w_cq).reshape(B, S, X_HEADS, X_HEAD_DIM)
    k = (memn @ w_ck).reshape(B, M, X_HEADS, X_HEAD_DIM)
    v = (memn @ w_cv).reshape(B, M, X_HEADS, X_HEAD_DIM)
    s = jnp.einsum('bshd,bmhd->bhsm', q, k).astype(jnp.float32) * (X_HEAD_DIM ** -0.5)
    p = jax.nn.softmax(s, axis=-1).astype(v.dtype)
    o = jnp.einsum('bhsm,bmhd->bshd', p, v).reshape(B, S, D_MODEL)
    return o @ w_co


def hierarchical_moe(hn, w_rg, b_rg, w_re, b_re, w_gate, w_up, w_down):
    B, S, D = hn.shape
    T = B * S
    xf = hn.reshape(T, D)
    g_logits = (xf @ w_rg).astype(jnp.float32) + b_rg.astype(jnp.float32)
    g_prob = jax.nn.softmax(g_logits, axis=-1)
    _, g_idx = lax.top_k(g_logits, 1)
    p_group = jnp.take_along_axis(g_prob, g_idx, axis=-1)
    e_logits = ((xf @ w_re).astype(jnp.float32) + b_re.astype(jnp.float32)).reshape(T, N_GROUPS, EXPERTS_PER_GROUP)
    sel = jnp.broadcast_to(g_idx[:, :, None], (T, 1, EXPERTS_PER_GROUP))
    in_group = jnp.take_along_axis(e_logits, sel, axis=1)[:, 0]
    e_top, e_local = lax.top_k(in_group, TOP_K)
    gate = jax.nn.softmax(e_top, axis=-1) * p_group
    e_idx = g_idx * EXPERTS_PER_GROUP + e_local

    A = T * TOP_K
    e_flat = e_idx.reshape(A)
    tok_flat = jnp.arange(A, dtype=jnp.int32) // TOP_K
    w_flat = gate.reshape(A)
    order = jnp.argsort(e_flat)
    e_s, tok_s, w_s = e_flat[order], tok_flat[order], w_flat[order]
    counts = jnp.zeros((N_EXPERTS,), jnp.int32).at[e_flat].add(1)
    padded = ((counts + MOE_BLOCK - 1) // MOE_BLOCK) * MOE_BLOCK
    start = jnp.cumsum(counts) - counts
    pend = jnp.cumsum(padded)
    pstart = pend - padded
    dest = pstart[e_s] + (jnp.arange(A, dtype=jnp.int32) - start[e_s])
    R = A + N_EXPERTS * MOE_BLOCK
    row_tok = jnp.zeros((R,), jnp.int32).at[dest].set(tok_s)
    row_w = jnp.zeros((R,), jnp.float32).at[dest].set(w_s)
    n_blk = R // MOE_BLOCK
    blk_e = jnp.minimum(jnp.searchsorted(pend, jnp.arange(n_blk, dtype=jnp.int32) * MOE_BLOCK, side='right'), N_EXPERTS - 1)
    xs = xf[row_tok].reshape(n_blk, MOE_BLOCK, D)

    def expert_block(args):
        xb, e = args
        hid = jax.nn.silu(xb @ w_gate[e]) * (xb @ w_up[e])
        return hid @ w_down[e]

    ys = lax.map(expert_block, (xs, blk_e)).reshape(R, D)
    out = jnp.zeros_like(xf).at[row_tok].add(ys * row_w[:, None].astype(ys.dtype))
    return out.reshape(B, S, D)


def setup_inputs(seed: int = 0) -> dict:
    key = jax.random.key(seed)
    ks = list(jax.random.split(key, 32))
    L = DEPTH
    cnt = [0]

    def nk():
        cnt[0] += 1
        return ks[cnt[0] - 1]

    def nrm(shape, fan_in):
        return jax.random.normal(nk(), shape, jnp.float32) * (fan_in ** -0.5)

    def gain(shape):
        return 1.0 + 0.05 * jax.random.normal(nk(), shape, jnp.float32)

    def small(shape, scale):
        return scale * jax.random.normal(nk(), shape, jnp.float32)

    return {
        'x': jax.random.normal(nk(), (BATCH, SEQ, D_MODEL), jnp.float32),
        'mem': jax.random.normal(nk(), (BATCH, MEM_LEN, D_MODEL), jnp.float32),
        'norm_mix': gain((L, D_MODEL)),
        'w_in': nrm((L, D_MODEL, IN_COLS), D_MODEL),
        'attn_out_norm': gain((L, ATT_Q)),
        'sink_logit': small((L, N_Q_HEADS), 0.5),
        'w_gla_gf': nrm((L, GLA_LOWRANK, GLA_QK), GLA_LOWRANK),
        'b_gla_gf': small((L, GLA_QK), 0.1),
        'w_gla_gb': nrm((L, GLA_LOWRANK, GLA_QK), GLA_LOWRANK),
        'b_gla_gb': small((L, GLA_QK), 0.1),
        'gla_out_norm': gain((L, GLA_DV)),
        'w_out': nrm((L, MIX_WIDTH, D_MODEL), MIX_WIDTH),
        'norm_cross': gain((L, D_MODEL)),
        'norm_mem': gain((L, D_MODEL)),
        'w_cq': nrm((L, D_MODEL, D_MODEL), D_MODEL),
        'w_ck': nrm((L, D_MODEL, D_MODEL), D_MODEL),
        'w_cv': nrm((L, D_MODEL, D_MODEL), D_MODEL),
        'w_co': nrm((L, D_MODEL, D_MODEL), D_MODEL),
        'norm_ffn': gain((L, D_MODEL)),
        'w_router_group': nrm((L, D_MODEL, N_GROUPS), D_MODEL),
        'b_router_group': small((L, N_GROUPS), 0.01),
        'w_router_expert': nrm((L, D_MODEL, N_EXPERTS), D_MODEL),
        'b_router_expert': small((L, N_EXPERTS), 0.01),
        'w_gate': nrm((L, N_EXPERTS, D_MODEL, D_FF_EXPERT), D_MODEL),
        'w_up': nrm((L, N_EXPERTS, D_MODEL, D_FF_EXPERT), D_MODEL),
        'w_down': nrm((L, N_EXPERTS, D_FF_EXPERT, D_MODEL), D_FF_EXPERT),
        'norm_final': gain((D_MODEL,)),
    }


def reference(x, mem, norm_mix, w_in, attn_out_norm, sink_logit, w_gla_gf, b_gla_gf, w_gla_gb, b_gla_gb,
              gla_out_norm, w_out, norm_cross, norm_mem, w_cq, w_ck, w_cv, w_co, norm_ffn,
              w_router_group, b_router_group, w_router_expert, b_router_expert, w_gate, w_up, w_down,
              norm_final):
    h = x
    for l in range(DEPTH):
        h = h + parallel_head_mixer(rmsnorm(h, norm_mix[l]), w_in[l], attn_out_norm[l], sink_logit[l],
                                    w_gla_gf[l], b_gla_gf[l], w_gla_gb[l], b_gla_gb[l], gla_out_norm[l], w_out[l])
        h = h + memory_cross_attention(rmsnorm(h, norm_cross[l]), rmsnorm(mem, norm_mem[l]),
                                       w_cq[l], w_ck[l], w_cv[l], w_co[l])
        h = h + hierarchical_moe(rmsnorm(h, norm_ffn[l]), w_router_group[l], b_router_group[l],
                                 w_router_expert[l], b_router_expert[l], w_gate[l], w_up[l], w_down[l])
    return rmsnorm(h, norm_final)
```

```python
import functools

import jax
import jax.numpy as jnp
from jax import lax
from jax.experimental import pallas as pl
from jax.experimental.pallas import tpu as pltpu

F32 = jnp.float32
BF16 = jnp.bfloat16

D_MODEL = 2048
N_Q_HEADS = 8
N_KV_HEADS = 2
Q_PER_KV = N_Q_HEADS // N_KV_HEADS
HEAD_DIM = 128
WINDOW = 128
WBLK = 128
GLA_HEADS = 4
GLA_DK = 128
GLA_DV = 256
GLA_LOWRANK = 16
GLA_TAU = 16.0
GLA_CHUNK = 64
ATT_Q = N_Q_HEADS * HEAD_DIM
ATT_KV = N_KV_HEADS * HEAD_DIM
GLA_QK = GLA_HEADS * GLA_DK
GLA_V = GLA_HEADS * GLA_DV
MAIN_COLS = ATT_Q + 2 * ATT_KV + 2 * GLA_QK + 2 * GLA_V
COL_KA = ATT_Q
COL_VA = COL_KA + ATT_KV
COL_QG = COL_VA + ATT_KV
COL_KG = COL_QG + GLA_QK
COL_VG = COL_KG + GLA_QK
COL_RG = COL_VG + GLA_V
X_HEADS = 4
X_HEAD_DIM = D_MODEL // X_HEADS
N_GROUPS = 4
EXPERTS_PER_GROUP = 8
N_EXPERTS = N_GROUPS * EXPERTS_PER_GROUP
TOP_K = 2
D_FF_EXPERT = D_MODEL // 2
MOE_BLOCK = 256
RMS_EPS = 1e-6
NEG_INF = -1e30

LANES = 128
VMEM_LIMIT = 56 * 1024 * 1024
W_STAGE_ROWS = 128


def _params(sem):
    return pltpu.CompilerParams(dimension_semantics=sem, vmem_limit_bytes=VMEM_LIMIT)


def _nt(a, b):
    return lax.dot_general(a, b, (((1,), (1,)), ((), ())), preferred_element_type=F32)


def _tn(a, b):
    return lax.dot_general(a, b, (((0,), (0,)), ((), ())), preferred_element_type=F32)


def _rms(x, gain):
    return x * lax.rsqrt(jnp.mean(x * x, axis=-1, keepdims=True) + RMS_EPS) * gain


def _load_weight_bf16(w_hbm, wb, stage, sem, k_rows, n_cols):
    n_chunks = k_rows // W_STAGE_ROWS

    def copy(c):
        return pltpu.make_async_copy(
            w_hbm.at[pl.ds(c * W_STAGE_ROWS, W_STAGE_ROWS), pl.ds(0, n_cols)],
            stage.at[c % 2], sem.at[c % 2])

    copy(0).start()
    for c in range(n_chunks):
        if c + 1 < n_chunks:
            copy(c + 1).start()
        copy(c).wait()
        wb[c * W_STAGE_ROWS:(c + 1) * W_STAGE_ROWS, :] = stage[c % 2].astype(BF16)


def _dense_body(*refs, part_widths, has_norm, has_extra, has_res, n_cols, n_chunk):
    it = iter(refs)
    x_refs = [next(it) for _ in part_widths]
    g_ref = next(it) if has_norm else None
    w_hbm = next(it)
    ew_ref = next(it) if has_extra else None
    res_ref = next(it) if has_res else None
    o_ref = next(it)
    eo_ref = next(it) if has_extra else None
    wb, stage, sem = next(it), next(it), next(it)
    u_ref = next(it) if has_norm else None
    k_rows = sum(part_widths)

    @pl.when(pl.program_id(0) == 0)
    def _():
        _load_weight_bf16(w_hbm, wb, stage, sem, k_rows, n_cols)

    if has_norm:
        u_ref[...] = _rms(x_refs[0][...], g_ref[...]).astype(BF16)
        lhs = [(u_ref, 0, k_rows)]
    else:
        lhs, off = [], 0
        for r, kw in zip(x_refs, part_widths):
            lhs.append((r, off, kw))
            off += kw
    for n0 in range(0, n_cols, n_chunk):
        acc = None
        for r, off, kw in lhs:
            d = jnp.dot(r[...], wb[off:off + kw, n0:n0 + n_chunk], preferred_element_type=F32)
            acc = d if acc is None else acc + d
        if has_res:
            acc = acc + res_ref[:, n0:n0 + n_chunk]
        o_ref[:, n0:n0 + n_chunk] = acc.astype(o_ref.dtype)
    if has_extra:
        eo_ref[...] = jnp.dot(u_ref[...], ew_ref[...], preferred_element_type=F32)


def _dense(xs, w, *, n_cols, out_dtype, gain=None, extra_w=None, res=None, tm=512, n_chunk=512, name):
    rows = xs[0].shape[0]
    part_widths = tuple(x.shape[1] for x in xs)
    k_rows = sum(part_widths)
    has_norm, has_extra, has_res = gain is not None, extra_w is not None, res is not None
    row_spec = lambda width: pl.BlockSpec((tm, width), lambda i: (i, 0))
    full_spec = lambda a: pl.BlockSpec(a.shape, lambda i: (0, 0))
    args, in_specs = list(xs), [row_spec(kw) for kw in part_widths]
    if has_norm:
        args.append(gain.reshape(1, k_rows).astype(F32))
        in_specs.append(full_spec(args[-1]))
    args.append(w)
    in_specs.append(pl.BlockSpec(memory_space=pl.ANY))
    if has_extra:
        args.append(extra_w)
        in_specs.append(full_spec(extra_w))
    if has_res:
        args.append(res)
        in_specs.append(row_spec(n_cols))
    out_shape = [jax.ShapeDtypeStruct((rows, n_cols), out_dtype)]
    out_specs = [row_spec(n_cols)]
    if has_extra:
        out_shape.append(jax.ShapeDtypeStruct((rows, extra_w.shape[1]), F32))
        out_specs.append(row_spec(extra_w.shape[1]))
    scratch = [pltpu.VMEM((k_rows, n_cols), BF16),
               pltpu.VMEM((2, W_STAGE_ROWS, n_cols), F32),
               pltpu.SemaphoreType.DMA((2,))]
    if has_norm:
        scratch.append(pltpu.VMEM((tm, k_rows), BF16))
    body = functools.partial(_dense_body, part_widths=part_widths, has_norm=has_norm, has_extra=has_extra,
                             has_res=has_res, n_cols=n_cols, n_chunk=n_chunk)
    outs = pl.pallas_call(
        body, grid=(rows // tm,), in_specs=in_specs, out_specs=out_specs, out_shape=out_shape,
        scratch_shapes=scratch, compiler_params=_params(("arbitrary",)), name=name)(*args)
    return outs if has_extra else outs[0]


ATT_TQ = 512


def _winattn_body(sink_ref, q_ref, kp_ref, km_ref, kn_ref, vp_ref, vm_ref, vn_ref, g_ref, o_ref,
                  kcat, vcat, obuf, *, seq):
    s0 = pl.program_id(1) * ATT_TQ
    kcat[0:WBLK, :] = kp_ref[0]
    kcat[WBLK:WBLK + ATT_TQ, :] = km_ref[0]
    kcat[WBLK + ATT_TQ:, :] = kn_ref[0]
    vcat[0:WBLK, :] = vp_ref[0]
    vcat[WBLK:WBLK + ATT_TQ, :] = vm_ref[0]
    vcat[WBLK + ATT_TQ:, :] = vn_ref[0]
    qi = lax.broadcasted_iota(jnp.int32, (WBLK, 3 * WBLK), 0) + WBLK
    ki = lax.broadcasted_iota(jnp.int32, (WBLK, 3 * WBLK), 1)
    dist_i = jnp.abs(ki - qi)
    dist = dist_i.astype(F32)
    scale = HEAD_DIM ** -0.5
    for qb in range(ATT_TQ // WBLK):
        kabs = s0 + (qb - 1) * WBLK + ki
        valid = (dist_i <= WINDOW) & (kabs >= 0) & (kabs < seq)
        for h in range(N_KV_HEADS):
            kk = kcat[qb * WBLK:(qb + 3) * WBLK, h * HEAD_DIM:(h + 1) * HEAD_DIM]
            vv = vcat[qb * WBLK:(qb + 3) * WBLK, h * HEAD_DIM:(h + 1) * HEAD_DIM]
            for g in range(Q_PER_KV):
                j = h * Q_PER_KV + g
                slope = 2.0 ** (-8.0 * (j + 1) / N_Q_HEADS)
                q = q_ref[0, qb * WBLK:(qb + 1) * WBLK, j * HEAD_DIM:(j + 1) * HEAD_DIM]
                s = _nt(q, kk) * scale - slope * dist
                s = jnp.where(valid, s, NEG_INF)
                sink = sink_ref[j]
                m = jnp.maximum(jnp.max(s, axis=-1, keepdims=True), sink)
                p = jnp.exp(s - m)
                denom = jnp.sum(p, axis=-1, keepdims=True) + jnp.exp(sink - m)
                pv = jnp.dot(p.astype(BF16), vv, preferred_element_type=F32)
                obuf[:, j * HEAD_DIM:(j + 1) * HEAD_DIM] = pv / denom
        o_ref[0, qb * WBLK:(qb + 1) * WBLK, :] = _rms(obuf[...], g_ref[...]).astype(o_ref.dtype)


def _window_attention(proj, sink_logit, gain, batch, seq):
    p3 = proj.reshape(batch, seq, MAIN_COLS)
    nq = seq // ATT_TQ
    per = ATT_TQ // WBLK
    last = seq // WBLK - 1
    main = lambda col: pl.BlockSpec((1, ATT_TQ, ATT_KV), lambda b, i: (b, i, col))
    prev = lambda col: pl.BlockSpec((1, WBLK, ATT_KV), lambda b, i: (b, jnp.maximum(i * per - 1, 0), col))
    nxt = lambda col: pl.BlockSpec((1, WBLK, ATT_KV), lambda b, i: (b, jnp.minimum(i * per + per, last), col))
    ck, cv = COL_KA // ATT_KV, COL_VA // ATT_KV
    return pl.pallas_call(
        functools.partial(_winattn_body, seq=seq),
        grid=(batch, nq),
        in_specs=[pl.BlockSpec(memory_space=pltpu.SMEM),
                  pl.BlockSpec((1, ATT_TQ, ATT_Q), lambda b, i: (b, i, 0)),
                  prev(ck), main(ck), nxt(ck), prev(cv), main(cv), nxt(cv),
                  pl.BlockSpec((1, ATT_Q), lambda b, i: (0, 0))],
        out_specs=pl.BlockSpec((1, ATT_TQ, ATT_Q), lambda b, i: (b, i, 0)),
        out_shape=jax.ShapeDtypeStruct((batch, seq, ATT_Q), BF16),
        scratch_shapes=[pltpu.VMEM((ATT_TQ + 2 * WBLK, ATT_KV), BF16),
                        pltpu.VMEM((ATT_TQ + 2 * WBLK, ATT_KV), BF16),
                        pltpu.VMEM((WBLK, ATT_Q), F32)],
        compiler_params=_params(("parallel", "parallel")), name="window_attention",
    )(sink_logit.astype(F32), p3, p3, p3, p3, p3, p3, p3, gain.reshape(1, ATT_Q).astype(F32))


GLA_SCAN_ROWS = 256
GLA_EPI_ROWS = 512


def _split3(x):
    hi = x.astype(BF16)
    r1 = x - hi.astype(F32)
    mid = r1.astype(BF16)
    lo = (r1 - mid.astype(F32)).astype(BF16)
    return hi, mid, lo


def _gla_body(q_ref, k_ref, v_ref, r_ref, lr_ref, wf_ref, wb_ref, bf_ref, bb_ref, gn_ref, o_ref,
              cum_f, cum_b, acc, st_f, st_b, *, seq):
    C = GLA_CHUNK
    n_chunks = seq // C
    ri = lax.broadcasted_iota(jnp.int32, (GLA_SCAN_ROWS, GLA_SCAN_ROWS), 0)
    ci = lax.broadcasted_iota(jnp.int32, (GLA_SCAN_ROWS, GLA_SCAN_ROWS), 1)
    same = (ri // C) == (ci // C)
    tri_f = jnp.where(same & (ci <= ri), 1.0, 0.0).astype(BF16)
    tri_b = jnp.where(same & (ci >= ri), 1.0, 0.0).astype(BF16)

    def scan_body(t, carry):
        rows = pl.ds(pl.multiple_of(t * GLA_SCAN_ROWS, GLA_SCAN_ROWS), GLA_SCAN_ROWS)
        lr = lr_ref[0, rows, :].astype(BF16)
        for w_ref, b_ref, tri, dst in ((wf_ref, bf_ref, tri_f, cum_f), (wb_ref, bb_ref, tri_b, cum_b)):
            z = jnp.dot(lr, w_ref[...], preferred_element_type=F32) + b_ref[...]
            g = (jnp.minimum(z, 0.0) - jnp.log1p(jnp.exp(-jnp.abs(z)))) / GLA_TAU
            dst[rows, :] = sum(jnp.dot(tri, part, preferred_element_type=F32) for part in _split3(g))
        return carry

    lax.fori_loop(0, seq // GLA_SCAN_ROWS, scan_body, 0)

    acc[...] = jnp.zeros_like(acc)
    st_f[...] = jnp.zeros_like(st_f)
    st_b[...] = jnp.zeros_like(st_b)
    rr = lax.broadcasted_iota(jnp.int32, (C, C), 0)
    cc = lax.broadcasted_iota(jnp.int32, (C, C), 1)
    scale = GLA_DK ** -0.5

    def one_chunk(c, cum, st, forward):
        rows = pl.ds(pl.multiple_of(c * C, C), C)
        b = cum[rows, :]
        b_end = b[C - 1:C, :] if forward else b[0:1, :]
        q = q_ref[0, rows, :].astype(F32) * scale
        k = k_ref[0, rows, :].astype(F32)
        v = v_ref[0, rows, :]
        q_dec = (q * jnp.exp(b)).astype(BF16)
        k_inc = (k * jnp.exp(-b)).astype(BF16)
        k_dec = (k * jnp.exp(b_end - b)).astype(BF16)
        attn = _nt(q_dec, k_inc)
        attn = jnp.where((cc <= rr) if forward else (cc >= rr), attn, 0.0)
        state_t = st[...]
        o = jnp.dot(attn.astype(BF16), v, preferred_element_type=F32) + _nt(q_dec, state_t.astype(BF16))
        st[...] = state_t * jnp.exp(b_end) + _tn(v, k_dec)
        acc[rows, :] = acc[rows, :] + o

    def chunk_body(i, carry):
        one_chunk(i, cum_f, st_f, True)
        one_chunk(n_chunks - 1 - i, cum_b, st_b, False)
        return carry

    lax.fori_loop(0, n_chunks, chunk_body, 0)

    def epi_body(t, carry):
        rows = pl.ds(pl.multiple_of(t * GLA_EPI_ROWS, GLA_EPI_ROWS), GLA_EPI_ROWS)
        r = r_ref[0, rows, :].astype(F32)
        o_ref[0, rows, :] = (_rms(acc[rows, :], gn_ref[...]) * (r * jax.nn.sigmoid(r))).astype(o_ref.dtype)
        return carry

    lax.fori_loop(0, seq // GLA_EPI_ROWS, epi_body, 0)


def _gla(proj, lr, w_gf, b_gf, w_gb, b_gb, gain, batch, seq):
    p3 = proj.reshape(batch, seq, MAIN_COLS)
    lr3 = lr.reshape(batch, seq, LANES)
    wf = jnp.zeros((LANES, GLA_QK), F32).at[:GLA_LOWRANK].set(w_gf).astype(BF16)
    wb = jnp.zeros((LANES, GLA_QK), F32).at[GLA_LOWRANK:2 * GLA_LOWRANK].set(w_gb).astype(BF16)
    seq_blk = lambda width, col0: pl.BlockSpec((1, seq, width), lambda b, h: (b, 0, col0 // width + h))
    head_w = pl.BlockSpec((LANES, GLA_DK), lambda b, h: (0, h))
    head_b = pl.BlockSpec((1, GLA_DK), lambda b, h: (0, h))
    return pl.pallas_call(
        functools.partial(_gla_body, seq=seq),
        grid=(batch, GLA_HEADS),
        in_specs=[seq_blk(GLA_DK, COL_QG), seq_blk(GLA_DK, COL_KG), seq_blk(GLA_DV, COL_VG),
                  seq_blk(GLA_DV, COL_RG),
                  pl.BlockSpec((1, seq, LANES), lambda b, h: (b, 0, 0)),
                  head_w, head_w, head_b, head_b,
                  pl.BlockSpec((1, GLA_DV), lambda b, h: (0, 0))],
        out_specs=pl.BlockSpec((1, seq, GLA_DV), lambda b, h: (b, 0, h)),
        out_shape=jax.ShapeDtypeStruct((batch, seq, GLA_V), BF16),
        scratch_shapes=[pltpu.VMEM((seq, GLA_DK), F32), pltpu.VMEM((seq, GLA_DK), F32),
                        pltpu.VMEM((seq, GLA_DV), F32),
                        pltpu.VMEM((GLA_DV, GLA_DK), F32), pltpu.VMEM((GLA_DV, GLA_DK), F32)],
        compiler_params=_params(("parallel", "parallel")), name="gla",
    )(p3, p3, p3, p3, lr3, wf, wb, b_gf.reshape(1, GLA_QK).astype(F32), b_gb.reshape(1, GLA_QK).astype(F32),
      gain.reshape(1, GLA_DV).astype(F32))


XATT_TQ = 512


def _xattn_body(q_ref, k_ref, v_ref, o_ref):
    scale = X_HEAD_DIM ** -0.5
    for h in range(X_HEADS):
        cols = slice(h * X_HEAD_DIM, (h + 1) * X_HEAD_DIM)
        s = _nt(q_ref[0, :, cols], k_ref[0, :, cols]) * scale
        m = jnp.max(s, axis=-1, keepdims=True)
        p = jnp.exp(s - m)
        denom = jnp.sum(p, axis=-1, keepdims=True)
        pv = jnp.dot(p.astype(BF16), v_ref[0, :, cols], preferred_element_type=F32)
        o_ref[0, :, cols] = (pv / denom).astype(o_ref.dtype)


def _cross_attention(q, kx, vx, batch, seq, mem_len):
    q3 = q.reshape(batch, seq, D_MODEL)
    k3 = kx.reshape(batch, mem_len, D_MODEL)
    v3 = vx.reshape(batch, mem_len, D_MODEL)
    mem_spec = pl.BlockSpec((1, mem_len, D_MODEL), lambda b, i: (b, 0, 0))
    tile = pl.BlockSpec((1, XATT_TQ, D_MODEL), lambda b, i: (b, i, 0))
    out = pl.pallas_call(
        _xattn_body, grid=(batch, seq // XATT_TQ), in_specs=[tile, mem_spec, mem_spec], out_specs=tile,
        out_shape=jax.ShapeDtypeStruct((batch, seq, D_MODEL), BF16),
        compiler_params=_params(("parallel", "parallel")), name="cross_attention")(q3, k3, v3)
    return out.reshape(batch * seq, D_MODEL)


ROUTER_TM = 512


def _router_body(h_ref, g_ref, w_ref, b_ref, hn_ref, idx_ref, gate_ref):
    hn = _rms(h_ref[...], g_ref[...])
    hn_ref[...] = hn
    logits = jnp.dot(hn.astype(BF16), w_ref[...], preferred_element_type=F32) + b_ref[...]
    lane = lax.broadcasted_iota(jnp.int32, logits.shape, 1).astype(F32)
    ninf = -jnp.inf
    first = lambda hit: jnp.min(jnp.where(hit, lane, float(LANES)), axis=-1, keepdims=True)
    in_groups = lane < N_GROUPS
    gl = jnp.where(in_groups, logits, ninf)
    gmax = jnp.max(gl, axis=-1, keepdims=True)
    g_idx = first(gl == gmax)
    p_group = 1.0 / jnp.sum(jnp.where(in_groups, jnp.exp(logits - gmax), 0.0), axis=-1, keepdims=True)
    lo = N_GROUPS + EXPERTS_PER_GROUP * g_idx
    el = jnp.where((lane >= lo) & (lane < lo + EXPERTS_PER_GROUP), logits, ninf)
    e1 = jnp.max(el, axis=-1, keepdims=True)
    i1 = first(el == e1)
    el2 = jnp.where(lane == i1, ninf, el)
    e2 = jnp.max(el2, axis=-1, keepdims=True)
    i2 = first(el2 == e2)
    t = jnp.exp(e2 - e1)
    w1 = p_group / (1.0 + t)
    w2 = p_group * t / (1.0 + t)
    idx_ref[...] = jnp.where(lane == 0, i1 - N_GROUPS, jnp.where(lane == 1, i2 - N_GROUPS, 0.0)).astype(jnp.int32)
    gate_ref[...] = jnp.where(lane == 0, w1, jnp.where(lane == 1, w2, 0.0))


def _router(h, gain, w_rg, b_rg, w_re, b_re):
    rows = h.shape[0]
    n_log = N_GROUPS + N_EXPERTS
    w = jnp.zeros((D_MODEL, LANES), F32).at[:, :N_GROUPS].set(w_rg).at[:, N_GROUPS:n_log].set(w_re).astype(BF16)
    b = jnp.zeros((1, LANES), F32).at[0, :N_GROUPS].set(b_rg).at[0, N_GROUPS:n_log].set(b_re)
    tile = lambda width: pl.BlockSpec((ROUTER_TM, width), lambda i: (i, 0))
    full = lambda a: pl.BlockSpec(a.shape, lambda i: (0, 0))
    g2 = gain.reshape(1, D_MODEL).astype(F32)
    return pl.pallas_call(
        _router_body, grid=(rows // ROUTER_TM,),
        in_specs=[tile(D_MODEL), full(g2), full(w), full(b)],
        out_specs=[tile(D_MODEL), tile(LANES), tile(LANES)],
        out_shape=[jax.ShapeDtypeStruct((rows, D_MODEL), F32),
                   jax.ShapeDtypeStruct((rows, LANES), jnp.int32),
                   jax.ShapeDtypeStruct((rows, LANES), F32)],
        compiler_params=_params(("parallel",)), name="moe_router")(h, g2, w, b)


W_CAST_ROWS = 256


def _cast_weight(src_ref, dst_ref):
    def body(c, carry):
        rows = pl.ds(pl.multiple_of(c * W_CAST_ROWS, W_CAST_ROWS), W_CAST_ROWS)
        dst_ref[rows, :] = src_ref[rows, :].astype(BF16)
        return carry
    lax.fori_loop(0, src_ref.shape[0] // W_CAST_ROWS, body, 0)


def _expert_changed(blk_e, blk):
    prev = blk_e[jnp.maximum(blk - 1, 0)]
    return (blk == 0) | (blk_e[blk] != prev)


def _moe_up_body(blk_e, n_real, tok_ref, tok_next_ref, hn_hbm, wg_ref, wu_ref, hid_ref,
                 xs, wgb, wub, sem):
    blk = pl.program_id(0)

    def gather(idx_ref, slot):
        def body(r, carry):
            tok = idx_ref[0, 0, r]
            pltpu.make_async_copy(hn_hbm.at[pl.ds(tok, 1), :], xs.at[slot, pl.ds(r, 1), :], sem.at[slot]).start()
            return carry
        lax.fori_loop(0, MOE_BLOCK, body, 0)

    @pl.when(blk == 0)
    def _():
        gather(tok_ref, 0)

    @pl.when(blk < n_real[0])
    def _():
        slot = blk % 2

        @pl.when(blk + 1 < n_real[0])
        def _():
            gather(tok_next_ref, 1 - slot)

        @pl.when(_expert_changed(blk_e, blk))
        def _():
            _cast_weight(wg_ref, wgb)
            _cast_weight(wu_ref, wub)

        pltpu.make_async_copy(hn_hbm.at[pl.ds(0, MOE_BLOCK), :], xs.at[slot], sem.at[slot]).wait()
        x = xs[slot].astype(BF16)
        a = jnp.dot(x, wgb[...], preferred_element_type=F32)
        u = jnp.dot(x, wub[...], preferred_element_type=F32)
        hid_ref[...] = (a * jax.nn.sigmoid(a) * u).astype(hid_ref.dtype)

    @pl.when(blk >= n_real[0])
    def _():
        hid_ref[...] = jnp.zeros_like(hid_ref)


def _moe_down_body(blk_e, n_real, dst_ref, hid_ref, wd_ref, out_hbm, ys, wdb, sem, *, n_assign):
    blk = pl.program_id(0)

    def wait(slot):
        pltpu.make_async_copy(ys.at[slot], out_hbm.at[pl.ds(0, MOE_BLOCK), :], sem.at[slot]).wait()

    @pl.when(blk == 0)
    def _():
        ys[0] = jnp.zeros((MOE_BLOCK, D_MODEL), F32)
        for s in range(2):
            trash = pltpu.make_async_copy(
                ys.at[0], out_hbm.at[pl.ds(n_assign + s * MOE_BLOCK, MOE_BLOCK), :], sem.at[0])
            trash.start()
            trash.wait()

    @pl.when(blk < n_real[0])
    def _():
        slot = blk % 2

        @pl.when(_expert_changed(blk_e, blk))
        def _():
            _cast_weight(wd_ref, wdb)

        ys[slot] = jnp.dot(hid_ref[...], wdb[...], preferred_element_type=F32)

        def body(r, carry):
            dst = dst_ref[0, 0, r]
            pltpu.make_async_copy(ys.at[slot, pl.ds(r, 1), :], out_hbm.at[pl.ds(dst, 1), :], sem.at[slot]).start()
            return carry
        lax.fori_loop(0, MOE_BLOCK, body, 0)

        @pl.when(blk > 0)
        def _():
            wait(1 - slot)

        @pl.when(blk == n_real[0] - 1)
        def _():
            wait(slot)


def _moe_forward(hn, idx, w_gate, w_up, w_down, n_tokens):
    A = n_tokens * TOP_K
    out_rows = A + 2 * MOE_BLOCK
    R = A + N_EXPERTS * MOE_BLOCK
    n_blk = R // MOE_BLOCK
    e_flat = idx[:, :TOP_K].reshape(A)
    onehot = (e_flat[:, None] == jnp.arange(N_EXPERTS, dtype=jnp.int32)[None, :]).astype(jnp.int32)
    csum = jnp.cumsum(onehot, axis=0)
    counts = csum[-1]
    padded = ((counts + MOE_BLOCK - 1) // MOE_BLOCK) * MOE_BLOCK
    pend = jnp.cumsum(padded)
    pstart = pend - padded
    dest = jnp.sum(onehot * (csum - 1 + pstart[None, :]), axis=1)
    row_a = jnp.full((R,), -1, jnp.int32).at[dest].set(jnp.arange(A, dtype=jnp.int32), unique_indices=True)
    n_real = (pend[-1] // MOE_BLOCK).astype(jnp.int32).reshape(1)
    blk_start = jnp.arange(n_blk, dtype=jnp.int32) * MOE_BLOCK
    blk_e = jnp.minimum(jnp.sum((pend[None, :] <= blk_start[:, None]).astype(jnp.int32), axis=1), N_EXPERTS - 1)
    row_tok = (jnp.maximum(row_a, 0) // TOP_K).reshape(n_blk, 1, MOE_BLOCK)
    r_in_blk = jnp.arange(R, dtype=jnp.int32) % MOE_BLOCK
    slot_of_row = (jnp.arange(R, dtype=jnp.int32) // MOE_BLOCK) % 2
    row_dst = jnp.where(row_a >= 0, row_a, A + slot_of_row * MOE_BLOCK + r_in_blk).reshape(n_blk, 1, MOE_BLOCK)

    clamp = lambda i, nr: jnp.minimum(i, nr[0] - 1)
    idx_spec = lambda shift: pl.BlockSpec((1, 1, MOE_BLOCK), lambda i, be, nr: (clamp(i + shift, nr), 0, 0),
                                          memory_space=pltpu.SMEM)
    w_spec = lambda k, n: pl.BlockSpec((None, k, n), lambda i, be, nr: (be[clamp(i, nr)], 0, 0))
    hid = pl.pallas_call(
        _moe_up_body,
        grid_spec=pltpu.PrefetchScalarGridSpec(
            num_scalar_prefetch=2, grid=(n_blk,),
            in_specs=[idx_spec(0), idx_spec(1), pl.BlockSpec(memory_space=pl.ANY),
                      w_spec(D_MODEL, D_FF_EXPERT), w_spec(D_MODEL, D_FF_EXPERT)],
            out_specs=pl.BlockSpec((MOE_BLOCK, D_FF_EXPERT), lambda i, be, nr: (i, 0)),
            scratch_shapes=[pltpu.VMEM((2, MOE_BLOCK, D_MODEL), F32),
                            pltpu.VMEM((D_MODEL, D_FF_EXPERT), BF16), pltpu.VMEM((D_MODEL, D_FF_EXPERT), BF16),
                            pltpu.SemaphoreType.DMA((2,))]),
        out_shape=jax.ShapeDtypeStruct((R, D_FF_EXPERT), BF16),
        compiler_params=_params(("arbitrary",)), name="moe_up",
    )(blk_e, n_real, row_tok, row_tok, hn, w_gate, w_up)
    return pl.pallas_call(
        functools.partial(_moe_down_body, n_assign=A),
        grid_spec=pltpu.PrefetchScalarGridSpec(
            num_scalar_prefetch=2, grid=(n_blk,),
            in_specs=[idx_spec(0),
                      pl.BlockSpec((MOE_BLOCK, D_FF_EXPERT), lambda i, be, nr: (clamp(i, nr), 0)),
                      w_spec(D_FF_EXPERT, D_MODEL)],
            out_specs=pl.BlockSpec(memory_space=pl.ANY),
            scratch_shapes=[pltpu.VMEM((2, MOE_BLOCK, D_MODEL), F32),
                            pltpu.VMEM((D_FF_EXPERT, D_MODEL), BF16),
                            pltpu.SemaphoreType.DMA((2,))]),
        out_shape=jax.ShapeDtypeStruct((out_rows, D_MODEL), F32),
        compiler_params=_params(("arbitrary",)), name="moe_down",
    )(blk_e, n_real, row_dst, hid, w_down)


FINAL_TM = 512


def _final_body(h_ref, y_ref, gate_ref, g_ref, o_ref):
    gate = gate_ref[...]
    h = h_ref[...] + gate[:, 0:1] * y_ref[:, :D_MODEL] + gate[:, 1:2] * y_ref[:, D_MODEL:]
    o_ref[...] = _rms(h, g_ref[...])


def _final(h, ys, gate, gain):
    rows = h.shape[0]
    y2 = ys.reshape(ys.shape[0] // TOP_K, TOP_K * D_MODEL)
    tile = lambda width: pl.BlockSpec((FINAL_TM, width), lambda i: (i, 0))
    g2 = gain.reshape(1, D_MODEL).astype(F32)
    return pl.pallas_call(
        _final_body, grid=(rows // FINAL_TM,),
        in_specs=[tile(D_MODEL), tile(TOP_K * D_MODEL), tile(LANES), pl.BlockSpec((1, D_MODEL), lambda i: (0, 0))],
        out_specs=tile(D_MODEL), out_shape=jax.ShapeDtypeStruct((rows, D_MODEL), F32),
        compiler_params=_params(("parallel",)), name="combine_final_norm")(h, y2, gate, g2)


def kernel(x, mem, norm_mix, w_in, attn_out_norm, sink_logit, w_gla_gf, b_gla_gf, w_gla_gb, b_gla_gb, gla_out_norm, w_out, norm_cross, norm_mem, w_cq, w_ck, w_cv, w_co, norm_ffn, w_router_group, b_router_group, w_router_expert, b_router_expert, w_gate, w_up, w_down, norm_final):
    batch, seq, _ = x.shape
    mem_len = mem.shape[1]
    n_tokens = batch * seq
    h = x.reshape(n_tokens, D_MODEL)
    memf = mem.reshape(batch * mem_len, D_MODEL)
    assert norm_mix.shape[0] == 1, "the combine step is fused with the final norm: single-layer stacks only"
    for l in range(norm_mix.shape[0]):
        w_lr = jnp.zeros((D_MODEL, LANES), F32).at[:, :2 * GLA_LOWRANK].set(w_in[l][:, MAIN_COLS:]).astype(BF16)
        proj, lr = _dense([h], w_in[l], n_cols=MAIN_COLS, out_dtype=BF16, gain=norm_mix[l], extra_w=w_lr,
                          name="in_proj")
        o_a = _window_attention(proj, sink_logit[l], attn_out_norm[l], batch, seq).reshape(n_tokens, ATT_Q)
        o_g = _gla(proj, lr, w_gla_gf[l], b_gla_gf[l], w_gla_gb[l], b_gla_gb[l], gla_out_norm[l],
                   batch, seq).reshape(n_tokens, GLA_V)
        h = _dense([o_a, o_g], w_out[l], n_cols=D_MODEL, out_dtype=F32, res=h, name="out_proj")
        kx = _dense([memf], w_ck[l], n_cols=D_MODEL, out_dtype=BF16, gain=norm_mem[l], name="mem_k_proj")
        vx = _dense([memf], w_cv[l], n_cols=D_MODEL, out_dtype=BF16, gain=norm_mem[l], name="mem_v_proj")
        q = _dense([h], w_cq[l], n_cols=D_MODEL, out_dtype=BF16, gain=norm_cross[l], name="cross_q_proj")
        o = _cross_attention(q, kx, vx, batch, seq, mem_len)
        h = _dense([o], w_co[l], n_cols=D_MODEL, out_dtype=F32, res=h, name="cross_out_proj")
        hn, idx, gate = _router(h, norm_ffn[l], w_router_group[l], b_router_group[l],
                                w_router_expert[l], b_router_expert[l])
        ys = _moe_forward(hn, idx, w_gate[l], w_up[l], w_down[l], n_tokens)
    return _final(h, ys, gate, norm_final).reshape(batch, seq, D_MODEL)
```

```python
import functools

import jax
import jax.numpy as jnp
from jax import lax
from jax.experimental import pallas as pl
from jax.experimental.pallas import tpu as pltpu

F32 = jnp.float32
BF16 = jnp.bfloat16

D_MODEL = 2048
N_Q_HEADS = 8
N_KV_HEADS = 2
Q_PER_KV = N_Q_HEADS // N_KV_HEADS
HEAD_DIM = 128
WINDOW = 128
WBLK = 128
GLA_HEADS = 4
GLA_DK = 128
GLA_DV = 256
GLA_LOWRANK = 16
GLA_TAU = 16.0
GLA_CHUNK = 64
ATT_Q = N_Q_HEADS * HEAD_DIM
ATT_KV = N_KV_HEADS * HEAD_DIM
GLA_QK = GLA_HEADS * GLA_DK
GLA_V = GLA_HEADS * GLA_DV
MAIN_COLS = ATT_Q + 2 * ATT_KV + 2 * GLA_QK + 2 * GLA_V
COL_KA = ATT_Q
COL_VA = COL_KA + ATT_KV
COL_QG = COL_VA + ATT_KV
COL_KG = COL_QG + GLA_QK
COL_VG = COL_KG + GLA_QK
COL_RG = COL_VG + GLA_V
X_HEADS = 4
X_HEAD_DIM = D_MODEL // X_HEADS
N_GROUPS = 4
EXPERTS_PER_GROUP = 8
N_EXPERTS = N_GROUPS * EXPERTS_PER_GROUP
TOP_K = 2
D_FF_EXPERT = D_MODEL // 2
MOE_BLOCK = 256
RMS_EPS = 1e-6
NEG_INF = -1e30

LANES = 128
VMEM_LIMIT = 56 * 1024 * 1024
W_STAGE_ROWS = 128


def _params(sem):
    return pltpu.CompilerParams(dimension_semantics=sem, vmem_limit_bytes=VMEM_LIMIT)


def _nt(a, b):
    return lax.dot_general(a, b, (((1,), (1,)), ((), ())), preferred_element_type=F32)


def _tn(a, b):
    return lax.dot_general(a, b, (((0,), (0,)), ((), ())), preferred_element_type=F32)


def _rms(x, gain):
    return x * lax.rsqrt(jnp.mean(x * x, axis=-1, keepdims=True) + RMS_EPS) * gain


def _load_weight_bf16(w_hbm, wb, stage, sem, k_rows, n_cols):
    n_chunks = k_rows // W_STAGE_ROWS

    def copy(c):
        return pltpu.make_async_copy(
            w_hbm.at[pl.ds(c * W_STAGE_ROWS, W_STAGE_ROWS), pl.ds(0, n_cols)],
            stage.at[c % 2], sem.at[c % 2])

    copy(0).start()
    for c in range(n_chunks):
        if c + 1 < n_chunks:
            copy(c + 1).start()
        copy(c).wait()
        wb[c * W_STAGE_ROWS:(c + 1) * W_STAGE_ROWS, :] = stage[c % 2].astype(BF16)


def _dense_body(*refs, part_widths, has_norm, has_extra, has_res, n_cols, n_chunk):
    it = iter(refs)
    x_refs = [next(it) for _ in part_widths]
    g_ref = next(it) if has_norm else None
    w_hbm = next(it)
    ew_ref = next(it) if has_extra else None
    res_ref = next(it) if has_res else None
    o_ref = next(it)
    eo_ref = next(it) if has_extra else None
    wb, stage, sem = next(it), next(it), next(it)
    u_ref = next(it) if has_norm else None
    k_rows = sum(part_widths)

    @pl.when(pl.program_id(0) == 0)
    def _():
        _load_weight_bf16(w_hbm, wb, stage, sem, k_rows, n_cols)

    if has_norm:
        u_ref[...] = _rms(x_refs[0][...], g_ref[...]).astype(BF16)
        lhs = [(u_ref, 0, k_rows)]
    else:
        lhs, off = [], 0
        for r, kw in zip(x_refs, part_widths):
            lhs.append((r, off, kw))
            off += kw
    for n0 in range(0, n_cols, n_chunk):
        acc = None
        for r, off, kw in lhs:
            d = jnp.dot(r[...], wb[off:off + kw, n0:n0 + n_chunk], preferred_element_type=F32)
            acc = d if acc is None else acc + d
        if has_res:
            acc = acc + res_ref[:, n0:n0 + n_chunk]
        o_ref[:, n0:n0 + n_chunk] = acc.astype(o_ref.dtype)
    if has_extra:
        eo_ref[...] = jnp.dot(u_ref[...], ew_ref[...], preferred_element_type=F32)


def _dense(xs, w, *, n_cols, out_dtype, gain=None, extra_w=None, res=None, tm=512, n_chunk=512, name):
    rows = xs[0].shape[0]
    part_widths = tuple(x.shape[1] for x in xs)
    k_rows = sum(part_widths)
    has_norm, has_extra, has_res = gain is not None, extra_w is not None, res is not None
    row_spec = lambda width: pl.BlockSpec((tm, width), lambda i: (i, 0))
    full_spec = lambda a: pl.BlockSpec(a.shape, lambda i: (0, 0))
    args, in_specs = list(xs), [row_spec(kw) for kw in part_widths]
    if has_norm:
        args.append(gain.reshape(1, k_rows).astype(F32))
        in_specs.append(full_spec(args[-1]))
    args.append(w)
    in_specs.append(pl.BlockSpec(memory_space=pl.ANY))
    if has_extra:
        args.append(extra_w)
        in_specs.append(full_spec(extra_w))
    if has_res:
        args.append(res)
        in_specs.append(row_spec(n_cols))
    out_shape = [jax.ShapeDtypeStruct((rows, n_cols), out_dtype)]
    out_specs = [row_spec(n_cols)]
    if has_extra:
        out_shape.append(jax.ShapeDtypeStruct((rows, extra_w.shape[1]), F32))
        out_specs.append(row_spec(extra_w.shape[1]))
    scratch = [pltpu.VMEM((k_rows, n_cols), BF16),
               pltpu.VMEM((2, W_STAGE_ROWS, n_cols), F32),
               pltpu.SemaphoreType.DMA((2,))]
    if has_norm:
        scratch.append(pltpu.VMEM((tm, k_rows), BF16))
    body = functools.partial(_dense_body, part_widths=part_widths, has_norm=has_norm, has_extra=has_extra,
                             has_res=has_res, n_cols=n_cols, n_chunk=n_chunk)
    outs = pl.pallas_call(
        body, grid=(rows // tm,), in_specs=in_specs, out_specs=out_specs, out_shape=out_shape,
        scratch_shapes=scratch, compiler_params=_params(("arbitrary",)), name=name)(*args)
    return outs if has_extra else outs[0]


ATT_TQ = 512


def _winattn_body(sink_ref, q_ref, kp_ref, km_ref, kn_ref, vp_ref, vm_ref, vn_ref, g_ref, o_ref,
                  kcat, vcat, obuf, *, seq):
    s0 = pl.program_id(1) * ATT_TQ
    kcat[0:WBLK, :] = kp_ref[0]
    kcat[WBLK:WBLK + ATT_TQ, :] = km_ref[0]
    kcat[WBLK + ATT_TQ:, :] = kn_ref[0]
    vcat[0:WBLK, :] = vp_ref[0]
    vcat[WBLK:WBLK + ATT_TQ, :] = vm_ref[0]
    vcat[WBLK + ATT_TQ:, :] = vn_ref[0]
    qi = lax.broadcasted_iota(jnp.int32, (WBLK, 3 * WBLK), 0) + WBLK
    ki = lax.broadcasted_iota(jnp.int32, (WBLK, 3 * WBLK), 1)
    dist_i = jnp.abs(ki - qi)
    neg_dist = jnp.where(dist_i <= WINDOW, -dist_i.astype(F32), NEG_INF)
    scale = HEAD_DIM ** -0.5
    for qb in range(ATT_TQ // WBLK):
        kabs = s0 + (qb - 1) * WBLK + ki
        bias_unit = jnp.where((kabs >= 0) & (kabs < seq), neg_dist, NEG_INF)
        head_cols = lambda j: slice(j * HEAD_DIM, (j + 1) * HEAD_DIM)
        kv_cols = lambda j: head_cols(j // Q_PER_KV)
        key_rows = slice(qb * WBLK, (qb + 3) * WBLK)
        scores = []
        for j in range(N_Q_HEADS):
            slope = 2.0 ** (-8.0 * (j + 1) / N_Q_HEADS)
            q = q_ref[0, qb * WBLK:(qb + 1) * WBLK, head_cols(j)]
            scores.append(_nt(q, kcat[key_rows, kv_cols(j)]) * scale + slope * bias_unit)
        probs, denoms = [], []
        for j in range(N_Q_HEADS):
            sink = sink_ref[j]
            m = jnp.maximum(jnp.max(scores[j], axis=-1, keepdims=True), sink)
            p = jnp.exp(scores[j] - m)
            denoms.append(jnp.sum(p, axis=-1, keepdims=True) + jnp.exp(sink - m))
            probs.append(p.astype(BF16))
        for j in range(N_Q_HEADS):
            pv = jnp.dot(probs[j], vcat[key_rows, kv_cols(j)], preferred_element_type=F32)
            obuf[qb, :, head_cols(j)] = pv / denoms[j]
        o_ref[0, qb * WBLK:(qb + 1) * WBLK, :] = _rms(obuf[qb], g_ref[...]).astype(o_ref.dtype)


def _window_attention(proj, sink_logit, gain, batch, seq):
    p3 = proj.reshape(batch, seq, MAIN_COLS)
    nq = seq // ATT_TQ
    per = ATT_TQ // WBLK
    last = seq // WBLK - 1
    main = lambda col: pl.BlockSpec((1, ATT_TQ, ATT_KV), lambda b, i: (b, i, col))
    prev = lambda col: pl.BlockSpec((1, WBLK, ATT_KV), lambda b, i: (b, jnp.maximum(i * per - 1, 0), col))
    nxt = lambda col: pl.BlockSpec((1, WBLK, ATT_KV), lambda b, i: (b, jnp.minimum(i * per + per, last), col))
    ck, cv = COL_KA // ATT_KV, COL_VA // ATT_KV
    return pl.pallas_call(
        functools.partial(_winattn_body, seq=seq),
        grid=(batch, nq),
        in_specs=[pl.BlockSpec(memory_space=pltpu.SMEM),
                  pl.BlockSpec((1, ATT_TQ, ATT_Q), lambda b, i: (b, i, 0)),
                  prev(ck), main(ck), nxt(ck), prev(cv), main(cv), nxt(cv),
                  pl.BlockSpec((1, ATT_Q), lambda b, i: (0, 0))],
        out_specs=pl.BlockSpec((1, ATT_TQ, ATT_Q), lambda b, i: (b, i, 0)),
        out_shape=jax.ShapeDtypeStruct((batch, seq, ATT_Q), BF16),
        scratch_shapes=[pltpu.VMEM((ATT_TQ + 2 * WBLK, ATT_KV), BF16),
                        pltpu.VMEM((ATT_TQ + 2 * WBLK, ATT_KV), BF16),
                        pltpu.VMEM((ATT_TQ // WBLK, WBLK, ATT_Q), F32)],
        compiler_params=_params(("parallel", "parallel")), name="window_attention",
    )(sink_logit.astype(F32), p3, p3, p3, p3, p3, p3, p3, gain.reshape(1, ATT_Q).astype(F32))


GLA_SCAN_ROWS = 256
GLA_EPI_ROWS = 512
GLA_UNROLL = 2


def _split3(x):
    hi = x.astype(BF16)
    r1 = x - hi.astype(F32)
    mid = r1.astype(BF16)
    lo = (r1 - mid.astype(F32)).astype(BF16)
    return hi, mid, lo


def _gla_body(q_ref, k_ref, v_ref, r_ref, lr_ref, wf_ref, wb_ref, bf_ref, bb_ref, gn_ref, o_ref,
              cum_f, cum_b, acc, st_f, st_b, *, seq):
    C = GLA_CHUNK
    n_chunks = seq // C
    ri = lax.broadcasted_iota(jnp.int32, (GLA_SCAN_ROWS, GLA_SCAN_ROWS), 0)
    ci = lax.broadcasted_iota(jnp.int32, (GLA_SCAN_ROWS, GLA_SCAN_ROWS), 1)
    same = (ri // C) == (ci // C)
    tri_f = jnp.where(same & (ci <= ri), 1.0, 0.0).astype(BF16)
    tri_b = jnp.where(same & (ci >= ri), 1.0, 0.0).astype(BF16)

    def scan_body(t, carry):
        rows = pl.ds(pl.multiple_of(t * GLA_SCAN_ROWS, GLA_SCAN_ROWS), GLA_SCAN_ROWS)
        lr = lr_ref[0, rows, :].astype(BF16)
        zs = [jnp.dot(lr, w_ref[...], preferred_element_type=F32) + b_ref[...]
              for w_ref, b_ref in ((wf_ref, bf_ref), (wb_ref, bb_ref))]
        gs = [_split3((jnp.minimum(z, 0.0) - jnp.log1p(jnp.exp(-jnp.abs(z)))) / GLA_TAU) for z in zs]
        for g3, tri, dst in zip(gs, (tri_f, tri_b), (cum_f, cum_b)):
            dst[rows, :] = sum(jnp.dot(tri, part, preferred_element_type=F32) for part in g3)
        return carry

    lax.fori_loop(0, seq // GLA_SCAN_ROWS, scan_body, 0)

    acc[...] = jnp.zeros_like(acc)
    st_f[...] = jnp.zeros_like(st_f)
    st_b[...] = jnp.zeros_like(st_b)
    rr = lax.broadcasted_iota(jnp.int32, (C, C), 0)
    cc = lax.broadcasted_iota(jnp.int32, (C, C), 1)
    scale = GLA_DK ** -0.5

    def chunk_body(i, carry):
        jobs = []
        for u in range(GLA_UNROLL):
            c = i * GLA_UNROLL + u
            jobs += [(c, cum_f, True), (n_chunks - 1 - c, cum_b, False)]
        prep = []
        for c, cum, forward in jobs:
            rows = pl.ds(pl.multiple_of(c * C, C), C)
            b = cum[rows, :]
            b_end = b[C - 1:C, :] if forward else b[0:1, :]
            q = q_ref[0, rows, :].astype(F32) * scale
            k = k_ref[0, rows, :].astype(F32)
            v = v_ref[0, rows, :]
            q_dec = (q * jnp.exp(b)).astype(BF16)
            k_inc = (k * jnp.exp(-b)).astype(BF16)
            k_dec = (k * jnp.exp(b_end - b)).astype(BF16)
            prep.append((rows, v, q_dec, k_inc, k_dec, jnp.exp(b_end)))
        attn = [_nt(q_dec, k_inc) for _, _, q_dec, k_inc, _, _ in prep]
        kv_t = [_tn(v, k_dec) for _, v, _, _, k_dec, _ in prep]
        state_t = {True: st_f[...], False: st_b[...]}
        o_inter = []
        for (_, _, forward), (_, _, q_dec, _, _, decay), kv in zip(jobs, prep, kv_t):
            o_inter.append(_nt(q_dec, state_t[forward].astype(BF16)))
            state_t[forward] = state_t[forward] * decay + kv
        st_f[...] = state_t[True]
        st_b[...] = state_t[False]
        for (_, _, forward), (rows, v, _, _, _, _), a, oi in zip(jobs, prep, attn, o_inter):
            a = jnp.where((cc <= rr) if forward else (cc >= rr), a, 0.0)
            o = jnp.dot(a.astype(BF16), v, preferred_element_type=F32) + oi
            acc[rows, :] = acc[rows, :] + o
        return carry

    lax.fori_loop(0, n_chunks // GLA_UNROLL, chunk_body, 0)

    def epi_body(t, carry):
        rows = pl.ds(pl.multiple_of(t * GLA_EPI_ROWS, GLA_EPI_ROWS), GLA_EPI_ROWS)
        r = r_ref[0, rows, :].astype(F32)
        o_ref[0, rows, :] = (_rms(acc[rows, :], gn_ref[...]) * (r * jax.nn.sigmoid(r))).astype(o_ref.dtype)
        return carry

    lax.fori_loop(0, seq // GLA_EPI_ROWS, epi_body, 0)


def _gla(proj, lr, w_gf, b_gf, w_gb, b_gb, gain, batch, seq):
    p3 = proj.reshape(batch, seq, MAIN_COLS)
    lr3 = lr.reshape(batch, seq, LANES)
    wf = jnp.zeros((LANES, GLA_QK), F32).at[:GLA_LOWRANK].set(w_gf).astype(BF16)
    wb = jnp.zeros((LANES, GLA_QK), F32).at[GLA_LOWRANK:2 * GLA_LOWRANK].set(w_gb).astype(BF16)
    seq_blk = lambda width, col0: pl.BlockSpec((1, seq, width), lambda b, h: (b, 0, col0 // width + h))
    head_w = pl.BlockSpec((LANES, GLA_DK), lambda b, h: (0, h))
    head_b = pl.BlockSpec((1, GLA_DK), lambda b, h: (0, h))
    return pl.pallas_call(
        functools.partial(_gla_body, seq=seq),
        grid=(batch, GLA_HEADS),
        in_specs=[seq_blk(GLA_DK, COL_QG), seq_blk(GLA_DK, COL_KG), seq_blk(GLA_DV, COL_VG),
                  seq_blk(GLA_DV, COL_RG),
                  pl.BlockSpec((1, seq, LANES), lambda b, h: (b, 0, 0)),
                  head_w, head_w, head_b, head_b,
                  pl.BlockSpec((1, GLA_DV), lambda b, h: (0, 0))],
        out_specs=pl.BlockSpec((1, seq, GLA_DV), lambda b, h: (b, 0, h)),
        out_shape=jax.ShapeDtypeStruct((batch, seq, GLA_V), BF16),
        scratch_shapes=[pltpu.VMEM((seq, GLA_DK), F32), pltpu.VMEM((seq, GLA_DK), F32),
                        pltpu.VMEM((seq, GLA_DV), F32),
                        pltpu.VMEM((GLA_DV, GLA_DK), F32), pltpu.VMEM((GLA_DV, GLA_DK), F32)],
        compiler_params=_params(("parallel", "parallel")), name="gla",
    )(p3, p3, p3, p3, lr3, wf, wb, b_gf.reshape(1, GLA_QK).astype(F32), b_gb.reshape(1, GLA_QK).astype(F32),
      gain.reshape(1, GLA_DV).astype(F32))


XATT_TQ = 512


def _xattn_body(q_ref, k_ref, v_ref, o_ref):
    scale = X_HEAD_DIM ** -0.5
    for h in range(X_HEADS):
        cols = slice(h * X_HEAD_DIM, (h + 1) * X_HEAD_DIM)
        s = _nt(q_ref[0, :, cols], k_ref[0, :, cols]) * scale
        m = jnp.max(s, axis=-1, keepdims=True)
        p = jnp.exp(s - m)
        denom = jnp.sum(p, axis=-1, keepdims=True)
        pv = jnp.dot(p.astype(BF16), v_ref[0, :, cols], preferred_element_type=F32)
        o_ref[0, :, cols] = (pv / denom).astype(o_ref.dtype)


def _cross_attention(q, kx, vx, batch, seq, mem_len):
    q3 = q.reshape(batch, seq, D_MODEL)
    k3 = kx.reshape(batch, mem_len, D_MODEL)
    v3 = vx.reshape(batch, mem_len, D_MODEL)
    mem_spec = pl.BlockSpec((1, mem_len, D_MODEL), lambda b, i: (b, 0, 0))
    tile = pl.BlockSpec((1, XATT_TQ, D_MODEL), lambda b, i: (b, i, 0))
    out = pl.pallas_call(
        _xattn_body, grid=(batch, seq // XATT_TQ), in_specs=[tile, mem_spec, mem_spec], out_specs=tile,
        out_shape=jax.ShapeDtypeStruct((batch, seq, D_MODEL), BF16),
        compiler_params=_params(("parallel", "parallel")), name="cross_attention")(q3, k3, v3)
    return out.reshape(batch * seq, D_MODEL)


ROUTER_TM = 512


def _router_body(h_ref, g_ref, w_ref, b_ref, hn_ref, idx_ref, gate_ref):
    hn = _rms(h_ref[...], g_ref[...])
    hn_ref[...] = hn
    logits = jnp.dot(hn.astype(BF16), w_ref[...], preferred_element_type=F32) + b_ref[...]
    lane = lax.broadcasted_iota(jnp.int32, logits.shape, 1).astype(F32)
    ninf = -jnp.inf
    first = lambda hit: jnp.min(jnp.where(hit, lane, float(LANES)), axis=-1, keepdims=True)
    in_groups = lane < N_GROUPS
    gl = jnp.where(in_groups, logits, ninf)
    gmax = jnp.max(gl, axis=-1, keepdims=True)
    g_idx = first(gl == gmax)
    p_group = 1.0 / jnp.sum(jnp.where(in_groups, jnp.exp(logits - gmax), 0.0), axis=-1, keepdims=True)
    lo = N_GROUPS + EXPERTS_PER_GROUP * g_idx
    el = jnp.where((lane >= lo) & (lane < lo + EXPERTS_PER_GROUP), logits, ninf)
    e1 = jnp.max(el, axis=-1, keepdims=True)
    i1 = first(el == e1)
    el2 = jnp.where(lane == i1, ninf, el)
    e2 = jnp.max(el2, axis=-1, keepdims=True)
    i2 = first(el2 == e2)
    t = jnp.exp(e2 - e1)
    w1 = p_group / (1.0 + t)
    w2 = p_group * t / (1.0 + t)
    idx_ref[...] = jnp.where(lane == 0, i1 - N_GROUPS, jnp.where(lane == 1, i2 - N_GROUPS, 0.0)).astype(jnp.int32)
    gate_ref[...] = jnp.where(lane == 0, w1, jnp.where(lane == 1, w2, 0.0))


def _router(h, gain, w_rg, b_rg, w_re, b_re):
    rows = h.shape[0]
    n_log = N_GROUPS + N_EXPERTS
    w = jnp.zeros((D_MODEL, LANES), F32).at[:, :N_GROUPS].set(w_rg).at[:, N_GROUPS:n_log].set(w_re).astype(BF16)
    b = jnp.zeros((1, LANES), F32).at[0, :N_GROUPS].set(b_rg).at[0, N_GROUPS:n_log].set(b_re)
    tile = lambda width: pl.BlockSpec((ROUTER_TM, width), lambda i: (i, 0))
    full = lambda a: pl.BlockSpec(a.shape, lambda i: (0, 0))
    g2 = gain.reshape(1, D_MODEL).astype(F32)
    return pl.pallas_call(
        _router_body, grid=(rows // ROUTER_TM,),
        in_specs=[tile(D_MODEL), full(g2), full(w), full(b)],
        out_specs=[tile(D_MODEL), tile(LANES), tile(LANES)],
        out_shape=[jax.ShapeDtypeStruct((rows, D_MODEL), F32),
                   jax.ShapeDtypeStruct((rows, LANES), jnp.int32),
                   jax.ShapeDtypeStruct((rows, LANES), F32)],
        compiler_params=_params(("parallel",)), name="moe_router")(h, g2, w, b)


W_CAST_ROWS = 256


def _cast_weight(src_ref, dst_ref):
    def body(c, carry):
        rows = pl.ds(pl.multiple_of(c * W_CAST_ROWS, W_CAST_ROWS), W_CAST_ROWS)
        dst_ref[rows, :] = src_ref[rows, :].astype(BF16)
        return carry
    lax.fori_loop(0, src_ref.shape[0] // W_CAST_ROWS, body, 0)


def _expert_changed(blk_e, blk):
    prev = blk_e[jnp.maximum(blk - 1, 0)]
    return (blk == 0) | (blk_e[blk] != prev)


MOE_CHUNKS = 4


def _moe_up_body(blk_e, n_real, tok_ref, tok_next_ref, hn_hbm, wg_ref, wu_ref, hid_ref,
                 xs, xb, wgb, wub, sem):
    blk = pl.program_id(0)

    def row_copy(idx_ref, slot, r):
        return pltpu.make_async_copy(hn_hbm.at[pl.ds(idx_ref[0, 0, r], 1), :], xs.at[slot, pl.ds(r, 1), :],
                                     sem.at[slot])

    def wait(slot):
        pltpu.make_async_copy(hn_hbm.at[pl.ds(0, MOE_BLOCK), :], xs.at[slot], sem.at[slot]).wait()

    @pl.when(blk == 0)
    def _():
        def body(r, carry):
            row_copy(tok_ref, 0, r).start()
            return carry
        lax.fori_loop(0, MOE_BLOCK, body, 0)

    @pl.when(blk < n_real[0])
    def _():
        slot = blk % 2

        @pl.when(_expert_changed(blk_e, blk))
        def _():
            _cast_weight(wg_ref, wgb)
            _cast_weight(wu_ref, wub)

        wait(slot)
        xb[...] = xs[slot].astype(BF16)
        cw = D_FF_EXPERT // MOE_CHUNKS
        rows_per_chunk = MOE_BLOCK // MOE_CHUNKS
        for c in range(MOE_CHUNKS):
            cols = slice(c * cw, (c + 1) * cw)
            a = jnp.dot(xb[...], wgb[:, cols], preferred_element_type=F32)
            u = jnp.dot(xb[...], wub[:, cols], preferred_element_type=F32)
            hid_ref[:, cols] = (a * jax.nn.sigmoid(a) * u).astype(hid_ref.dtype)
            for r in range(c * rows_per_chunk, (c + 1) * rows_per_chunk):
                row_copy(tok_next_ref, 1 - slot, r).start()

        @pl.when(blk == n_real[0] - 1)
        def _():
            wait(1 - slot)

    @pl.when(blk >= n_real[0])
    def _():
        hid_ref[...] = jnp.zeros_like(hid_ref)


def _moe_down_body(blk_e, n_real, dst_ref, dst_prev_ref, hid_ref, wd_ref, out_hbm, ys, wdb, sem, *, n_assign):
    blk = pl.program_id(0)

    def row_copy(idx_ref, slot, r):
        return pltpu.make_async_copy(ys.at[slot, pl.ds(r, 1), :], out_hbm.at[pl.ds(idx_ref[0, 0, r], 1), :],
                                     sem.at[slot])

    def wait(slot):
        pltpu.make_async_copy(ys.at[slot], out_hbm.at[pl.ds(0, MOE_BLOCK), :], sem.at[slot]).wait()

    @pl.when(blk == 0)
    def _():
        ys[1] = jnp.zeros((MOE_BLOCK, D_MODEL), F32)
        for s in range(2):
            trash = pltpu.make_async_copy(
                ys.at[1], out_hbm.at[pl.ds(n_assign + s * MOE_BLOCK, MOE_BLOCK), :], sem.at[1])
            trash.start()
            trash.wait()

    @pl.when(blk < n_real[0])
    def _():
        slot = blk % 2

        @pl.when(_expert_changed(blk_e, blk))
        def _():
            _cast_weight(wd_ref, wdb)

        @pl.when(blk > 0)
        def _():
            wait(slot)

        cw = D_MODEL // MOE_CHUNKS
        rows_per_chunk = MOE_BLOCK // MOE_CHUNKS
        for c in range(MOE_CHUNKS):
            cols = slice(c * cw, (c + 1) * cw)
            ys[slot, :, cols] = jnp.dot(hid_ref[...], wdb[:, cols], preferred_element_type=F32)
            for r in range(c * rows_per_chunk, (c + 1) * rows_per_chunk):
                row_copy(dst_prev_ref, 1 - slot, r).start()

        @pl.when(blk == n_real[0] - 1)
        def _():
            def body(r, carry):
                row_copy(dst_ref, slot, r).start()
                return carry
            lax.fori_loop(0, MOE_BLOCK, body, 0)
            wait(1 - slot)
            wait(slot)


def _moe_forward(hn, idx, w_gate, w_up, w_down, n_tokens):
    A = n_tokens * TOP_K
    out_rows = A + 2 * MOE_BLOCK
    R = A + N_EXPERTS * MOE_BLOCK
    n_blk = R // MOE_BLOCK
    e_flat = idx[:, :TOP_K].reshape(A)
    onehot = (e_flat[:, None] == jnp.arange(N_EXPERTS, dtype=jnp.int32)[None, :]).astype(jnp.int32)
    csum = jnp.cumsum(onehot, axis=0)
    counts = csum[-1]
    padded = ((counts + MOE_BLOCK - 1) // MOE_BLOCK) * MOE_BLOCK
    pend = jnp.cumsum(padded)
    pstart = pend - padded
    dest = jnp.sum(onehot * (csum - 1 + pstart[None, :]), axis=1)
    row_a = jnp.full((R,), -1, jnp.int32).at[dest].set(jnp.arange(A, dtype=jnp.int32), unique_indices=True)
    n_real = (pend[-1] // MOE_BLOCK).astype(jnp.int32).reshape(1)
    blk_start = jnp.arange(n_blk, dtype=jnp.int32) * MOE_BLOCK
    blk_e = jnp.minimum(jnp.sum((pend[None, :] <= blk_start[:, None]).astype(jnp.int32), axis=1), N_EXPERTS - 1)
    row_tok = (jnp.maximum(row_a, 0) // TOP_K).reshape(n_blk, 1, MOE_BLOCK)
    row = jnp.arange(R, dtype=jnp.int32)
    trash = A + ((row // MOE_BLOCK) % 2) * MOE_BLOCK + row % MOE_BLOCK
    row_dst = jnp.where(row_a >= 0, (row_a % TOP_K) * n_tokens + row_a // TOP_K, trash).reshape(n_blk, 1, MOE_BLOCK)
    row_dst_prev = jnp.concatenate(
        [(A + MOE_BLOCK + jnp.arange(MOE_BLOCK, dtype=jnp.int32)).reshape(1, 1, MOE_BLOCK), row_dst[:-1]], axis=0)

    clamp = lambda i, nr: jnp.minimum(i, nr[0] - 1)
    idx_spec = lambda shift: pl.BlockSpec((1, 1, MOE_BLOCK), lambda i, be, nr: (clamp(i + shift, nr), 0, 0),
                                          memory_space=pltpu.SMEM)
    w_spec = lambda k, n: pl.BlockSpec((None, k, n), lambda i, be, nr: (be[clamp(i, nr)], 0, 0))
    hid = pl.pallas_call(
        _moe_up_body,
        grid_spec=pltpu.PrefetchScalarGridSpec(
            num_scalar_prefetch=2, grid=(n_blk,),
            in_specs=[idx_spec(0), idx_spec(1), pl.BlockSpec(memory_space=pl.ANY),
                      w_spec(D_MODEL, D_FF_EXPERT), w_spec(D_MODEL, D_FF_EXPERT)],
            out_specs=pl.BlockSpec((MOE_BLOCK, D_FF_EXPERT), lambda i, be, nr: (i, 0)),
            scratch_shapes=[pltpu.VMEM((2, MOE_BLOCK, D_MODEL), F32), pltpu.VMEM((MOE_BLOCK, D_MODEL), BF16),
                            pltpu.VMEM((D_MODEL, D_FF_EXPERT), BF16), pltpu.VMEM((D_MODEL, D_FF_EXPERT), BF16),
                            pltpu.SemaphoreType.DMA((2,))]),
        out_shape=jax.ShapeDtypeStruct((R, D_FF_EXPERT), BF16),
        compiler_params=_params(("arbitrary",)), name="moe_up",
    )(blk_e, n_real, row_tok, row_tok, hn, w_gate, w_up)
    return pl.pallas_call(
        functools.partial(_moe_down_body, n_assign=A),
        grid_spec=pltpu.PrefetchScalarGridSpec(
            num_scalar_prefetch=2, grid=(n_blk,),
            in_specs=[idx_spec(0), idx_spec(0),
                      pl.BlockSpec((MOE_BLOCK, D_FF_EXPERT), lambda i, be, nr: (clamp(i, nr), 0)),
                      w_spec(D_FF_EXPERT, D_MODEL)],
            out_specs=pl.BlockSpec(memory_space=pl.ANY),
            scratch_shapes=[pltpu.VMEM((2, MOE_BLOCK, D_MODEL), F32),
                            pltpu.VMEM((D_FF_EXPERT, D_MODEL), BF16),
                            pltpu.SemaphoreType.DMA((2,))]),
        out_shape=jax.ShapeDtypeStruct((out_rows, D_MODEL), F32),
        compiler_params=_params(("arbitrary",)), name="moe_down",
    )(blk_e, n_real, row_dst, row_dst_prev, hid, w_down)


FINAL_TM = 512


def _final_body(h_ref, y0_ref, y1_ref, gate_ref, g_ref, o_ref):
    gate = gate_ref[...]
    h = h_ref[...] + gate[:, 0:1] * y0_ref[...] + gate[:, 1:2] * y1_ref[...]
    o_ref[...] = _rms(h, g_ref[...])


def _final(h, ys, gate, gain):
    rows = h.shape[0]
    n_tiles = rows // FINAL_TM
    tile = lambda width: pl.BlockSpec((FINAL_TM, width), lambda i: (i, 0))
    g2 = gain.reshape(1, D_MODEL).astype(F32)
    return pl.pallas_call(
        _final_body, grid=(n_tiles,),
        in_specs=[tile(D_MODEL), tile(D_MODEL), pl.BlockSpec((FINAL_TM, D_MODEL), lambda i: (i + n_tiles, 0)),
                  tile(LANES), pl.BlockSpec((1, D_MODEL), lambda i: (0, 0))],
        out_specs=tile(D_MODEL), out_shape=jax.ShapeDtypeStruct((rows, D_MODEL), F32),
        compiler_params=_params(("parallel",)), name="combine_final_norm")(h, ys, ys, gate, g2)


def kernel(x, mem, norm_mix, w_in, attn_out_norm, sink_logit, w_gla_gf, b_gla_gf, w_gla_gb, b_gla_gb, gla_out_norm, w_out, norm_cross, norm_mem, w_cq, w_ck, w_cv, w_co, norm_ffn, w_router_group, b_router_group, w_router_expert, b_router_expert, w_gate, w_up, w_down, norm_final):
    batch, seq, _ = x.shape
    mem_len = mem.shape[1]
    n_tokens = batch * seq
    h = x.reshape(n_tokens, D_MODEL)
    memf = mem.reshape(batch * mem_len, D_MODEL)
    assert norm_mix.shape[0] == 1, "the combine step is fused with the final norm: single-layer stacks only"
    for l in range(norm_mix.shape[0]):
        w_lr = jnp.zeros((D_MODEL, LANES), F32).at[:, :2 * GLA_LOWRANK].set(w_in[l][:, MAIN_COLS:]).astype(BF16)
        proj, lr = _dense([h], w_in[l], n_cols=MAIN_COLS, out_dtype=BF16, gain=norm_mix[l], extra_w=w_lr,
                          name="in_proj")
        o_a = _window_attention(proj, sink_logit[l], attn_out_norm[l], batch, seq).reshape(n_tokens, ATT_Q)
        o_g = _gla(proj, lr, w_gla_gf[l], b_gla_gf[l], w_gla_gb[l], b_gla_gb[l], gla_out_norm[l],
                   batch, seq).reshape(n_tokens, GLA_V)
        h = _dense([o_a, o_g], w_out[l], n_cols=D_MODEL, out_dtype=F32, res=h, name="out_proj")
        kx = _dense([memf], w_ck[l], n_cols=D_MODEL, out_dtype=BF16, gain=norm_mem[l], name="mem_k_proj")
        vx = _dense([memf], w_cv[l], n_cols=D_MODEL, out_dtype=BF16, gain=norm_mem[l], name="mem_v_proj")
        q = _dense([h], w_cq[l], n_cols=D_MODEL, out_dtype=BF16, gain=norm_cross[l], name="cross_q_proj")
        o = _cross_attention(q, kx, vx, batch, seq, mem_len)
        h = _dense([o], w_co[l], n_cols=D_MODEL, out_dtype=F32, res=h, name="cross_out_proj")
        hn, idx, gate = _router(h, norm_ffn[l], w_router_group[l], b_router_group[l],
                                w_router_expert[l], b_router_expert[l])
        ys = _moe_forward(hn, idx, w_gate[l], w_up[l], w_down[l], n_tokens)
    return _final(h, ys, gate, norm_final).reshape(batch, seq, D_MODEL)
```

```python
import functools

import jax
import jax.numpy as jnp
from jax import lax
from jax.experimental import pallas as pl
from jax.experimental.pallas import tpu as pltpu

F32 = jnp.float32
BF16 = jnp.bfloat16

D_MODEL = 2048
N_Q_HEADS = 8
N_KV_HEADS = 2
Q_PER_KV = N_Q_HEADS // N_KV_HEADS
HEAD_DIM = 128
WINDOW = 128
WBLK = 128
GLA_HEADS = 4
GLA_DK = 128
GLA_DV = 256
GLA_LOWRANK = 16
GLA_TAU = 16.0
GLA_CHUNK = 64
ATT_Q = N_Q_HEADS * HEAD_DIM
ATT_KV = N_KV_HEADS * HEAD_DIM
GLA_QK = GLA_HEADS * GLA_DK
GLA_V = GLA_HEADS * GLA_DV
MAIN_COLS = ATT_Q + 2 * ATT_KV + 2 * GLA_QK + 2 * GLA_V
COL_KA = ATT_Q
COL_VA = COL_KA + ATT_KV
COL_QG = COL_VA + ATT_KV
COL_KG = COL_QG + GLA_QK
COL_VG = COL_KG + GLA_QK
COL_RG = COL_VG + GLA_V
X_HEADS = 4
X_HEAD_DIM = D_MODEL // X_HEADS
N_GROUPS = 4
EXPERTS_PER_GROUP = 8
N_EXPERTS = N_GROUPS * EXPERTS_PER_GROUP
TOP_K = 2
D_FF_EXPERT = D_MODEL // 2
MOE_BLOCK = 256
RMS_EPS = 1e-6
NEG_INF = -1e30

LANES = 128
VMEM_LIMIT = 56 * 1024 * 1024
W_STAGE_ROWS = 128


def _params(sem):
    return pltpu.CompilerParams(dimension_semantics=sem, vmem_limit_bytes=VMEM_LIMIT)


def _nt(a, b):
    return lax.dot_general(a, b, (((1,), (1,)), ((), ())), preferred_element_type=F32)


def _tn(a, b):
    return lax.dot_general(a, b, (((0,), (0,)), ((), ())), preferred_element_type=F32)


def _rms(x, gain):
    return x * lax.rsqrt(jnp.mean(x * x, axis=-1, keepdims=True) + RMS_EPS) * gain


def _load_weight_bf16(w_hbm, wb, stage, sem, k_rows, n_cols):
    n_chunks = k_rows // W_STAGE_ROWS

    def copy(c):
        return pltpu.make_async_copy(
            w_hbm.at[pl.ds(c * W_STAGE_ROWS, W_STAGE_ROWS), pl.ds(0, n_cols)],
            stage.at[c % 2], sem.at[c % 2])

    copy(0).start()
    for c in range(n_chunks):
        if c + 1 < n_chunks:
            copy(c + 1).start()
        copy(c).wait()
        wb[c * W_STAGE_ROWS:(c + 1) * W_STAGE_ROWS, :] = stage[c % 2].astype(BF16)


def _dense_body(*refs, part_widths, has_norm, has_extra, has_res, n_cols, n_chunk):
    it = iter(refs)
    x_refs = [next(it) for _ in part_widths]
    g_ref = next(it) if has_norm else None
    w_hbm = next(it)
    ew_ref = next(it) if has_extra else None
    res_ref = next(it) if has_res else None
    o_ref = next(it)
    eo_ref = next(it) if has_extra else None
    wb, stage, sem = next(it), next(it), next(it)
    u_ref = next(it) if has_norm else None
    k_rows = sum(part_widths)

    @pl.when(pl.program_id(0) == 0)
    def _():
        _load_weight_bf16(w_hbm, wb, stage, sem, k_rows, n_cols)

    if has_norm:
        u_ref[...] = _rms(x_refs[0][...], g_ref[...]).astype(BF16)
        lhs = [(u_ref, 0, k_rows)]
    else:
        lhs, off = [], 0
        for r, kw in zip(x_refs, part_widths):
            lhs.append((r, off, kw))
            off += kw
    for n0 in range(0, n_cols, n_chunk):
        acc = None
        for r, off, kw in lhs:
            d = jnp.dot(r[...], wb[off:off + kw, n0:n0 + n_chunk], preferred_element_type=F32)
            acc = d if acc is None else acc + d
        if has_res:
            acc = acc + res_ref[:, n0:n0 + n_chunk]
        o_ref[:, n0:n0 + n_chunk] = acc.astype(o_ref.dtype)
    if has_extra:
        eo_ref[...] = jnp.dot(u_ref[...], ew_ref[...], preferred_element_type=F32)


def _dense(xs, w, *, n_cols, out_dtype, gain=None, extra_w=None, res=None, tm=512, n_chunk=512, name):
    rows = xs[0].shape[0]
    part_widths = tuple(x.shape[1] for x in xs)
    k_rows = sum(part_widths)
    has_norm, has_extra, has_res = gain is not None, extra_w is not None, res is not None
    row_spec = lambda width: pl.BlockSpec((tm, width), lambda i: (i, 0))
    full_spec = lambda a: pl.BlockSpec(a.shape, lambda i: (0, 0))
    args, in_specs = list(xs), [row_spec(kw) for kw in part_widths]
    if has_norm:
        args.append(gain.reshape(1, k_rows).astype(F32))
        in_specs.append(full_spec(args[-1]))
    args.append(w)
    in_specs.append(pl.BlockSpec(memory_space=pl.ANY))
    if has_extra:
        args.append(extra_w)
        in_specs.append(full_spec(extra_w))
    if has_res:
        args.append(res)
        in_specs.append(row_spec(n_cols))
    out_shape = [jax.ShapeDtypeStruct((rows, n_cols), out_dtype)]
    out_specs = [row_spec(n_cols)]
    if has_extra:
        out_shape.append(jax.ShapeDtypeStruct((rows, extra_w.shape[1]), F32))
        out_specs.append(row_spec(extra_w.shape[1]))
    scratch = [pltpu.VMEM((k_rows, n_cols), BF16),
               pltpu.VMEM((2, W_STAGE_ROWS, n_cols), F32),
               pltpu.SemaphoreType.DMA((2,))]
    if has_norm:
        scratch.append(pltpu.VMEM((tm, k_rows), BF16))
    body = functools.partial(_dense_body, part_widths=part_widths, has_norm=has_norm, has_extra=has_extra,
                             has_res=has_res, n_cols=n_cols, n_chunk=n_chunk)
    outs = pl.pallas_call(
        body, grid=(rows // tm,), in_specs=in_specs, out_specs=out_specs, out_shape=out_shape,
        scratch_shapes=scratch, compiler_params=_params(("arbitrary",)), name=name)(*args)
    return outs if has_extra else outs[0]


ATT_TQ = 512


def _winattn_body(sink_ref, q_ref, kp_ref, km_ref, kn_ref, vp_ref, vm_ref, vn_ref, g_ref, o_ref,
                  kcat, vcat, obuf, *, seq):
    s0 = pl.program_id(1) * ATT_TQ
    kcat[0:WBLK, :] = kp_ref[0]
    kcat[WBLK:WBLK + ATT_TQ, :] = km_ref[0]
    kcat[WBLK + ATT_TQ:, :] = kn_ref[0]
    vcat[0:WBLK, :] = vp_ref[0]
    vcat[WBLK:WBLK + ATT_TQ, :] = vm_ref[0]
    vcat[WBLK + ATT_TQ:, :] = vn_ref[0]
    qi = lax.broadcasted_iota(jnp.int32, (WBLK, 3 * WBLK), 0) + WBLK
    ki = lax.broadcasted_iota(jnp.int32, (WBLK, 3 * WBLK), 1)
    dist_i = jnp.abs(ki - qi)
    neg_dist = jnp.where(dist_i <= WINDOW, -dist_i.astype(F32), NEG_INF)
    scale = HEAD_DIM ** -0.5
    for qb in range(ATT_TQ // WBLK):
        kabs = s0 + (qb - 1) * WBLK + ki
        bias_unit = jnp.where((kabs >= 0) & (kabs < seq), neg_dist, NEG_INF)
        head_cols = lambda j: slice(j * HEAD_DIM, (j + 1) * HEAD_DIM)
        kv_cols = lambda j: head_cols(j // Q_PER_KV)
        key_rows = slice(qb * WBLK, (qb + 3) * WBLK)
        scores = []
        for j in range(N_Q_HEADS):
            slope = 2.0 ** (-8.0 * (j + 1) / N_Q_HEADS)
            q = q_ref[0, qb * WBLK:(qb + 1) * WBLK, head_cols(j)]
            scores.append(_nt(q, kcat[key_rows, kv_cols(j)]) * scale + slope * bias_unit)
        probs, denoms = [], []
        for j in range(N_Q_HEADS):
            sink = sink_ref[j]
            m = jnp.maximum(jnp.max(scores[j], axis=-1, keepdims=True), sink)
            p = jnp.exp(scores[j] - m)
            denoms.append(jnp.sum(p, axis=-1, keepdims=True) + jnp.exp(sink - m))
            probs.append(p.astype(BF16))
        for j in range(N_Q_HEADS):
            pv = jnp.dot(probs[j], vcat[key_rows, kv_cols(j)], preferred_element_type=F32)
            obuf[qb, :, head_cols(j)] = pv / denoms[j]
        o_ref[0, qb * WBLK:(qb + 1) * WBLK, :] = _rms(obuf[qb], g_ref[...]).astype(o_ref.dtype)


def _window_attention(proj, sink_logit, gain, batch, seq):
    p3 = proj.reshape(batch, seq, MAIN_COLS)
    nq = seq // ATT_TQ
    per = ATT_TQ // WBLK
    last = seq // WBLK - 1
    main = lambda col: pl.BlockSpec((1, ATT_TQ, ATT_KV), lambda b, i: (b, i, col))
    prev = lambda col: pl.BlockSpec((1, WBLK, ATT_KV), lambda b, i: (b, jnp.maximum(i * per - 1, 0), col))
    nxt = lambda col: pl.BlockSpec((1, WBLK, ATT_KV), lambda b, i: (b, jnp.minimum(i * per + per, last), col))
    ck, cv = COL_KA // ATT_KV, COL_VA // ATT_KV
    return pl.pallas_call(
        functools.partial(_winattn_body, seq=seq),
        grid=(batch, nq),
        in_specs=[pl.BlockSpec(memory_space=pltpu.SMEM),
                  pl.BlockSpec((1, ATT_TQ, ATT_Q), lambda b, i: (b, i, 0)),
                  prev(ck), main(ck), nxt(ck), prev(cv), main(cv), nxt(cv),
                  pl.BlockSpec((1, ATT_Q), lambda b, i: (0, 0))],
        out_specs=pl.BlockSpec((1, ATT_TQ, ATT_Q), lambda b, i: (b, i, 0)),
        out_shape=jax.ShapeDtypeStruct((batch, seq, ATT_Q), BF16),
        scratch_shapes=[pltpu.VMEM((ATT_TQ + 2 * WBLK, ATT_KV), BF16),
                        pltpu.VMEM((ATT_TQ + 2 * WBLK, ATT_KV), BF16),
                        pltpu.VMEM((ATT_TQ // WBLK, WBLK, ATT_Q), F32)],
        compiler_params=_params(("parallel", "parallel")), name="window_attention",
    )(sink_logit.astype(F32), p3, p3, p3, p3, p3, p3, p3, gain.reshape(1, ATT_Q).astype(F32))


GLA_SCAN_ROWS = 256
GLA_SCAN_UNROLL = 2
GLA_EPI_ROWS = 512
GLA_UNROLL = 4


def _split3(x):
    hi = x.astype(BF16)
    r1 = x - hi.astype(F32)
    mid = r1.astype(BF16)
    lo = (r1 - mid.astype(F32)).astype(BF16)
    return hi, mid, lo


def _gla_body(q_ref, k_ref, v_ref, r_ref, lr_ref, wf_ref, wb_ref, bf_ref, bb_ref, gn_ref, o_ref,
              cum_f, cum_b, acc, st_f, st_b, *, seq):
    C = GLA_CHUNK
    n_chunks = seq // C
    ri = lax.broadcasted_iota(jnp.int32, (GLA_SCAN_ROWS, GLA_SCAN_ROWS), 0)
    ci = lax.broadcasted_iota(jnp.int32, (GLA_SCAN_ROWS, GLA_SCAN_ROWS), 1)
    same = (ri // C) == (ci // C)
    tri_f = jnp.where(same & (ci <= ri), 1.0, 0.0).astype(BF16)
    tri_b = jnp.where(same & (ci >= ri), 1.0, 0.0).astype(BF16)

    def scan_body(t, carry):
        row_sets = [pl.ds(pl.multiple_of((t * GLA_SCAN_UNROLL + u) * GLA_SCAN_ROWS, GLA_SCAN_ROWS), GLA_SCAN_ROWS)
                    for u in range(GLA_SCAN_UNROLL)]
        lrs = [lr_ref[0, rows, :].astype(BF16) for rows in row_sets]
        zs = [jnp.dot(lr, w_ref[...], preferred_element_type=F32) + b_ref[...]
              for lr in lrs for w_ref, b_ref in ((wf_ref, bf_ref), (wb_ref, bb_ref))]
        gs = [_split3((jnp.minimum(z, 0.0) - jnp.log1p(jnp.exp(-jnp.abs(z)))) / GLA_TAU) for z in zs]
        for n, g3 in enumerate(gs):
            tri, dst = ((tri_f, cum_f), (tri_b, cum_b))[n % 2]
            dst[row_sets[n // 2], :] = sum(jnp.dot(tri, part, preferred_element_type=F32) for part in g3)
        return carry

    lax.fori_loop(0, seq // (GLA_SCAN_ROWS * GLA_SCAN_UNROLL), scan_body, 0)

    acc[...] = jnp.zeros_like(acc)
    st_f[...] = jnp.zeros_like(st_f)
    st_b[...] = jnp.zeros_like(st_b)
    rr = lax.broadcasted_iota(jnp.int32, (C, C), 0)
    cc = lax.broadcasted_iota(jnp.int32, (C, C), 1)
    scale = GLA_DK ** -0.5

    def chunk_body(i, carry):
        jobs = []
        for u in range(GLA_UNROLL):
            c = i * GLA_UNROLL + u
            jobs += [(c, cum_f, True), (n_chunks - 1 - c, cum_b, False)]
        prep = []
        for c, cum, forward in jobs:
            rows = pl.ds(pl.multiple_of(c * C, C), C)
            b = cum[rows, :]
            b_end = b[C - 1:C, :] if forward else b[0:1, :]
            q = q_ref[0, rows, :].astype(F32) * scale
            k = k_ref[0, rows, :].astype(F32)
            v = v_ref[0, rows, :]
            q_dec = (q * jnp.exp(b)).astype(BF16)
            k_inc = (k * jnp.exp(-b)).astype(BF16)
            k_dec = (k * jnp.exp(b_end - b)).astype(BF16)
            prep.append((rows, v, q_dec, k_inc, k_dec, jnp.exp(b_end)))
        attn = [_nt(q_dec, k_inc) for _, _, q_dec, k_inc, _, _ in prep]
        kv_t = [_tn(v, k_dec) for _, v, _, _, k_dec, _ in prep]
        state_t = {True: st_f[...], False: st_b[...]}
        o_inter = []
        for (_, _, forward), (_, _, q_dec, _, _, decay), kv in zip(jobs, prep, kv_t):
            o_inter.append(_nt(q_dec, state_t[forward].astype(BF16)))
            state_t[forward] = state_t[forward] * decay + kv
        st_f[...] = state_t[True]
        st_b[...] = state_t[False]
        for (_, _, forward), (rows, v, _, _, _, _), a, oi in zip(jobs, prep, attn, o_inter):
            a = jnp.where((cc <= rr) if forward else (cc >= rr), a, 0.0)
            o = jnp.dot(a.astype(BF16), v, preferred_element_type=F32) + oi
            acc[rows, :] = acc[rows, :] + o
        return carry

    lax.fori_loop(0, n_chunks // GLA_UNROLL, chunk_body, 0)

    def epi_body(t, carry):
        rows = pl.ds(pl.multiple_of(t * GLA_EPI_ROWS, GLA_EPI_ROWS), GLA_EPI_ROWS)
        r = r_ref[0, rows, :].astype(F32)
        o_ref[0, rows, :] = (_rms(acc[rows, :], gn_ref[...]) * (r * jax.nn.sigmoid(r))).astype(o_ref.dtype)
        return carry

    lax.fori_loop(0, seq // GLA_EPI_ROWS, epi_body, 0)


def _gla(proj, lr, w_gf, b_gf, w_gb, b_gb, gain, batch, seq):
    p3 = proj.reshape(batch, seq, MAIN_COLS)
    lr3 = lr.reshape(batch, seq, LANES)
    wf = jnp.zeros((LANES, GLA_QK), F32).at[:GLA_LOWRANK].set(w_gf).astype(BF16)
    wb = jnp.zeros((LANES, GLA_QK), F32).at[GLA_LOWRANK:2 * GLA_LOWRANK].set(w_gb).astype(BF16)
    seq_blk = lambda width, col0: pl.BlockSpec((1, seq, width), lambda b, h: (b, 0, col0 // width + h))
    head_w = pl.BlockSpec((LANES, GLA_DK), lambda b, h: (0, h))
    head_b = pl.BlockSpec((1, GLA_DK), lambda b, h: (0, h))
    return pl.pallas_call(
        functools.partial(_gla_body, seq=seq),
        grid=(batch, GLA_HEADS),
        in_specs=[seq_blk(GLA_DK, COL_QG), seq_blk(GLA_DK, COL_KG), seq_blk(GLA_DV, COL_VG),
                  seq_blk(GLA_DV, COL_RG),
                  pl.BlockSpec((1, seq, LANES), lambda b, h: (b, 0, 0)),
                  head_w, head_w, head_b, head_b,
                  pl.BlockSpec((1, GLA_DV), lambda b, h: (0, 0))],
        out_specs=pl.BlockSpec((1, seq, GLA_DV), lambda b, h: (b, 0, h)),
        out_shape=jax.ShapeDtypeStruct((batch, seq, GLA_V), BF16),
        scratch_shapes=[pltpu.VMEM((seq, GLA_DK), F32), pltpu.VMEM((seq, GLA_DK), F32),
                        pltpu.VMEM((seq, GLA_DV), F32),
                        pltpu.VMEM((GLA_DV, GLA_DK), F32), pltpu.VMEM((GLA_DV, GLA_DK), F32)],
        compiler_params=_params(("parallel", "parallel")), name="gla",
    )(p3, p3, p3, p3, lr3, wf, wb, b_gf.reshape(1, GLA_QK).astype(F32), b_gb.reshape(1, GLA_QK).astype(F32),
      gain.reshape(1, GLA_DV).astype(F32))


XATT_TQ = 512


def _xattn_body(q_ref, k_ref, v_ref, o_ref):
    scale = X_HEAD_DIM ** -0.5
    for h in range(X_HEADS):
        cols = slice(h * X_HEAD_DIM, (h + 1) * X_HEAD_DIM)
        s = _nt(q_ref[0, :, cols], k_ref[0, :, cols]) * scale
        m = jnp.max(s, axis=-1, keepdims=True)
        p = jnp.exp(s - m)
        denom = jnp.sum(p, axis=-1, keepdims=True)
        pv = jnp.dot(p.astype(BF16), v_ref[0, :, cols], preferred_element_type=F32)
        o_ref[0, :, cols] = (pv / denom).astype(o_ref.dtype)


def _cross_attention(q, kx, vx, batch, seq, mem_len):
    q3 = q.reshape(batch, seq, D_MODEL)
    k3 = kx.reshape(batch, mem_len, D_MODEL)
    v3 = vx.reshape(batch, mem_len, D_MODEL)
    mem_spec = pl.BlockSpec((1, mem_len, D_MODEL), lambda b, i: (b, 0, 0))
    tile = pl.BlockSpec((1, XATT_TQ, D_MODEL), lambda b, i: (b, i, 0))
    out = pl.pallas_call(
        _xattn_body, grid=(batch, seq // XATT_TQ), in_specs=[tile, mem_spec, mem_spec], out_specs=tile,
        out_shape=jax.ShapeDtypeStruct((batch, seq, D_MODEL), BF16),
        compiler_params=_params(("parallel", "parallel")), name="cross_attention")(q3, k3, v3)
    return out.reshape(batch * seq, D_MODEL)


ROUTER_TM = 512


def _router_body(h_ref, g_ref, w_ref, b_ref, hn_ref, idx_ref, gate_ref):
    hn = _rms(h_ref[...], g_ref[...])
    hn_ref[...] = hn
    logits = jnp.dot(hn.astype(BF16), w_ref[...], preferred_element_type=F32) + b_ref[...]
    lane = lax.broadcasted_iota(jnp.int32, logits.shape, 1).astype(F32)
    ninf = -jnp.inf
    first = lambda hit: jnp.min(jnp.where(hit, lane, float(LANES)), axis=-1, keepdims=True)
    in_groups = lane < N_GROUPS
    gl = jnp.where(in_groups, logits, ninf)
    gmax = jnp.max(gl, axis=-1, keepdims=True)
    g_idx = first(gl == gmax)
    p_group = 1.0 / jnp.sum(jnp.where(in_groups, jnp.exp(logits - gmax), 0.0), axis=-1, keepdims=True)
    lo = N_GROUPS + EXPERTS_PER_GROUP * g_idx
    el = jnp.where((lane >= lo) & (lane < lo + EXPERTS_PER_GROUP), logits, ninf)
    e1 = jnp.max(el, axis=-1, keepdims=True)
    i1 = first(el == e1)
    el2 = jnp.where(lane == i1, ninf, el)
    e2 = jnp.max(el2, axis=-1, keepdims=True)
    i2 = first(el2 == e2)
    t = jnp.exp(e2 - e1)
    w1 = p_group / (1.0 + t)
    w2 = p_group * t / (1.0 + t)
    idx_ref[...] = jnp.where(lane == 0, i1 - N_GROUPS, jnp.where(lane == 1, i2 - N_GROUPS, 0.0)).astype(jnp.int32)
    gate_ref[...] = jnp.where(lane == 0, w1, jnp.where(lane == 1, w2, 0.0))


def _router(h, gain, w_rg, b_rg, w_re, b_re):
    rows = h.shape[0]
    n_log = N_GROUPS + N_EXPERTS
    w = jnp.zeros((D_MODEL, LANES), F32).at[:, :N_GROUPS].set(w_rg).at[:, N_GROUPS:n_log].set(w_re).astype(BF16)
    b = jnp.zeros((1, LANES), F32).at[0, :N_GROUPS].set(b_rg).at[0, N_GROUPS:n_log].set(b_re)
    tile = lambda width: pl.BlockSpec((ROUTER_TM, width), lambda i: (i, 0))
    full = lambda a: pl.BlockSpec(a.shape, lambda i: (0, 0))
    g2 = gain.reshape(1, D_MODEL).astype(F32)
    return pl.pallas_call(
        _router_body, grid=(rows // ROUTER_TM,),
        in_specs=[tile(D_MODEL), full(g2), full(w), full(b)],
        out_specs=[tile(D_MODEL), tile(LANES), tile(LANES)],
        out_shape=[jax.ShapeDtypeStruct((rows, D_MODEL), F32),
                   jax.ShapeDtypeStruct((rows, LANES), jnp.int32),
                   jax.ShapeDtypeStruct((rows, LANES), F32)],
        compiler_params=_params(("parallel",)), name="moe_router")(h, g2, w, b)


W_CAST_ROWS = 256


def _cast_weight(src_ref, dst_ref):
    def body(c, carry):
        rows = pl.ds(pl.multiple_of(c * W_CAST_ROWS, W_CAST_ROWS), W_CAST_ROWS)
        dst_ref[rows, :] = src_ref[rows, :].astype(BF16)
        return carry
    lax.fori_loop(0, src_ref.shape[0] // W_CAST_ROWS, body, 0)


def _expert_changed(blk_e, blk):
    prev = blk_e[jnp.maximum(blk - 1, 0)]
    return (blk == 0) | (blk_e[blk] != prev)


WEIGHT_DMA_PRIORITY = 1


def _switch_expert(blk_e, nxt_e, blk, w_hbms, stages, dsts, sem):
    def copies(e):
        return [pltpu.make_async_copy(w.at[e], st, sem.at[i]) for i, (w, st) in enumerate(zip(w_hbms, stages))]

    @pl.when(blk == 0)
    def _():
        for cp in copies(blk_e[0]):
            cp.start(priority=WEIGHT_DMA_PRIORITY)

    @pl.when(_expert_changed(blk_e, blk))
    def _():
        for cp, st, dst in zip(copies(blk_e[blk]), stages, dsts):
            cp.wait()
            _cast_weight(st, dst)

        @pl.when(nxt_e[blk] >= 0)
        def _():
            for cp in copies(nxt_e[blk]):
                cp.start(priority=WEIGHT_DMA_PRIORITY)


MOE_CHUNKS = 4


def _moe_up_body(blk_e, n_real, nxt_e, tok_ref, tok_next_ref, hn_hbm, wg_hbm, wu_hbm, hid_ref,
                 xs, xb, wg_stage, wu_stage, wgb, wub, sem, wsem):
    blk = pl.program_id(0)

    def row_copy(idx_ref, slot, r):
        return pltpu.make_async_copy(hn_hbm.at[pl.ds(idx_ref[0, 0, r], 1), :], xs.at[slot, pl.ds(r, 1), :],
                                     sem.at[slot])

    def wait(slot):
        pltpu.make_async_copy(hn_hbm.at[pl.ds(0, MOE_BLOCK), :], xs.at[slot], sem.at[slot]).wait()

    @pl.when(blk == 0)
    def _():
        def body(r, carry):
            row_copy(tok_ref, 0, r).start()
            return carry
        lax.fori_loop(0, MOE_BLOCK, body, 0)

    @pl.when(blk < n_real[0])
    def _():
        slot = blk % 2

        _switch_expert(blk_e, nxt_e, blk, (wg_hbm, wu_hbm), (wg_stage, wu_stage), (wgb, wub), wsem)
        wait(slot)
        xb[...] = xs[slot].astype(BF16)
        cw = D_FF_EXPERT // MOE_CHUNKS
        rows_per_chunk = MOE_BLOCK // MOE_CHUNKS
        for c in range(MOE_CHUNKS):
            cols = slice(c * cw, (c + 1) * cw)
            a = jnp.dot(xb[...], wgb[:, cols], preferred_element_type=F32)
            u = jnp.dot(xb[...], wub[:, cols], preferred_element_type=F32)
            hid_ref[:, cols] = (a * jax.nn.sigmoid(a) * u).astype(hid_ref.dtype)
            for r in range(c * rows_per_chunk, (c + 1) * rows_per_chunk):
                row_copy(tok_next_ref, 1 - slot, r).start()

        @pl.when(blk == n_real[0] - 1)
        def _():
            wait(1 - slot)

    @pl.when(blk >= n_real[0])
    def _():
        hid_ref[...] = jnp.zeros_like(hid_ref)


def _moe_down_body(blk_e, n_real, nxt_e, dst_ref, dst_prev_ref, hid_ref, wd_hbm, out_hbm,
                   ys, wd_stage, wdb, sem, wsem, *, n_assign):
    blk = pl.program_id(0)

    def row_copy(idx_ref, slot, r):
        return pltpu.make_async_copy(ys.at[slot, pl.ds(r, 1), :], out_hbm.at[pl.ds(idx_ref[0, 0, r], 1), :],
                                     sem.at[slot])

    def wait(slot):
        pltpu.make_async_copy(ys.at[slot], out_hbm.at[pl.ds(0, MOE_BLOCK), :], sem.at[slot]).wait()

    @pl.when(blk == 0)
    def _():
        ys[1] = jnp.zeros((MOE_BLOCK, D_MODEL), F32)
        for s in range(2):
            trash = pltpu.make_async_copy(
                ys.at[1], out_hbm.at[pl.ds(n_assign + s * MOE_BLOCK, MOE_BLOCK), :], sem.at[1])
            trash.start()
            trash.wait()

    @pl.when(blk < n_real[0])
    def _():
        slot = blk % 2

        _switch_expert(blk_e, nxt_e, blk, (wd_hbm,), (wd_stage,), (wdb,), wsem)
        @pl.when(blk > 0)
        def _():
            wait(slot)

        cw = D_MODEL // MOE_CHUNKS
        rows_per_chunk = MOE_BLOCK // MOE_CHUNKS
        for c in range(MOE_CHUNKS):
            cols = slice(c * cw, (c + 1) * cw)
            ys[slot, :, cols] = jnp.dot(hid_ref[...], wdb[:, cols], preferred_element_type=F32)
            for r in range(c * rows_per_chunk, (c + 1) * rows_per_chunk):
                row_copy(dst_prev_ref, 1 - slot, r).start(priority=r % 2)

        @pl.when(blk == n_real[0] - 1)
        def _():
            def body(r, carry):
                row_copy(dst_ref, slot, r).start()
                return carry
            lax.fori_loop(0, MOE_BLOCK, body, 0)
            wait(1 - slot)
            wait(slot)


def _moe_forward(hn, idx, w_gate, w_up, w_down, n_tokens):
    A = n_tokens * TOP_K
    out_rows = A + 2 * MOE_BLOCK
    R = A + N_EXPERTS * MOE_BLOCK
    n_blk = R // MOE_BLOCK
    e_flat = idx[:, :TOP_K].reshape(A)
    onehot = (e_flat[:, None] == jnp.arange(N_EXPERTS, dtype=jnp.int32)[None, :]).astype(jnp.int32)
    csum = jnp.cumsum(onehot, axis=0)
    counts = csum[-1]
    padded = ((counts + MOE_BLOCK - 1) // MOE_BLOCK) * MOE_BLOCK
    pend = jnp.cumsum(padded)
    pstart = pend - padded
    dest = jnp.sum(onehot * (csum - 1 + pstart[None, :]), axis=1)
    row_a = jnp.full((R,), -1, jnp.int32).at[dest].set(jnp.arange(A, dtype=jnp.int32), unique_indices=True)
    n_real = (pend[-1] // MOE_BLOCK).astype(jnp.int32).reshape(1)
    blk_start = jnp.arange(n_blk, dtype=jnp.int32) * MOE_BLOCK
    blk_e = jnp.minimum(jnp.sum((pend[None, :] <= blk_start[:, None]).astype(jnp.int32), axis=1), N_EXPERTS - 1)
    row_tok = (jnp.maximum(row_a, 0) // TOP_K).reshape(n_blk, 1, MOE_BLOCK)
    row = jnp.arange(R, dtype=jnp.int32)
    trash = A + ((row // MOE_BLOCK) % 2) * MOE_BLOCK + row % MOE_BLOCK
    row_dst = jnp.where(row_a >= 0, (row_a % TOP_K) * n_tokens + row_a // TOP_K, trash).reshape(n_blk, 1, MOE_BLOCK)
    row_dst_prev = jnp.concatenate(
        [(A + MOE_BLOCK + jnp.arange(MOE_BLOCK, dtype=jnp.int32)).reshape(1, 1, MOE_BLOCK), row_dst[:-1]], axis=0)

    run_end = pend[blk_e] // MOE_BLOCK
    nxt_e = jnp.where(run_end < n_real[0], blk_e[jnp.minimum(run_end, n_blk - 1)], -1).astype(jnp.int32)

    clamp = lambda i, nr: jnp.minimum(i, nr[0] - 1)
    idx_spec = lambda shift: pl.BlockSpec((1, 1, MOE_BLOCK), lambda i, be, nr, nx: (clamp(i + shift, nr), 0, 0),
                                          memory_space=pltpu.SMEM)
    hbm = pl.BlockSpec(memory_space=pl.ANY)
    hid = pl.pallas_call(
        _moe_up_body,
        grid_spec=pltpu.PrefetchScalarGridSpec(
            num_scalar_prefetch=3, grid=(n_blk,),
            in_specs=[idx_spec(0), idx_spec(1), hbm, hbm, hbm],
            out_specs=pl.BlockSpec((MOE_BLOCK, D_FF_EXPERT), lambda i, be, nr, nx: (i, 0)),
            scratch_shapes=[pltpu.VMEM((2, MOE_BLOCK, D_MODEL), F32), pltpu.VMEM((MOE_BLOCK, D_MODEL), BF16),
                            pltpu.VMEM((D_MODEL, D_FF_EXPERT), F32), pltpu.VMEM((D_MODEL, D_FF_EXPERT), F32),
                            pltpu.VMEM((D_MODEL, D_FF_EXPERT), BF16), pltpu.VMEM((D_MODEL, D_FF_EXPERT), BF16),
                            pltpu.SemaphoreType.DMA((2,)), pltpu.SemaphoreType.DMA((2,))]),
        out_shape=jax.ShapeDtypeStruct((R, D_FF_EXPERT), BF16),
        compiler_params=_params(("arbitrary",)), name="moe_up",
    )(blk_e, n_real, nxt_e, row_tok, row_tok, hn, w_gate, w_up)
    return pl.pallas_call(
        functools.partial(_moe_down_body, n_assign=A),
        grid_spec=pltpu.PrefetchScalarGridSpec(
            num_scalar_prefetch=3, grid=(n_blk,),
            in_specs=[idx_spec(0), idx_spec(0),
                      pl.BlockSpec((MOE_BLOCK, D_FF_EXPERT), lambda i, be, nr, nx: (clamp(i, nr), 0)),
                      hbm],
            out_specs=hbm,
            scratch_shapes=[pltpu.VMEM((2, MOE_BLOCK, D_MODEL), F32),
                            pltpu.VMEM((D_FF_EXPERT, D_MODEL), F32), pltpu.VMEM((D_FF_EXPERT, D_MODEL), BF16),
                            pltpu.SemaphoreType.DMA((2,)), pltpu.SemaphoreType.DMA((1,))]),
        out_shape=jax.ShapeDtypeStruct((out_rows, D_MODEL), F32),
        compiler_params=_params(("arbitrary",)), name="moe_down",
    )(blk_e, n_real, nxt_e, row_dst, row_dst_prev, hid, w_down)


FINAL_TM = 512


def _final_body(h_ref, y0_ref, y1_ref, gate_ref, g_ref, o_ref):
    gate = gate_ref[...]
    h = h_ref[...] + gate[:, 0:1] * y0_ref[...] + gate[:, 1:2] * y1_ref[...]
    o_ref[...] = _rms(h, g_ref[...])


def _final(h, ys, gate, gain):
    rows = h.shape[0]
    n_tiles = rows // FINAL_TM
    tile = lambda width: pl.BlockSpec((FINAL_TM, width), lambda i: (i, 0))
    g2 = gain.reshape(1, D_MODEL).astype(F32)
    return pl.pallas_call(
        _final_body, grid=(n_tiles,),
        in_specs=[tile(D_MODEL), tile(D_MODEL), pl.BlockSpec((FINAL_TM, D_MODEL), lambda i: (i + n_tiles, 0)),
                  tile(LANES), pl.BlockSpec((1, D_MODEL), lambda i: (0, 0))],
        out_specs=tile(D_MODEL), out_shape=jax.ShapeDtypeStruct((rows, D_MODEL), F32),
        compiler_params=_params(("parallel",)), name="combine_final_norm")(h, ys, ys, gate, g2)


def kernel(x, mem, norm_mix, w_in, attn_out_norm, sink_logit, w_gla_gf, b_gla_gf, w_gla_gb, b_gla_gb, gla_out_norm, w_out, norm_cross, norm_mem, w_cq, w_ck, w_cv, w_co, norm_ffn, w_router_group, b_router_group, w_router_expert, b_router_expert, w_gate, w_up, w_down, norm_final):
    batch, seq, _ = x.shape
    mem_len = mem.shape[1]
    n_tokens = batch * seq
    h = x.reshape(n_tokens, D_MODEL)
    memf = mem.reshape(batch * mem_len, D_MODEL)
    assert norm_mix.shape[0] == 1, "the combine step is fused with the final norm: single-layer stacks only"
    for l in range(norm_mix.shape[0]):
        w_lr = jnp.zeros((D_MODEL, LANES), F32).at[:, :2 * GLA_LOWRANK].set(w_in[l][:, MAIN_COLS:]).astype(BF16)
        proj, lr = _dense([h], w_in[l], n_cols=MAIN_COLS, out_dtype=BF16, gain=norm_mix[l], extra_w=w_lr,
                          name="in_proj")
        o_a = _window_attention(proj, sink_logit[l], attn_out_norm[l], batch, seq).reshape(n_tokens, ATT_Q)
        o_g = _gla(proj, lr, w_gla_gf[l], b_gla_gf[l], w_gla_gb[l], b_gla_gb[l], gla_out_norm[l],
                   batch, seq).reshape(n_tokens, GLA_V)
        h = _dense([o_a, o_g], w_out[l], n_cols=D_MODEL, out_dtype=F32, res=h, name="out_proj")
        kx = _dense([memf], w_ck[l], n_cols=D_MODEL, out_dtype=BF16, gain=norm_mem[l], name="mem_k_proj")
        vx = _dense([memf], w_cv[l], n_cols=D_MODEL, out_dtype=BF16, gain=norm_mem[l], name="mem_v_proj")
        q = _dense([h], w_cq[l], n_cols=D_MODEL, out_dtype=BF16, gain=norm_cross[l], name="cross_q_proj")
        o = _cross_attention(q, kx, vx, batch, seq, mem_len)
        h = _dense([o], w_co[l], n_cols=D_MODEL, out_dtype=F32, res=h, name="cross_out_proj")
        hn, idx, gate = _router(h, norm_ffn[l], w_router_group[l], b_router_group[l],
                                w_router_expert[l], b_router_expert[l])
        ys = _moe_forward(hn, idx, w_gate[l], w_up[l], w_down[l], n_tokens)
    return _final(h, ys, gate, norm_final).reshape(batch, seq, D_MODEL)
```

```python
import functools

import jax
import jax.numpy as jnp
from jax import lax
from jax.experimental import pallas as pl
from jax.experimental.pallas import tpu as pltpu

F32 = jnp.float32
BF16 = jnp.bfloat16

D_MODEL = 2048
N_Q_HEADS = 8
N_KV_HEADS = 2
Q_PER_KV = N_Q_HEADS // N_KV_HEADS
HEAD_DIM = 128
WINDOW = 128
WBLK = 128
GLA_HEADS = 4
GLA_DK = 128
GLA_DV = 256
GLA_LOWRANK = 16
GLA_TAU = 16.0
GLA_CHUNK = 64
ATT_Q = N_Q_HEADS * HEAD_DIM
ATT_KV = N_KV_HEADS * HEAD_DIM
GLA_QK = GLA_HEADS * GLA_DK
GLA_V = GLA_HEADS * GLA_DV
MAIN_COLS = ATT_Q + 2 * ATT_KV + 2 * GLA_QK + 2 * GLA_V
COL_KA = ATT_Q
COL_VA = COL_KA + ATT_KV
COL_QG = COL_VA + ATT_KV
COL_KG = COL_QG + GLA_QK
COL_VG = COL_KG + GLA_QK
COL_RG = COL_VG + GLA_V
X_HEADS = 4
X_HEAD_DIM = D_MODEL // X_HEADS
N_GROUPS = 4
EXPERTS_PER_GROUP = 8
N_EXPERTS = N_GROUPS * EXPERTS_PER_GROUP
TOP_K = 2
D_FF_EXPERT = D_MODEL // 2
MOE_BLOCK = 256
RMS_EPS = 1e-6
NEG_INF = -1e30

LANES = 128
VMEM_LIMIT = 56 * 1024 * 1024
W_STAGE_ROWS = 128


def _params(sem):
    return pltpu.CompilerParams(dimension_semantics=sem, vmem_limit_bytes=VMEM_LIMIT)


def _nt(a, b):
    return lax.dot_general(a, b, (((1,), (1,)), ((), ())), preferred_element_type=F32)


def _tn(a, b):
    return lax.dot_general(a, b, (((0,), (0,)), ((), ())), preferred_element_type=F32)


def _rms(x, gain):
    return x * lax.rsqrt(jnp.mean(x * x, axis=-1, keepdims=True) + RMS_EPS) * gain


def _load_weight_bf16(w_hbm, layer, wb, stage, sem, k_rows, n_cols):
    n_chunks = k_rows // W_STAGE_ROWS

    def copy(c):
        return pltpu.make_async_copy(
            w_hbm.at[layer, pl.ds(c * W_STAGE_ROWS, W_STAGE_ROWS), pl.ds(0, n_cols)],
            stage.at[c % 2], sem.at[c % 2])

    copy(0).start()
    for c in range(n_chunks):
        if c + 1 < n_chunks:
            copy(c + 1).start()
        copy(c).wait()
        wb[c * W_STAGE_ROWS:(c + 1) * W_STAGE_ROWS, :] = stage[c % 2].astype(BF16)


def _dense_body(*refs, part_widths, has_norm, has_extra, has_res, n_cols, n_chunk, layer):
    it = iter(refs)
    x_refs = [next(it) for _ in part_widths]
    g_ref = next(it) if has_norm else None
    w_hbm = next(it)
    ew_ref = next(it) if has_extra else None
    res_ref = next(it) if has_res else None
    o_ref = next(it)
    eo_ref = next(it) if has_extra else None
    wb, stage, sem = next(it), next(it), next(it)
    u_ref = next(it) if has_norm else None
    k_rows = sum(part_widths)

    @pl.when(pl.program_id(0) == 0)
    def _():
        _load_weight_bf16(w_hbm, layer, wb, stage, sem, k_rows, n_cols)

    if has_norm:
        u_ref[...] = _rms(x_refs[0][...], g_ref[...]).astype(BF16)
        lhs = [(u_ref, 0, k_rows)]
    else:
        lhs, off = [], 0
        for r, kw in zip(x_refs, part_widths):
            lhs.append((r, off, kw))
            off += kw
    for n0 in range(0, n_cols, n_chunk):
        acc = None
        for r, off, kw in lhs:
            d = jnp.dot(r[...], wb[off:off + kw, n0:n0 + n_chunk], preferred_element_type=F32)
            acc = d if acc is None else acc + d
        if has_res:
            acc = acc + res_ref[:, n0:n0 + n_chunk]
        o_ref[:, n0:n0 + n_chunk] = acc.astype(o_ref.dtype)
    if has_extra:
        eo_ref[...] = jnp.dot(u_ref[...], ew_ref[...], preferred_element_type=F32)


def _dense(xs, w, layer, *, n_cols, out_dtype, gain=None, extra_w=None, res=None, tm=512, n_chunk=512, name):
    rows = xs[0].shape[0]
    part_widths = tuple(x.shape[1] for x in xs)
    k_rows = sum(part_widths)
    has_norm, has_extra, has_res = gain is not None, extra_w is not None, res is not None
    row_spec = lambda width: pl.BlockSpec((tm, width), lambda i: (i, 0))
    full_spec = lambda a: pl.BlockSpec(a.shape, lambda i: (0, 0))
    args, in_specs = list(xs), [row_spec(kw) for kw in part_widths]
    if has_norm:
        args.append(gain.reshape(1, k_rows).astype(F32))
        in_specs.append(full_spec(args[-1]))
    args.append(w)
    in_specs.append(pl.BlockSpec(memory_space=pl.ANY))
    if has_extra:
        args.append(extra_w)
        in_specs.append(full_spec(extra_w))
    if has_res:
        args.append(res)
        in_specs.append(row_spec(n_cols))
    out_shape = [jax.ShapeDtypeStruct((rows, n_cols), out_dtype)]
    out_specs = [row_spec(n_cols)]
    if has_extra:
        out_shape.append(jax.ShapeDtypeStruct((rows, extra_w.shape[1]), F32))
        out_specs.append(row_spec(extra_w.shape[1]))
    scratch = [pltpu.VMEM((k_rows, n_cols), BF16),
               pltpu.VMEM((2, W_STAGE_ROWS, n_cols), F32),
               pltpu.SemaphoreType.DMA((2,))]
    if has_norm:
        scratch.append(pltpu.VMEM((tm, k_rows), BF16))
    body = functools.partial(_dense_body, part_widths=part_widths, has_norm=has_norm, has_extra=has_extra,
                             has_res=has_res, n_cols=n_cols, n_chunk=n_chunk, layer=layer)
    outs = pl.pallas_call(
        body, grid=(rows // tm,), in_specs=in_specs, out_specs=out_specs, out_shape=out_shape,
        scratch_shapes=scratch, compiler_params=_params(("arbitrary",)), name=name)(*args)
    return outs if has_extra else outs[0]


ATT_TQ = 512


def _winattn_body(sink_ref, q_ref, kp_ref, km_ref, kn_ref, vp_ref, vm_ref, vn_ref, g_ref, o_ref,
                  kcat, vcat, obuf, *, seq):
    s0 = pl.program_id(1) * ATT_TQ
    kcat[0:WBLK, :] = kp_ref[0]
    kcat[WBLK:WBLK + ATT_TQ, :] = km_ref[0]
    kcat[WBLK + ATT_TQ:, :] = kn_ref[0]
    vcat[0:WBLK, :] = vp_ref[0]
    vcat[WBLK:WBLK + ATT_TQ, :] = vm_ref[0]
    vcat[WBLK + ATT_TQ:, :] = vn_ref[0]
    qi = lax.broadcasted_iota(jnp.int32, (WBLK, 3 * WBLK), 0) + WBLK
    ki = lax.broadcasted_iota(jnp.int32, (WBLK, 3 * WBLK), 1)
    dist_i = jnp.abs(ki - qi)
    neg_dist = jnp.where(dist_i <= WINDOW, -dist_i.astype(F32), NEG_INF)
    scale = HEAD_DIM ** -0.5
    for qb in range(ATT_TQ // WBLK):
        kabs = s0 + (qb - 1) * WBLK + ki
        bias_unit = jnp.where((kabs >= 0) & (kabs < seq), neg_dist, NEG_INF)
        head_cols = lambda j: slice(j * HEAD_DIM, (j + 1) * HEAD_DIM)
        kv_cols = lambda j: head_cols(j // Q_PER_KV)
        key_rows = slice(qb * WBLK, (qb + 3) * WBLK)
        scores = []
        for j in range(N_Q_HEADS):
            slope = 2.0 ** (-8.0 * (j + 1) / N_Q_HEADS)
            q = q_ref[0, qb * WBLK:(qb + 1) * WBLK, head_cols(j)]
            scores.append(_nt(q, kcat[key_rows, kv_cols(j)]) * scale + slope * bias_unit)
        probs, denoms = [], []
        for j in range(N_Q_HEADS):
            sink = sink_ref[j]
            m = jnp.maximum(jnp.max(scores[j], axis=-1, keepdims=True), sink)
            p = jnp.exp(scores[j] - m)
            denoms.append(jnp.sum(p, axis=-1, keepdims=True) + jnp.exp(sink - m))
            probs.append(p.astype(BF16))
        for j in range(N_Q_HEADS):
            pv = jnp.dot(probs[j], vcat[key_rows, kv_cols(j)], preferred_element_type=F32)
            obuf[qb, :, head_cols(j)] = pv / denoms[j]
        o_ref[0, qb * WBLK:(qb + 1) * WBLK, :] = _rms(obuf[qb], g_ref[...]).astype(o_ref.dtype)


def _window_attention(proj, sink_logit, gain, batch, seq):
    p3 = proj.reshape(batch, seq, MAIN_COLS)
    nq = seq // ATT_TQ
    per = ATT_TQ // WBLK
    last = seq // WBLK - 1
    main = lambda col: pl.BlockSpec((1, ATT_TQ, ATT_KV), lambda b, i: (b, i, col))
    prev = lambda col: pl.BlockSpec((1, WBLK, ATT_KV), lambda b, i: (b, jnp.maximum(i * per - 1, 0), col))
    nxt = lambda col: pl.BlockSpec((1, WBLK, ATT_KV), lambda b, i: (b, jnp.minimum(i * per + per, last), col))
    ck, cv = COL_KA // ATT_KV, COL_VA // ATT_KV
    return pl.pallas_call(
        functools.partial(_winattn_body, seq=seq),
        grid=(batch, nq),
        in_specs=[pl.BlockSpec(memory_space=pltpu.SMEM),
                  pl.BlockSpec((1, ATT_TQ, ATT_Q), lambda b, i: (b, i, 0)),
                  prev(ck), main(ck), nxt(ck), prev(cv), main(cv), nxt(cv),
                  pl.BlockSpec((1, ATT_Q), lambda b, i: (0, 0))],
        out_specs=pl.BlockSpec((1, ATT_TQ, ATT_Q), lambda b, i: (b, i, 0)),
        out_shape=jax.ShapeDtypeStruct((batch, seq, ATT_Q), BF16),
        scratch_shapes=[pltpu.VMEM((ATT_TQ + 2 * WBLK, ATT_KV), BF16),
                        pltpu.VMEM((ATT_TQ + 2 * WBLK, ATT_KV), BF16),
                        pltpu.VMEM((ATT_TQ // WBLK, WBLK, ATT_Q), F32)],
        compiler_params=_params(("parallel", "parallel")), name="window_attention",
    )(sink_logit.astype(F32), p3, p3, p3, p3, p3, p3, p3, gain.reshape(1, ATT_Q).astype(F32))


GLA_SCAN_ROWS = 256
GLA_SCAN_UNROLL = 2
GLA_EPI_ROWS = 512
GLA_UNROLL = 4


def _split3(x):
    hi = x.astype(BF16)
    r1 = x - hi.astype(F32)
    mid = r1.astype(BF16)
    lo = (r1 - mid.astype(F32)).astype(BF16)
    return hi, mid, lo


def _gla_body(q_ref, k_ref, v_ref, r_ref, lr_ref, wf_ref, wb_ref, bf_ref, bb_ref, gn_ref, o_ref,
              cum_f, cum_b, acc, st_f, st_b, *, seq):
    C = GLA_CHUNK
    n_chunks = seq // C
    ri = lax.broadcasted_iota(jnp.int32, (GLA_SCAN_ROWS, GLA_SCAN_ROWS), 0)
    ci = lax.broadcasted_iota(jnp.int32, (GLA_SCAN_ROWS, GLA_SCAN_ROWS), 1)
    same = (ri // C) == (ci // C)
    tri_f = jnp.where(same & (ci <= ri), 1.0, 0.0).astype(BF16)
    tri_b = jnp.where(same & (ci >= ri), 1.0, 0.0).astype(BF16)

    def scan_body(t, carry):
        row_sets = [pl.ds(pl.multiple_of((t * GLA_SCAN_UNROLL + u) * GLA_SCAN_ROWS, GLA_SCAN_ROWS), GLA_SCAN_ROWS)
                    for u in range(GLA_SCAN_UNROLL)]
        lrs = [lr_ref[0, rows, :].astype(BF16) for rows in row_sets]
        zs = [jnp.dot(lr, w_ref[...], preferred_element_type=F32) + b_ref[...]
              for lr in lrs for w_ref, b_ref in ((wf_ref, bf_ref), (wb_ref, bb_ref))]
        gs = [_split3((jnp.minimum(z, 0.0) - jnp.log1p(jnp.exp(-jnp.abs(z)))) / GLA_TAU) for z in zs]
        for n, g3 in enumerate(gs):
            tri, dst = ((tri_f, cum_f), (tri_b, cum_b))[n % 2]
            dst[row_sets[n // 2], :] = sum(jnp.dot(tri, part, preferred_element_type=F32) for part in g3)
        return carry

    lax.fori_loop(0, seq // (GLA_SCAN_ROWS * GLA_SCAN_UNROLL), scan_body, 0)

    acc[...] = jnp.zeros_like(acc)
    st_f[...] = jnp.zeros_like(st_f)
    st_b[...] = jnp.zeros_like(st_b)
    rr = lax.broadcasted_iota(jnp.int32, (C, C), 0)
    cc = lax.broadcasted_iota(jnp.int32, (C, C), 1)
    scale = GLA_DK ** -0.5

    def chunk_body(i, carry):
        jobs = []
        for u in range(GLA_UNROLL):
            c = i * GLA_UNROLL + u
            jobs += [(c, cum_f, True), (n_chunks - 1 - c, cum_b, False)]
        prep = []
        for c, cum, forward in jobs:
            rows = pl.ds(pl.multiple_of(c * C, C), C)
            b = cum[rows, :]
            b_end = b[C - 1:C, :] if forward else b[0:1, :]
            q = q_ref[0, rows, :].astype(F32) * scale
            k = k_ref[0, rows, :].astype(F32)
            v = v_ref[0, rows, :]
            q_dec = (q * jnp.exp(b)).astype(BF16)
            k_inc = (k * jnp.exp(-b)).astype(BF16)
            k_dec = (k * jnp.exp(b_end - b)).astype(BF16)
            prep.append((rows, v, q_dec, k_inc, k_dec, jnp.exp(b_end)))
        attn = [_nt(q_dec, k_inc) for _, _, q_dec, k_inc, _, _ in prep]
        kv_t = [_tn(v, k_dec) for _, v, _, _, k_dec, _ in prep]
        state_t = {True: st_f[...], False: st_b[...]}
        o_inter = []
        for (_, _, forward), (_, _, q_dec, _, _, decay), kv in zip(jobs, prep, kv_t):
            o_inter.append(_nt(q_dec, state_t[forward].astype(BF16)))
            state_t[forward] = state_t[forward] * decay + kv
        st_f[...] = state_t[True]
        st_b[...] = state_t[False]
        for (_, _, forward), (rows, v, _, _, _, _), a, oi in zip(jobs, prep, attn, o_inter):
            a = jnp.where((cc <= rr) if forward else (cc >= rr), a, 0.0)
            o = jnp.dot(a.astype(BF16), v, preferred_element_type=F32) + oi
            acc[rows, :] = acc[rows, :] + o
        return carry

    lax.fori_loop(0, n_chunks // GLA_UNROLL, chunk_body, 0)

    def epi_body(t, carry):
        rows = pl.ds(pl.multiple_of(t * GLA_EPI_ROWS, GLA_EPI_ROWS), GLA_EPI_ROWS)
        r = r_ref[0, rows, :].astype(F32)
        o_ref[0, rows, :] = (_rms(acc[rows, :], gn_ref[...]) * (r * jax.nn.sigmoid(r))).astype(o_ref.dtype)
        return carry

    lax.fori_loop(0, seq // GLA_EPI_ROWS, epi_body, 0)


def _gla(proj, lr, w_gf, b_gf, w_gb, b_gb, gain, batch, seq):
    p3 = proj.reshape(batch, seq, MAIN_COLS)
    lr3 = lr.reshape(batch, seq, LANES)
    wf = jnp.zeros((LANES, GLA_QK), F32).at[:GLA_LOWRANK].set(w_gf).astype(BF16)
    wb = jnp.zeros((LANES, GLA_QK), F32).at[GLA_LOWRANK:2 * GLA_LOWRANK].set(w_gb).astype(BF16)
    seq_blk = lambda width, col0: pl.BlockSpec((1, seq, width), lambda b, h: (b, 0, col0 // width + h))
    head_w = pl.BlockSpec((LANES, GLA_DK), lambda b, h: (0, h))
    head_b = pl.BlockSpec((1, GLA_DK), lambda b, h: (0, h))
    return pl.pallas_call(
        functools.partial(_gla_body, seq=seq),
        grid=(batch, GLA_HEADS),
        in_specs=[seq_blk(GLA_DK, COL_QG), seq_blk(GLA_DK, COL_KG), seq_blk(GLA_DV, COL_VG),
                  seq_blk(GLA_DV, COL_RG),
                  pl.BlockSpec((1, seq, LANES), lambda b, h: (b, 0, 0)),
                  head_w, head_w, head_b, head_b,
                  pl.BlockSpec((1, GLA_DV), lambda b, h: (0, 0))],
        out_specs=pl.BlockSpec((1, seq, GLA_DV), lambda b, h: (b, 0, h)),
        out_shape=jax.ShapeDtypeStruct((batch, seq, GLA_V), BF16),
        scratch_shapes=[pltpu.VMEM((seq, GLA_DK), F32), pltpu.VMEM((seq, GLA_DK), F32),
                        pltpu.VMEM((seq, GLA_DV), F32),
                        pltpu.VMEM((GLA_DV, GLA_DK), F32), pltpu.VMEM((GLA_DV, GLA_DK), F32)],
        compiler_params=_params(("parallel", "parallel")), name="gla",
    )(p3, p3, p3, p3, lr3, wf, wb, b_gf.reshape(1, GLA_QK).astype(F32), b_gb.reshape(1, GLA_QK).astype(F32),
      gain.reshape(1, GLA_DV).astype(F32))


XATT_TQ = 512


def _xattn_body(q_ref, k_ref, v_ref, o_ref):
    scale = X_HEAD_DIM ** -0.5
    for h in range(X_HEADS):
        cols = slice(h * X_HEAD_DIM, (h + 1) * X_HEAD_DIM)
        s = _nt(q_ref[0, :, cols], k_ref[0, :, cols]) * scale
        m = jnp.max(s, axis=-1, keepdims=True)
        p = jnp.exp(s - m)
        denom = jnp.sum(p, axis=-1, keepdims=True)
        pv = jnp.dot(p.astype(BF16), v_ref[0, :, cols], preferred_element_type=F32)
        o_ref[0, :, cols] = (pv / denom).astype(o_ref.dtype)


def _cross_attention(q, kx, vx, batch, seq, mem_len):
    q3 = q.reshape(batch, seq, D_MODEL)
    k3 = kx.reshape(batch, mem_len, D_MODEL)
    v3 = vx.reshape(batch, mem_len, D_MODEL)
    mem_spec = pl.BlockSpec((1, mem_len, D_MODEL), lambda b, i: (b, 0, 0))
    tile = pl.BlockSpec((1, XATT_TQ, D_MODEL), lambda b, i: (b, i, 0))
    out = pl.pallas_call(
        _xattn_body, grid=(batch, seq // XATT_TQ), in_specs=[tile, mem_spec, mem_spec], out_specs=tile,
        out_shape=jax.ShapeDtypeStruct((batch, seq, D_MODEL), BF16),
        compiler_params=_params(("parallel", "parallel")), name="cross_attention")(q3, k3, v3)
    return out.reshape(batch * seq, D_MODEL)


ROUTER_TM = 512


HALF = D_MODEL // 2


def _pack_halves(x):
    return pltpu.pack_elementwise([x[:, :HALF], x[:, HALF:]], packed_dtype=BF16)


def _unpack_halves(words):
    return [pltpu.unpack_elementwise(words, index=i, packed_dtype=BF16, unpacked_dtype=F32) for i in range(2)]


def _router_body(h_ref, g_ref, w_ref, b_ref, hn_ref, idx_ref, gate_ref):
    hn = _rms(h_ref[...], g_ref[...])
    hn_ref[...] = _pack_halves(hn)
    logits = jnp.dot(hn.astype(BF16), w_ref[...], preferred_element_type=F32) + b_ref[...]
    lane = lax.broadcasted_iota(jnp.int32, logits.shape, 1).astype(F32)
    ninf = -jnp.inf
    first = lambda hit: jnp.min(jnp.where(hit, lane, float(LANES)), axis=-1, keepdims=True)
    in_groups = lane < N_GROUPS
    gl = jnp.where(in_groups, logits, ninf)
    gmax = jnp.max(gl, axis=-1, keepdims=True)
    g_idx = first(gl == gmax)
    p_group = 1.0 / jnp.sum(jnp.where(in_groups, jnp.exp(logits - gmax), 0.0), axis=-1, keepdims=True)
    lo = N_GROUPS + EXPERTS_PER_GROUP * g_idx
    el = jnp.where((lane >= lo) & (lane < lo + EXPERTS_PER_GROUP), logits, ninf)
    e1 = jnp.max(el, axis=-1, keepdims=True)
    i1 = first(el == e1)
    el2 = jnp.where(lane == i1, ninf, el)
    e2 = jnp.max(el2, axis=-1, keepdims=True)
    i2 = first(el2 == e2)
    t = jnp.exp(e2 - e1)
    w1 = p_group / (1.0 + t)
    w2 = p_group * t / (1.0 + t)
    idx_ref[...] = jnp.where(lane == 0, i1 - N_GROUPS, jnp.where(lane == 1, i2 - N_GROUPS, 0.0)).astype(jnp.int32)
    gate_ref[...] = jnp.where(lane == 0, w1, jnp.where(lane == 1, w2, 0.0))


def _router(h, gain, w_rg, b_rg, w_re, b_re):
    rows = h.shape[0]
    n_log = N_GROUPS + N_EXPERTS
    w = jnp.zeros((D_MODEL, LANES), F32).at[:, :N_GROUPS].set(w_rg).at[:, N_GROUPS:n_log].set(w_re).astype(BF16)
    b = jnp.zeros((1, LANES), F32).at[0, :N_GROUPS].set(b_rg).at[0, N_GROUPS:n_log].set(b_re)
    tile = lambda width: pl.BlockSpec((ROUTER_TM, width), lambda i: (i, 0))
    full = lambda a: pl.BlockSpec(a.shape, lambda i: (0, 0))
    g2 = gain.reshape(1, D_MODEL).astype(F32)
    return pl.pallas_call(
        _router_body, grid=(rows // ROUTER_TM,),
        in_specs=[tile(D_MODEL), full(g2), full(w), full(b)],
        out_specs=[tile(HALF), tile(LANES), tile(LANES)],
        out_shape=[jax.ShapeDtypeStruct((rows, HALF), jnp.uint32),
                   jax.ShapeDtypeStruct((rows, LANES), jnp.int32),
                   jax.ShapeDtypeStruct((rows, LANES), F32)],
        compiler_params=_params(("parallel",)), name="moe_router")(h, g2, w, b)


W_CAST_ROWS = 256


def _cast_weight(src_ref, dst_ref):
    def body(c, carry):
        rows = pl.ds(pl.multiple_of(c * W_CAST_ROWS, W_CAST_ROWS), W_CAST_ROWS)
        dst_ref[rows, :] = src_ref[rows, :].astype(BF16)
        return carry
    lax.fori_loop(0, src_ref.shape[0] // W_CAST_ROWS, body, 0)


def _expert_changed(blk_e, blk):
    prev = blk_e[jnp.maximum(blk - 1, 0)]
    return (blk == 0) | (blk_e[blk] != prev)


WEIGHT_DMA_PRIORITY = 1


def _switch_expert(blk_e, nxt_e, blk, layer, w_hbms, stages, dsts, sem):
    def copies(e):
        return [pltpu.make_async_copy(w.at[layer, e], st, sem.at[i])
                for i, (w, st) in enumerate(zip(w_hbms, stages))]

    @pl.when(blk == 0)
    def _():
        for cp in copies(blk_e[0]):
            cp.start(priority=WEIGHT_DMA_PRIORITY)

    @pl.when(_expert_changed(blk_e, blk))
    def _():
        for cp, st, dst in zip(copies(blk_e[blk]), stages, dsts):
            cp.wait()
            _cast_weight(st, dst)

        @pl.when(nxt_e[blk] >= 0)
        def _():
            for cp in copies(nxt_e[blk]):
                cp.start(priority=WEIGHT_DMA_PRIORITY)


MOE_CHUNKS = 4


def _moe_up_body(blk_e, n_real, nxt_e, tok_ref, tok_next_ref, hn_hbm, wg_hbm, wu_hbm, hid_ref,
                 xs, xb, wg_stage, wu_stage, wgb, wub, sem, wsem, *, layer):
    blk = pl.program_id(0)

    def row_copy(idx_ref, slot, r):
        return pltpu.make_async_copy(hn_hbm.at[pl.ds(idx_ref[0, 0, r], 1), :], xs.at[slot, pl.ds(r, 1), :],
                                     sem.at[slot])

    def wait(slot):
        pltpu.make_async_copy(hn_hbm.at[pl.ds(0, MOE_BLOCK), :], xs.at[slot], sem.at[slot]).wait()

    @pl.when(blk == 0)
    def _():
        def body(r, carry):
            row_copy(tok_ref, 0, r).start()
            return carry
        lax.fori_loop(0, MOE_BLOCK, body, 0)

    @pl.when(blk < n_real[0])
    def _():
        slot = blk % 2

        _switch_expert(blk_e, nxt_e, blk, layer, (wg_hbm, wu_hbm), (wg_stage, wu_stage), (wgb, wub), wsem)
        wait(slot)
        lo, hi = _unpack_halves(xs[slot])
        xb[:, :HALF] = lo.astype(BF16)
        xb[:, HALF:] = hi.astype(BF16)
        cw = D_FF_EXPERT // MOE_CHUNKS
        rows_per_chunk = MOE_BLOCK // MOE_CHUNKS
        for c in range(MOE_CHUNKS):
            cols = slice(c * cw, (c + 1) * cw)
            a = jnp.dot(xb[...], wgb[:, cols], preferred_element_type=F32)
            u = jnp.dot(xb[...], wub[:, cols], preferred_element_type=F32)
            hid_ref[:, cols] = (a * jax.nn.sigmoid(a) * u).astype(hid_ref.dtype)
            for r in range(c * rows_per_chunk, (c + 1) * rows_per_chunk):
                row_copy(tok_next_ref, 1 - slot, r).start()

        @pl.when(blk == n_real[0] - 1)
        def _():
            wait(1 - slot)

    @pl.when(blk >= n_real[0])
    def _():
        hid_ref[...] = jnp.zeros_like(hid_ref)


def _moe_down_body(blk_e, n_real, nxt_e, dst_ref, dst_prev_ref, hid_ref, wd_hbm, out_hbm,
                   ys, wd_stage, wdb, sem, wsem, *, n_assign, layer):
    blk = pl.program_id(0)

    def row_copy(idx_ref, slot, r):
        return pltpu.make_async_copy(ys.at[slot, pl.ds(r, 1), :], out_hbm.at[pl.ds(idx_ref[0, 0, r], 1), :],
                                     sem.at[slot])

    def wait(slot):
        pltpu.make_async_copy(ys.at[slot], out_hbm.at[pl.ds(0, MOE_BLOCK), :], sem.at[slot]).wait()

    @pl.when(blk == 0)
    def _():
        ys[1] = jnp.zeros((MOE_BLOCK, HALF), jnp.uint32)
        for s in range(2):
            trash = pltpu.make_async_copy(
                ys.at[1], out_hbm.at[pl.ds(n_assign + s * MOE_BLOCK, MOE_BLOCK), :], sem.at[1])
            trash.start()
            trash.wait()

    @pl.when(blk < n_real[0])
    def _():
        slot = blk % 2

        _switch_expert(blk_e, nxt_e, blk, layer, (wd_hbm,), (wd_stage,), (wdb,), wsem)
        @pl.when(blk > 0)
        def _():
            wait(slot)

        cw = HALF // MOE_CHUNKS
        rows_per_chunk = MOE_BLOCK // MOE_CHUNKS
        for c in range(MOE_CHUNKS):
            cols = slice(c * cw, (c + 1) * cw)
            hi_cols = slice(HALF + c * cw, HALF + (c + 1) * cw)
            ys[slot, :, cols] = pltpu.pack_elementwise(
                [jnp.dot(hid_ref[...], wdb[:, cols], preferred_element_type=F32),
                 jnp.dot(hid_ref[...], wdb[:, hi_cols], preferred_element_type=F32)], packed_dtype=BF16)
            for r in range(c * rows_per_chunk, (c + 1) * rows_per_chunk):
                row_copy(dst_prev_ref, 1 - slot, r).start(priority=r % 2)

        @pl.when(blk == n_real[0] - 1)
        def _():
            def body(r, carry):
                row_copy(dst_ref, slot, r).start()
                return carry
            lax.fori_loop(0, MOE_BLOCK, body, 0)
            wait(1 - slot)
            wait(slot)


def _moe_forward(hn, idx, w_gate, w_up, w_down, layer, n_tokens):
    A = n_tokens * TOP_K
    out_rows = A + 2 * MOE_BLOCK
    R = A + N_EXPERTS * MOE_BLOCK
    n_blk = R // MOE_BLOCK
    e_flat = idx[:, :TOP_K].reshape(A)
    onehot = (e_flat[:, None] == jnp.arange(N_EXPERTS, dtype=jnp.int32)[None, :]).astype(jnp.int32)
    csum = jnp.cumsum(onehot, axis=0)
    counts = csum[-1]
    padded = ((counts + MOE_BLOCK - 1) // MOE_BLOCK) * MOE_BLOCK
    pend = jnp.cumsum(padded)
    pstart = pend - padded
    dest = jnp.sum(onehot * (csum - 1 + pstart[None, :]), axis=1)
    row_a = jnp.full((R,), -1, jnp.int32).at[dest].set(jnp.arange(A, dtype=jnp.int32), unique_indices=True)
    n_real = (pend[-1] // MOE_BLOCK).astype(jnp.int32).reshape(1)
    blk_start = jnp.arange(n_blk, dtype=jnp.int32) * MOE_BLOCK
    blk_e = jnp.minimum(jnp.sum((pend[None, :] <= blk_start[:, None]).astype(jnp.int32), axis=1), N_EXPERTS - 1)
    row_tok = (jnp.maximum(row_a, 0) // TOP_K).reshape(n_blk, 1, MOE_BLOCK)
    row = jnp.arange(R, dtype=jnp.int32)
    trash = A + ((row // MOE_BLOCK) % 2) * MOE_BLOCK + row % MOE_BLOCK
    row_dst = jnp.where(row_a >= 0, (row_a % TOP_K) * n_tokens + row_a // TOP_K, trash).reshape(n_blk, 1, MOE_BLOCK)
    row_dst_prev = jnp.concatenate(
        [(A + MOE_BLOCK + jnp.arange(MOE_BLOCK, dtype=jnp.int32)).reshape(1, 1, MOE_BLOCK), row_dst[:-1]], axis=0)

    run_end = pend[blk_e] // MOE_BLOCK
    nxt_e = jnp.where(run_end < n_real[0], blk_e[jnp.minimum(run_end, n_blk - 1)], -1).astype(jnp.int32)

    clamp = lambda i, nr: jnp.minimum(i, nr[0] - 1)
    idx_spec = lambda shift: pl.BlockSpec((1, 1, MOE_BLOCK), lambda i, be, nr, nx: (clamp(i + shift, nr), 0, 0),
                                          memory_space=pltpu.SMEM)
    hbm = pl.BlockSpec(memory_space=pl.ANY)
    hid = pl.pallas_call(
        functools.partial(_moe_up_body, layer=layer),
        grid_spec=pltpu.PrefetchScalarGridSpec(
            num_scalar_prefetch=3, grid=(n_blk,),
            in_specs=[idx_spec(0), idx_spec(1), hbm, hbm, hbm],
            out_specs=pl.BlockSpec((MOE_BLOCK, D_FF_EXPERT), lambda i, be, nr, nx: (i, 0)),
            scratch_shapes=[pltpu.VMEM((2, MOE_BLOCK, HALF), jnp.uint32), pltpu.VMEM((MOE_BLOCK, D_MODEL), BF16),
                            pltpu.VMEM((D_MODEL, D_FF_EXPERT), F32), pltpu.VMEM((D_MODEL, D_FF_EXPERT), F32),
                            pltpu.VMEM((D_MODEL, D_FF_EXPERT), BF16), pltpu.VMEM((D_MODEL, D_FF_EXPERT), BF16),
                            pltpu.SemaphoreType.DMA((2,)), pltpu.SemaphoreType.DMA((2,))]),
        out_shape=jax.ShapeDtypeStruct((R, D_FF_EXPERT), BF16),
        compiler_params=_params(("arbitrary",)), name="moe_up",
    )(blk_e, n_real, nxt_e, row_tok, row_tok, hn, w_gate, w_up)
    return pl.pallas_call(
        functools.partial(_moe_down_body, n_assign=A, layer=layer),
        grid_spec=pltpu.PrefetchScalarGridSpec(
            num_scalar_prefetch=3, grid=(n_blk,),
            in_specs=[idx_spec(0), idx_spec(0),
                      pl.BlockSpec((MOE_BLOCK, D_FF_EXPERT), lambda i, be, nr, nx: (clamp(i, nr), 0)),
                      hbm],
            out_specs=hbm,
            scratch_shapes=[pltpu.VMEM((2, MOE_BLOCK, HALF), jnp.uint32),
                            pltpu.VMEM((D_FF_EXPERT, D_MODEL), F32), pltpu.VMEM((D_FF_EXPERT, D_MODEL), BF16),
                            pltpu.SemaphoreType.DMA((2,)), pltpu.SemaphoreType.DMA((1,))]),
        out_shape=jax.ShapeDtypeStruct((out_rows, HALF), jnp.uint32),
        compiler_params=_params(("arbitrary",)), name="moe_down",
    )(blk_e, n_real, nxt_e, row_dst, row_dst_prev, hid, w_down)


FINAL_TM = 512


def _final_body(h_ref, y0_ref, y1_ref, gate_ref, g_ref, o_ref):
    gate = gate_ref[...]
    y0, y1 = _unpack_halves(y0_ref[...]), _unpack_halves(y1_ref[...])
    halves = [h_ref[:, i * HALF:(i + 1) * HALF] + gate[:, 0:1] * y0[i] + gate[:, 1:2] * y1[i] for i in range(2)]
    mean_sq = sum(jnp.sum(hh * hh, axis=-1, keepdims=True) for hh in halves) / D_MODEL
    inv = lax.rsqrt(mean_sq + RMS_EPS)
    for i, hh in enumerate(halves):
        o_ref[:, i * HALF:(i + 1) * HALF] = hh * inv * g_ref[:, i * HALF:(i + 1) * HALF]


def _final(h, ys, gate, gain):
    rows = h.shape[0]
    n_tiles = rows // FINAL_TM
    tile = lambda width: pl.BlockSpec((FINAL_TM, width), lambda i: (i, 0))
    g2 = gain.reshape(1, D_MODEL).astype(F32)
    return pl.pallas_call(
        _final_body, grid=(n_tiles,),
        in_specs=[tile(D_MODEL), tile(HALF), pl.BlockSpec((FINAL_TM, HALF), lambda i: (i + n_tiles, 0)),
                  tile(LANES), pl.BlockSpec((1, D_MODEL), lambda i: (0, 0))],
        out_specs=tile(D_MODEL), out_shape=jax.ShapeDtypeStruct((rows, D_MODEL), F32),
        compiler_params=_params(("parallel",)), name="combine_final_norm")(h, ys, ys, gate, g2)


def kernel(x, mem, norm_mix, w_in, attn_out_norm, sink_logit, w_gla_gf, b_gla_gf, w_gla_gb, b_gla_gb, gla_out_norm, w_out, norm_cross, norm_mem, w_cq, w_ck, w_cv, w_co, norm_ffn, w_router_group, b_router_group, w_router_expert, b_router_expert, w_gate, w_up, w_down, norm_final):
    batch, seq, _ = x.shape
    mem_len = mem.shape[1]
    n_tokens = batch * seq
    h = x.reshape(n_tokens, D_MODEL)
    memf = mem.reshape(batch * mem_len, D_MODEL)
    assert norm_mix.shape[0] == 1, "the combine step is fused with the final norm: single-layer stacks only"
    for l in range(norm_mix.shape[0]):
        w_lr = jnp.zeros((D_MODEL, LANES), F32).at[:, :2 * GLA_LOWRANK].set(w_in[l][:, MAIN_COLS:]).astype(BF16)
        proj, lr = _dense([h], w_in, l, n_cols=MAIN_COLS, out_dtype=BF16, gain=norm_mix[l], extra_w=w_lr,
                          name="in_proj")
        o_a = _window_attention(proj, sink_logit[l], attn_out_norm[l], batch, seq).reshape(n_tokens, ATT_Q)
        o_g = _gla(proj, lr, w_gla_gf[l], b_gla_gf[l], w_gla_gb[l], b_gla_gb[l], gla_out_norm[l],
                   batch, seq).reshape(n_tokens, GLA_V)
        h = _dense([o_a, o_g], w_out, l, n_cols=D_MODEL, out_dtype=F32, res=h, name="out_proj")
        kx = _dense([memf], w_ck, l, n_cols=D_MODEL, out_dtype=BF16, gain=norm_mem[l], name="mem_k_proj")
        vx = _dense([memf], w_cv, l, n_cols=D_MODEL, out_dtype=BF16, gain=norm_mem[l], name="mem_v_proj")
        q = _dense([h], w_cq, l, n_cols=D_MODEL, out_dtype=BF16, gain=norm_cross[l], name="cross_q_proj")
        o = _cross_attention(q, kx, vx, batch, seq, mem_len)
        h = _dense([o], w_co, l, n_cols=D_MODEL, out_dtype=F32, res=h, name="cross_out_proj")
        hn, idx, gate = _router(h, norm_ffn[l], w_router_group[l], b_router_group[l],
                                w_router_expert[l], b_router_expert[l])
        ys = _moe_forward(hn, idx, w_gate, w_up, w_down, l, n_tokens)
    return _final(h, ys, gate, norm_final).reshape(batch, seq, D_MODEL)
```

```python
import functools

import jax
import jax.numpy as jnp
from jax import lax
from jax.experimental import pallas as pl
from jax.experimental.pallas import tpu as pltpu

F32 = jnp.float32
BF16 = jnp.bfloat16

D_MODEL = 2048
N_Q_HEADS = 8
N_KV_HEADS = 2
Q_PER_KV = N_Q_HEADS // N_KV_HEADS
HEAD_DIM = 128
WINDOW = 128
WBLK = 128
GLA_HEADS = 4
GLA_DK = 128
GLA_DV = 256
GLA_LOWRANK = 16
GLA_TAU = 16.0
GLA_CHUNK = 64
ATT_Q = N_Q_HEADS * HEAD_DIM
ATT_KV = N_KV_HEADS * HEAD_DIM
GLA_QK = GLA_HEADS * GLA_DK
GLA_V = GLA_HEADS * GLA_DV
MAIN_COLS = ATT_Q + 2 * ATT_KV + 2 * GLA_QK + 2 * GLA_V
COL_KA = ATT_Q
COL_VA = COL_KA + ATT_KV
COL_QG = COL_VA + ATT_KV
COL_KG = COL_QG + GLA_QK
COL_VG = COL_KG + GLA_QK
COL_RG = COL_VG + GLA_V
X_HEADS = 4
X_HEAD_DIM = D_MODEL // X_HEADS
N_GROUPS = 4
EXPERTS_PER_GROUP = 8
N_EXPERTS = N_GROUPS * EXPERTS_PER_GROUP
TOP_K = 2
D_FF_EXPERT = D_MODEL // 2
MOE_BLOCK = 256
RMS_EPS = 1e-6
NEG_INF = -1e30

LANES = 128
VMEM_LIMIT = 56 * 1024 * 1024
W_STAGE_ROWS = 128


def _params(sem):
    return pltpu.CompilerParams(dimension_semantics=sem, vmem_limit_bytes=VMEM_LIMIT)


def _nt(a, b):
    return lax.dot_general(a, b, (((1,), (1,)), ((), ())), preferred_element_type=F32)


def _tn(a, b):
    return lax.dot_general(a, b, (((0,), (0,)), ((), ())), preferred_element_type=F32)


def _rms(x, gain):
    return x * lax.rsqrt(jnp.mean(x * x, axis=-1, keepdims=True) + RMS_EPS) * gain


def _load_weight_bf16(w_hbm, layer, wb, stage, sem, k_rows, n_cols):
    n_chunks = k_rows // W_STAGE_ROWS

    def copy(c):
        return pltpu.make_async_copy(
            w_hbm.at[layer, pl.ds(c * W_STAGE_ROWS, W_STAGE_ROWS), pl.ds(0, n_cols)],
            stage.at[c % 2], sem.at[c % 2])

    copy(0).start()
    for c in range(n_chunks):
        if c + 1 < n_chunks:
            copy(c + 1).start()
        copy(c).wait()
        wb[c * W_STAGE_ROWS:(c + 1) * W_STAGE_ROWS, :] = stage[c % 2].astype(BF16)


def _dense_body(*refs, part_widths, has_norm, has_extra, has_res, n_cols, n_chunk, layer):
    it = iter(refs)
    x_refs = [next(it) for _ in part_widths]
    g_ref = next(it) if has_norm else None
    w_hbm = next(it)
    ew_ref = next(it) if has_extra else None
    res_ref = next(it) if has_res else None
    o_ref = next(it)
    eo_ref = next(it) if has_extra else None
    wb, stage, sem = next(it), next(it), next(it)
    u_ref = next(it) if has_norm else None
    k_rows = sum(part_widths)

    @pl.when(pl.program_id(0) == 0)
    def _():
        _load_weight_bf16(w_hbm, layer, wb, stage, sem, k_rows, n_cols)

    if has_norm:
        u_ref[...] = _rms(x_refs[0][...], g_ref[...]).astype(BF16)
        lhs = [(u_ref, 0, k_rows)]
    else:
        lhs, off = [], 0
        for r, kw in zip(x_refs, part_widths):
            lhs.append((r, off, kw))
            off += kw
    for n0 in range(0, n_cols, n_chunk):
        acc = None
        for r, off, kw in lhs:
            d = jnp.dot(r[...], wb[off:off + kw, n0:n0 + n_chunk], preferred_element_type=F32)
            acc = d if acc is None else acc + d
        if has_res:
            acc = acc + res_ref[:, n0:n0 + n_chunk]
        o_ref[:, n0:n0 + n_chunk] = acc.astype(o_ref.dtype)
    if has_extra:
        eo_ref[...] = jnp.dot(u_ref[...], ew_ref[...], preferred_element_type=F32)


def _dense(xs, w, layer, *, n_cols, out_dtype, gain=None, extra_w=None, res=None, tm=512, n_chunk=512, name):
    rows = xs[0].shape[0]
    part_widths = tuple(x.shape[1] for x in xs)
    k_rows = sum(part_widths)
    has_norm, has_extra, has_res = gain is not None, extra_w is not None, res is not None
    row_spec = lambda width: pl.BlockSpec((tm, width), lambda i: (i, 0))
    full_spec = lambda a: pl.BlockSpec(a.shape, lambda i: (0, 0))
    args, in_specs = list(xs), [row_spec(kw) for kw in part_widths]
    if has_norm:
        args.append(gain.reshape(1, k_rows).astype(F32))
        in_specs.append(full_spec(args[-1]))
    args.append(w)
    in_specs.append(pl.BlockSpec(memory_space=pl.ANY))
    if has_extra:
        args.append(extra_w)
        in_specs.append(full_spec(extra_w))
    if has_res:
        args.append(res)
        in_specs.append(row_spec(n_cols))
    out_shape = [jax.ShapeDtypeStruct((rows, n_cols), out_dtype)]
    out_specs = [row_spec(n_cols)]
    if has_extra:
        out_shape.append(jax.ShapeDtypeStruct((rows, extra_w.shape[1]), F32))
        out_specs.append(row_spec(extra_w.shape[1]))
    scratch = [pltpu.VMEM((k_rows, n_cols), BF16),
               pltpu.VMEM((2, W_STAGE_ROWS, n_cols), F32),
               pltpu.SemaphoreType.DMA((2,))]
    if has_norm:
        scratch.append(pltpu.VMEM((tm, k_rows), BF16))
    body = functools.partial(_dense_body, part_widths=part_widths, has_norm=has_norm, has_extra=has_extra,
                             has_res=has_res, n_cols=n_cols, n_chunk=n_chunk, layer=layer)
    outs = pl.pallas_call(
        body, grid=(rows // tm,), in_specs=in_specs, out_specs=out_specs, out_shape=out_shape,
        scratch_shapes=scratch, compiler_params=_params(("arbitrary",)), name=name)(*args)
    return outs if has_extra else outs[0]


ATT_TQ = 512


def _winattn_body(sink_ref, q_ref, kp_ref, km_ref, kn_ref, vp_ref, vm_ref, vn_ref, g_ref, o_ref,
                  kcat, vcat, obuf, *, seq):
    s0 = pl.program_id(1) * ATT_TQ
    kcat[0:WBLK, :] = kp_ref[0]
    kcat[WBLK:WBLK + ATT_TQ, :] = km_ref[0]
    kcat[WBLK + ATT_TQ:, :] = kn_ref[0]
    vcat[0:WBLK, :] = vp_ref[0]
    vcat[WBLK:WBLK + ATT_TQ, :] = vm_ref[0]
    vcat[WBLK + ATT_TQ:, :] = vn_ref[0]
    qi = lax.broadcasted_iota(jnp.int32, (WBLK, 3 * WBLK), 0) + WBLK
    ki = lax.broadcasted_iota(jnp.int32, (WBLK, 3 * WBLK), 1)
    dist_i = jnp.abs(ki - qi)
    neg_dist = jnp.where(dist_i <= WINDOW, -dist_i.astype(F32), NEG_INF)
    scale = HEAD_DIM ** -0.5
    for qb in range(ATT_TQ // WBLK):
        kabs = s0 + (qb - 1) * WBLK + ki
        bias_unit = jnp.where((kabs >= 0) & (kabs < seq), neg_dist, NEG_INF)
        head_cols = lambda j: slice(j * HEAD_DIM, (j + 1) * HEAD_DIM)
        kv_cols = lambda j: head_cols(j // Q_PER_KV)
        key_rows = slice(qb * WBLK, (qb + 3) * WBLK)
        scores = []
        for j in range(N_Q_HEADS):
            slope = 2.0 ** (-8.0 * (j + 1) / N_Q_HEADS)
            q = q_ref[0, qb * WBLK:(qb + 1) * WBLK, head_cols(j)]
            scores.append(_nt(q, kcat[key_rows, kv_cols(j)]) * scale + slope * bias_unit)
        probs, denoms = [], []
        for j in range(N_Q_HEADS):
            sink = sink_ref[j]
            m = jnp.maximum(jnp.max(scores[j], axis=-1, keepdims=True), sink)
            p = jnp.exp(scores[j] - m)
            denoms.append(jnp.sum(p, axis=-1, keepdims=True) + jnp.exp(sink - m))
            probs.append(p.astype(BF16))
        for j in range(N_Q_HEADS):
            pv = jnp.dot(probs[j], vcat[key_rows, kv_cols(j)], preferred_element_type=F32)
            obuf[qb, :, head_cols(j)] = pv / denoms[j]
        o_ref[0, qb * WBLK:(qb + 1) * WBLK, :] = _rms(obuf[qb], g_ref[...]).astype(o_ref.dtype)


def _window_attention(proj, sink_logit, gain, batch, seq):
    p3 = proj.reshape(batch, seq, MAIN_COLS)
    nq = seq // ATT_TQ
    per = ATT_TQ // WBLK
    last = seq // WBLK - 1
    main = lambda col: pl.BlockSpec((1, ATT_TQ, ATT_KV), lambda b, i: (b, i, col))
    prev = lambda col: pl.BlockSpec((1, WBLK, ATT_KV), lambda b, i: (b, jnp.maximum(i * per - 1, 0), col))
    nxt = lambda col: pl.BlockSpec((1, WBLK, ATT_KV), lambda b, i: (b, jnp.minimum(i * per + per, last), col))
    ck, cv = COL_KA // ATT_KV, COL_VA // ATT_KV
    return pl.pallas_call(
        functools.partial(_winattn_body, seq=seq),
        grid=(batch, nq),
        in_specs=[pl.BlockSpec(memory_space=pltpu.SMEM),
                  pl.BlockSpec((1, ATT_TQ, ATT_Q), lambda b, i: (b, i, 0)),
                  prev(ck), main(ck), nxt(ck), prev(cv), main(cv), nxt(cv),
                  pl.BlockSpec((1, ATT_Q), lambda b, i: (0, 0))],
        out_specs=pl.BlockSpec((1, ATT_TQ, ATT_Q), lambda b, i: (b, i, 0)),
        out_shape=jax.ShapeDtypeStruct((batch, seq, ATT_Q), BF16),
        scratch_shapes=[pltpu.VMEM((ATT_TQ + 2 * WBLK, ATT_KV), BF16),
                        pltpu.VMEM((ATT_TQ + 2 * WBLK, ATT_KV), BF16),
                        pltpu.VMEM((ATT_TQ // WBLK, WBLK, ATT_Q), F32)],
        compiler_params=_params(("parallel", "parallel")), name="window_attention",
    )(sink_logit.astype(F32), p3, p3, p3, p3, p3, p3, p3, gain.reshape(1, ATT_Q).astype(F32))


GLA_SCAN_ROWS = 256
GLA_SCAN_UNROLL = 2
GLA_EPI_ROWS = 512
GLA_UNROLL = 4


def _split3(x):
    hi = x.astype(BF16)
    r1 = x - hi.astype(F32)
    mid = r1.astype(BF16)
    lo = (r1 - mid.astype(F32)).astype(BF16)
    return hi, mid, lo


def _gla_body(q_ref, k_ref, v_ref, r_ref, lr_ref, wf_ref, wb_ref, bf_ref, bb_ref, gn_ref, o_ref,
              cum_f, cum_b, acc, st_f, st_b, *, seq):
    C = GLA_CHUNK
    n_chunks = seq // C
    ri = lax.broadcasted_iota(jnp.int32, (GLA_SCAN_ROWS, GLA_SCAN_ROWS), 0)
    ci = lax.broadcasted_iota(jnp.int32, (GLA_SCAN_ROWS, GLA_SCAN_ROWS), 1)
    same = (ri // C) == (ci // C)
    tri_f = jnp.where(same & (ci <= ri), 1.0, 0.0).astype(BF16)
    tri_b = jnp.where(same & (ci >= ri), 1.0, 0.0).astype(BF16)

    def scan_body(t, carry):
        row_sets = [pl.ds(pl.multiple_of((t * GLA_SCAN_UNROLL + u) * GLA_SCAN_ROWS, GLA_SCAN_ROWS), GLA_SCAN_ROWS)
                    for u in range(GLA_SCAN_UNROLL)]
        lrs = [lr_ref[0, rows, :].astype(BF16) for rows in row_sets]
        zs = [jnp.dot(lr, w_ref[...], preferred_element_type=F32) + b_ref[...]
              for lr in lrs for w_ref, b_ref in ((wf_ref, bf_ref), (wb_ref, bb_ref))]
        gs = [_split3((jnp.minimum(z, 0.0) - jnp.log1p(jnp.exp(-jnp.abs(z)))) / GLA_TAU) for z in zs]
        for n, g3 in enumerate(gs):
            tri, dst = ((tri_f, cum_f), (tri_b, cum_b))[n % 2]
            dst[row_sets[n // 2], :] = sum(jnp.dot(tri, part, preferred_element_type=F32) for part in g3)
        return carry

    lax.fori_loop(0, seq // (GLA_SCAN_ROWS * GLA_SCAN_UNROLL), scan_body, 0)

    acc[...] = jnp.zeros_like(acc)
    st_f[...] = jnp.zeros_like(st_f)
    st_b[...] = jnp.zeros_like(st_b)
    rr = lax.broadcasted_iota(jnp.int32, (C, C), 0)
    cc = lax.broadcasted_iota(jnp.int32, (C, C), 1)
    scale = GLA_DK ** -0.5

    def chunk_body(i, carry):
        jobs = []
        for u in range(GLA_UNROLL):
            c = i * GLA_UNROLL + u
            jobs += [(c, cum_f, True), (n_chunks - 1 - c, cum_b, False)]
        prep = []
        for c, cum, forward in jobs:
            rows = pl.ds(pl.multiple_of(c * C, C), C)
            b = cum[rows, :]
            b_end = b[C - 1:C, :] if forward else b[0:1, :]
            q = q_ref[0, rows, :].astype(F32) * scale
            k = k_ref[0, rows, :].astype(F32)
            v = v_ref[0, rows, :]
            q_dec = (q * jnp.exp(b)).astype(BF16)
            k_inc = (k * jnp.exp(-b)).astype(BF16)
            k_dec = (k * jnp.exp(b_end - b)).astype(BF16)
            prep.append((rows, v, q_dec, k_inc, k_dec, jnp.exp(b_end)))
        attn = [_nt(q_dec, k_inc) for _, _, q_dec, k_inc, _, _ in prep]
        kv_t = [_tn(v, k_dec) for _, v, _, _, k_dec, _ in prep]
        state_t = {True: st_f[...], False: st_b[...]}
        o_inter = []
        for (_, _, forward), (_, _, q_dec, _, _, decay), kv in zip(jobs, prep, kv_t):
            o_inter.append(_nt(q_dec, state_t[forward].astype(BF16)))
            state_t[forward] = state_t[forward] * decay + kv
        st_f[...] = state_t[True]
        st_b[...] = state_t[False]
        for (_, _, forward), (rows, v, _, _, _, _), a, oi in zip(jobs, prep, attn, o_inter):
            a = jnp.where((cc <= rr) if forward else (cc >= rr), a, 0.0)
            o = jnp.dot(a.astype(BF16), v, preferred_element_type=F32) + oi
            acc[rows, :] = acc[rows, :] + o
        return carry

    lax.fori_loop(0, n_chunks // GLA_UNROLL, chunk_body, 0)

    def epi_body(t, carry):
        rows = pl.ds(pl.multiple_of(t * GLA_EPI_ROWS, GLA_EPI_ROWS), GLA_EPI_ROWS)
        r = r_ref[0, rows, :].astype(F32)
        o_ref[0, rows, :] = (_rms(acc[rows, :], gn_ref[...]) * (r * jax.nn.sigmoid(r))).astype(o_ref.dtype)
        return carry

    lax.fori_loop(0, seq // GLA_EPI_ROWS, epi_body, 0)


def _gla(proj, lr, w_gf, b_gf, w_gb, b_gb, gain, batch, seq):
    p3 = proj.reshape(batch, seq, MAIN_COLS)
    lr3 = lr.reshape(batch, seq, LANES)
    wf = jnp.zeros((LANES, GLA_QK), F32).at[:GLA_LOWRANK].set(w_gf).astype(BF16)
    wb = jnp.zeros((LANES, GLA_QK), F32).at[GLA_LOWRANK:2 * GLA_LOWRANK].set(w_gb).astype(BF16)
    seq_blk = lambda width, col0: pl.BlockSpec((1, seq, width), lambda b, h: (b, 0, col0 // width + h))
    head_w = pl.BlockSpec((LANES, GLA_DK), lambda b, h: (0, h))
    head_b = pl.BlockSpec((1, GLA_DK), lambda b, h: (0, h))
    return pl.pallas_call(
        functools.partial(_gla_body, seq=seq),
        grid=(batch, GLA_HEADS),
        in_specs=[seq_blk(GLA_DK, COL_QG), seq_blk(GLA_DK, COL_KG), seq_blk(GLA_DV, COL_VG),
                  seq_blk(GLA_DV, COL_RG),
                  pl.BlockSpec((1, seq, LANES), lambda b, h: (b, 0, 0)),
                  head_w, head_w, head_b, head_b,
                  pl.BlockSpec((1, GLA_DV), lambda b, h: (0, 0))],
        out_specs=pl.BlockSpec((1, seq, GLA_DV), lambda b, h: (b, 0, h)),
        out_shape=jax.ShapeDtypeStruct((batch, seq, GLA_V), BF16),
        scratch_shapes=[pltpu.VMEM((seq, GLA_DK), F32), pltpu.VMEM((seq, GLA_DK), F32),
                        pltpu.VMEM((seq, GLA_DV), F32),
                        pltpu.VMEM((GLA_DV, GLA_DK), F32), pltpu.VMEM((GLA_DV, GLA_DK), F32)],
        compiler_params=_params(("parallel", "parallel")), name="gla",
    )(p3, p3, p3, p3, lr3, wf, wb, b_gf.reshape(1, GLA_QK).astype(F32), b_gb.reshape(1, GLA_QK).astype(F32),
      gain.reshape(1, GLA_DV).astype(F32))


XATT_TQ = 512


def _xattn_body(q_ref, k_ref, v_ref, o_ref):
    scale = X_HEAD_DIM ** -0.5
    for h in range(X_HEADS):
        cols = slice(h * X_HEAD_DIM, (h + 1) * X_HEAD_DIM)
        s = _nt(q_ref[0, :, cols], k_ref[0, :, cols]) * scale
        m = jnp.max(s, axis=-1, keepdims=True)
        p = jnp.exp(s - m)
        denom = jnp.sum(p, axis=-1, keepdims=True)
        pv = jnp.dot(p.astype(BF16), v_ref[0, :, cols], preferred_element_type=F32)
        o_ref[0, :, cols] = (pv / denom).astype(o_ref.dtype)


def _cross_attention(q, kx, vx, batch, seq, mem_len):
    q3 = q.reshape(batch, seq, D_MODEL)
    k3 = kx.reshape(batch, mem_len, D_MODEL)
    v3 = vx.reshape(batch, mem_len, D_MODEL)
    mem_spec = pl.BlockSpec((1, mem_len, D_MODEL), lambda b, i: (b, 0, 0))
    tile = pl.BlockSpec((1, XATT_TQ, D_MODEL), lambda b, i: (b, i, 0))
    out = pl.pallas_call(
        _xattn_body, grid=(batch, seq // XATT_TQ), in_specs=[tile, mem_spec, mem_spec], out_specs=tile,
        out_shape=jax.ShapeDtypeStruct((batch, seq, D_MODEL), BF16),
        compiler_params=_params(("parallel", "parallel")), name="cross_attention")(q3, k3, v3)
    return out.reshape(batch * seq, D_MODEL)


ROUTER_TM = 512


HALF = D_MODEL // 2


def _pack_halves(x):
    return pltpu.pack_elementwise([x[:, :HALF], x[:, HALF:]], packed_dtype=BF16)


def _unpack_halves(words):
    return [pltpu.unpack_elementwise(words, index=i, packed_dtype=BF16, unpacked_dtype=F32) for i in range(2)]


def _router_body(h_ref, g_ref, w_ref, b_ref, hn_ref, idx_ref, gate_ref):
    hn = _rms(h_ref[...], g_ref[...])
    hn_ref[...] = _pack_halves(hn)
    logits = jnp.dot(hn.astype(BF16), w_ref[...], preferred_element_type=F32) + b_ref[...]
    lane = lax.broadcasted_iota(jnp.int32, logits.shape, 1).astype(F32)
    ninf = -jnp.inf
    first = lambda hit: jnp.min(jnp.where(hit, lane, float(LANES)), axis=-1, keepdims=True)
    in_groups = lane < N_GROUPS
    gl = jnp.where(in_groups, logits, ninf)
    gmax = jnp.max(gl, axis=-1, keepdims=True)
    g_idx = first(gl == gmax)
    p_group = 1.0 / jnp.sum(jnp.where(in_groups, jnp.exp(logits - gmax), 0.0), axis=-1, keepdims=True)
    lo = N_GROUPS + EXPERTS_PER_GROUP * g_idx
    el = jnp.where((lane >= lo) & (lane < lo + EXPERTS_PER_GROUP), logits, ninf)
    e1 = jnp.max(el, axis=-1, keepdims=True)
    i1 = first(el == e1)
    el2 = jnp.where(lane == i1, ninf, el)
    e2 = jnp.max(el2, axis=-1, keepdims=True)
    i2 = first(el2 == e2)
    t = jnp.exp(e2 - e1)
    w1 = p_group / (1.0 + t)
    w2 = p_group * t / (1.0 + t)
    idx_ref[...] = jnp.where(lane == 0, i1 - N_GROUPS, jnp.where(lane == 1, i2 - N_GROUPS, 0.0)).astype(jnp.int32)
    gate_ref[...] = jnp.where(lane == 0, w1, jnp.where(lane == 1, w2, 0.0))


def _router(h, gain, w_rg, b_rg, w_re, b_re):
    rows = h.shape[0]
    n_log = N_GROUPS + N_EXPERTS
    w = jnp.zeros((D_MODEL, LANES), F32).at[:, :N_GROUPS].set(w_rg).at[:, N_GROUPS:n_log].set(w_re).astype(BF16)
    b = jnp.zeros((1, LANES), F32).at[0, :N_GROUPS].set(b_rg).at[0, N_GROUPS:n_log].set(b_re)
    tile = lambda width: pl.BlockSpec((ROUTER_TM, width), lambda i: (i, 0))
    full = lambda a: pl.BlockSpec(a.shape, lambda i: (0, 0))
    g2 = gain.reshape(1, D_MODEL).astype(F32)
    return pl.pallas_call(
        _router_body, grid=(rows // ROUTER_TM,),
        in_specs=[tile(D_MODEL), full(g2), full(w), full(b)],
        out_specs=[tile(HALF), tile(LANES), tile(LANES)],
        out_shape=[jax.ShapeDtypeStruct((rows, HALF), jnp.uint32),
                   jax.ShapeDtypeStruct((rows, LANES), jnp.int32),
                   jax.ShapeDtypeStruct((rows, LANES), F32)],
        compiler_params=_params(("parallel",)), name="moe_router")(h, g2, w, b)


W_CAST_ROWS = 256


def _cast_weight(src_ref, dst_ref):
    def body(c, carry):
        rows = pl.ds(pl.multiple_of(c * W_CAST_ROWS, W_CAST_ROWS), W_CAST_ROWS)
        dst_ref[rows, :] = src_ref[rows, :].astype(BF16)
        return carry
    lax.fori_loop(0, src_ref.shape[0] // W_CAST_ROWS, body, 0)


def _expert_changed(blk_e, blk):
    prev = blk_e[jnp.maximum(blk - 1, 0)]
    return (blk == 0) | (blk_e[blk] != prev)


WEIGHT_DMA_PRIORITY = 1


def _switch_expert(blk_e, nxt_e, blk, layer, w_hbms, stages, dsts, sem):
    def copies(e):
        return [pltpu.make_async_copy(w.at[layer, e], st, sem.at[i])
                for i, (w, st) in enumerate(zip(w_hbms, stages))]

    @pl.when(blk == 0)
    def _():
        for cp in copies(blk_e[0]):
            cp.start(priority=WEIGHT_DMA_PRIORITY)

    @pl.when(_expert_changed(blk_e, blk))
    def _():
        for cp, st, dst in zip(copies(blk_e[blk]), stages, dsts):
            cp.wait()
            _cast_weight(st, dst)

        @pl.when(nxt_e[blk] >= 0)
        def _():
            for cp in copies(nxt_e[blk]):
                cp.start(priority=WEIGHT_DMA_PRIORITY)


GATHER_AHEAD = 3
GATHER_SLOTS = GATHER_AHEAD + 1
MOE_CHUNKS = 4


def _moe_up_body(blk_e, n_real, nxt_e, *refs, layer):
    tok_refs = refs[:GATHER_AHEAD + 1]
    hn_hbm, wg_hbm, wu_hbm, hid_ref, xs, xb, wg_stage, wu_stage, wgb, wub, sem, wsem = refs[GATHER_AHEAD + 1:]
    blk = pl.program_id(0)

    def row_copy(idx_ref, slot, r):
        return pltpu.make_async_copy(hn_hbm.at[pl.ds(idx_ref[0, 0, r], 1), :], xs.at[slot, pl.ds(r, 1), :],
                                     sem.at[slot])

    def wait(slot):
        pltpu.make_async_copy(hn_hbm.at[pl.ds(0, MOE_BLOCK), :], xs.at[slot], sem.at[slot]).wait()

    @pl.when(blk == 0)
    def _():
        def body(r, carry):
            for s in range(GATHER_AHEAD):
                row_copy(tok_refs[s], s, r).start()
            return carry
        lax.fori_loop(0, MOE_BLOCK, body, 0)

    @pl.when(blk < n_real[0])
    def _():
        slot = blk % GATHER_SLOTS
        ahead_slot = (blk + GATHER_AHEAD) % GATHER_SLOTS

        _switch_expert(blk_e, nxt_e, blk, layer, (wg_hbm, wu_hbm), (wg_stage, wu_stage), (wgb, wub), wsem)
        wait(slot)
        lo, hi = _unpack_halves(xs[slot])
        xb[:, :HALF] = lo.astype(BF16)
        xb[:, HALF:] = hi.astype(BF16)
        cw = D_FF_EXPERT // MOE_CHUNKS
        rows_per_chunk = MOE_BLOCK // MOE_CHUNKS
        for c in range(MOE_CHUNKS):
            cols = slice(c * cw, (c + 1) * cw)
            a = jnp.dot(xb[...], wgb[:, cols], preferred_element_type=F32)
            u = jnp.dot(xb[...], wub[:, cols], preferred_element_type=F32)
            hid_ref[:, cols] = (a * jax.nn.sigmoid(a) * u).astype(hid_ref.dtype)
            for r in range(c * rows_per_chunk, (c + 1) * rows_per_chunk):
                row_copy(tok_refs[GATHER_AHEAD], ahead_slot, r).start()

        @pl.when(blk == n_real[0] - 1)
        def _():
            for s in range(1, GATHER_SLOTS):
                wait((blk + s) % GATHER_SLOTS)

    @pl.when(blk >= n_real[0])
    def _():
        hid_ref[...] = jnp.zeros_like(hid_ref)


SCATTER_SLOTS = 3


def _moe_down_body(blk_e, n_real, nxt_e, dst_ref, dst_prev_ref, hid_ref, wd_hbm, out_hbm,
                   ys, wd_stage, wdb, sem, wsem, *, n_assign, layer):
    blk = pl.program_id(0)
    dummy_slot = SCATTER_SLOTS - 1

    def row_copy(idx_ref, slot, r):
        return pltpu.make_async_copy(ys.at[slot, pl.ds(r, 1), :], out_hbm.at[pl.ds(idx_ref[0, 0, r], 1), :],
                                     sem.at[slot])

    def wait(slot):
        pltpu.make_async_copy(ys.at[slot], out_hbm.at[pl.ds(0, MOE_BLOCK), :], sem.at[slot]).wait()

    @pl.when(blk == 0)
    def _():
        ys[dummy_slot] = jnp.zeros((MOE_BLOCK, HALF), jnp.uint32)
        for s in range(SCATTER_SLOTS):
            trash = pltpu.make_async_copy(
                ys.at[dummy_slot], out_hbm.at[pl.ds(n_assign + s * MOE_BLOCK, MOE_BLOCK), :], sem.at[dummy_slot])
            trash.start()
            trash.wait()

    @pl.when(blk < n_real[0])
    def _():
        slot = blk % SCATTER_SLOTS
        prev_slot = (blk + SCATTER_SLOTS - 1) % SCATTER_SLOTS

        _switch_expert(blk_e, nxt_e, blk, layer, (wd_hbm,), (wd_stage,), (wdb,), wsem)

        @pl.when(blk >= SCATTER_SLOTS - 1)
        def _():
            wait(slot)

        cw = HALF // MOE_CHUNKS
        rows_per_chunk = MOE_BLOCK // MOE_CHUNKS
        for c in range(MOE_CHUNKS):
            cols = slice(c * cw, (c + 1) * cw)
            hi_cols = slice(HALF + c * cw, HALF + (c + 1) * cw)
            ys[slot, :, cols] = pltpu.pack_elementwise(
                [jnp.dot(hid_ref[...], wdb[:, cols], preferred_element_type=F32),
                 jnp.dot(hid_ref[...], wdb[:, hi_cols], preferred_element_type=F32)], packed_dtype=BF16)
            for r in range(c * rows_per_chunk, (c + 1) * rows_per_chunk):
                row_copy(dst_prev_ref, prev_slot, r).start(priority=r % 2)

        @pl.when(blk == n_real[0] - 1)
        def _():
            def body(r, carry):
                row_copy(dst_ref, slot, r).start()
                return carry
            lax.fori_loop(0, MOE_BLOCK, body, 0)
            for s in range(SCATTER_SLOTS):
                wait(s)


def _moe_forward(hn, idx, w_gate, w_up, w_down, layer, n_tokens):
    A = n_tokens * TOP_K
    out_rows = A + SCATTER_SLOTS * MOE_BLOCK
    R = A + N_EXPERTS * MOE_BLOCK
    n_blk = R // MOE_BLOCK
    e_flat = idx[:, :TOP_K].reshape(A)
    onehot = (e_flat[:, None] == jnp.arange(N_EXPERTS, dtype=jnp.int32)[None, :]).astype(jnp.int32)
    csum = jnp.cumsum(onehot, axis=0)
    counts = csum[-1]
    padded = ((counts + MOE_BLOCK - 1) // MOE_BLOCK) * MOE_BLOCK
    pend = jnp.cumsum(padded)
    pstart = pend - padded
    dest = jnp.sum(onehot * (csum - 1 + pstart[None, :]), axis=1)
    row_a = jnp.full((R,), -1, jnp.int32).at[dest].set(jnp.arange(A, dtype=jnp.int32), unique_indices=True)
    n_real = (pend[-1] // MOE_BLOCK).astype(jnp.int32).reshape(1)
    blk_start = jnp.arange(n_blk, dtype=jnp.int32) * MOE_BLOCK
    blk_e = jnp.minimum(jnp.sum((pend[None, :] <= blk_start[:, None]).astype(jnp.int32), axis=1), N_EXPERTS - 1)
    row_tok = (jnp.maximum(row_a, 0) // TOP_K).reshape(n_blk, 1, MOE_BLOCK)
    row = jnp.arange(R, dtype=jnp.int32)
    trash = A + ((row // MOE_BLOCK) % SCATTER_SLOTS) * MOE_BLOCK + row % MOE_BLOCK
    row_dst = jnp.where(row_a >= 0, (row_a % TOP_K) * n_tokens + row_a // TOP_K, trash).reshape(n_blk, 1, MOE_BLOCK)
    row_dst_prev = jnp.concatenate(
        [(A + (SCATTER_SLOTS - 1) * MOE_BLOCK + jnp.arange(MOE_BLOCK, dtype=jnp.int32)).reshape(1, 1, MOE_BLOCK),
         row_dst[:-1]], axis=0)

    run_end = pend[blk_e] // MOE_BLOCK
    nxt_e = jnp.where(run_end < n_real[0], blk_e[jnp.minimum(run_end, n_blk - 1)], -1).astype(jnp.int32)

    clamp = lambda i, nr: jnp.minimum(i, nr[0] - 1)
    idx_spec = lambda shift: pl.BlockSpec((1, 1, MOE_BLOCK), lambda i, be, nr, nx: (clamp(i + shift, nr), 0, 0),
                                          memory_space=pltpu.SMEM)
    hbm = pl.BlockSpec(memory_space=pl.ANY)
    hid = pl.pallas_call(
        functools.partial(_moe_up_body, layer=layer),
        grid_spec=pltpu.PrefetchScalarGridSpec(
            num_scalar_prefetch=3, grid=(n_blk,),
            in_specs=[idx_spec(s) for s in range(GATHER_AHEAD + 1)] + [hbm, hbm, hbm],
            out_specs=pl.BlockSpec((MOE_BLOCK, D_FF_EXPERT), lambda i, be, nr, nx: (i, 0)),
            scratch_shapes=[pltpu.VMEM((GATHER_SLOTS, MOE_BLOCK, HALF), jnp.uint32),
                            pltpu.VMEM((MOE_BLOCK, D_MODEL), BF16),
                            pltpu.VMEM((D_MODEL, D_FF_EXPERT), F32), pltpu.VMEM((D_MODEL, D_FF_EXPERT), F32),
                            pltpu.VMEM((D_MODEL, D_FF_EXPERT), BF16), pltpu.VMEM((D_MODEL, D_FF_EXPERT), BF16),
                            pltpu.SemaphoreType.DMA((GATHER_SLOTS,)), pltpu.SemaphoreType.DMA((2,))]),
        out_shape=jax.ShapeDtypeStruct((R, D_FF_EXPERT), BF16),
        compiler_params=_params(("arbitrary",)), name="moe_up",
    )(blk_e, n_real, nxt_e, *([row_tok] * (GATHER_AHEAD + 1)), hn, w_gate, w_up)
    return pl.pallas_call(
        functools.partial(_moe_down_body, n_assign=A, layer=layer),
        grid_spec=pltpu.PrefetchScalarGridSpec(
            num_scalar_prefetch=3, grid=(n_blk,),
            in_specs=[idx_spec(0), idx_spec(0),
                      pl.BlockSpec((MOE_BLOCK, D_FF_EXPERT), lambda i, be, nr, nx: (clamp(i, nr), 0)),
                      hbm],
            out_specs=hbm,
            scratch_shapes=[pltpu.VMEM((SCATTER_SLOTS, MOE_BLOCK, HALF), jnp.uint32),
                            pltpu.VMEM((D_FF_EXPERT, D_MODEL), F32), pltpu.VMEM((D_FF_EXPERT, D_MODEL), BF16),
                            pltpu.SemaphoreType.DMA((SCATTER_SLOTS,)), pltpu.SemaphoreType.DMA((1,))]),
        out_shape=jax.ShapeDtypeStruct((out_rows, HALF), jnp.uint32),
        compiler_params=_params(("arbitrary",)), name="moe_down",
    )(blk_e, n_real, nxt_e, row_dst, row_dst_prev, hid, w_down)


FINAL_TM = 512


def _final_body(h_ref, y0_ref, y1_ref, gate_ref, g_ref, o_ref):
    gate = gate_ref[...]
    y0, y1 = _unpack_halves(y0_ref[...]), _unpack_halves(y1_ref[...])
    halves = [h_ref[:, i * HALF:(i + 1) * HALF] + gate[:, 0:1] * y0[i] + gate[:, 1:2] * y1[i] for i in range(2)]
    mean_sq = sum(jnp.sum(hh * hh, axis=-1, keepdims=True) for hh in halves) / D_MODEL
    inv = lax.rsqrt(mean_sq + RMS_EPS)
    for i, hh in enumerate(halves):
        o_ref[:, i * HALF:(i + 1) * HALF] = hh * inv * g_ref[:, i * HALF:(i + 1) * HALF]


def _final(h, ys, gate, gain):
    rows = h.shape[0]
    n_tiles = rows // FINAL_TM
    tile = lambda width: pl.BlockSpec((FINAL_TM, width), lambda i: (i, 0))
    g2 = gain.reshape(1, D_MODEL).astype(F32)
    return pl.pallas_call(
        _final_body, grid=(n_tiles,),
        in_specs=[tile(D_MODEL), tile(HALF), pl.BlockSpec((FINAL_TM, HALF), lambda i: (i + n_tiles, 0)),
                  tile(LANES), pl.BlockSpec((1, D_MODEL), lambda i: (0, 0))],
        out_specs=tile(D_MODEL), out_shape=jax.ShapeDtypeStruct((rows, D_MODEL), F32),
        compiler_params=_params(("parallel",)), name="combine_final_norm")(h, ys, ys, gate, g2)


def kernel(x, mem, norm_mix, w_in, attn_out_norm, sink_logit, w_gla_gf, b_gla_gf, w_gla_gb, b_gla_gb, gla_out_norm, w_out, norm_cross, norm_mem, w_cq, w_ck, w_cv, w_co, norm_ffn, w_router_group, b_router_group, w_router_expert, b_router_expert, w_gate, w_up, w_down, norm_final):
    batch, seq, _ = x.shape
    mem_len = mem.shape[1]
    n_tokens = batch * seq
    h = x.reshape(n_tokens, D_MODEL)
    memf = mem.reshape(batch * mem_len, D_MODEL)
    assert norm_mix.shape[0] == 1, "the combine step is fused with the final norm: single-layer stacks only"
    for l in range(norm_mix.shape[0]):
        w_lr = jnp.zeros((D_MODEL, LANES), F32).at[:, :2 * GLA_LOWRANK].set(w_in[l][:, MAIN_COLS:]).astype(BF16)
        proj, lr = _dense([h], w_in, l, n_cols=MAIN_COLS, out_dtype=BF16, gain=norm_mix[l], extra_w=w_lr,
                          name="in_proj")
        o_a = _window_attention(proj, sink_logit[l], attn_out_norm[l], batch, seq).reshape(n_tokens, ATT_Q)
        o_g = _gla(proj, lr, w_gla_gf[l], b_gla_gf[l], w_gla_gb[l], b_gla_gb[l], gla_out_norm[l],
                   batch, seq).reshape(n_tokens, GLA_V)
        h = _dense([o_a, o_g], w_out, l, n_cols=D_MODEL, out_dtype=F32, res=h, name="out_proj")
        kx = _dense([memf], w_ck, l, n_cols=D_MODEL, out_dtype=BF16, gain=norm_mem[l], name="mem_k_proj")
        vx = _dense([memf], w_cv, l, n_cols=D_MODEL, out_dtype=BF16, gain=norm_mem[l], name="mem_v_proj")
        q = _dense([h], w_cq, l, n_cols=D_MODEL, out_dtype=BF16, gain=norm_cross[l], name="cross_q_proj")
        o = _cross_attention(q, kx, vx, batch, seq, mem_len)
        h = _dense([o], w_co, l, n_cols=D_MODEL, out_dtype=F32, res=h, name="cross_out_proj")
        hn, idx, gate = _router(h, norm_ffn[l], w_router_group[l], b_router_group[l],
                                w_router_expert[l], b_router_expert[l])
        ys = _moe_forward(hn, idx, w_gate, w_up, w_down, l, n_tokens)
    return _final(h, ys, gate, norm_final).reshape(batch, seq, D_MODEL)
```

```python
import functools

import jax
import jax.numpy as jnp
from jax import lax
from jax.experimental import pallas as pl
from jax.experimental.pallas import tpu as pltpu

F32 = jnp.float32
BF16 = jnp.bfloat16

D_MODEL = 2048
N_Q_HEADS = 8
N_KV_HEADS = 2
Q_PER_KV = N_Q_HEADS // N_KV_HEADS
HEAD_DIM = 128
WINDOW = 128
WBLK = 128
GLA_HEADS = 4
GLA_DK = 128
GLA_DV = 256
GLA_LOWRANK = 16
GLA_TAU = 16.0
GLA_CHUNK = 64
ATT_Q = N_Q_HEADS * HEAD_DIM
ATT_KV = N_KV_HEADS * HEAD_DIM
GLA_QK = GLA_HEADS * GLA_DK
GLA_V = GLA_HEADS * GLA_DV
MAIN_COLS = ATT_Q + 2 * ATT_KV + 2 * GLA_QK + 2 * GLA_V
COL_KA = ATT_Q
COL_VA = COL_KA + ATT_KV
COL_QG = COL_VA + ATT_KV
COL_KG = COL_QG + GLA_QK
COL_VG = COL_KG + GLA_QK
COL_RG = COL_VG + GLA_V
X_HEADS = 4
X_HEAD_DIM = D_MODEL // X_HEADS
N_GROUPS = 4
EXPERTS_PER_GROUP = 8
N_EXPERTS = N_GROUPS * EXPERTS_PER_GROUP
TOP_K = 2
D_FF_EXPERT = D_MODEL // 2
MOE_BLOCK = 256
RMS_EPS = 1e-6
NEG_INF = -1e30

LANES = 128
VMEM_LIMIT = 56 * 1024 * 1024
W_STAGE_ROWS = 128


def _params(sem):
    return pltpu.CompilerParams(dimension_semantics=sem, vmem_limit_bytes=VMEM_LIMIT)


def _nt(a, b):
    return lax.dot_general(a, b, (((1,), (1,)), ((), ())), preferred_element_type=F32)


def _tn(a, b):
    return lax.dot_general(a, b, (((0,), (0,)), ((), ())), preferred_element_type=F32)


def _rms(x, gain):
    return x * lax.rsqrt(jnp.mean(x * x, axis=-1, keepdims=True) + RMS_EPS) * gain


def _load_weight_bf16(w_hbm, layer, wb, stage, sem, k_rows, n_cols):
    n_chunks = k_rows // W_STAGE_ROWS

    def copy(c):
        return pltpu.make_async_copy(
            w_hbm.at[layer, pl.ds(c * W_STAGE_ROWS, W_STAGE_ROWS), pl.ds(0, n_cols)],
            stage.at[c % 2], sem.at[c % 2])

    copy(0).start()
    for c in range(n_chunks):
        if c + 1 < n_chunks:
            copy(c + 1).start()
        copy(c).wait()
        wb[c * W_STAGE_ROWS:(c + 1) * W_STAGE_ROWS, :] = stage[c % 2].astype(BF16)


def _dense_body(*refs, part_widths, has_norm, has_extra, has_res, n_cols, n_chunk, layer):
    it = iter(refs)
    x_refs = [next(it) for _ in part_widths]
    g_ref = next(it) if has_norm else None
    w_hbm = next(it)
    ew_ref = next(it) if has_extra else None
    res_ref = next(it) if has_res else None
    o_ref = next(it)
    eo_ref = next(it) if has_extra else None
    wb, stage, sem = next(it), next(it), next(it)
    u_ref = next(it) if has_norm else None
    k_rows = sum(part_widths)

    @pl.when(pl.program_id(0) == 0)
    def _():
        _load_weight_bf16(w_hbm, layer, wb, stage, sem, k_rows, n_cols)

    if has_norm:
        u_ref[...] = _rms(x_refs[0][...], g_ref[...]).astype(BF16)
        lhs = [(u_ref, 0, k_rows)]
    else:
        lhs, off = [], 0
        for r, kw in zip(x_refs, part_widths):
            lhs.append((r, off, kw))
            off += kw
    for n0 in range(0, n_cols, n_chunk):
        acc = None
        for r, off, kw in lhs:
            d = jnp.dot(r[...], wb[off:off + kw, n0:n0 + n_chunk], preferred_element_type=F32)
            acc = d if acc is None else acc + d
        if has_res:
            acc = acc + res_ref[:, n0:n0 + n_chunk]
        o_ref[:, n0:n0 + n_chunk] = acc.astype(o_ref.dtype)
    if has_extra:
        eo_ref[...] = jnp.dot(u_ref[...], ew_ref[...], preferred_element_type=F32)


def _dense(xs, w, layer, *, n_cols, out_dtype, gain=None, extra_w=None, res=None, tm=512, n_chunk=512, name):
    rows = xs[0].shape[0]
    part_widths = tuple(x.shape[1] for x in xs)
    k_rows = sum(part_widths)
    has_norm, has_extra, has_res = gain is not None, extra_w is not None, res is not None
    row_spec = lambda width: pl.BlockSpec((tm, width), lambda i: (i, 0))
    full_spec = lambda a: pl.BlockSpec(a.shape, lambda i: (0, 0))
    args, in_specs = list(xs), [row_spec(kw) for kw in part_widths]
    if has_norm:
        args.append(gain.reshape(1, k_rows).astype(F32))
        in_specs.append(full_spec(args[-1]))
    args.append(w)
    in_specs.append(pl.BlockSpec(memory_space=pl.ANY))
    if has_extra:
        args.append(extra_w)
        in_specs.append(full_spec(extra_w))
    if has_res:
        args.append(res)
        in_specs.append(row_spec(n_cols))
    out_shape = [jax.ShapeDtypeStruct((rows, n_cols), out_dtype)]
    out_specs = [row_spec(n_cols)]
    if has_extra:
        out_shape.append(jax.ShapeDtypeStruct((rows, extra_w.shape[1]), F32))
        out_specs.append(row_spec(extra_w.shape[1]))
    scratch = [pltpu.VMEM((k_rows, n_cols), BF16),
               pltpu.VMEM((2, W_STAGE_ROWS, n_cols), F32),
               pltpu.SemaphoreType.DMA((2,))]
    if has_norm:
        scratch.append(pltpu.VMEM((tm, k_rows), BF16))
    body = functools.partial(_dense_body, part_widths=part_widths, has_norm=has_norm, has_extra=has_extra,
                             has_res=has_res, n_cols=n_cols, n_chunk=n_chunk, layer=layer)
    outs = pl.pallas_call(
        body, grid=(rows // tm,), in_specs=in_specs, out_specs=out_specs, out_shape=out_shape,
        scratch_shapes=scratch, compiler_params=_params(("arbitrary",)), name=name)(*args)
    return outs if has_extra else outs[0]


ATT_TQ = 512


def _winattn_body(sink_ref, q_ref, kp_ref, km_ref, kn_ref, vp_ref, vm_ref, vn_ref, g_ref, o_ref,
                  kcat, vcat, obuf, *, seq):
    s0 = pl.program_id(1) * ATT_TQ
    kcat[0:WBLK, :] = kp_ref[0]
    kcat[WBLK:WBLK + ATT_TQ, :] = km_ref[0]
    kcat[WBLK + ATT_TQ:, :] = kn_ref[0]
    vcat[0:WBLK, :] = vp_ref[0]
    vcat[WBLK:WBLK + ATT_TQ, :] = vm_ref[0]
    vcat[WBLK + ATT_TQ:, :] = vn_ref[0]
    qi = lax.broadcasted_iota(jnp.int32, (WBLK, 3 * WBLK), 0) + WBLK
    ki = lax.broadcasted_iota(jnp.int32, (WBLK, 3 * WBLK), 1)
    dist_i = jnp.abs(ki - qi)
    neg_dist = jnp.where(dist_i <= WINDOW, -dist_i.astype(F32), NEG_INF)
    scale = HEAD_DIM ** -0.5
    for qb in range(ATT_TQ // WBLK):
        kabs = s0 + (qb - 1) * WBLK + ki
        bias_unit = jnp.where((kabs >= 0) & (kabs < seq), neg_dist, NEG_INF)
        head_cols = lambda j: slice(j * HEAD_DIM, (j + 1) * HEAD_DIM)
        kv_cols = lambda j: head_cols(j // Q_PER_KV)
        key_rows = slice(qb * WBLK, (qb + 3) * WBLK)
        scores = []
        for j in range(N_Q_HEADS):
            slope = 2.0 ** (-8.0 * (j + 1) / N_Q_HEADS)
            q = q_ref[0, qb * WBLK:(qb + 1) * WBLK, head_cols(j)]
            scores.append(_nt(q, kcat[key_rows, kv_cols(j)]) * scale + slope * bias_unit)
        probs, denoms = [], []
        for j in range(N_Q_HEADS):
            sink = sink_ref[j]
            m = jnp.maximum(jnp.max(scores[j], axis=-1, keepdims=True), sink)
            p = jnp.exp(scores[j] - m)
            denoms.append(jnp.sum(p, axis=-1, keepdims=True) + jnp.exp(sink - m))
            probs.append(p.astype(BF16))
        for j in range(N_Q_HEADS):
            pv = jnp.dot(probs[j], vcat[key_rows, kv_cols(j)], preferred_element_type=F32)
            obuf[qb, :, head_cols(j)] = pv / denoms[j]
        o_ref[0, qb * WBLK:(qb + 1) * WBLK, :] = _rms(obuf[qb], g_ref[...]).astype(o_ref.dtype)


def _window_attention(proj, sink_logit, gain, batch, seq):
    p3 = proj.reshape(batch, seq, MAIN_COLS)
    nq = seq // ATT_TQ
    per = ATT_TQ // WBLK
    last = seq // WBLK - 1
    main = lambda col: pl.BlockSpec((1, ATT_TQ, ATT_KV), lambda b, i: (b, i, col))
    prev = lambda col: pl.BlockSpec((1, WBLK, ATT_KV), lambda b, i: (b, jnp.maximum(i * per - 1, 0), col))
    nxt = lambda col: pl.BlockSpec((1, WBLK, ATT_KV), lambda b, i: (b, jnp.minimum(i * per + per, last), col))
    ck, cv = COL_KA // ATT_KV, COL_VA // ATT_KV
    return pl.pallas_call(
        functools.partial(_winattn_body, seq=seq),
        grid=(batch, nq),
        in_specs=[pl.BlockSpec(memory_space=pltpu.SMEM),
                  pl.BlockSpec((1, ATT_TQ, ATT_Q), lambda b, i: (b, i, 0)),
                  prev(ck), main(ck), nxt(ck), prev(cv), main(cv), nxt(cv),
                  pl.BlockSpec((1, ATT_Q), lambda b, i: (0, 0))],
        out_specs=pl.BlockSpec((1, ATT_TQ, ATT_Q), lambda b, i: (b, i, 0)),
        out_shape=jax.ShapeDtypeStruct((batch, seq, ATT_Q), BF16),
        scratch_shapes=[pltpu.VMEM((ATT_TQ + 2 * WBLK, ATT_KV), BF16),
                        pltpu.VMEM((ATT_TQ + 2 * WBLK, ATT_KV), BF16),
                        pltpu.VMEM((ATT_TQ // WBLK, WBLK, ATT_Q), F32)],
        compiler_params=_params(("parallel", "parallel")), name="window_attention",
    )(sink_logit.astype(F32), p3, p3, p3, p3, p3, p3, p3, gain.reshape(1, ATT_Q).astype(F32))


GLA_SCAN_ROWS = 256
GLA_SCAN_UNROLL = 2
GLA_EPI_ROWS = 512
GLA_UNROLL = 4


def _split3(x):
    hi = x.astype(BF16)
    r1 = x - hi.astype(F32)
    mid = r1.astype(BF16)
    lo = (r1 - mid.astype(F32)).astype(BF16)
    return hi, mid, lo


def _gla_body(q_ref, k_ref, v_ref, r_ref, lr_ref, wf_ref, wb_ref, bf_ref, bb_ref, gn_ref, o_ref,
              cum_f, cum_b, acc, st_f, st_b, *, seq):
    C = GLA_CHUNK
    n_chunks = seq // C
    ri = lax.broadcasted_iota(jnp.int32, (GLA_SCAN_ROWS, GLA_SCAN_ROWS), 0)
    ci = lax.broadcasted_iota(jnp.int32, (GLA_SCAN_ROWS, GLA_SCAN_ROWS), 1)
    same = (ri // C) == (ci // C)
    tri_f = jnp.where(same & (ci <= ri), 1.0, 0.0).astype(BF16)
    tri_b = jnp.where(same & (ci >= ri), 1.0, 0.0).astype(BF16)

    def scan_body(t, carry):
        row_sets = [pl.ds(pl.multiple_of((t * GLA_SCAN_UNROLL + u) * GLA_SCAN_ROWS, GLA_SCAN_ROWS), GLA_SCAN_ROWS)
                    for u in range(GLA_SCAN_UNROLL)]
        lrs = [lr_ref[0, rows, :].astype(BF16) for rows in row_sets]
        zs = [jnp.dot(lr, w_ref[...], preferred_element_type=F32) + b_ref[...]
              for lr in lrs for w_ref, b_ref in ((wf_ref, bf_ref), (wb_ref, bb_ref))]
        gs = [_split3((jnp.minimum(z, 0.0) - jnp.log1p(jnp.exp(-jnp.abs(z)))) / GLA_TAU) for z in zs]
        for n, g3 in enumerate(gs):
            tri, dst = ((tri_f, cum_f), (tri_b, cum_b))[n % 2]
            dst[row_sets[n // 2], :] = sum(jnp.dot(tri, part, preferred_element_type=F32) for part in g3)
        return carry

    lax.fori_loop(0, seq // (GLA_SCAN_ROWS * GLA_SCAN_UNROLL), scan_body, 0)

    acc[...] = jnp.zeros_like(acc)
    st_f[...] = jnp.zeros_like(st_f)
    st_b[...] = jnp.zeros_like(st_b)
    rr = lax.broadcasted_iota(jnp.int32, (C, C), 0)
    cc = lax.broadcasted_iota(jnp.int32, (C, C), 1)
    scale = GLA_DK ** -0.5

    def chunk_body(i, carry):
        jobs = []
        for u in range(GLA_UNROLL):
            c = i * GLA_UNROLL + u
            jobs += [(c, cum_f, True), (n_chunks - 1 - c, cum_b, False)]
        prep = []
        for c, cum, forward in jobs:
            rows = pl.ds(pl.multiple_of(c * C, C), C)
            b = cum[rows, :]
            b_end = b[C - 1:C, :] if forward else b[0:1, :]
            q = q_ref[0, rows, :].astype(F32) * scale
            k = k_ref[0, rows, :].astype(F32)
            v = v_ref[0, rows, :]
            q_dec = (q * jnp.exp(b)).astype(BF16)
            k_inc = (k * jnp.exp(-b)).astype(BF16)
            k_dec = (k * jnp.exp(b_end - b)).astype(BF16)
            prep.append((rows, v, q_dec, k_inc, k_dec, jnp.exp(b_end)))
        attn = [_nt(q_dec, k_inc) for _, _, q_dec, k_inc, _, _ in prep]
        kv_t = [_tn(v, k_dec) for _, v, _, _, k_dec, _ in prep]
        state_t = {True: st_f[...], False: st_b[...]}
        o_inter = []
        for (_, _, forward), (_, _, q_dec, _, _, decay), kv in zip(jobs, prep, kv_t):
            o_inter.append(_nt(q_dec, state_t[forward].astype(BF16)))
            state_t[forward] = state_t[forward] * decay + kv
        st_f[...] = state_t[True]
        st_b[...] = state_t[False]
        for (_, _, forward), (rows, v, _, _, _, _), a, oi in zip(jobs, prep, attn, o_inter):
            a = jnp.where((cc <= rr) if forward else (cc >= rr), a, 0.0)
            o = jnp.dot(a.astype(BF16), v, preferred_element_type=F32) + oi
            acc[rows, :] = acc[rows, :] + o
        return carry

    lax.fori_loop(0, n_chunks // GLA_UNROLL, chunk_body, 0)

    def epi_body(t, carry):
        rows = pl.ds(pl.multiple_of(t * GLA_EPI_ROWS, GLA_EPI_ROWS), GLA_EPI_ROWS)
        r = r_ref[0, rows, :].astype(F32)
        o_ref[0, rows, :] = (_rms(acc[rows, :], gn_ref[...]) * (r * jax.nn.sigmoid(r))).astype(o_ref.dtype)
        return carry

    lax.fori_loop(0, seq // GLA_EPI_ROWS, epi_body, 0)


def _gla(proj, lr, w_gf, b_gf, w_gb, b_gb, gain, batch, seq):
    p3 = proj.reshape(batch, seq, MAIN_COLS)
    lr3 = lr.reshape(batch, seq, LANES)
    wf = jnp.zeros((LANES, GLA_QK), F32).at[:GLA_LOWRANK].set(w_gf).astype(BF16)
    wb = jnp.zeros((LANES, GLA_QK), F32).at[GLA_LOWRANK:2 * GLA_LOWRANK].set(w_gb).astype(BF16)
    seq_blk = lambda width, col0: pl.BlockSpec((1, seq, width), lambda b, h: (b, 0, col0 // width + h))
    head_w = pl.BlockSpec((LANES, GLA_DK), lambda b, h: (0, h))
    head_b = pl.BlockSpec((1, GLA_DK), lambda b, h: (0, h))
    return pl.pallas_call(
        functools.partial(_gla_body, seq=seq),
        grid=(batch, GLA_HEADS),
        in_specs=[seq_blk(GLA_DK, COL_QG), seq_blk(GLA_DK, COL_KG), seq_blk(GLA_DV, COL_VG),
                  seq_blk(GLA_DV, COL_RG),
                  pl.BlockSpec((1, seq, LANES), lambda b, h: (b, 0, 0)),
                  head_w, head_w, head_b, head_b,
                  pl.BlockSpec((1, GLA_DV), lambda b, h: (0, 0))],
        out_specs=pl.BlockSpec((1, seq, GLA_DV), lambda b, h: (b, 0, h)),
        out_shape=jax.ShapeDtypeStruct((batch, seq, GLA_V), BF16),
        scratch_shapes=[pltpu.VMEM((seq, GLA_DK), F32), pltpu.VMEM((seq, GLA_DK), F32),
                        pltpu.VMEM((seq, GLA_DV), F32),
                        pltpu.VMEM((GLA_DV, GLA_DK), F32), pltpu.VMEM((GLA_DV, GLA_DK), F32)],
        compiler_params=_params(("parallel", "parallel")), name="gla",
    )(p3, p3, p3, p3, lr3, wf, wb, b_gf.reshape(1, GLA_QK).astype(F32), b_gb.reshape(1, GLA_QK).astype(F32),
      gain.reshape(1, GLA_DV).astype(F32))


XATT_TQ = 512


def _xattn_body(q_ref, k_ref, v_ref, o_ref):
    scale = X_HEAD_DIM ** -0.5
    for h in range(X_HEADS):
        cols = slice(h * X_HEAD_DIM, (h + 1) * X_HEAD_DIM)
        s = _nt(q_ref[0, :, cols], k_ref[0, :, cols]) * scale
        m = jnp.max(s, axis=-1, keepdims=True)
        p = jnp.exp(s - m)
        denom = jnp.sum(p, axis=-1, keepdims=True)
        pv = jnp.dot(p.astype(BF16), v_ref[0, :, cols], preferred_element_type=F32)
        o_ref[0, :, cols] = (pv / denom).astype(o_ref.dtype)


def _cross_attention(q, kx, vx, batch, seq, mem_len):
    q3 = q.reshape(batch, seq, D_MODEL)
    k3 = kx.reshape(batch, mem_len, D_MODEL)
    v3 = vx.reshape(batch, mem_len, D_MODEL)
    mem_spec = pl.BlockSpec((1, mem_len, D_MODEL), lambda b, i: (b, 0, 0))
    tile = pl.BlockSpec((1, XATT_TQ, D_MODEL), lambda b, i: (b, i, 0))
    out = pl.pallas_call(
        _xattn_body, grid=(batch, seq // XATT_TQ), in_specs=[tile, mem_spec, mem_spec], out_specs=tile,
        out_shape=jax.ShapeDtypeStruct((batch, seq, D_MODEL), BF16),
        compiler_params=_params(("parallel", "parallel")), name="cross_attention")(q3, k3, v3)
    return out.reshape(batch * seq, D_MODEL)


ROUTER_TM = 512


HALF = D_MODEL // 2


def _pack_halves(x):
    return pltpu.pack_elementwise([x[:, :HALF], x[:, HALF:]], packed_dtype=BF16)


def _unpack_halves(words):
    return [pltpu.unpack_elementwise(words, index=i, packed_dtype=BF16, unpacked_dtype=F32) for i in range(2)]


ID_SPLIT = 32


def _router_body(h_ref, g_ref, w_ref, b_ref, hn_ref, gate_ref, order_ref, count_ref, *, n_tokens):
    hn = _rms(h_ref[...], g_ref[...])
    hn_ref[...] = _pack_halves(hn)
    logits = jnp.dot(hn.astype(BF16), w_ref[...], preferred_element_type=F32) + b_ref[...]
    lane = lax.broadcasted_iota(jnp.int32, logits.shape, 1).astype(F32)
    ninf = -jnp.inf
    first = lambda hit: jnp.min(jnp.where(hit, lane, float(LANES)), axis=-1, keepdims=True)
    in_groups = lane < N_GROUPS
    gl = jnp.where(in_groups, logits, ninf)
    gmax = jnp.max(gl, axis=-1, keepdims=True)
    g_idx = first(gl == gmax)
    p_group = 1.0 / jnp.sum(jnp.where(in_groups, jnp.exp(logits - gmax), 0.0), axis=-1, keepdims=True)
    lo = N_GROUPS + EXPERTS_PER_GROUP * g_idx
    el = jnp.where((lane >= lo) & (lane < lo + EXPERTS_PER_GROUP), logits, ninf)
    e1 = jnp.max(el, axis=-1, keepdims=True)
    i1 = first(el == e1)
    el2 = jnp.where(lane == i1, ninf, el)
    e2 = jnp.max(el2, axis=-1, keepdims=True)
    i2 = first(el2 == e2)
    t = jnp.exp(e2 - e1)
    w1 = p_group / (1.0 + t)
    w2 = p_group * t / (1.0 + t)
    gate_ref[...] = jnp.where(lane == 0, w1, jnp.where(lane == 1, w2, 0.0))

    tm = logits.shape[0]
    hit0, hit1 = lane == i1 - N_GROUPS, lane == i2 - N_GROUPS
    member = jnp.where(hit0 | hit1, 1.0, 0.0)
    member_b = member.astype(BF16)
    earlier = (lax.broadcasted_iota(jnp.int32, (tm, tm), 1) < lax.broadcasted_iota(jnp.int32, (tm, tm), 0))
    rank = jnp.dot(jnp.where(earlier, 1.0, 0.0).astype(BF16), member_b, preferred_element_type=F32)
    lower = (lax.broadcasted_iota(jnp.int32, (LANES, LANES), 0) < lax.broadcasted_iota(jnp.int32, (LANES, LANES), 1))
    run_start = jnp.sum(jnp.dot(member_b, jnp.where(lower, 1.0, 0.0).astype(BF16), preferred_element_type=F32),
                        axis=0, keepdims=True)
    pos = rank + run_start
    positions = [jnp.sum(jnp.where(hit, pos, 0.0), axis=-1, keepdims=True) for hit in (hit0, hit1)]
    out_lane = lax.broadcasted_iota(jnp.int32, (tm, TOP_K * tm), 1).astype(F32)
    digit_row = lax.broadcasted_iota(jnp.int32, (8, tm), 0)
    local_tok = lax.broadcasted_iota(jnp.int32, (8, tm), 1)
    digits = None
    for s, p in enumerate(positions):
        a = s * tm + local_tok
        lhs = jnp.where(digit_row == 0, a // ID_SPLIT, jnp.where(digit_row == 1, a % ID_SPLIT, 0))
        part = jnp.dot(lhs.astype(F32).astype(BF16), jnp.where(out_lane == p, 1.0, 0.0).astype(BF16),
                       preferred_element_type=F32)
        digits = part if digits is None else digits + part
    a_sorted = (digits[0:1] * ID_SPLIT + digits[1:2]).astype(jnp.int32)
    tok = pl.program_id(0) * tm + a_sorted % tm
    row8 = lax.broadcasted_iota(jnp.int32, (8, TOP_K * tm), 0)
    order_ref[0] = jnp.where(row8 == 0, tok, jnp.where(row8 == 1, (a_sorted // tm) * n_tokens + tok, 0))
    count_ref[0] = jnp.broadcast_to(jnp.sum(member, axis=0, keepdims=True), (8, LANES)).astype(jnp.int32)


def _router(h, gain, w_rg, b_rg, w_re, b_re):
    rows = h.shape[0]
    n_log = N_GROUPS + N_EXPERTS
    w = jnp.zeros((D_MODEL, LANES), F32).at[:, :N_GROUPS].set(w_rg).at[:, N_GROUPS:n_log].set(w_re).astype(BF16)
    b = jnp.zeros((1, LANES), F32).at[0, :N_GROUPS].set(b_rg).at[0, N_GROUPS:n_log].set(b_re)
    tile = lambda width: pl.BlockSpec((ROUTER_TM, width), lambda i: (i, 0))
    full = lambda a: pl.BlockSpec(a.shape, lambda i: (0, 0))
    g2 = gain.reshape(1, D_MODEL).astype(F32)
    n_tiles = rows // ROUTER_TM
    per_tile = lambda width: pl.BlockSpec((1, 8, width), lambda i: (i, 0, 0))
    return pl.pallas_call(
        functools.partial(_router_body, n_tokens=rows), grid=(n_tiles,),
        in_specs=[tile(D_MODEL), full(g2), full(w), full(b)],
        out_specs=[tile(HALF), tile(LANES), per_tile(TOP_K * ROUTER_TM), per_tile(LANES)],
        out_shape=[jax.ShapeDtypeStruct((rows, HALF), jnp.uint32),
                   jax.ShapeDtypeStruct((rows, LANES), F32),
                   jax.ShapeDtypeStruct((n_tiles, 8, TOP_K * ROUTER_TM), jnp.int32),
                   jax.ShapeDtypeStruct((n_tiles, 8, LANES), jnp.int32)],
        compiler_params=_params(("parallel",)), name="moe_router")(h, g2, w, b)


W_CAST_ROWS = 256


def _cast_weight(src_ref, dst_ref):
    def body(c, carry):
        rows = pl.ds(pl.multiple_of(c * W_CAST_ROWS, W_CAST_ROWS), W_CAST_ROWS)
        dst_ref[rows, :] = src_ref[rows, :].astype(BF16)
        return carry
    lax.fori_loop(0, src_ref.shape[0] // W_CAST_ROWS, body, 0)


def _expert_changed(blk_e, blk):
    prev = blk_e[jnp.maximum(blk - 1, 0)]
    return (blk == 0) | (blk_e[blk] != prev)


WEIGHT_DMA_PRIORITY = 1


def _switch_expert(blk_e, nxt_e, blk, layer, w_hbms, stages, dsts, sem):
    def copies(e):
        return [pltpu.make_async_copy(w.at[layer, e], st, sem.at[i])
                for i, (w, st) in enumerate(zip(w_hbms, stages))]

    @pl.when(blk == 0)
    def _():
        for cp in copies(blk_e[0]):
            cp.start(priority=WEIGHT_DMA_PRIORITY)

    @pl.when(_expert_changed(blk_e, blk))
    def _():
        for cp, st, dst in zip(copies(blk_e[blk]), stages, dsts):
            cp.wait()
            _cast_weight(st, dst)

        @pl.when(nxt_e[blk] >= 0)
        def _():
            for cp in copies(nxt_e[blk]):
                cp.start(priority=WEIGHT_DMA_PRIORITY)


GATHER_AHEAD = 3
GATHER_SLOTS = GATHER_AHEAD + 1
MOE_CHUNKS = 4


def _moe_up_body(blk_e, n_real, nxt_e, *refs, layer):
    tok_refs = refs[:GATHER_AHEAD + 1]
    hn_hbm, wg_hbm, wu_hbm, hid_ref, xs, xb, wg_stage, wu_stage, wgb, wub, sem, wsem = refs[GATHER_AHEAD + 1:]
    blk = pl.program_id(0)

    def row_copy(idx_ref, slot, r):
        return pltpu.make_async_copy(hn_hbm.at[pl.ds(idx_ref[0, 0, r], 1), :], xs.at[slot, pl.ds(r, 1), :],
                                     sem.at[slot])

    def wait(slot):
        pltpu.make_async_copy(hn_hbm.at[pl.ds(0, MOE_BLOCK), :], xs.at[slot], sem.at[slot]).wait()

    @pl.when(blk == 0)
    def _():
        def body(r, carry):
            for s in range(GATHER_AHEAD):
                row_copy(tok_refs[s], s, r).start()
            return carry
        lax.fori_loop(0, MOE_BLOCK, body, 0)

    @pl.when(blk < n_real[0])
    def _():
        slot = blk % GATHER_SLOTS
        ahead_slot = (blk + GATHER_AHEAD) % GATHER_SLOTS

        _switch_expert(blk_e, nxt_e, blk, layer, (wg_hbm, wu_hbm), (wg_stage, wu_stage), (wgb, wub), wsem)
        wait(slot)
        lo, hi = _unpack_halves(xs[slot])
        xb[:, :HALF] = lo.astype(BF16)
        xb[:, HALF:] = hi.astype(BF16)
        cw = D_FF_EXPERT // MOE_CHUNKS
        rows_per_chunk = MOE_BLOCK // MOE_CHUNKS
        for c in range(MOE_CHUNKS):
            cols = slice(c * cw, (c + 1) * cw)
            a = jnp.dot(xb[...], wgb[:, cols], preferred_element_type=F32)
            u = jnp.dot(xb[...], wub[:, cols], preferred_element_type=F32)
            hid_ref[:, cols] = (a * jax.nn.sigmoid(a) * u).astype(hid_ref.dtype)
            for r in range(c * rows_per_chunk, (c + 1) * rows_per_chunk):
                row_copy(tok_refs[GATHER_AHEAD], ahead_slot, r).start()

        @pl.when(blk == n_real[0] - 1)
        def _():
            for s in range(1, GATHER_SLOTS):
                wait((blk + s) % GATHER_SLOTS)

    @pl.when(blk >= n_real[0])
    def _():
        hid_ref[...] = jnp.zeros_like(hid_ref)


SCATTER_SLOTS = 3


def _moe_down_body(blk_e, n_real, nxt_e, dst_ref, dst_prev_ref, hid_ref, wd_hbm, out_hbm,
                   ys, wd_stage, wdb, sem, wsem, *, n_assign, layer):
    blk = pl.program_id(0)
    dummy_slot = SCATTER_SLOTS - 1

    def row_copy(idx_ref, slot, r):
        return pltpu.make_async_copy(ys.at[slot, pl.ds(r, 1), :], out_hbm.at[pl.ds(idx_ref[0, 0, r], 1), :],
                                     sem.at[slot])

    def wait(slot):
        pltpu.make_async_copy(ys.at[slot], out_hbm.at[pl.ds(0, MOE_BLOCK), :], sem.at[slot]).wait()

    @pl.when(blk == 0)
    def _():
        ys[dummy_slot] = jnp.zeros((MOE_BLOCK, HALF), jnp.uint32)
        for s in range(SCATTER_SLOTS):
            trash = pltpu.make_async_copy(
                ys.at[dummy_slot], out_hbm.at[pl.ds(n_assign + s * MOE_BLOCK, MOE_BLOCK), :], sem.at[dummy_slot])
            trash.start()
            trash.wait()

    @pl.when(blk < n_real[0])
    def _():
        slot = blk % SCATTER_SLOTS
        prev_slot = (blk + SCATTER_SLOTS - 1) % SCATTER_SLOTS

        _switch_expert(blk_e, nxt_e, blk, layer, (wd_hbm,), (wd_stage,), (wdb,), wsem)

        @pl.when(blk >= SCATTER_SLOTS - 1)
        def _():
            wait(slot)

        cw = HALF // MOE_CHUNKS
        rows_per_chunk = MOE_BLOCK // MOE_CHUNKS
        for c in range(MOE_CHUNKS):
            cols = slice(c * cw, (c + 1) * cw)
            hi_cols = slice(HALF + c * cw, HALF + (c + 1) * cw)
            ys[slot, :, cols] = pltpu.pack_elementwise(
                [jnp.dot(hid_ref[...], wdb[:, cols], preferred_element_type=F32),
                 jnp.dot(hid_ref[...], wdb[:, hi_cols], preferred_element_type=F32)], packed_dtype=BF16)
            for r in range(c * rows_per_chunk, (c + 1) * rows_per_chunk):
                row_copy(dst_prev_ref, prev_slot, r).start(priority=r % 2)

        @pl.when(blk == n_real[0] - 1)
        def _():
            def body(r, carry):
                row_copy(dst_ref, slot, r).start()
                return carry
            lax.fori_loop(0, MOE_BLOCK, body, 0)
            for s in range(SCATTER_SLOTS):
                wait(s)


def _moe_forward(hn, order, counts, w_gate, w_up, w_down, layer, n_tokens):
    A = n_tokens * TOP_K
    out_rows = A + SCATTER_SLOTS * MOE_BLOCK
    R = A + N_EXPERTS * MOE_BLOCK
    n_blk = R // MOE_BLOCK
    i32 = jnp.int32
    n_tiles, per_tile = order.shape[0], order.shape[2]
    experts = jnp.arange(N_EXPERTS, dtype=i32)
    n = counts[:, 0, :N_EXPERTS]
    total = jnp.sum(n, axis=0)
    padded = ((total + MOE_BLOCK - 1) // MOE_BLOCK) * MOE_BLOCK
    pend = jnp.cumsum(padded)
    pstart = pend - padded
    n_real = (pend[-1] // MOE_BLOCK).astype(i32).reshape(1)
    blk = jnp.arange(n_blk, dtype=i32)
    blk_e = jnp.minimum(jnp.sum((pend[None, :] <= (blk * MOE_BLOCK)[:, None]).astype(i32), axis=1), N_EXPERTS - 1)
    is_e = (blk_e[:, None] == experts[None, :]).astype(i32)
    of_block = lambda per_expert: jnp.sum(is_e * per_expert[None, :], axis=1)
    cum_incl = jnp.cumsum(n, axis=0)
    run_shift = (jnp.cumsum(n, axis=1) - n) - (cum_incl - n)
    per_tile_of_block = lambda m: jnp.sum(is_e[:, None, :] * m[None, :, :], axis=2)
    cum_b, shift_b = per_tile_of_block(cum_incl), per_tile_of_block(run_shift)
    in_blk = jnp.arange(MOE_BLOCK, dtype=i32)
    k = (blk * MOE_BLOCK - of_block(pstart))[:, None] + in_blk[None, :]
    tile_of = jnp.minimum(jnp.sum((cum_b[:, None, :] <= k[:, :, None]).astype(i32), axis=2), n_tiles - 1)
    is_tile = (tile_of[:, :, None] == jnp.arange(n_tiles, dtype=i32)[None, None, :]).astype(i32)
    entry = tile_of * per_tile + k + jnp.sum(is_tile * shift_b[:, None, :], axis=2)
    pad_entry = A + (blk % SCATTER_SLOTS)[:, None] * MOE_BLOCK + in_blk[None, :]
    entry = jnp.where(k < of_block(total)[:, None], entry, pad_entry)
    n_pad = SCATTER_SLOTS * MOE_BLOCK
    tok_table = jnp.concatenate([order[:, 0, :].reshape(A), jnp.zeros((n_pad,), i32)])
    dst_table = jnp.concatenate([order[:, 1, :].reshape(A), A + jnp.arange(n_pad, dtype=i32)])
    row_tok = tok_table[entry].reshape(n_blk, 1, MOE_BLOCK)
    row_dst = dst_table[entry].reshape(n_blk, 1, MOE_BLOCK)
    row_dst_prev = jnp.concatenate(
        [(A + (SCATTER_SLOTS - 1) * MOE_BLOCK + in_blk).reshape(1, 1, MOE_BLOCK), row_dst[:-1]], axis=0)

    run_end = pend[blk_e] // MOE_BLOCK
    nxt_e = jnp.where(run_end < n_real[0], blk_e[jnp.minimum(run_end, n_blk - 1)], -1).astype(jnp.int32)

    clamp = lambda i, nr: jnp.minimum(i, nr[0] - 1)
    idx_spec = lambda shift: pl.BlockSpec((1, 1, MOE_BLOCK), lambda i, be, nr, nx: (clamp(i + shift, nr), 0, 0),
                                          memory_space=pltpu.SMEM)
    hbm = pl.BlockSpec(memory_space=pl.ANY)
    hid = pl.pallas_call(
        functools.partial(_moe_up_body, layer=layer),
        grid_spec=pltpu.PrefetchScalarGridSpec(
            num_scalar_prefetch=3, grid=(n_blk,),
            in_specs=[idx_spec(s) for s in range(GATHER_AHEAD + 1)] + [hbm, hbm, hbm],
            out_specs=pl.BlockSpec((MOE_BLOCK, D_FF_EXPERT), lambda i, be, nr, nx: (i, 0)),
            scratch_shapes=[pltpu.VMEM((GATHER_SLOTS, MOE_BLOCK, HALF), jnp.uint32),
                            pltpu.VMEM((MOE_BLOCK, D_MODEL), BF16),
                            pltpu.VMEM((D_MODEL, D_FF_EXPERT), F32), pltpu.VMEM((D_MODEL, D_FF_EXPERT), F32),
                            pltpu.VMEM((D_MODEL, D_FF_EXPERT), BF16), pltpu.VMEM((D_MODEL, D_FF_EXPERT), BF16),
                            pltpu.SemaphoreType.DMA((GATHER_SLOTS,)), pltpu.SemaphoreType.DMA((2,))]),
        out_shape=jax.ShapeDtypeStruct((R, D_FF_EXPERT), BF16),
        compiler_params=_params(("arbitrary",)), name="moe_up",
    )(blk_e, n_real, nxt_e, *([row_tok] * (GATHER_AHEAD + 1)), hn, w_gate, w_up)
    return pl.pallas_call(
        functools.partial(_moe_down_body, n_assign=A, layer=layer),
        grid_spec=pltpu.PrefetchScalarGridSpec(
            num_scalar_prefetch=3, grid=(n_blk,),
            in_specs=[idx_spec(0), idx_spec(0),
                      pl.BlockSpec((MOE_BLOCK, D_FF_EXPERT), lambda i, be, nr, nx: (clamp(i, nr), 0)),
                      hbm],
            out_specs=hbm,
            scratch_shapes=[pltpu.VMEM((SCATTER_SLOTS, MOE_BLOCK, HALF), jnp.uint32),
                            pltpu.VMEM((D_FF_EXPERT, D_MODEL), F32), pltpu.VMEM((D_FF_EXPERT, D_MODEL), BF16),
                            pltpu.SemaphoreType.DMA((SCATTER_SLOTS,)), pltpu.SemaphoreType.DMA((1,))]),
        out_shape=jax.ShapeDtypeStruct((out_rows, HALF), jnp.uint32),
        compiler_params=_params(("arbitrary",)), name="moe_down",
    )(blk_e, n_real, nxt_e, row_dst, row_dst_prev, hid, w_down)


FINAL_TM = 512


def _final_body(h_ref, y0_ref, y1_ref, gate_ref, g_ref, o_ref):
    gate = gate_ref[...]
    y0, y1 = _unpack_halves(y0_ref[...]), _unpack_halves(y1_ref[...])
    halves = [h_ref[:, i * HALF:(i + 1) * HALF] + gate[:, 0:1] * y0[i] + gate[:, 1:2] * y1[i] for i in range(2)]
    mean_sq = sum(jnp.sum(hh * hh, axis=-1, keepdims=True) for hh in halves) / D_MODEL
    inv = lax.rsqrt(mean_sq + RMS_EPS)
    for i, hh in enumerate(halves):
        o_ref[:, i * HALF:(i + 1) * HALF] = hh * inv * g_ref[:, i * HALF:(i + 1) * HALF]


def _final(h, ys, gate, gain):
    rows = h.shape[0]
    n_tiles = rows // FINAL_TM
    tile = lambda width: pl.BlockSpec((FINAL_TM, width), lambda i: (i, 0))
    g2 = gain.reshape(1, D_MODEL).astype(F32)
    return pl.pallas_call(
        _final_body, grid=(n_tiles,),
        in_specs=[tile(D_MODEL), tile(HALF), pl.BlockSpec((FINAL_TM, HALF), lambda i: (i + n_tiles, 0)),
                  tile(LANES), pl.BlockSpec((1, D_MODEL), lambda i: (0, 0))],
        out_specs=tile(D_MODEL), out_shape=jax.ShapeDtypeStruct((rows, D_MODEL), F32),
        compiler_params=_params(("parallel",)), name="combine_final_norm")(h, ys, ys, gate, g2)


def kernel(x, mem, norm_mix, w_in, attn_out_norm, sink_logit, w_gla_gf, b_gla_gf, w_gla_gb, b_gla_gb, gla_out_norm, w_out, norm_cross, norm_mem, w_cq, w_ck, w_cv, w_co, norm_ffn, w_router_group, b_router_group, w_router_expert, b_router_expert, w_gate, w_up, w_down, norm_final):
    batch, seq, _ = x.shape
    mem_len = mem.shape[1]
    n_tokens = batch * seq
    h = x.reshape(n_tokens, D_MODEL)
    memf = mem.reshape(batch * mem_len, D_MODEL)
    assert norm_mix.shape[0] == 1, "the combine step is fused with the final norm: single-layer stacks only"
    for l in range(norm_mix.shape[0]):
        w_lr = jnp.zeros((D_MODEL, LANES), F32).at[:, :2 * GLA_LOWRANK].set(w_in[l][:, MAIN_COLS:]).astype(BF16)
        proj, lr = _dense([h], w_in, l, n_cols=MAIN_COLS, out_dtype=BF16, gain=norm_mix[l], extra_w=w_lr,
                          name="in_proj")
        o_a = _window_attention(proj, sink_logit[l], attn_out_norm[l], batch, seq).reshape(n_tokens, ATT_Q)
        o_g = _gla(proj, lr, w_gla_gf[l], b_gla_gf[l], w_gla_gb[l], b_gla_gb[l], gla_out_norm[l],
                   batch, seq).reshape(n_tokens, GLA_V)
        h = _dense([o_a, o_g], w_out, l, n_cols=D_MODEL, out_dtype=F32, res=h, name="out_proj")
        kx = _dense([memf], w_ck, l, n_cols=D_MODEL, out_dtype=BF16, gain=norm_mem[l], name="mem_k_proj")
        vx = _dense([memf], w_cv, l, n_cols=D_MODEL, out_dtype=BF16, gain=norm_mem[l], name="mem_v_proj")
        q = _dense([h], w_cq, l, n_cols=D_MODEL, out_dtype=BF16, gain=norm_cross[l], name="cross_q_proj")
        o = _cross_attention(q, kx, vx, batch, seq, mem_len)
        h = _dense([o], w_co, l, n_cols=D_MODEL, out_dtype=F32, res=h, name="cross_out_proj")
        hn, gate, order, counts = _router(h, norm_ffn[l], w_router_group[l], b_router_group[l],
                                          w_router_expert[l], b_router_expert[l])
        ys = _moe_forward(hn, order, counts, w_gate, w_up, w_down, l, n_tokens)
    return _final(h, ys, gate, norm_final).reshape(batch, seq, D_MODEL)
```

```python
import functools

import jax
import jax.numpy as jnp
from jax import lax
from jax.experimental import pallas as pl
from jax.experimental.pallas import tpu as pltpu

F32 = jnp.float32
BF16 = jnp.bfloat16

D_MODEL = 2048
N_Q_HEADS = 8
N_KV_HEADS = 2
Q_PER_KV = N_Q_HEADS // N_KV_HEADS
HEAD_DIM = 128
WINDOW = 128
WBLK = 128
GLA_HEADS = 4
GLA_DK = 128
GLA_DV = 256
GLA_LOWRANK = 16
GLA_TAU = 16.0
GLA_CHUNK = 64
ATT_Q = N_Q_HEADS * HEAD_DIM
ATT_KV = N_KV_HEADS * HEAD_DIM
GLA_QK = GLA_HEADS * GLA_DK
GLA_V = GLA_HEADS * GLA_DV
MAIN_COLS = ATT_Q + 2 * ATT_KV + 2 * GLA_QK + 2 * GLA_V
COL_KA = ATT_Q
COL_VA = COL_KA + ATT_KV
COL_QG = COL_VA + ATT_KV
COL_KG = COL_QG + GLA_QK
COL_VG = COL_KG + GLA_QK
COL_RG = COL_VG + GLA_V
X_HEADS = 4
X_HEAD_DIM = D_MODEL // X_HEADS
N_GROUPS = 4
EXPERTS_PER_GROUP = 8
N_EXPERTS = N_GROUPS * EXPERTS_PER_GROUP
TOP_K = 2
D_FF_EXPERT = D_MODEL // 2
MOE_BLOCK = 256
RMS_EPS = 1e-6
NEG_INF = -1e30

LANES = 128
VMEM_LIMIT = 56 * 1024 * 1024
W_STAGE_ROWS = 128


def _params(sem):
    return pltpu.CompilerParams(dimension_semantics=sem, vmem_limit_bytes=VMEM_LIMIT)


def _nt(a, b):
    return lax.dot_general(a, b, (((1,), (1,)), ((), ())), preferred_element_type=F32)


def _tn(a, b):
    return lax.dot_general(a, b, (((0,), (0,)), ((), ())), preferred_element_type=F32)


def _rms(x, gain):
    return x * lax.rsqrt(jnp.mean(x * x, axis=-1, keepdims=True) + RMS_EPS) * gain


def _load_weight_bf16(w_hbm, layer, wb, stage, sem, k_rows, n_cols):
    n_chunks = k_rows // W_STAGE_ROWS

    def copy(c):
        return pltpu.make_async_copy(
            w_hbm.at[layer, pl.ds(c * W_STAGE_ROWS, W_STAGE_ROWS), pl.ds(0, n_cols)],
            stage.at[c % 2], sem.at[c % 2])

    copy(0).start()
    for c in range(n_chunks):
        if c + 1 < n_chunks:
            copy(c + 1).start()
        copy(c).wait()
        wb[c * W_STAGE_ROWS:(c + 1) * W_STAGE_ROWS, :] = stage[c % 2].astype(BF16)


def _dense_body(*refs, part_widths, has_norm, has_extra, has_res, n_cols, n_chunk, layer):
    it = iter(refs)
    x_refs = [next(it) for _ in part_widths]
    g_ref = next(it) if has_norm else None
    w_hbm = next(it)
    ew_ref = next(it) if has_extra else None
    res_ref = next(it) if has_res else None
    o_ref = next(it)
    eo_ref = next(it) if has_extra else None
    wb, stage, sem = next(it), next(it), next(it)
    u_ref = next(it) if has_norm else None
    k_rows = sum(part_widths)

    @pl.when(pl.program_id(0) == 0)
    def _():
        _load_weight_bf16(w_hbm, layer, wb, stage, sem, k_rows, n_cols)

    if has_norm:
        u_ref[...] = _rms(x_refs[0][...], g_ref[...]).astype(BF16)
        lhs = [(u_ref, 0, k_rows)]
    else:
        lhs, off = [], 0
        for r, kw in zip(x_refs, part_widths):
            lhs.append((r, off, kw))
            off += kw
    for n0 in range(0, n_cols, n_chunk):
        acc = None
        for r, off, kw in lhs:
            d = jnp.dot(r[...], wb[off:off + kw, n0:n0 + n_chunk], preferred_element_type=F32)
            acc = d if acc is None else acc + d
        if has_res:
            acc = acc + res_ref[:, n0:n0 + n_chunk]
        o_ref[:, n0:n0 + n_chunk] = acc.astype(o_ref.dtype)
    if has_extra:
        eo_ref[...] = jnp.dot(u_ref[...], ew_ref[...], preferred_element_type=F32)


def _dense(xs, w, layer, *, n_cols, out_dtype, gain=None, extra_w=None, res=None, tm=512, n_chunk=512, name):
    rows = xs[0].shape[0]
    part_widths = tuple(x.shape[1] for x in xs)
    k_rows = sum(part_widths)
    has_norm, has_extra, has_res = gain is not None, extra_w is not None, res is not None
    row_spec = lambda width: pl.BlockSpec((tm, width), lambda i: (i, 0))
    full_spec = lambda a: pl.BlockSpec(a.shape, lambda i: (0, 0))
    args, in_specs = list(xs), [row_spec(kw) for kw in part_widths]
    if has_norm:
        args.append(gain.reshape(1, k_rows).astype(F32))
        in_specs.append(full_spec(args[-1]))
    args.append(w)
    in_specs.append(pl.BlockSpec(memory_space=pl.ANY))
    if has_extra:
        args.append(extra_w)
        in_specs.append(full_spec(extra_w))
    if has_res:
        args.append(res)
        in_specs.append(row_spec(n_cols))
    out_shape = [jax.ShapeDtypeStruct((rows, n_cols), out_dtype)]
    out_specs = [row_spec(n_cols)]
    if has_extra:
        out_shape.append(jax.ShapeDtypeStruct((rows, extra_w.shape[1]), F32))
        out_specs.append(row_spec(extra_w.shape[1]))
    scratch = [pltpu.VMEM((k_rows, n_cols), BF16),
               pltpu.VMEM((2, W_STAGE_ROWS, n_cols), F32),
               pltpu.SemaphoreType.DMA((2,))]
    if has_norm:
        scratch.append(pltpu.VMEM((tm, k_rows), BF16))
    body = functools.partial(_dense_body, part_widths=part_widths, has_norm=has_norm, has_extra=has_extra,
                             has_res=has_res, n_cols=n_cols, n_chunk=n_chunk, layer=layer)
    outs = pl.pallas_call(
        body, grid=(rows // tm,), in_specs=in_specs, out_specs=out_specs, out_shape=out_shape,
        scratch_shapes=scratch, compiler_params=_params(("arbitrary",)), name=name)(*args)
    return outs if has_extra else outs[0]


ATT_TQ = 512


def _winattn_body(sink_ref, q_ref, kp_ref, km_ref, kn_ref, vp_ref, vm_ref, vn_ref, g_ref, o_ref,
                  kcat, vcat, obuf, *, seq):
    s0 = pl.program_id(1) * ATT_TQ
    kcat[0:WBLK, :] = kp_ref[0]
    kcat[WBLK:WBLK + ATT_TQ, :] = km_ref[0]
    kcat[WBLK + ATT_TQ:, :] = kn_ref[0]
    vcat[0:WBLK, :] = vp_ref[0]
    vcat[WBLK:WBLK + ATT_TQ, :] = vm_ref[0]
    vcat[WBLK + ATT_TQ:, :] = vn_ref[0]
    qi = lax.broadcasted_iota(jnp.int32, (WBLK, 3 * WBLK), 0) + WBLK
    ki = lax.broadcasted_iota(jnp.int32, (WBLK, 3 * WBLK), 1)
    dist_i = jnp.abs(ki - qi)
    neg_dist = jnp.where(dist_i <= WINDOW, -dist_i.astype(F32), NEG_INF)
    scale = HEAD_DIM ** -0.5
    for qb in range(ATT_TQ // WBLK):
        kabs = s0 + (qb - 1) * WBLK + ki
        bias_unit = jnp.where((kabs >= 0) & (kabs < seq), neg_dist, NEG_INF)
        head_cols = lambda j: slice(j * HEAD_DIM, (j + 1) * HEAD_DIM)
        kv_cols = lambda j: head_cols(j // Q_PER_KV)
        key_rows = slice(qb * WBLK, (qb + 3) * WBLK)
        scores = []
        for j in range(N_Q_HEADS):
            slope = 2.0 ** (-8.0 * (j + 1) / N_Q_HEADS)
            q = q_ref[0, qb * WBLK:(qb + 1) * WBLK, head_cols(j)]
            scores.append(_nt(q, kcat[key_rows, kv_cols(j)]) * scale + slope * bias_unit)
        probs, denoms = [], []
        for j in range(N_Q_HEADS):
            sink = sink_ref[j]
            m = jnp.maximum(jnp.max(scores[j], axis=-1, keepdims=True), sink)
            p = jnp.exp(scores[j] - m)
            denoms.append(jnp.sum(p, axis=-1, keepdims=True) + jnp.exp(sink - m))
            probs.append(p.astype(BF16))
        for j in range(N_Q_HEADS):
            pv = jnp.dot(probs[j], vcat[key_rows, kv_cols(j)], preferred_element_type=F32)
            obuf[qb, :, head_cols(j)] = pv / denoms[j]
        o_ref[0, qb * WBLK:(qb + 1) * WBLK, :] = _rms(obuf[qb], g_ref[...]).astype(o_ref.dtype)


def _window_attention(proj, sink_logit, gain, batch, seq):
    p3 = proj.reshape(batch, seq, MAIN_COLS)
    nq = seq // ATT_TQ
    per = ATT_TQ // WBLK
    last = seq // WBLK - 1
    main = lambda col: pl.BlockSpec((1, ATT_TQ, ATT_KV), lambda b, i: (b, i, col))
    prev = lambda col: pl.BlockSpec((1, WBLK, ATT_KV), lambda b, i: (b, jnp.maximum(i * per - 1, 0), col))
    nxt = lambda col: pl.BlockSpec((1, WBLK, ATT_KV), lambda b, i: (b, jnp.minimum(i * per + per, last), col))
    ck, cv = COL_KA // ATT_KV, COL_VA // ATT_KV
    return pl.pallas_call(
        functools.partial(_winattn_body, seq=seq),
        grid=(batch, nq),
        in_specs=[pl.BlockSpec(memory_space=pltpu.SMEM),
                  pl.BlockSpec((1, ATT_TQ, ATT_Q), lambda b, i: (b, i, 0)),
                  prev(ck), main(ck), nxt(ck), prev(cv), main(cv), nxt(cv),
                  pl.BlockSpec((1, ATT_Q), lambda b, i: (0, 0))],
        out_specs=pl.BlockSpec((1, ATT_TQ, ATT_Q), lambda b, i: (b, i, 0)),
        out_shape=jax.ShapeDtypeStruct((batch, seq, ATT_Q), BF16),
        scratch_shapes=[pltpu.VMEM((ATT_TQ + 2 * WBLK, ATT_KV), BF16),
                        pltpu.VMEM((ATT_TQ + 2 * WBLK, ATT_KV), BF16),
                        pltpu.VMEM((ATT_TQ // WBLK, WBLK, ATT_Q), F32)],
        compiler_params=_params(("parallel", "parallel")), name="window_attention",
    )(sink_logit.astype(F32), p3, p3, p3, p3, p3, p3, p3, gain.reshape(1, ATT_Q).astype(F32))


GLA_SCAN_ROWS = 256
GLA_SCAN_UNROLL = 2
GLA_EPI_ROWS = 512
GLA_UNROLL = 4


def _split3(x):
    hi = x.astype(BF16)
    r1 = x - hi.astype(F32)
    mid = r1.astype(BF16)
    lo = (r1 - mid.astype(F32)).astype(BF16)
    return hi, mid, lo


def _gla_body(q_ref, k_ref, v_ref, r_ref, lr_ref, wf_ref, wb_ref, bf_ref, bb_ref, gn_ref, o_ref,
              cum_f, cum_b, acc, st_f, st_b, *, seq):
    C = GLA_CHUNK
    n_chunks = seq // C
    ri = lax.broadcasted_iota(jnp.int32, (GLA_SCAN_ROWS, GLA_SCAN_ROWS), 0)
    ci = lax.broadcasted_iota(jnp.int32, (GLA_SCAN_ROWS, GLA_SCAN_ROWS), 1)
    same = (ri // C) == (ci // C)
    tri_f = jnp.where(same & (ci <= ri), 1.0, 0.0).astype(BF16)
    tri_b = jnp.where(same & (ci >= ri), 1.0, 0.0).astype(BF16)

    def scan_body(t, carry):
        row_sets = [pl.ds(pl.multiple_of((t * GLA_SCAN_UNROLL + u) * GLA_SCAN_ROWS, GLA_SCAN_ROWS), GLA_SCAN_ROWS)
                    for u in range(GLA_SCAN_UNROLL)]
        lrs = [lr_ref[0, rows, :].astype(BF16) for rows in row_sets]
        zs = [jnp.dot(lr, w_ref[...], preferred_element_type=F32) + b_ref[...]
              for lr in lrs for w_ref, b_ref in ((wf_ref, bf_ref), (wb_ref, bb_ref))]
        gs = [_split3((jnp.minimum(z, 0.0) - jnp.log1p(jnp.exp(-jnp.abs(z)))) / GLA_TAU) for z in zs]
        for n, g3 in enumerate(gs):
            tri, dst = ((tri_f, cum_f), (tri_b, cum_b))[n % 2]
            dst[row_sets[n // 2], :] = sum(jnp.dot(tri, part, preferred_element_type=F32) for part in g3)
        return carry

    lax.fori_loop(0, seq // (GLA_SCAN_ROWS * GLA_SCAN_UNROLL), scan_body, 0)

    acc[...] = jnp.zeros_like(acc)
    st_f[...] = jnp.zeros_like(st_f)
    st_b[...] = jnp.zeros_like(st_b)
    rr = lax.broadcasted_iota(jnp.int32, (C, C), 0)
    cc = lax.broadcasted_iota(jnp.int32, (C, C), 1)
    scale = GLA_DK ** -0.5

    def chunk_body(i, carry):
        jobs = []
        for u in range(GLA_UNROLL):
            c = i * GLA_UNROLL + u
            jobs += [(c, cum_f, True), (n_chunks - 1 - c, cum_b, False)]
        prep = []
        for c, cum, forward in jobs:
            rows = pl.ds(pl.multiple_of(c * C, C), C)
            b = cum[rows, :]
            b_end = b[C - 1:C, :] if forward else b[0:1, :]
            q = q_ref[0, rows, :].astype(F32) * scale
            k = k_ref[0, rows, :].astype(F32)
            v = v_ref[0, rows, :]
            q_dec = (q * jnp.exp(b)).astype(BF16)
            k_inc = (k * jnp.exp(-b)).astype(BF16)
            k_dec = (k * jnp.exp(b_end - b)).astype(BF16)
            prep.append((rows, v, q_dec, k_inc, k_dec, jnp.exp(b_end)))
        attn = [_nt(q_dec, k_inc) for _, _, q_dec, k_inc, _, _ in prep]
        kv_t = [_tn(v, k_dec) for _, v, _, _, k_dec, _ in prep]
        state_t = {True: st_f[...], False: st_b[...]}
        o_inter = []
        for (_, _, forward), (_, _, q_dec, _, _, decay), kv in zip(jobs, prep, kv_t):
            o_inter.append(_nt(q_dec, state_t[forward].astype(BF16)))
            state_t[forward] = state_t[forward] * decay + kv
        st_f[...] = state_t[True]
        st_b[...] = state_t[False]
        for (_, _, forward), (rows, v, _, _, _, _), a, oi in zip(jobs, prep, attn, o_inter):
            a = jnp.where((cc <= rr) if forward else (cc >= rr), a, 0.0)
            o = jnp.dot(a.astype(BF16), v, preferred_element_type=F32) + oi
            acc[rows, :] = acc[rows, :] + o
        return carry

    lax.fori_loop(0, n_chunks // GLA_UNROLL, chunk_body, 0)

    def epi_body(t, carry):
        rows = pl.ds(pl.multiple_of(t * GLA_EPI_ROWS, GLA_EPI_ROWS), GLA_EPI_ROWS)
        r = r_ref[0, rows, :].astype(F32)
        o_ref[0, rows, :] = (_rms(acc[rows, :], gn_ref[...]) * (r * jax.nn.sigmoid(r))).astype(o_ref.dtype)
        return carry

    lax.fori_loop(0, seq // GLA_EPI_ROWS, epi_body, 0)


def _gla(proj, lr, w_gf, b_gf, w_gb, b_gb, gain, batch, seq):
    p3 = proj.reshape(batch, seq, MAIN_COLS)
    lr3 = lr.reshape(batch, seq, LANES)
    wf = jnp.zeros((LANES, GLA_QK), F32).at[:GLA_LOWRANK].set(w_gf).astype(BF16)
    wb = jnp.zeros((LANES, GLA_QK), F32).at[GLA_LOWRANK:2 * GLA_LOWRANK].set(w_gb).astype(BF16)
    seq_blk = lambda width, col0: pl.BlockSpec((1, seq, width), lambda b, h: (b, 0, col0 // width + h))
    head_w = pl.BlockSpec((LANES, GLA_DK), lambda b, h: (0, h))
    head_b = pl.BlockSpec((1, GLA_DK), lambda b, h: (0, h))
    return pl.pallas_call(
        functools.partial(_gla_body, seq=seq),
        grid=(batch, GLA_HEADS),
        in_specs=[seq_blk(GLA_DK, COL_QG), seq_blk(GLA_DK, COL_KG), seq_blk(GLA_DV, COL_VG),
                  seq_blk(GLA_DV, COL_RG),
                  pl.BlockSpec((1, seq, LANES), lambda b, h: (b, 0, 0)),
                  head_w, head_w, head_b, head_b,
                  pl.BlockSpec((1, GLA_DV), lambda b, h: (0, 0))],
        out_specs=pl.BlockSpec((1, seq, GLA_DV), lambda b, h: (b, 0, h)),
        out_shape=jax.ShapeDtypeStruct((batch, seq, GLA_V), BF16),
        scratch_shapes=[pltpu.VMEM((seq, GLA_DK), F32), pltpu.VMEM((seq, GLA_DK), F32),
                        pltpu.VMEM((seq, GLA_DV), F32),
                        pltpu.VMEM((GLA_DV, GLA_DK), F32), pltpu.VMEM((GLA_DV, GLA_DK), F32)],
        compiler_params=_params(("parallel", "parallel")), name="gla",
    )(p3, p3, p3, p3, lr3, wf, wb, b_gf.reshape(1, GLA_QK).astype(F32), b_gb.reshape(1, GLA_QK).astype(F32),
      gain.reshape(1, GLA_DV).astype(F32))


XATT_TQ = 512


def _xattn_body(q_ref, k_ref, v_ref, o_ref):
    scale = X_HEAD_DIM ** -0.5
    for h in range(X_HEADS):
        cols = slice(h * X_HEAD_DIM, (h + 1) * X_HEAD_DIM)
        s = _nt(q_ref[0, :, cols], k_ref[0, :, cols]) * scale
        m = jnp.max(s, axis=-1, keepdims=True)
        p = jnp.exp(s - m)
        denom = jnp.sum(p, axis=-1, keepdims=True)
        pv = jnp.dot(p.astype(BF16), v_ref[0, :, cols], preferred_element_type=F32)
        o_ref[0, :, cols] = (pv / denom).astype(o_ref.dtype)


def _cross_attention(q, kx, vx, batch, seq, mem_len):
    q3 = q.reshape(batch, seq, D_MODEL)
    k3 = kx.reshape(batch, mem_len, D_MODEL)
    v3 = vx.reshape(batch, mem_len, D_MODEL)
    mem_spec = pl.BlockSpec((1, mem_len, D_MODEL), lambda b, i: (b, 0, 0))
    tile = pl.BlockSpec((1, XATT_TQ, D_MODEL), lambda b, i: (b, i, 0))
    out = pl.pallas_call(
        _xattn_body, grid=(batch, seq // XATT_TQ), in_specs=[tile, mem_spec, mem_spec], out_specs=tile,
        out_shape=jax.ShapeDtypeStruct((batch, seq, D_MODEL), BF16),
        compiler_params=_params(("parallel", "parallel")), name="cross_attention")(q3, k3, v3)
    return out.reshape(batch * seq, D_MODEL)


ROUTER_TM = 512


HALF = D_MODEL // 2


def _pack_halves(x):
    return pltpu.pack_elementwise([x[:, :HALF], x[:, HALF:]], packed_dtype=BF16)


def _unpack_halves(words):
    return [pltpu.unpack_elementwise(words, index=i, packed_dtype=BF16, unpacked_dtype=F32) for i in range(2)]


ID_SPLIT = 32


def _router_body(h_ref, g_ref, w_ref, b_ref, hn_ref, gate_ref, order_ref, count_ref, *, n_tokens):
    hn = _rms(h_ref[...], g_ref[...])
    hn_ref[...] = _pack_halves(hn)
    logits = jnp.dot(hn.astype(BF16), w_ref[...], preferred_element_type=F32) + b_ref[...]
    lane = lax.broadcasted_iota(jnp.int32, logits.shape, 1).astype(F32)
    ninf = -jnp.inf
    first = lambda hit: jnp.min(jnp.where(hit, lane, float(LANES)), axis=-1, keepdims=True)
    in_groups = lane < N_GROUPS
    gl = jnp.where(in_groups, logits, ninf)
    gmax = jnp.max(gl, axis=-1, keepdims=True)
    g_idx = first(gl == gmax)
    p_group = 1.0 / jnp.sum(jnp.where(in_groups, jnp.exp(logits - gmax), 0.0), axis=-1, keepdims=True)
    lo = N_GROUPS + EXPERTS_PER_GROUP * g_idx
    el = jnp.where((lane >= lo) & (lane < lo + EXPERTS_PER_GROUP), logits, ninf)
    e1 = jnp.max(el, axis=-1, keepdims=True)
    i1 = first(el == e1)
    el2 = jnp.where(lane == i1, ninf, el)
    e2 = jnp.max(el2, axis=-1, keepdims=True)
    i2 = first(el2 == e2)
    t = jnp.exp(e2 - e1)
    w1 = p_group / (1.0 + t)
    w2 = p_group * t / (1.0 + t)
    gate_ref[...] = jnp.where(lane == 0, w1, jnp.where(lane == 1, w2, 0.0))

    tm = logits.shape[0]
    hit0, hit1 = lane == i1 - N_GROUPS, lane == i2 - N_GROUPS
    member = jnp.where(hit0 | hit1, 1.0, 0.0)
    member_b = member.astype(BF16)
    earlier = (lax.broadcasted_iota(jnp.int32, (tm, tm), 1) < lax.broadcasted_iota(jnp.int32, (tm, tm), 0))
    rank = jnp.dot(jnp.where(earlier, 1.0, 0.0).astype(BF16), member_b, preferred_element_type=F32)
    lower = (lax.broadcasted_iota(jnp.int32, (LANES, LANES), 0) < lax.broadcasted_iota(jnp.int32, (LANES, LANES), 1))
    run_start = jnp.sum(jnp.dot(member_b, jnp.where(lower, 1.0, 0.0).astype(BF16), preferred_element_type=F32),
                        axis=0, keepdims=True)
    pos = rank + run_start
    positions = [jnp.sum(jnp.where(hit, pos, 0.0), axis=-1, keepdims=True) for hit in (hit0, hit1)]
    out_lane = lax.broadcasted_iota(jnp.int32, (tm, TOP_K * tm), 1).astype(F32)
    digit_row = lax.broadcasted_iota(jnp.int32, (8, tm), 0)
    local_tok = lax.broadcasted_iota(jnp.int32, (8, tm), 1)
    digits = None
    for s, p in enumerate(positions):
        a = s * tm + local_tok
        lhs = jnp.where(digit_row == 0, a // ID_SPLIT, jnp.where(digit_row == 1, a % ID_SPLIT, 0))
        part = jnp.dot(lhs.astype(F32).astype(BF16), jnp.where(out_lane == p, 1.0, 0.0).astype(BF16),
                       preferred_element_type=F32)
        digits = part if digits is None else digits + part
    a_sorted = (digits[0:1] * ID_SPLIT + digits[1:2]).astype(jnp.int32)
    tok = pl.program_id(0) * tm + a_sorted % tm
    row8 = lax.broadcasted_iota(jnp.int32, (8, TOP_K * tm), 0)
    order_ref[0] = jnp.where(row8 == 0, tok, jnp.where(row8 == 1, (a_sorted // tm) * n_tokens + tok, 0))
    count_ref[0] = jnp.broadcast_to(jnp.sum(member, axis=0, keepdims=True), (8, LANES)).astype(jnp.int32)


def _router(h, gain, w_rg, b_rg, w_re, b_re):
    rows = h.shape[0]
    n_log = N_GROUPS + N_EXPERTS
    w = jnp.zeros((D_MODEL, LANES), F32).at[:, :N_GROUPS].set(w_rg).at[:, N_GROUPS:n_log].set(w_re).astype(BF16)
    b = jnp.zeros((1, LANES), F32).at[0, :N_GROUPS].set(b_rg).at[0, N_GROUPS:n_log].set(b_re)
    tile = lambda width: pl.BlockSpec((ROUTER_TM, width), lambda i: (i, 0))
    full = lambda a: pl.BlockSpec(a.shape, lambda i: (0, 0))
    g2 = gain.reshape(1, D_MODEL).astype(F32)
    n_tiles = rows // ROUTER_TM
    per_tile = lambda width: pl.BlockSpec((1, 8, width), lambda i: (i, 0, 0))
    return pl.pallas_call(
        functools.partial(_router_body, n_tokens=rows), grid=(n_tiles,),
        in_specs=[tile(D_MODEL), full(g2), full(w), full(b)],
        out_specs=[tile(HALF), tile(LANES), per_tile(TOP_K * ROUTER_TM), per_tile(LANES)],
        out_shape=[jax.ShapeDtypeStruct((rows, HALF), jnp.uint32),
                   jax.ShapeDtypeStruct((rows, LANES), F32),
                   jax.ShapeDtypeStruct((n_tiles, 8, TOP_K * ROUTER_TM), jnp.int32),
                   jax.ShapeDtypeStruct((n_tiles, 8, LANES), jnp.int32)],
        compiler_params=_params(("parallel",)), name="moe_router")(h, g2, w, b)


W_CAST_ROWS = 256


def _cast_weight(src_ref, dst_ref):
    def body(c, carry):
        rows = pl.ds(pl.multiple_of(c * W_CAST_ROWS, W_CAST_ROWS), W_CAST_ROWS)
        dst_ref[rows, :] = src_ref[rows, :].astype(BF16)
        return carry
    lax.fori_loop(0, src_ref.shape[0] // W_CAST_ROWS, body, 0)


def _expert_changed(blk_e, blk):
    prev = blk_e[jnp.maximum(blk - 1, 0)]
    return (blk == 0) | (blk_e[blk] != prev)


WEIGHT_DMA_PRIORITY = 1


def _switch_expert(blk_e, nxt_e, blk, layer, w_hbms, stages, dsts, sem):
    def copies(e):
        return [pltpu.make_async_copy(w.at[layer, e], st, sem.at[i])
                for i, (w, st) in enumerate(zip(w_hbms, stages))]

    @pl.when(blk == 0)
    def _():
        for cp in copies(blk_e[0]):
            cp.start(priority=WEIGHT_DMA_PRIORITY)

    @pl.when(_expert_changed(blk_e, blk))
    def _():
        for cp, st, dst in zip(copies(blk_e[blk]), stages, dsts):
            cp.wait()
            _cast_weight(st, dst)

        @pl.when(nxt_e[blk] >= 0)
        def _():
            for cp in copies(nxt_e[blk]):
                cp.start(priority=WEIGHT_DMA_PRIORITY)


GATHER_AHEAD = 3
GATHER_SLOTS = GATHER_AHEAD + 1
MOE_CHUNKS = 4


SCATTER_SLOTS = 3


def _experts_body(blk_e, n_real, nxt_e, *refs, n_assign, layer):
    tok_refs = refs[:GATHER_AHEAD + 1]
    (dst_ref, dst_prev_ref, hn_hbm, wg_hbm, wu_hbm, wd_hbm, out_hbm, xs, xb, hid, ys,
     wg_stage, wu_stage, wd_stage, wgb, wub, wdb, gsem, ssem, wsem) = refs[GATHER_AHEAD + 1:]
    blk = pl.program_id(0)
    dummy_slot = SCATTER_SLOTS - 1

    def gather_row(idx_ref, slot, r):
        return pltpu.make_async_copy(hn_hbm.at[pl.ds(idx_ref[0, 0, r], 1), :], xs.at[slot, pl.ds(r, 1), :],
                                     gsem.at[slot])

    def scatter_row(idx_ref, slot, r):
        return pltpu.make_async_copy(ys.at[slot, pl.ds(r, 1), :], out_hbm.at[pl.ds(idx_ref[0, 0, r], 1), :],
                                     ssem.at[slot])

    def wait_gather(slot):
        pltpu.make_async_copy(hn_hbm.at[pl.ds(0, MOE_BLOCK), :], xs.at[slot], gsem.at[slot]).wait()

    def wait_scatter(slot):
        pltpu.make_async_copy(ys.at[slot], out_hbm.at[pl.ds(0, MOE_BLOCK), :], ssem.at[slot]).wait()

    @pl.when(blk == 0)
    def _():
        def body(r, carry):
            for s in range(GATHER_AHEAD):
                gather_row(tok_refs[s], s, r).start()
            return carry
        lax.fori_loop(0, MOE_BLOCK, body, 0)
        ys[dummy_slot] = jnp.zeros((MOE_BLOCK, HALF), jnp.uint32)
        for s in range(SCATTER_SLOTS):
            trash = pltpu.make_async_copy(
                ys.at[dummy_slot], out_hbm.at[pl.ds(n_assign + s * MOE_BLOCK, MOE_BLOCK), :], ssem.at[dummy_slot])
            trash.start()
            trash.wait()

    @pl.when(blk < n_real[0])
    def _():
        in_slot = blk % GATHER_SLOTS
        ahead_slot = (blk + GATHER_AHEAD) % GATHER_SLOTS
        out_slot = blk % SCATTER_SLOTS
        prev_slot = (blk + SCATTER_SLOTS - 1) % SCATTER_SLOTS

        _switch_expert(blk_e, nxt_e, blk, layer, (wg_hbm, wu_hbm, wd_hbm), (wg_stage, wu_stage, wd_stage),
                       (wgb, wub, wdb), wsem)
        wait_gather(in_slot)

        @pl.when(blk >= SCATTER_SLOTS - 1)
        def _():
            wait_scatter(out_slot)

        lo, hi = _unpack_halves(xs[in_slot])
        xb[:, :HALF] = lo.astype(BF16)
        xb[:, HALF:] = hi.astype(BF16)
        for r in range(MOE_BLOCK):
            scatter_row(dst_prev_ref, prev_slot, r).start(priority=r % 2)
            gather_row(tok_refs[GATHER_AHEAD], ahead_slot, r).start()
        cw = D_FF_EXPERT // MOE_CHUNKS
        for c in range(MOE_CHUNKS):
            cols = slice(c * cw, (c + 1) * cw)
            a = jnp.dot(xb[...], wgb[:, cols], preferred_element_type=F32)
            u = jnp.dot(xb[...], wub[:, cols], preferred_element_type=F32)
            hid[:, cols] = (a * jax.nn.sigmoid(a) * u).astype(BF16)
        cw = HALF // MOE_CHUNKS
        for c in range(MOE_CHUNKS):
            cols = slice(c * cw, (c + 1) * cw)
            hi_cols = slice(HALF + c * cw, HALF + (c + 1) * cw)
            ys[out_slot, :, cols] = pltpu.pack_elementwise(
                [jnp.dot(hid[...], wdb[:, cols], preferred_element_type=F32),
                 jnp.dot(hid[...], wdb[:, hi_cols], preferred_element_type=F32)], packed_dtype=BF16)

        @pl.when(blk == n_real[0] - 1)
        def _():
            def body(r, carry):
                scatter_row(dst_ref, out_slot, r).start()
                return carry
            lax.fori_loop(0, MOE_BLOCK, body, 0)
            for s in range(1, GATHER_SLOTS):
                wait_gather((blk + s) % GATHER_SLOTS)
            for s in range(SCATTER_SLOTS):
                wait_scatter(s)


def _moe_forward(hn, order, counts, w_gate, w_up, w_down, layer, n_tokens):
    A = n_tokens * TOP_K
    out_rows = A + SCATTER_SLOTS * MOE_BLOCK
    R = A + N_EXPERTS * MOE_BLOCK
    n_blk = R // MOE_BLOCK
    i32 = jnp.int32
    n_tiles, per_tile = order.shape[0], order.shape[2]
    experts = jnp.arange(N_EXPERTS, dtype=i32)
    n = counts[:, 0, :N_EXPERTS]
    total = jnp.sum(n, axis=0)
    padded = ((total + MOE_BLOCK - 1) // MOE_BLOCK) * MOE_BLOCK
    pend = jnp.cumsum(padded)
    pstart = pend - padded
    n_real = (pend[-1] // MOE_BLOCK).astype(i32).reshape(1)
    blk = jnp.arange(n_blk, dtype=i32)
    blk_e = jnp.minimum(jnp.sum((pend[None, :] <= (blk * MOE_BLOCK)[:, None]).astype(i32), axis=1), N_EXPERTS - 1)
    is_e = (blk_e[:, None] == experts[None, :]).astype(i32)
    of_block = lambda per_expert: jnp.sum(is_e * per_expert[None, :], axis=1)
    cum_incl = jnp.cumsum(n, axis=0)
    run_shift = (jnp.cumsum(n, axis=1) - n) - (cum_incl - n)
    per_tile_of_block = lambda m: jnp.sum(is_e[:, None, :] * m[None, :, :], axis=2)
    cum_b, shift_b = per_tile_of_block(cum_incl), per_tile_of_block(run_shift)
    in_blk = jnp.arange(MOE_BLOCK, dtype=i32)
    k = (blk * MOE_BLOCK - of_block(pstart))[:, None] + in_blk[None, :]
    tile_of = jnp.minimum(jnp.sum((cum_b[:, None, :] <= k[:, :, None]).astype(i32), axis=2), n_tiles - 1)
    is_tile = (tile_of[:, :, None] == jnp.arange(n_tiles, dtype=i32)[None, None, :]).astype(i32)
    entry = tile_of * per_tile + k + jnp.sum(is_tile * shift_b[:, None, :], axis=2)
    pad_entry = A + (blk % SCATTER_SLOTS)[:, None] * MOE_BLOCK + in_blk[None, :]
    entry = jnp.where(k < of_block(total)[:, None], entry, pad_entry)
    n_pad = SCATTER_SLOTS * MOE_BLOCK
    tok_table = jnp.concatenate([order[:, 0, :].reshape(A), jnp.zeros((n_pad,), i32)])
    dst_table = jnp.concatenate([order[:, 1, :].reshape(A), A + jnp.arange(n_pad, dtype=i32)])
    row_tok = tok_table[entry].reshape(n_blk, 1, MOE_BLOCK)
    row_dst = dst_table[entry].reshape(n_blk, 1, MOE_BLOCK)
    row_dst_prev = jnp.concatenate(
        [(A + (SCATTER_SLOTS - 1) * MOE_BLOCK + in_blk).reshape(1, 1, MOE_BLOCK), row_dst[:-1]], axis=0)

    run_end = pend[blk_e] // MOE_BLOCK
    nxt_e = jnp.where(run_end < n_real[0], blk_e[jnp.minimum(run_end, n_blk - 1)], -1).astype(jnp.int32)

    clamp = lambda i, nr: jnp.minimum(i, nr[0] - 1)
    idx_spec = lambda shift: pl.BlockSpec((1, 1, MOE_BLOCK), lambda i, be, nr, nx: (clamp(i + shift, nr), 0, 0),
                                          memory_space=pltpu.SMEM)
    hbm = pl.BlockSpec(memory_space=pl.ANY)
    up_shape, down_shape = (D_MODEL, D_FF_EXPERT), (D_FF_EXPERT, D_MODEL)
    return pl.pallas_call(
        functools.partial(_experts_body, n_assign=A, layer=layer),
        grid_spec=pltpu.PrefetchScalarGridSpec(
            num_scalar_prefetch=3, grid=(n_blk,),
            in_specs=[idx_spec(s) for s in range(GATHER_AHEAD + 1)] + [idx_spec(0), idx_spec(0)] + [hbm] * 4,
            out_specs=hbm,
            scratch_shapes=[pltpu.VMEM((GATHER_SLOTS, MOE_BLOCK, HALF), jnp.uint32),
                            pltpu.VMEM((MOE_BLOCK, D_MODEL), BF16),
                            pltpu.VMEM((MOE_BLOCK, D_FF_EXPERT), BF16),
                            pltpu.VMEM((SCATTER_SLOTS, MOE_BLOCK, HALF), jnp.uint32),
                            pltpu.VMEM(up_shape, F32), pltpu.VMEM(up_shape, F32), pltpu.VMEM(down_shape, F32),
                            pltpu.VMEM(up_shape, BF16), pltpu.VMEM(up_shape, BF16), pltpu.VMEM(down_shape, BF16),
                            pltpu.SemaphoreType.DMA((GATHER_SLOTS,)), pltpu.SemaphoreType.DMA((SCATTER_SLOTS,)),
                            pltpu.SemaphoreType.DMA((3,))]),
        out_shape=jax.ShapeDtypeStruct((out_rows, HALF), jnp.uint32),
        compiler_params=_params(("arbitrary",)), name="moe_experts",
    )(blk_e, n_real, nxt_e, *([row_tok] * (GATHER_AHEAD + 1)), row_dst, row_dst_prev, hn, w_gate, w_up, w_down)


FINAL_TM = 512


def _final_body(h_ref, y0_ref, y1_ref, gate_ref, g_ref, o_ref):
    gate = gate_ref[...]
    y0, y1 = _unpack_halves(y0_ref[...]), _unpack_halves(y1_ref[...])
    halves = [h_ref[:, i * HALF:(i + 1) * HALF] + gate[:, 0:1] * y0[i] + gate[:, 1:2] * y1[i] for i in range(2)]
    mean_sq = sum(jnp.sum(hh * hh, axis=-1, keepdims=True) for hh in halves) / D_MODEL
    inv = lax.rsqrt(mean_sq + RMS_EPS)
    for i, hh in enumerate(halves):
        o_ref[:, i * HALF:(i + 1) * HALF] = hh * inv * g_ref[:, i * HALF:(i + 1) * HALF]


def _final(h, ys, gate, gain):
    rows = h.shape[0]
    n_tiles = rows // FINAL_TM
    tile = lambda width: pl.BlockSpec((FINAL_TM, width), lambda i: (i, 0))
    g2 = gain.reshape(1, D_MODEL).astype(F32)
    return pl.pallas_call(
        _final_body, grid=(n_tiles,),
        in_specs=[tile(D_MODEL), tile(HALF), pl.BlockSpec((FINAL_TM, HALF), lambda i: (i + n_tiles, 0)),
                  tile(LANES), pl.BlockSpec((1, D_MODEL), lambda i: (0, 0))],
        out_specs=tile(D_MODEL), out_shape=jax.ShapeDtypeStruct((rows, D_MODEL), F32),
        compiler_params=_params(("parallel",)), name="combine_final_norm")(h, ys, ys, gate, g2)


def kernel(x, mem, norm_mix, w_in, attn_out_norm, sink_logit, w_gla_gf, b_gla_gf, w_gla_gb, b_gla_gb, gla_out_norm, w_out, norm_cross, norm_mem, w_cq, w_ck, w_cv, w_co, norm_ffn, w_router_group, b_router_group, w_router_expert, b_router_expert, w_gate, w_up, w_down, norm_final):
    batch, seq, _ = x.shape
    mem_len = mem.shape[1]
    n_tokens = batch * seq
    h = x.reshape(n_tokens, D_MODEL)
    memf = mem.reshape(batch * mem_len, D_MODEL)
    assert norm_mix.shape[0] == 1, "the combine step is fused with the final norm: single-layer stacks only"
    for l in range(norm_mix.shape[0]):
        w_lr = jnp.zeros((D_MODEL, LANES), F32).at[:, :2 * GLA_LOWRANK].set(w_in[l][:, MAIN_COLS:]).astype(BF16)
        proj, lr = _dense([h], w_in, l, n_cols=MAIN_COLS, out_dtype=BF16, gain=norm_mix[l], extra_w=w_lr,
                          name="in_proj")
        o_a = _window_attention(proj, sink_logit[l], attn_out_norm[l], batch, seq).reshape(n_tokens, ATT_Q)
        o_g = _gla(proj, lr, w_gla_gf[l], b_gla_gf[l], w_gla_gb[l], b_gla_gb[l], gla_out_norm[l],
                   batch, seq).reshape(n_tokens, GLA_V)
        h = _dense([o_a, o_g], w_out, l, n_cols=D_MODEL, out_dtype=F32, res=h, name="out_proj")
        kx = _dense([memf], w_ck, l, n_cols=D_MODEL, out_dtype=BF16, gain=norm_mem[l], name="mem_k_proj")
        vx = _dense([memf], w_cv, l, n_cols=D_MODEL, out_dtype=BF16, gain=norm_mem[l], name="mem_v_proj")
        q = _dense([h], w_cq, l, n_cols=D_MODEL, out_dtype=BF16, gain=norm_cross[l], name="cross_q_proj")
        o = _cross_attention(q, kx, vx, batch, seq, mem_len)
        h = _dense([o], w_co, l, n_cols=D_MODEL, out_dtype=F32, res=h, name="cross_out_proj")
        hn, gate, order, counts = _router(h, norm_ffn[l], w_router_group[l], b_router_group[l],
                                          w_router_expert[l], b_router_expert[l])
        ys = _moe_forward(hn, order, counts, w_gate, w_up, w_down, l, n_tokens)
    return _final(h, ys, gate, norm_final).reshape(batch, seq, D_MODEL)
```

```python
import functools

import jax
import jax.numpy as jnp
from jax import lax
from jax.experimental import pallas as pl
from jax.experimental.pallas import tpu as pltpu

F32 = jnp.float32
BF16 = jnp.bfloat16

D_MODEL = 2048
N_Q_HEADS = 8
N_KV_HEADS = 2
Q_PER_KV = N_Q_HEADS // N_KV_HEADS
HEAD_DIM = 128
WINDOW = 128
WBLK = 128
GLA_HEADS = 4
GLA_DK = 128
GLA_DV = 256
GLA_LOWRANK = 16
GLA_TAU = 16.0
GLA_CHUNK = 64
ATT_Q = N_Q_HEADS * HEAD_DIM
ATT_KV = N_KV_HEADS * HEAD_DIM
GLA_QK = GLA_HEADS * GLA_DK
GLA_V = GLA_HEADS * GLA_DV
MAIN_COLS = ATT_Q + 2 * ATT_KV + 2 * GLA_QK + 2 * GLA_V
COL_KA = ATT_Q
COL_VA = COL_KA + ATT_KV
COL_QG = COL_VA + ATT_KV
COL_KG = COL_QG + GLA_QK
COL_VG = COL_KG + GLA_QK
COL_RG = COL_VG + GLA_V
X_HEADS = 4
X_HEAD_DIM = D_MODEL // X_HEADS
N_GROUPS = 4
EXPERTS_PER_GROUP = 8
N_EXPERTS = N_GROUPS * EXPERTS_PER_GROUP
TOP_K = 2
D_FF_EXPERT = D_MODEL // 2
MOE_BLOCK = 256
RMS_EPS = 1e-6
NEG_INF = -1e30

LANES = 128
VMEM_LIMIT = 56 * 1024 * 1024
W_STAGE_ROWS = 128


def _params(sem):
    return pltpu.CompilerParams(dimension_semantics=sem, vmem_limit_bytes=VMEM_LIMIT)


def _nt(a, b):
    return lax.dot_general(a, b, (((1,), (1,)), ((), ())), preferred_element_type=F32)


def _tn(a, b):
    return lax.dot_general(a, b, (((0,), (0,)), ((), ())), preferred_element_type=F32)


def _rms(x, gain):
    return x * lax.rsqrt(jnp.mean(x * x, axis=-1, keepdims=True) + RMS_EPS) * gain


def _load_weight_bf16(w_hbm, layer, wb, stage, sem, k_rows, n_cols):
    n_chunks = k_rows // W_STAGE_ROWS

    def copy(c):
        return pltpu.make_async_copy(
            w_hbm.at[layer, pl.ds(c * W_STAGE_ROWS, W_STAGE_ROWS), pl.ds(0, n_cols)],
            stage.at[c % 2], sem.at[c % 2])

    copy(0).start()
    for c in range(n_chunks):
        if c + 1 < n_chunks:
            copy(c + 1).start()
        copy(c).wait()
        wb[c * W_STAGE_ROWS:(c + 1) * W_STAGE_ROWS, :] = stage[c % 2].astype(BF16)


def _dense_body(*refs, part_widths, has_norm, has_extra, has_res, n_cols, n_chunk, layer):
    it = iter(refs)
    x_refs = [next(it) for _ in part_widths]
    g_ref = next(it) if has_norm else None
    w_hbm = next(it)
    ew_ref = next(it) if has_extra else None
    res_ref = next(it) if has_res else None
    o_ref = next(it)
    eo_ref = next(it) if has_extra else None
    wb, stage, sem = next(it), next(it), next(it)
    u_ref = next(it) if has_norm else None
    k_rows = sum(part_widths)

    @pl.when(pl.program_id(0) == 0)
    def _():
        _load_weight_bf16(w_hbm, layer, wb, stage, sem, k_rows, n_cols)

    if has_norm:
        u_ref[...] = _rms(x_refs[0][...], g_ref[...]).astype(BF16)
        lhs = [(u_ref, 0, k_rows)]
    else:
        lhs, off = [], 0
        for r, kw in zip(x_refs, part_widths):
            lhs.append((r, off, kw))
            off += kw
    for n0 in range(0, n_cols, n_chunk):
        acc = None
        for r, off, kw in lhs:
            d = jnp.dot(r[...], wb[off:off + kw, n0:n0 + n_chunk], preferred_element_type=F32)
            acc = d if acc is None else acc + d
        if has_res:
            acc = acc + res_ref[:, n0:n0 + n_chunk]
        o_ref[:, n0:n0 + n_chunk] = acc.astype(o_ref.dtype)
    if has_extra:
        eo_ref[...] = jnp.dot(u_ref[...], ew_ref[...], preferred_element_type=F32)


def _dense(xs, w, layer, *, n_cols, out_dtype, gain=None, extra_w=None, res=None, tm=512, n_chunk=512, name):
    rows = xs[0].shape[0]
    part_widths = tuple(x.shape[1] for x in xs)
    k_rows = sum(part_widths)
    has_norm, has_extra, has_res = gain is not None, extra_w is not None, res is not None
    row_spec = lambda width: pl.BlockSpec((tm, width), lambda i: (i, 0))
    full_spec = lambda a: pl.BlockSpec(a.shape, lambda i: (0, 0))
    args, in_specs = list(xs), [row_spec(kw) for kw in part_widths]
    if has_norm:
        args.append(gain.reshape(1, k_rows).astype(F32))
        in_specs.append(full_spec(args[-1]))
    args.append(w)
    in_specs.append(pl.BlockSpec(memory_space=pl.ANY))
    if has_extra:
        args.append(extra_w)
        in_specs.append(full_spec(extra_w))
    if has_res:
        args.append(res)
        in_specs.append(row_spec(n_cols))
    out_shape = [jax.ShapeDtypeStruct((rows, n_cols), out_dtype)]
    out_specs = [row_spec(n_cols)]
    if has_extra:
        out_shape.append(jax.ShapeDtypeStruct((rows, extra_w.shape[1]), F32))
        out_specs.append(row_spec(extra_w.shape[1]))
    scratch = [pltpu.VMEM((k_rows, n_cols), BF16),
               pltpu.VMEM((2, W_STAGE_ROWS, n_cols), F32),
               pltpu.SemaphoreType.DMA((2,))]
    if has_norm:
        scratch.append(pltpu.VMEM((tm, k_rows), BF16))
    body = functools.partial(_dense_body, part_widths=part_widths, has_norm=has_norm, has_extra=has_extra,
                             has_res=has_res, n_cols=n_cols, n_chunk=n_chunk, layer=layer)
    outs = pl.pallas_call(
        body, grid=(rows // tm,), in_specs=in_specs, out_specs=out_specs, out_shape=out_shape,
        scratch_shapes=scratch, compiler_params=_params(("arbitrary",)), name=name)(*args)
    return outs if has_extra else outs[0]


ATT_TQ = 512


def _winattn_body(sink_ref, q_ref, kp_ref, km_ref, kn_ref, vp_ref, vm_ref, vn_ref, g_ref, o_ref,
                  kcat, vcat, obuf, *, seq):
    s0 = pl.program_id(1) * ATT_TQ
    kcat[0:WBLK, :] = kp_ref[0]
    kcat[WBLK:WBLK + ATT_TQ, :] = km_ref[0]
    kcat[WBLK + ATT_TQ:, :] = kn_ref[0]
    vcat[0:WBLK, :] = vp_ref[0]
    vcat[WBLK:WBLK + ATT_TQ, :] = vm_ref[0]
    vcat[WBLK + ATT_TQ:, :] = vn_ref[0]
    qi = lax.broadcasted_iota(jnp.int32, (WBLK, 3 * WBLK), 0) + WBLK
    ki = lax.broadcasted_iota(jnp.int32, (WBLK, 3 * WBLK), 1)
    dist_i = jnp.abs(ki - qi)
    neg_dist = jnp.where(dist_i <= WINDOW, -dist_i.astype(F32), NEG_INF)
    scale = HEAD_DIM ** -0.5
    for qb in range(ATT_TQ // WBLK):
        kabs = s0 + (qb - 1) * WBLK + ki
        bias_unit = jnp.where((kabs >= 0) & (kabs < seq), neg_dist, NEG_INF)
        head_cols = lambda j: slice(j * HEAD_DIM, (j + 1) * HEAD_DIM)
        kv_cols = lambda j: head_cols(j // Q_PER_KV)
        key_rows = slice(qb * WBLK, (qb + 3) * WBLK)
        scores = []
        for j in range(N_Q_HEADS):
            slope = 2.0 ** (-8.0 * (j + 1) / N_Q_HEADS)
            q = q_ref[0, qb * WBLK:(qb + 1) * WBLK, head_cols(j)]
            scores.append(_nt(q, kcat[key_rows, kv_cols(j)]) * scale + slope * bias_unit)
        probs, denoms = [], []
        for j in range(N_Q_HEADS):
            sink = sink_ref[j]
            m = jnp.maximum(jnp.max(scores[j], axis=-1, keepdims=True), sink)
            p = jnp.exp(scores[j] - m)
            denoms.append(jnp.sum(p, axis=-1, keepdims=True) + jnp.exp(sink - m))
            probs.append(p.astype(BF16))
        for j in range(N_Q_HEADS):
            pv = jnp.dot(probs[j], vcat[key_rows, kv_cols(j)], preferred_element_type=F32)
            obuf[qb, :, head_cols(j)] = pv / denoms[j]
        o_ref[0, qb * WBLK:(qb + 1) * WBLK, :] = _rms(obuf[qb], g_ref[...]).astype(o_ref.dtype)


def _window_attention(proj, sink_logit, gain, batch, seq):
    p3 = proj.reshape(batch, seq, MAIN_COLS)
    nq = seq // ATT_TQ
    per = ATT_TQ // WBLK
    last = seq // WBLK - 1
    main = lambda col: pl.BlockSpec((1, ATT_TQ, ATT_KV), lambda b, i: (b, i, col))
    prev = lambda col: pl.BlockSpec((1, WBLK, ATT_KV), lambda b, i: (b, jnp.maximum(i * per - 1, 0), col))
    nxt = lambda col: pl.BlockSpec((1, WBLK, ATT_KV), lambda b, i: (b, jnp.minimum(i * per + per, last), col))
    ck, cv = COL_KA // ATT_KV, COL_VA // ATT_KV
    return pl.pallas_call(
        functools.partial(_winattn_body, seq=seq),
        grid=(batch, nq),
        in_specs=[pl.BlockSpec(memory_space=pltpu.SMEM),
                  pl.BlockSpec((1, ATT_TQ, ATT_Q), lambda b, i: (b, i, 0)),
                  prev(ck), main(ck), nxt(ck), prev(cv), main(cv), nxt(cv),
                  pl.BlockSpec((1, ATT_Q), lambda b, i: (0, 0))],
        out_specs=pl.BlockSpec((1, ATT_TQ, ATT_Q), lambda b, i: (b, i, 0)),
        out_shape=jax.ShapeDtypeStruct((batch, seq, ATT_Q), BF16),
        scratch_shapes=[pltpu.VMEM((ATT_TQ + 2 * WBLK, ATT_KV), BF16),
                        pltpu.VMEM((ATT_TQ + 2 * WBLK, ATT_KV), BF16),
                        pltpu.VMEM((ATT_TQ // WBLK, WBLK, ATT_Q), F32)],
        compiler_params=_params(("parallel", "parallel")), name="window_attention",
    )(sink_logit.astype(F32), p3, p3, p3, p3, p3, p3, p3, gain.reshape(1, ATT_Q).astype(F32))


GLA_SCAN_ROWS = 256
GLA_SCAN_UNROLL = 2
GLA_EPI_ROWS = 512
GLA_UNROLL = 4


def _split3(x):
    hi = x.astype(BF16)
    r1 = x - hi.astype(F32)
    mid = r1.astype(BF16)
    lo = (r1 - mid.astype(F32)).astype(BF16)
    return hi, mid, lo


def _gla_body(q_ref, k_ref, v_ref, r_ref, lr_ref, wf_ref, wb_ref, bf_ref, bb_ref, gn_ref, o_ref,
              cum_f, cum_b, acc, st_f, st_b, *, seq):
    C = GLA_CHUNK
    n_chunks = seq // C
    ri = lax.broadcasted_iota(jnp.int32, (GLA_SCAN_ROWS, GLA_SCAN_ROWS), 0)
    ci = lax.broadcasted_iota(jnp.int32, (GLA_SCAN_ROWS, GLA_SCAN_ROWS), 1)
    same = (ri // C) == (ci // C)
    tri_f = jnp.where(same & (ci <= ri), 1.0, 0.0).astype(BF16)
    tri_b = jnp.where(same & (ci >= ri), 1.0, 0.0).astype(BF16)

    def scan_body(t, carry):
        row_sets = [pl.ds(pl.multiple_of((t * GLA_SCAN_UNROLL + u) * GLA_SCAN_ROWS, GLA_SCAN_ROWS), GLA_SCAN_ROWS)
                    for u in range(GLA_SCAN_UNROLL)]
        lrs = [lr_ref[0, rows, :].astype(BF16) for rows in row_sets]
        zs = [jnp.dot(lr, w_ref[...], preferred_element_type=F32) + b_ref[...]
              for lr in lrs for w_ref, b_ref in ((wf_ref, bf_ref), (wb_ref, bb_ref))]
        gs = [_split3((jnp.minimum(z, 0.0) - jnp.log1p(jnp.exp(-jnp.abs(z)))) / GLA_TAU) for z in zs]
        for n, g3 in enumerate(gs):
            tri, dst = ((tri_f, cum_f), (tri_b, cum_b))[n % 2]
            dst[row_sets[n // 2], :] = sum(jnp.dot(tri, part, preferred_element_type=F32) for part in g3)
        return carry

    lax.fori_loop(0, seq // (GLA_SCAN_ROWS * GLA_SCAN_UNROLL), scan_body, 0)

    acc[...] = jnp.zeros_like(acc)
    st_f[...] = jnp.zeros_like(st_f)
    st_b[...] = jnp.zeros_like(st_b)
    rr = lax.broadcasted_iota(jnp.int32, (C, C), 0)
    cc = lax.broadcasted_iota(jnp.int32, (C, C), 1)
    scale = GLA_DK ** -0.5

    def chunk_body(i, carry):
        jobs = []
        for u in range(GLA_UNROLL):
            c = i * GLA_UNROLL + u
            jobs += [(c, cum_f, True), (n_chunks - 1 - c, cum_b, False)]
        prep = []
        for c, cum, forward in jobs:
            rows = pl.ds(pl.multiple_of(c * C, C), C)
            b = cum[rows, :]
            b_end = b[C - 1:C, :] if forward else b[0:1, :]
            q = q_ref[0, rows, :].astype(F32) * scale
            k = k_ref[0, rows, :].astype(F32)
            v = v_ref[0, rows, :]
            q_dec = (q * jnp.exp(b)).astype(BF16)
            k_inc = (k * jnp.exp(-b)).astype(BF16)
            k_dec = (k * jnp.exp(b_end - b)).astype(BF16)
            prep.append((rows, v, q_dec, k_inc, k_dec, jnp.exp(b_end)))
        attn = [_nt(q_dec, k_inc) for _, _, q_dec, k_inc, _, _ in prep]
        kv_t = [_tn(v, k_dec) for _, v, _, _, k_dec, _ in prep]
        state_t = {True: st_f[...], False: st_b[...]}
        o_inter = []
        for (_, _, forward), (_, _, q_dec, _, _, decay), kv in zip(jobs, prep, kv_t):
            o_inter.append(_nt(q_dec, state_t[forward].astype(BF16)))
            state_t[forward] = state_t[forward] * decay + kv
        st_f[...] = state_t[True]
        st_b[...] = state_t[False]
        for (_, _, forward), (rows, v, _, _, _, _), a, oi in zip(jobs, prep, attn, o_inter):
            a = jnp.where((cc <= rr) if forward else (cc >= rr), a, 0.0)
            o = jnp.dot(a.astype(BF16), v, preferred_element_type=F32) + oi
            acc[rows, :] = acc[rows, :] + o
        return carry

    lax.fori_loop(0, n_chunks // GLA_UNROLL, chunk_body, 0)

    def epi_body(t, carry):
        rows = pl.ds(pl.multiple_of(t * GLA_EPI_ROWS, GLA_EPI_ROWS), GLA_EPI_ROWS)
        r = r_ref[0, rows, :].astype(F32)
        o_ref[0, rows, :] = (_rms(acc[rows, :], gn_ref[...]) * (r * jax.nn.sigmoid(r))).astype(o_ref.dtype)
        return carry

    lax.fori_loop(0, seq // GLA_EPI_ROWS, epi_body, 0)


def _gla(proj, lr, w_gf, b_gf, w_gb, b_gb, gain, batch, seq):
    p3 = proj.reshape(batch, seq, MAIN_COLS)
    lr3 = lr.reshape(batch, seq, LANES)
    wf = jnp.zeros((LANES, GLA_QK), F32).at[:GLA_LOWRANK].set(w_gf).astype(BF16)
    wb = jnp.zeros((LANES, GLA_QK), F32).at[GLA_LOWRANK:2 * GLA_LOWRANK].set(w_gb).astype(BF16)
    seq_blk = lambda width, col0: pl.BlockSpec((1, seq, width), lambda b, h: (b, 0, col0 // width + h))
    head_w = pl.BlockSpec((LANES, GLA_DK), lambda b, h: (0, h))
    head_b = pl.BlockSpec((1, GLA_DK), lambda b, h: (0, h))
    return pl.pallas_call(
        functools.partial(_gla_body, seq=seq),
        grid=(batch, GLA_HEADS),
        in_specs=[seq_blk(GLA_DK, COL_QG), seq_blk(GLA_DK, COL_KG), seq_blk(GLA_DV, COL_VG),
                  seq_blk(GLA_DV, COL_RG),
                  pl.BlockSpec((1, seq, LANES), lambda b, h: (b, 0, 0)),
                  head_w, head_w, head_b, head_b,
                  pl.BlockSpec((1, GLA_DV), lambda b, h: (0, 0))],
        out_specs=pl.BlockSpec((1, seq, GLA_DV), lambda b, h: (b, 0, h)),
        out_shape=jax.ShapeDtypeStruct((batch, seq, GLA_V), BF16),
        scratch_shapes=[pltpu.VMEM((seq, GLA_DK), F32), pltpu.VMEM((seq, GLA_DK), F32),
                        pltpu.VMEM((seq, GLA_DV), F32),
                        pltpu.VMEM((GLA_DV, GLA_DK), F32), pltpu.VMEM((GLA_DV, GLA_DK), F32)],
        compiler_params=_params(("parallel", "parallel")), name="gla",
    )(p3, p3, p3, p3, lr3, wf, wb, b_gf.reshape(1, GLA_QK).astype(F32), b_gb.reshape(1, GLA_QK).astype(F32),
      gain.reshape(1, GLA_DV).astype(F32))


XATT_TQ = 512


XATT_CHUNK = 512


def _cross_body(h_ref, g_ref, wq_hbm, k_ref, v_ref, wo_hbm, o_ref, wqb, wob, stage, sem, u_scr, q_scr, a_scr,
                *, layer):
    @pl.when(pl.program_id(0) == 0)
    def _():
        _load_weight_bf16(wq_hbm, layer, wqb, stage, sem, D_MODEL, D_MODEL)
        _load_weight_bf16(wo_hbm, layer, wob, stage, sem, D_MODEL, D_MODEL)

    scale = X_HEAD_DIM ** -0.5
    heads = [slice(h * X_HEAD_DIM, (h + 1) * X_HEAD_DIM) for h in range(X_HEADS)]
    chunks = [slice(n0, n0 + XATT_CHUNK) for n0 in range(0, D_MODEL, XATT_CHUNK)]

    def norm(rows):
        u_scr[rows, :] = _rms(h_ref[rows, :], g_ref[...]).astype(BF16)

    def q_proj(rows):
        for cols in chunks:
            q_scr[rows, cols] = jnp.dot(u_scr[rows, :], wqb[:, cols], preferred_element_type=F32).astype(BF16)

    def scores(rows):
        return [_nt(q_scr[rows, hd], k_ref[0, :, hd]) * scale for hd in heads]

    def softmax(s_list):
        out = []
        for s in s_list:
            p = jnp.exp(s - jnp.max(s, axis=-1, keepdims=True))
            out.append((p.astype(BF16), jnp.sum(p, axis=-1, keepdims=True)))
        return out

    def attend(rows, probs):
        for hd, (p, denom) in zip(heads, probs):
            a_scr[rows, hd] = (jnp.dot(p, v_ref[0, :, hd], preferred_element_type=F32) / denom).astype(BF16)

    def o_proj(rows):
        for cols in chunks:
            o_ref[rows, cols] = h_ref[rows, cols] + jnp.dot(a_scr[rows, :], wob[:, cols],
                                                             preferred_element_type=F32)

    half = XATT_TQ // 2
    first, second = slice(0, half), slice(half, XATT_TQ)
    norm(first)
    norm(second)
    q_proj(first)
    s_first = scores(first)
    q_proj(second)
    p_first = softmax(s_first)
    attend(first, p_first)
    s_second = scores(second)
    o_proj(first)
    p_second = softmax(s_second)
    attend(second, p_second)
    o_proj(second)


def _cross_block(h, gain, w_cq, kx, vx, w_co, layer, batch, seq, mem_len):
    rows = h.shape[0]
    tiles_per_batch = seq // XATT_TQ
    k3 = kx.reshape(batch, mem_len, D_MODEL)
    v3 = vx.reshape(batch, mem_len, D_MODEL)
    g2 = gain.reshape(1, D_MODEL).astype(F32)
    tile = pl.BlockSpec((XATT_TQ, D_MODEL), lambda i: (i, 0))
    mem_spec = pl.BlockSpec((1, mem_len, D_MODEL), lambda i: (i // tiles_per_batch, 0, 0))
    hbm = pl.BlockSpec(memory_space=pl.ANY)
    return pl.pallas_call(
        functools.partial(_cross_body, layer=layer), grid=(rows // XATT_TQ,),
        in_specs=[tile, pl.BlockSpec((1, D_MODEL), lambda i: (0, 0)), hbm, mem_spec, mem_spec, hbm],
        out_specs=tile, out_shape=jax.ShapeDtypeStruct((rows, D_MODEL), F32),
        scratch_shapes=[pltpu.VMEM((D_MODEL, D_MODEL), BF16), pltpu.VMEM((D_MODEL, D_MODEL), BF16),
                        pltpu.VMEM((2, W_STAGE_ROWS, D_MODEL), F32), pltpu.SemaphoreType.DMA((2,)),
                        pltpu.VMEM((XATT_TQ, D_MODEL), BF16), pltpu.VMEM((XATT_TQ, D_MODEL), BF16),
                        pltpu.VMEM((XATT_TQ, D_MODEL), BF16)],
        compiler_params=_params(("arbitrary",)), name="cross_block")(h, g2, w_cq, k3, v3, w_co)


ROUTER_TM = 512


HALF = D_MODEL // 2


def _pack_halves(x):
    return pltpu.pack_elementwise([x[:, :HALF], x[:, HALF:]], packed_dtype=BF16)


def _unpack_halves(words):
    return [pltpu.unpack_elementwise(words, index=i, packed_dtype=BF16, unpacked_dtype=F32) for i in range(2)]


ID_SPLIT = 32


def _router_body(h_ref, g_ref, w_ref, b_ref, hn_ref, gate_ref, order_ref, count_ref, *, n_tokens):
    hn = _rms(h_ref[...], g_ref[...])
    hn_ref[...] = _pack_halves(hn)
    logits = jnp.dot(hn.astype(BF16), w_ref[...], preferred_element_type=F32) + b_ref[...]
    lane = lax.broadcasted_iota(jnp.int32, logits.shape, 1).astype(F32)
    ninf = -jnp.inf
    first = lambda hit: jnp.min(jnp.where(hit, lane, float(LANES)), axis=-1, keepdims=True)
    in_groups = lane < N_GROUPS
    gl = jnp.where(in_groups, logits, ninf)
    gmax = jnp.max(gl, axis=-1, keepdims=True)
    g_idx = first(gl == gmax)
    p_group = 1.0 / jnp.sum(jnp.where(in_groups, jnp.exp(logits - gmax), 0.0), axis=-1, keepdims=True)
    lo = N_GROUPS + EXPERTS_PER_GROUP * g_idx
    el = jnp.where((lane >= lo) & (lane < lo + EXPERTS_PER_GROUP), logits, ninf)
    e1 = jnp.max(el, axis=-1, keepdims=True)
    i1 = first(el == e1)
    el2 = jnp.where(lane == i1, ninf, el)
    e2 = jnp.max(el2, axis=-1, keepdims=True)
    i2 = first(el2 == e2)
    t = jnp.exp(e2 - e1)
    w1 = p_group / (1.0 + t)
    w2 = p_group * t / (1.0 + t)
    gate_ref[...] = jnp.where(lane == 0, w1, jnp.where(lane == 1, w2, 0.0))

    tm = logits.shape[0]
    hit0, hit1 = lane == i1 - N_GROUPS, lane == i2 - N_GROUPS
    member = jnp.where(hit0 | hit1, 1.0, 0.0)
    member_b = member.astype(BF16)
    earlier = (lax.broadcasted_iota(jnp.int32, (tm, tm), 1) < lax.broadcasted_iota(jnp.int32, (tm, tm), 0))
    rank = jnp.dot(jnp.where(earlier, 1.0, 0.0).astype(BF16), member_b, preferred_element_type=F32)
    lower = (lax.broadcasted_iota(jnp.int32, (LANES, LANES), 0) < lax.broadcasted_iota(jnp.int32, (LANES, LANES), 1))
    run_start = jnp.sum(jnp.dot(member_b, jnp.where(lower, 1.0, 0.0).astype(BF16), preferred_element_type=F32),
                        axis=0, keepdims=True)
    pos = rank + run_start
    positions = [jnp.sum(jnp.where(hit, pos, 0.0), axis=-1, keepdims=True) for hit in (hit0, hit1)]
    out_lane = lax.broadcasted_iota(jnp.int32, (tm, TOP_K * tm), 1).astype(F32)
    digit_row = lax.broadcasted_iota(jnp.int32, (8, tm), 0)
    local_tok = lax.broadcasted_iota(jnp.int32, (8, tm), 1)
    digits = None
    for s, p in enumerate(positions):
        a = s * tm + local_tok
        lhs = jnp.where(digit_row == 0, a // ID_SPLIT, jnp.where(digit_row == 1, a % ID_SPLIT, 0))
        part = jnp.dot(lhs.astype(F32).astype(BF16), jnp.where(out_lane == p, 1.0, 0.0).astype(BF16),
                       preferred_element_type=F32)
        digits = part if digits is None else digits + part
    a_sorted = (digits[0:1] * ID_SPLIT + digits[1:2]).astype(jnp.int32)
    tok = pl.program_id(0) * tm + a_sorted % tm
    row8 = lax.broadcasted_iota(jnp.int32, (8, TOP_K * tm), 0)
    order_ref[0] = jnp.where(row8 == 0, tok, jnp.where(row8 == 1, (a_sorted // tm) * n_tokens + tok, 0))
    count_ref[0] = jnp.broadcast_to(jnp.sum(member, axis=0, keepdims=True), (8, LANES)).astype(jnp.int32)


def _router(h, gain, w_rg, b_rg, w_re, b_re):
    rows = h.shape[0]
    n_log = N_GROUPS + N_EXPERTS
    w = jnp.zeros((D_MODEL, LANES), F32).at[:, :N_GROUPS].set(w_rg).at[:, N_GROUPS:n_log].set(w_re).astype(BF16)
    b = jnp.zeros((1, LANES), F32).at[0, :N_GROUPS].set(b_rg).at[0, N_GROUPS:n_log].set(b_re)
    tile = lambda width: pl.BlockSpec((ROUTER_TM, width), lambda i: (i, 0))
    full = lambda a: pl.BlockSpec(a.shape, lambda i: (0, 0))
    g2 = gain.reshape(1, D_MODEL).astype(F32)
    n_tiles = rows // ROUTER_TM
    per_tile = lambda width: pl.BlockSpec((1, 8, width), lambda i: (i, 0, 0))
    return pl.pallas_call(
        functools.partial(_router_body, n_tokens=rows), grid=(n_tiles,),
        in_specs=[tile(D_MODEL), full(g2), full(w), full(b)],
        out_specs=[tile(HALF), tile(LANES), per_tile(TOP_K * ROUTER_TM), per_tile(LANES)],
        out_shape=[jax.ShapeDtypeStruct((rows, HALF), jnp.uint32),
                   jax.ShapeDtypeStruct((rows, LANES), F32),
                   jax.ShapeDtypeStruct((n_tiles, 8, TOP_K * ROUTER_TM), jnp.int32),
                   jax.ShapeDtypeStruct((n_tiles, 8, LANES), jnp.int32)],
        compiler_params=_params(("parallel",)), name="moe_router")(h, g2, w, b)


W_CAST_ROWS = 256


def _cast_weight(src_ref, dst_ref):
    def body(c, carry):
        rows = pl.ds(pl.multiple_of(c * W_CAST_ROWS, W_CAST_ROWS), W_CAST_ROWS)
        dst_ref[rows, :] = src_ref[rows, :].astype(BF16)
        return carry
    lax.fori_loop(0, src_ref.shape[0] // W_CAST_ROWS, body, 0)


def _expert_changed(blk_e, blk):
    prev = blk_e[jnp.maximum(blk - 1, 0)]
    return (blk == 0) | (blk_e[blk] != prev)


WEIGHT_DMA_PRIORITY = 1


def _switch_expert(blk_e, nxt_e, blk, layer, w_hbms, stages, dsts, sem):
    def copies(e):
        return [pltpu.make_async_copy(w.at[layer, e], st, sem.at[i])
                for i, (w, st) in enumerate(zip(w_hbms, stages))]

    @pl.when(blk == 0)
    def _():
        for cp in copies(blk_e[0]):
            cp.start(priority=WEIGHT_DMA_PRIORITY)

    @pl.when(_expert_changed(blk_e, blk))
    def _():
        for cp, st, dst in zip(copies(blk_e[blk]), stages, dsts):
            cp.wait()
            _cast_weight(st, dst)

        @pl.when(nxt_e[blk] >= 0)
        def _():
            for cp in copies(nxt_e[blk]):
                cp.start(priority=WEIGHT_DMA_PRIORITY)


GATHER_AHEAD = 3
GATHER_SLOTS = GATHER_AHEAD + 1
MOE_CHUNKS = 4


SCATTER_SLOTS = 3


def _experts_body(blk_e, n_real, nxt_e, *refs, n_assign, layer):
    tok_refs = refs[:GATHER_AHEAD + 1]
    (dst_ref, dst_prev_ref, hn_hbm, wg_hbm, wu_hbm, wd_hbm, out_hbm, xs, xb, hid, ys,
     wg_stage, wu_stage, wd_stage, wgb, wub, wdb, gsem, ssem, wsem) = refs[GATHER_AHEAD + 1:]
    blk = pl.program_id(0)
    dummy_slot = SCATTER_SLOTS - 1

    def gather_row(idx_ref, slot, r):
        return pltpu.make_async_copy(hn_hbm.at[pl.ds(idx_ref[0, 0, r], 1), :], xs.at[slot, pl.ds(r, 1), :],
                                     gsem.at[slot])

    def scatter_row(idx_ref, slot, r):
        return pltpu.make_async_copy(ys.at[slot, pl.ds(r, 1), :], out_hbm.at[pl.ds(idx_ref[0, 0, r], 1), :],
                                     ssem.at[slot])

    def wait_gather(slot):
        pltpu.make_async_copy(hn_hbm.at[pl.ds(0, MOE_BLOCK), :], xs.at[slot], gsem.at[slot]).wait()

    def wait_scatter(slot):
        pltpu.make_async_copy(ys.at[slot], out_hbm.at[pl.ds(0, MOE_BLOCK), :], ssem.at[slot]).wait()

    @pl.when(blk == 0)
    def _():
        def body(r, carry):
            for s in range(GATHER_AHEAD):
                gather_row(tok_refs[s], s, r).start()
            return carry
        lax.fori_loop(0, MOE_BLOCK, body, 0)
        ys[dummy_slot] = jnp.zeros((MOE_BLOCK, HALF), jnp.uint32)
        for s in range(SCATTER_SLOTS):
            trash = pltpu.make_async_copy(
                ys.at[dummy_slot], out_hbm.at[pl.ds(n_assign + s * MOE_BLOCK, MOE_BLOCK), :], ssem.at[dummy_slot])
            trash.start()
            trash.wait()

    @pl.when(blk < n_real[0])
    def _():
        in_slot = blk % GATHER_SLOTS
        ahead_slot = (blk + GATHER_AHEAD) % GATHER_SLOTS
        out_slot = blk % SCATTER_SLOTS
        prev_slot = (blk + SCATTER_SLOTS - 1) % SCATTER_SLOTS

        _switch_expert(blk_e, nxt_e, blk, layer, (wg_hbm, wu_hbm, wd_hbm), (wg_stage, wu_stage, wd_stage),
                       (wgb, wub, wdb), wsem)
        wait_gather(in_slot)

        @pl.when(blk >= SCATTER_SLOTS - 1)
        def _():
            wait_scatter(out_slot)

        lo, hi = _unpack_halves(xs[in_slot])
        xb[:, :HALF] = lo.astype(BF16)
        xb[:, HALF:] = hi.astype(BF16)
        for r in range(MOE_BLOCK):
            scatter_row(dst_prev_ref, prev_slot, r).start(priority=r % 2)
            gather_row(tok_refs[GATHER_AHEAD], ahead_slot, r).start()
        cw = D_FF_EXPERT // MOE_CHUNKS
        for c in range(MOE_CHUNKS):
            cols = slice(c * cw, (c + 1) * cw)
            a = jnp.dot(xb[...], wgb[:, cols], preferred_element_type=F32)
            u = jnp.dot(xb[...], wub[:, cols], preferred_element_type=F32)
            hid[:, cols] = (a * jax.nn.sigmoid(a) * u).astype(BF16)
        cw = HALF // MOE_CHUNKS
        for c in range(MOE_CHUNKS):
            cols = slice(c * cw, (c + 1) * cw)
            hi_cols = slice(HALF + c * cw, HALF + (c + 1) * cw)
            ys[out_slot, :, cols] = pltpu.pack_elementwise(
                [jnp.dot(hid[...], wdb[:, cols], preferred_element_type=F32),
                 jnp.dot(hid[...], wdb[:, hi_cols], preferred_element_type=F32)], packed_dtype=BF16)

        @pl.when(blk == n_real[0] - 1)
        def _():
            def body(r, carry):
                scatter_row(dst_ref, out_slot, r).start()
                return carry
            lax.fori_loop(0, MOE_BLOCK, body, 0)
            for s in range(1, GATHER_SLOTS):
                wait_gather((blk + s) % GATHER_SLOTS)
            for s in range(SCATTER_SLOTS):
                wait_scatter(s)


def _moe_forward(hn, order, counts, w_gate, w_up, w_down, layer, n_tokens):
    A = n_tokens * TOP_K
    out_rows = A + SCATTER_SLOTS * MOE_BLOCK
    R = A + N_EXPERTS * MOE_BLOCK
    n_blk = R // MOE_BLOCK
    i32 = jnp.int32
    n_tiles, per_tile = order.shape[0], order.shape[2]
    experts = jnp.arange(N_EXPERTS, dtype=i32)
    n = counts[:, 0, :N_EXPERTS]
    total = jnp.sum(n, axis=0)
    padded = ((total + MOE_BLOCK - 1) // MOE_BLOCK) * MOE_BLOCK
    pend = jnp.cumsum(padded)
    pstart = pend - padded
    n_real = (pend[-1] // MOE_BLOCK).astype(i32).reshape(1)
    blk = jnp.arange(n_blk, dtype=i32)
    blk_e = jnp.minimum(jnp.sum((pend[None, :] <= (blk * MOE_BLOCK)[:, None]).astype(i32), axis=1), N_EXPERTS - 1)
    is_e = (blk_e[:, None] == experts[None, :]).astype(i32)
    of_block = lambda per_expert: jnp.sum(is_e * per_expert[None, :], axis=1)
    cum_incl = jnp.cumsum(n, axis=0)
    run_shift = (jnp.cumsum(n, axis=1) - n) - (cum_incl - n)
    per_tile_of_block = lambda m: jnp.sum(is_e[:, None, :] * m[None, :, :], axis=2)
    cum_b, shift_b = per_tile_of_block(cum_incl), per_tile_of_block(run_shift)
    in_blk = jnp.arange(MOE_BLOCK, dtype=i32)
    k = (blk * MOE_BLOCK - of_block(pstart))[:, None] + in_blk[None, :]
    tile_of = jnp.minimum(jnp.sum((cum_b[:, None, :] <= k[:, :, None]).astype(i32), axis=2), n_tiles - 1)
    is_tile = (tile_of[:, :, None] == jnp.arange(n_tiles, dtype=i32)[None, None, :]).astype(i32)
    entry = tile_of * per_tile + k + jnp.sum(is_tile * shift_b[:, None, :], axis=2)
    pad_entry = A + (blk % SCATTER_SLOTS)[:, None] * MOE_BLOCK + in_blk[None, :]
    entry = jnp.where(k < of_block(total)[:, None], entry, pad_entry)
    n_pad = SCATTER_SLOTS * MOE_BLOCK
    tok_table = jnp.concatenate([order[:, 0, :].reshape(A), jnp.zeros((n_pad,), i32)])
    dst_table = jnp.concatenate([order[:, 1, :].reshape(A), A + jnp.arange(n_pad, dtype=i32)])
    row_tok = tok_table[entry].reshape(n_blk, 1, MOE_BLOCK)
    row_dst = dst_table[entry].reshape(n_blk, 1, MOE_BLOCK)
    row_dst_prev = jnp.concatenate(
        [(A + (SCATTER_SLOTS - 1) * MOE_BLOCK + in_blk).reshape(1, 1, MOE_BLOCK), row_dst[:-1]], axis=0)

    run_end = pend[blk_e] // MOE_BLOCK
    nxt_e = jnp.where(run_end < n_real[0], blk_e[jnp.minimum(run_end, n_blk - 1)], -1).astype(jnp.int32)

    clamp = lambda i, nr: jnp.minimum(i, nr[0] - 1)
    idx_spec = lambda shift: pl.BlockSpec((1, 1, MOE_BLOCK), lambda i, be, nr, nx: (clamp(i + shift, nr), 0, 0),
                                          memory_space=pltpu.SMEM)
    hbm = pl.BlockSpec(memory_space=pl.ANY)
    up_shape, down_shape = (D_MODEL, D_FF_EXPERT), (D_FF_EXPERT, D_MODEL)
    return pl.pallas_call(
        functools.partial(_experts_body, n_assign=A, layer=layer),
        grid_spec=pltpu.PrefetchScalarGridSpec(
            num_scalar_prefetch=3, grid=(n_blk,),
            in_specs=[idx_spec(s) for s in range(GATHER_AHEAD + 1)] + [idx_spec(0), idx_spec(0)] + [hbm] * 4,
            out_specs=hbm,
            scratch_shapes=[pltpu.VMEM((GATHER_SLOTS, MOE_BLOCK, HALF), jnp.uint32),
                            pltpu.VMEM((MOE_BLOCK, D_MODEL), BF16),
                            pltpu.VMEM((MOE_BLOCK, D_FF_EXPERT), BF16),
                            pltpu.VMEM((SCATTER_SLOTS, MOE_BLOCK, HALF), jnp.uint32),
                            pltpu.VMEM(up_shape, F32), pltpu.VMEM(up_shape, F32), pltpu.VMEM(down_shape, F32),
                            pltpu.VMEM(up_shape, BF16), pltpu.VMEM(up_shape, BF16), pltpu.VMEM(down_shape, BF16),
                            pltpu.SemaphoreType.DMA((GATHER_SLOTS,)), pltpu.SemaphoreType.DMA((SCATTER_SLOTS,)),
                            pltpu.SemaphoreType.DMA((3,))]),
        out_shape=jax.ShapeDtypeStruct((out_rows, HALF), jnp.uint32),
        compiler_params=_params(("arbitrary",)), name="moe_experts",
    )(blk_e, n_real, nxt_e, *([row_tok] * (GATHER_AHEAD + 1)), row_dst, row_dst_prev, hn, w_gate, w_up, w_down)


FINAL_TM = 512


def _final_body(h_ref, y0_ref, y1_ref, gate_ref, g_ref, o_ref):
    gate = gate_ref[...]
    y0, y1 = _unpack_halves(y0_ref[...]), _unpack_halves(y1_ref[...])
    halves = [h_ref[:, i * HALF:(i + 1) * HALF] + gate[:, 0:1] * y0[i] + gate[:, 1:2] * y1[i] for i in range(2)]
    mean_sq = sum(jnp.sum(hh * hh, axis=-1, keepdims=True) for hh in halves) / D_MODEL
    inv = lax.rsqrt(mean_sq + RMS_EPS)
    for i, hh in enumerate(halves):
        o_ref[:, i * HALF:(i + 1) * HALF] = hh * inv * g_ref[:, i * HALF:(i + 1) * HALF]


def _final(h, ys, gate, gain):
    rows = h.shape[0]
    n_tiles = rows // FINAL_TM
    tile = lambda width: pl.BlockSpec((FINAL_TM, width), lambda i: (i, 0))
    g2 = gain.reshape(1, D_MODEL).astype(F32)
    return pl.pallas_call(
        _final_body, grid=(n_tiles,),
        in_specs=[tile(D_MODEL), tile(HALF), pl.BlockSpec((FINAL_TM, HALF), lambda i: (i + n_tiles, 0)),
                  tile(LANES), pl.BlockSpec((1, D_MODEL), lambda i: (0, 0))],
        out_specs=tile(D_MODEL), out_shape=jax.ShapeDtypeStruct((rows, D_MODEL), F32),
        compiler_params=_params(("parallel",)), name="combine_final_norm")(h, ys, ys, gate, g2)


def kernel(x, mem, norm_mix, w_in, attn_out_norm, sink_logit, w_gla_gf, b_gla_gf, w_gla_gb, b_gla_gb, gla_out_norm, w_out, norm_cross, norm_mem, w_cq, w_ck, w_cv, w_co, norm_ffn, w_router_group, b_router_group, w_router_expert, b_router_expert, w_gate, w_up, w_down, norm_final):
    batch, seq, _ = x.shape
    mem_len = mem.shape[1]
    n_tokens = batch * seq
    h = x.reshape(n_tokens, D_MODEL)
    memf = mem.reshape(batch * mem_len, D_MODEL)
    assert norm_mix.shape[0] == 1, "the combine step is fused with the final norm: single-layer stacks only"
    for l in range(norm_mix.shape[0]):
        w_lr = jnp.zeros((D_MODEL, LANES), F32).at[:, :2 * GLA_LOWRANK].set(w_in[l][:, MAIN_COLS:]).astype(BF16)
        proj, lr = _dense([h], w_in, l, n_cols=MAIN_COLS, out_dtype=BF16, gain=norm_mix[l], extra_w=w_lr,
                          name="in_proj")
        o_a = _window_attention(proj, sink_logit[l], attn_out_norm[l], batch, seq).reshape(n_tokens, ATT_Q)
        o_g = _gla(proj, lr, w_gla_gf[l], b_gla_gf[l], w_gla_gb[l], b_gla_gb[l], gla_out_norm[l],
                   batch, seq).reshape(n_tokens, GLA_V)
        h = _dense([o_a, o_g], w_out, l, n_cols=D_MODEL, out_dtype=F32, res=h, name="out_proj")
        kx = _dense([memf], w_ck, l, n_cols=D_MODEL, out_dtype=BF16, gain=norm_mem[l], name="mem_k_proj")
        vx = _dense([memf], w_cv, l, n_cols=D_MODEL, out_dtype=BF16, gain=norm_mem[l], name="mem_v_proj")
        h = _cross_block(h, norm_cross[l], w_cq, kx, vx, w_co, l, batch, seq, mem_len)
        hn, gate, order, counts = _router(h, norm_ffn[l], w_router_group[l], b_router_group[l],
                                          w_router_expert[l], b_router_expert[l])
        ys = _moe_forward(hn, order, counts, w_gate, w_up, w_down, l, n_tokens)
    return _final(h, ys, gate, norm_final).reshape(batch, seq, D_MODEL)
```

```python
import functools

import jax
import jax.numpy as jnp
from jax import lax
from jax.experimental import pallas as pl
from jax.experimental.pallas import tpu as pltpu

F32 = jnp.float32
BF16 = jnp.bfloat16

D_MODEL = 2048
N_Q_HEADS = 8
N_KV_HEADS = 2
Q_PER_KV = N_Q_HEADS // N_KV_HEADS
HEAD_DIM = 128
WINDOW = 128
WBLK = 128
GLA_HEADS = 4
GLA_DK = 128
GLA_DV = 256
GLA_LOWRANK = 16
GLA_TAU = 16.0
GLA_CHUNK = 64
ATT_Q = N_Q_HEADS * HEAD_DIM
ATT_KV = N_KV_HEADS * HEAD_DIM
GLA_QK = GLA_HEADS * GLA_DK
GLA_V = GLA_HEADS * GLA_DV
MAIN_COLS = ATT_Q + 2 * ATT_KV + 2 * GLA_QK + 2 * GLA_V
COL_KA = ATT_Q
COL_VA = COL_KA + ATT_KV
COL_QG = COL_VA + ATT_KV
COL_KG = COL_QG + GLA_QK
COL_VG = COL_KG + GLA_QK
COL_RG = COL_VG + GLA_V
X_HEADS = 4
X_HEAD_DIM = D_MODEL // X_HEADS
N_GROUPS = 4
EXPERTS_PER_GROUP = 8
N_EXPERTS = N_GROUPS * EXPERTS_PER_GROUP
TOP_K = 2
D_FF_EXPERT = D_MODEL // 2
MOE_BLOCK = 256
RMS_EPS = 1e-6
NEG_INF = -1e30

LANES = 128
VMEM_LIMIT = 56 * 1024 * 1024
W_STAGE_ROWS = 128


def _params(sem):
    return pltpu.CompilerParams(dimension_semantics=sem, vmem_limit_bytes=VMEM_LIMIT)


def _nt(a, b):
    return lax.dot_general(a, b, (((1,), (1,)), ((), ())), preferred_element_type=F32)


def _tn(a, b):
    return lax.dot_general(a, b, (((0,), (0,)), ((), ())), preferred_element_type=F32)


def _rms(x, gain):
    return x * lax.rsqrt(jnp.mean(x * x, axis=-1, keepdims=True) + RMS_EPS) * gain


def _load_weight_bf16(w_hbm, layer, wb, stage, sem, k_rows, n_cols):
    n_chunks = k_rows // W_STAGE_ROWS

    def copy(c):
        return pltpu.make_async_copy(
            w_hbm.at[layer, pl.ds(c * W_STAGE_ROWS, W_STAGE_ROWS), pl.ds(0, n_cols)],
            stage.at[c % 2], sem.at[c % 2])

    copy(0).start()
    for c in range(n_chunks):
        if c + 1 < n_chunks:
            copy(c + 1).start()
        copy(c).wait()
        wb[c * W_STAGE_ROWS:(c + 1) * W_STAGE_ROWS, :] = stage[c % 2].astype(BF16)


def _dense_body(*refs, part_widths, has_norm, has_extra, has_res, n_cols, n_chunk, layer):
    it = iter(refs)
    x_refs = [next(it) for _ in part_widths]
    g_ref = next(it) if has_norm else None
    w_hbm = next(it)
    ew_ref = next(it) if has_extra else None
    res_ref = next(it) if has_res else None
    o_ref = next(it)
    eo_ref = next(it) if has_extra else None
    wb, stage, sem = next(it), next(it), next(it)
    u_ref = next(it) if has_norm else None
    k_rows = sum(part_widths)

    @pl.when(pl.program_id(0) == 0)
    def _():
        _load_weight_bf16(w_hbm, layer, wb, stage, sem, k_rows, n_cols)

    if has_norm:
        u_ref[...] = _rms(x_refs[0][...], g_ref[...]).astype(BF16)
        lhs = [(u_ref, 0, k_rows)]
    else:
        lhs, off = [], 0
        for r, kw in zip(x_refs, part_widths):
            lhs.append((r, off, kw))
            off += kw
    for n0 in range(0, n_cols, n_chunk):
        acc = None
        for r, off, kw in lhs:
            d = jnp.dot(r[...], wb[off:off + kw, n0:n0 + n_chunk], preferred_element_type=F32)
            acc = d if acc is None else acc + d
        if has_res:
            acc = acc + res_ref[:, n0:n0 + n_chunk]
        o_ref[:, n0:n0 + n_chunk] = acc.astype(o_ref.dtype)
    if has_extra:
        eo_ref[...] = jnp.dot(u_ref[...], ew_ref[...], preferred_element_type=F32)


def _dense(xs, w, layer, *, n_cols, out_dtype, gain=None, extra_w=None, res=None, tm=512, n_chunk=512, name):
    rows = xs[0].shape[0]
    part_widths = tuple(x.shape[1] for x in xs)
    k_rows = sum(part_widths)
    has_norm, has_extra, has_res = gain is not None, extra_w is not None, res is not None
    row_spec = lambda width: pl.BlockSpec((tm, width), lambda i: (i, 0))
    full_spec = lambda a: pl.BlockSpec(a.shape, lambda i: (0, 0))
    args, in_specs = list(xs), [row_spec(kw) for kw in part_widths]
    if has_norm:
        args.append(gain.reshape(1, k_rows).astype(F32))
        in_specs.append(full_spec(args[-1]))
    args.append(w)
    in_specs.append(pl.BlockSpec(memory_space=pl.ANY))
    if has_extra:
        args.append(extra_w)
        in_specs.append(full_spec(extra_w))
    if has_res:
        args.append(res)
        in_specs.append(row_spec(n_cols))
    out_shape = [jax.ShapeDtypeStruct((rows, n_cols), out_dtype)]
    out_specs = [row_spec(n_cols)]
    if has_extra:
        out_shape.append(jax.ShapeDtypeStruct((rows, extra_w.shape[1]), F32))
        out_specs.append(row_spec(extra_w.shape[1]))
    scratch = [pltpu.VMEM((k_rows, n_cols), BF16),
               pltpu.VMEM((2, W_STAGE_ROWS, n_cols), F32),
               pltpu.SemaphoreType.DMA((2,))]
    if has_norm:
        scratch.append(pltpu.VMEM((tm, k_rows), BF16))
    body = functools.partial(_dense_body, part_widths=part_widths, has_norm=has_norm, has_extra=has_extra,
                             has_res=has_res, n_cols=n_cols, n_chunk=n_chunk, layer=layer)
    outs = pl.pallas_call(
        body, grid=(rows // tm,), in_specs=in_specs, out_specs=out_specs, out_shape=out_shape,
        scratch_shapes=scratch, compiler_params=_params(("arbitrary",)), name=name)(*args)
    return outs if has_extra else outs[0]


ATT_TQ = 512


def _winattn_body(sink_ref, q_ref, kp_ref, km_ref, kn_ref, vp_ref, vm_ref, vn_ref, g_ref, o_ref,
                  kcat, vcat, obuf, *, seq):
    s0 = pl.program_id(1) * ATT_TQ
    kcat[0:WBLK, :] = kp_ref[0]
    kcat[WBLK:WBLK + ATT_TQ, :] = km_ref[0]
    kcat[WBLK + ATT_TQ:, :] = kn_ref[0]
    vcat[0:WBLK, :] = vp_ref[0]
    vcat[WBLK:WBLK + ATT_TQ, :] = vm_ref[0]
    vcat[WBLK + ATT_TQ:, :] = vn_ref[0]
    qi = lax.broadcasted_iota(jnp.int32, (WBLK, 3 * WBLK), 0) + WBLK
    ki = lax.broadcasted_iota(jnp.int32, (WBLK, 3 * WBLK), 1)
    dist_i = jnp.abs(ki - qi)
    neg_dist = jnp.where(dist_i <= WINDOW, -dist_i.astype(F32), NEG_INF)
    scale = HEAD_DIM ** -0.5
    for qb in range(ATT_TQ // WBLK):
        kabs = s0 + (qb - 1) * WBLK + ki
        bias_unit = jnp.where((kabs >= 0) & (kabs < seq), neg_dist, NEG_INF)
        head_cols = lambda j: slice(j * HEAD_DIM, (j + 1) * HEAD_DIM)
        kv_cols = lambda j: head_cols(j // Q_PER_KV)
        key_rows = slice(qb * WBLK, (qb + 3) * WBLK)
        scores = []
        for j in range(N_Q_HEADS):
            slope = 2.0 ** (-8.0 * (j + 1) / N_Q_HEADS)
            q = q_ref[0, qb * WBLK:(qb + 1) * WBLK, head_cols(j)]
            scores.append(_nt(q, kcat[key_rows, kv_cols(j)]) * scale + slope * bias_unit)
        probs, denoms = [], []
        for j in range(N_Q_HEADS):
            sink = sink_ref[j]
            m = jnp.maximum(jnp.max(scores[j], axis=-1, keepdims=True), sink)
            p = jnp.exp(scores[j] - m)
            denoms.append(jnp.sum(p, axis=-1, keepdims=True) + jnp.exp(sink - m))
            probs.append(p.astype(BF16))
        for j in range(N_Q_HEADS):
            pv = jnp.dot(probs[j], vcat[key_rows, kv_cols(j)], preferred_element_type=F32)
            obuf[qb, :, head_cols(j)] = pv / denoms[j]
        o_ref[0, qb * WBLK:(qb + 1) * WBLK, :] = _rms(obuf[qb], g_ref[...]).astype(o_ref.dtype)


def _window_attention(proj, sink_logit, gain, batch, seq):
    p3 = proj.reshape(batch, seq, MAIN_COLS)
    nq = seq // ATT_TQ
    per = ATT_TQ // WBLK
    last = seq // WBLK - 1
    main = lambda col: pl.BlockSpec((1, ATT_TQ, ATT_KV), lambda b, i: (b, i, col))
    prev = lambda col: pl.BlockSpec((1, WBLK, ATT_KV), lambda b, i: (b, jnp.maximum(i * per - 1, 0), col))
    nxt = lambda col: pl.BlockSpec((1, WBLK, ATT_KV), lambda b, i: (b, jnp.minimum(i * per + per, last), col))
    ck, cv = COL_KA // ATT_KV, COL_VA // ATT_KV
    return pl.pallas_call(
        functools.partial(_winattn_body, seq=seq),
        grid=(batch, nq),
        in_specs=[pl.BlockSpec(memory_space=pltpu.SMEM),
                  pl.BlockSpec((1, ATT_TQ, ATT_Q), lambda b, i: (b, i, 0)),
                  prev(ck), main(ck), nxt(ck), prev(cv), main(cv), nxt(cv),
                  pl.BlockSpec((1, ATT_Q), lambda b, i: (0, 0))],
        out_specs=pl.BlockSpec((1, ATT_TQ, ATT_Q), lambda b, i: (b, i, 0)),
        out_shape=jax.ShapeDtypeStruct((batch, seq, ATT_Q), BF16),
        scratch_shapes=[pltpu.VMEM((ATT_TQ + 2 * WBLK, ATT_KV), BF16),
                        pltpu.VMEM((ATT_TQ + 2 * WBLK, ATT_KV), BF16),
                        pltpu.VMEM((ATT_TQ // WBLK, WBLK, ATT_Q), F32)],
        compiler_params=_params(("parallel", "parallel")), name="window_attention",
    )(sink_logit.astype(F32), p3, p3, p3, p3, p3, p3, p3, gain.reshape(1, ATT_Q).astype(F32))


GLA_SCAN_ROWS = 256
GLA_SCAN_UNROLL = 2
GLA_EPI_ROWS = 512
GLA_UNROLL = 4


def _split3(x):
    hi = x.astype(BF16)
    r1 = x - hi.astype(F32)
    mid = r1.astype(BF16)
    lo = (r1 - mid.astype(F32)).astype(BF16)
    return hi, mid, lo


def _gla_body(q_ref, k_ref, v_ref, r_ref, lr_ref, wf_ref, wb_ref, bf_ref, bb_ref, gn_ref, o_ref,
              cum_f, cum_b, acc, st_f, st_b, *, seq):
    C = GLA_CHUNK
    n_chunks = seq // C
    ri = lax.broadcasted_iota(jnp.int32, (GLA_SCAN_ROWS, GLA_SCAN_ROWS), 0)
    ci = lax.broadcasted_iota(jnp.int32, (GLA_SCAN_ROWS, GLA_SCAN_ROWS), 1)
    same = (ri // C) == (ci // C)
    tri_f = jnp.where(same & (ci <= ri), 1.0, 0.0).astype(BF16)
    tri_b = jnp.where(same & (ci >= ri), 1.0, 0.0).astype(BF16)

    def scan_body(t, carry):
        row_sets = [pl.ds(pl.multiple_of((t * GLA_SCAN_UNROLL + u) * GLA_SCAN_ROWS, GLA_SCAN_ROWS), GLA_SCAN_ROWS)
                    for u in range(GLA_SCAN_UNROLL)]
        lrs = [lr_ref[0, rows, :].astype(BF16) for rows in row_sets]
        zs = [jnp.dot(lr, w_ref[...], preferred_element_type=F32) + b_ref[...]
              for lr in lrs for w_ref, b_ref in ((wf_ref, bf_ref), (wb_ref, bb_ref))]
        gs = [_split3((jnp.minimum(z, 0.0) - jnp.log1p(jnp.exp(-jnp.abs(z)))) / GLA_TAU) for z in zs]
        for n, g3 in enumerate(gs):
            tri, dst = ((tri_f, cum_f), (tri_b, cum_b))[n % 2]
            dst[row_sets[n // 2], :] = sum(jnp.dot(tri, part, preferred_element_type=F32) for part in g3)
        return carry

    lax.fori_loop(0, seq // (GLA_SCAN_ROWS * GLA_SCAN_UNROLL), scan_body, 0)

    acc[...] = jnp.zeros_like(acc)
    st_f[...] = jnp.zeros_like(st_f)
    st_b[...] = jnp.zeros_like(st_b)
    rr = lax.broadcasted_iota(jnp.int32, (C, C), 0)
    cc = lax.broadcasted_iota(jnp.int32, (C, C), 1)
    scale = GLA_DK ** -0.5

    def chunk_body(i, carry):
        jobs = []
        for u in range(GLA_UNROLL):
            c = i * GLA_UNROLL + u
            jobs += [(c, cum_f, True), (n_chunks - 1 - c, cum_b, False)]
        prep = []
        for c, cum, forward in jobs:
            rows = pl.ds(pl.multiple_of(c * C, C), C)
            b = cum[rows, :]
            b_end = b[C - 1:C, :] if forward else b[0:1, :]
            q = q_ref[0, rows, :].astype(F32) * scale
            k = k_ref[0, rows, :].astype(F32)
            v = v_ref[0, rows, :]
            q_dec = (q * jnp.exp(b)).astype(BF16)
            k_inc = (k * jnp.exp(-b)).astype(BF16)
            k_dec = (k * jnp.exp(b_end - b)).astype(BF16)
            prep.append((rows, v, q_dec, k_inc, k_dec, jnp.exp(b_end)))
        attn = [_nt(q_dec, k_inc) for _, _, q_dec, k_inc, _, _ in prep]
        kv_t = [_tn(v, k_dec) for _, v, _, _, k_dec, _ in prep]
        state_t = {True: st_f[...], False: st_b[...]}
        o_inter = []
        for (_, _, forward), (_, _, q_dec, _, _, decay), kv in zip(jobs, prep, kv_t):
            o_inter.append(_nt(q_dec, state_t[forward].astype(BF16)))
            state_t[forward] = state_t[forward] * decay + kv
        st_f[...] = state_t[True]
        st_b[...] = state_t[False]
        for (_, _, forward), (rows, v, _, _, _, _), a, oi in zip(jobs, prep, attn, o_inter):
            a = jnp.where((cc <= rr) if forward else (cc >= rr), a, 0.0)
            o = jnp.dot(a.astype(BF16), v, preferred_element_type=F32) + oi
            acc[rows, :] = acc[rows, :] + o
        return carry

    lax.fori_loop(0, n_chunks // GLA_UNROLL, chunk_body, 0)

    def epi_body(t, carry):
        rows = pl.ds(pl.multiple_of(t * GLA_EPI_ROWS, GLA_EPI_ROWS), GLA_EPI_ROWS)
        r = r_ref[0, rows, :].astype(F32)
        o_ref[0, rows, :] = (_rms(acc[rows, :], gn_ref[...]) * (r * jax.nn.sigmoid(r))).astype(o_ref.dtype)
        return carry

    lax.fori_loop(0, seq // GLA_EPI_ROWS, epi_body, 0)


def _gla(proj, lr, w_gf, b_gf, w_gb, b_gb, gain, batch, seq):
    p3 = proj.reshape(batch, seq, MAIN_COLS)
    lr3 = lr.reshape(batch, seq, LANES)
    wf = jnp.zeros((LANES, GLA_QK), F32).at[:GLA_LOWRANK].set(w_gf).astype(BF16)
    wb = jnp.zeros((LANES, GLA_QK), F32).at[GLA_LOWRANK:2 * GLA_LOWRANK].set(w_gb).astype(BF16)
    seq_blk = lambda width, col0: pl.BlockSpec((1, seq, width), lambda b, h: (b, 0, col0 // width + h))
    head_w = pl.BlockSpec((LANES, GLA_DK), lambda b, h: (0, h))
    head_b = pl.BlockSpec((1, GLA_DK), lambda b, h: (0, h))
    return pl.pallas_call(
        functools.partial(_gla_body, seq=seq),
        grid=(batch, GLA_HEADS),
        in_specs=[seq_blk(GLA_DK, COL_QG), seq_blk(GLA_DK, COL_KG), seq_blk(GLA_DV, COL_VG),
                  seq_blk(GLA_DV, COL_RG),
                  pl.BlockSpec((1, seq, LANES), lambda b, h: (b, 0, 0)),
                  head_w, head_w, head_b, head_b,
                  pl.BlockSpec((1, GLA_DV), lambda b, h: (0, 0))],
        out_specs=pl.BlockSpec((1, seq, GLA_DV), lambda b, h: (b, 0, h)),
        out_shape=jax.ShapeDtypeStruct((batch, seq, GLA_V), BF16),
        scratch_shapes=[pltpu.VMEM((seq, GLA_DK), F32), pltpu.VMEM((seq, GLA_DK), F32),
                        pltpu.VMEM((seq, GLA_DV), F32),
                        pltpu.VMEM((GLA_DV, GLA_DK), F32), pltpu.VMEM((GLA_DV, GLA_DK), F32)],
        compiler_params=_params(("parallel", "parallel")), name="gla",
    )(p3, p3, p3, p3, lr3, wf, wb, b_gf.reshape(1, GLA_QK).astype(F32), b_gb.reshape(1, GLA_QK).astype(F32),
      gain.reshape(1, GLA_DV).astype(F32))


XATT_TQ = 512


XATT_CHUNK = 512


def _cross_body(h_ref, g_ref, wq_hbm, k_ref, v_ref, wo_hbm, gr_ref, wr_ref, br_ref,
                o_ref, hn_ref, gate_ref, order_ref, count_ref,
                wqb, wob, stage, sem, u_scr, q_scr, a_scr, prev, *, layer, n_tokens):
    step = pl.program_id(0)

    @pl.when(step == 0)
    def _():
        _load_weight_bf16(wq_hbm, layer, wqb, stage, sem, D_MODEL, D_MODEL)
        _load_weight_bf16(wo_hbm, layer, wob, stage, sem, D_MODEL, D_MODEL)
        prev[...] = jnp.zeros_like(prev)

    router = _route_tile(prev[...], gr_ref[...], wr_ref, br_ref, hn_ref, gate_ref, order_ref, count_ref,
                         jnp.maximum(step - 1, 0), n_tokens)

    scale = X_HEAD_DIM ** -0.5
    heads = [slice(h * X_HEAD_DIM, (h + 1) * X_HEAD_DIM) for h in range(X_HEADS)]
    chunks = [slice(n0, n0 + XATT_CHUNK) for n0 in range(0, D_MODEL, XATT_CHUNK)]

    def norm(rows):
        u_scr[rows, :] = _rms(h_ref[rows, :], g_ref[...]).astype(BF16)

    def q_proj(rows):
        for cols in chunks:
            q_scr[rows, cols] = jnp.dot(u_scr[rows, :], wqb[:, cols], preferred_element_type=F32).astype(BF16)

    def scores(rows):
        return [_nt(q_scr[rows, hd], k_ref[0, :, hd]) * scale for hd in heads]

    def softmax(s_list):
        out = []
        for s in s_list:
            p = jnp.exp(s - jnp.max(s, axis=-1, keepdims=True))
            out.append((p.astype(BF16), jnp.sum(p, axis=-1, keepdims=True)))
        return out

    def attend(rows, probs):
        for hd, (p, denom) in zip(heads, probs):
            a_scr[rows, hd] = (jnp.dot(p, v_ref[0, :, hd], preferred_element_type=F32) / denom).astype(BF16)

    def o_proj(rows):
        for cols in chunks:
            out = h_ref[rows, cols] + jnp.dot(a_scr[rows, :], wob[:, cols], preferred_element_type=F32)
            o_ref[rows, cols] = out
            prev[rows, cols] = out

    half = XATT_TQ // 2
    first, second = slice(0, half), slice(half, XATT_TQ)
    norm(first)
    q_proj(first)
    next(router)
    norm(second)
    s_first = scores(first)
    q_proj(second)
    next(router)
    p_first = softmax(s_first)
    attend(first, p_first)
    s_second = scores(second)
    o_proj(first)
    for _ in router:
        pass
    p_second = softmax(s_second)
    attend(second, p_second)
    o_proj(second)


def _cross_block(h, gain, w_cq, kx, vx, w_co, router_gain, router_w, router_b, layer, batch, seq, mem_len):
    rows = h.shape[0]
    n_tiles = rows // XATT_TQ
    tiles_per_batch = seq // XATT_TQ
    k3 = kx.reshape(batch, mem_len, D_MODEL)
    v3 = vx.reshape(batch, mem_len, D_MODEL)
    as_row = lambda g: g.reshape(1, D_MODEL).astype(F32)
    cur = lambda i: jnp.minimum(i, n_tiles - 1)
    lag = lambda i: jnp.maximum(i - 1, 0)
    tile = pl.BlockSpec((XATT_TQ, D_MODEL), lambda i: (cur(i), 0))
    mem_spec = pl.BlockSpec((1, mem_len, D_MODEL), lambda i: (cur(i) // tiles_per_batch, 0, 0))
    full = lambda a: pl.BlockSpec(a.shape, lambda i: (0, 0))
    lag_rows = lambda width: pl.BlockSpec((XATT_TQ, width), lambda i: (lag(i), 0))
    lag_tile = lambda width: pl.BlockSpec((1, 8, width), lambda i: (lag(i), 0, 0))
    hbm = pl.BlockSpec(memory_space=pl.ANY)
    g_cross, g_router = as_row(gain), as_row(router_gain)
    return pl.pallas_call(
        functools.partial(_cross_body, layer=layer, n_tokens=rows), grid=(n_tiles + 1,),
        in_specs=[tile, full(g_cross), hbm, mem_spec, mem_spec, hbm, full(g_router), full(router_w), full(router_b)],
        out_specs=[tile, lag_rows(HALF), lag_rows(LANES), lag_tile(TOP_K * XATT_TQ), lag_tile(LANES)],
        out_shape=[jax.ShapeDtypeStruct((rows, D_MODEL), F32),
                   jax.ShapeDtypeStruct((rows, HALF), jnp.uint32),
                   jax.ShapeDtypeStruct((rows, LANES), F32),
                   jax.ShapeDtypeStruct((n_tiles, 8, TOP_K * XATT_TQ), jnp.int32),
                   jax.ShapeDtypeStruct((n_tiles, 8, LANES), jnp.int32)],
        scratch_shapes=[pltpu.VMEM((D_MODEL, D_MODEL), BF16), pltpu.VMEM((D_MODEL, D_MODEL), BF16),
                        pltpu.VMEM((2, W_STAGE_ROWS, D_MODEL), F32), pltpu.SemaphoreType.DMA((2,)),
                        pltpu.VMEM((XATT_TQ, D_MODEL), BF16), pltpu.VMEM((XATT_TQ, D_MODEL), BF16),
                        pltpu.VMEM((XATT_TQ, D_MODEL), BF16), pltpu.VMEM((XATT_TQ, D_MODEL), F32)],
        compiler_params=_params(("arbitrary",)), name="cross_block",
    )(h, g_cross, w_cq, k3, v3, w_co, g_router, router_w, router_b)


HALF = D_MODEL // 2


def _pack_halves(x):
    return pltpu.pack_elementwise([x[:, :HALF], x[:, HALF:]], packed_dtype=BF16)


def _unpack_halves(words):
    return [pltpu.unpack_elementwise(words, index=i, packed_dtype=BF16, unpacked_dtype=F32) for i in range(2)]


ID_SPLIT = 32


def _route_tile(h, gain, w_ref, b_ref, hn_ref, gate_ref, order_ref, count_ref, tile_idx, n_tokens):
    hn = _rms(h, gain)
    hn_ref[...] = _pack_halves(hn)
    logits = jnp.dot(hn.astype(BF16), w_ref[...], preferred_element_type=F32) + b_ref[...]
    yield
    lane =lax.broadcasted_iota(jnp.int32, logits.shape, 1).astype(F32)
    ninf = -jnp.inf
    first = lambda hit: jnp.min(jnp.where(hit, lane, float(LANES)), axis=-1, keepdims=True)
    in_groups = lane < N_GROUPS
    gl = jnp.where(in_groups, logits, ninf)
    gmax = jnp.max(gl, axis=-1, keepdims=True)
    g_idx = first(gl == gmax)
    p_group = 1.0 / jnp.sum(jnp.where(in_groups, jnp.exp(logits - gmax), 0.0), axis=-1, keepdims=True)
    lo = N_GROUPS + EXPERTS_PER_GROUP * g_idx
    el = jnp.where((lane >= lo) & (lane < lo + EXPERTS_PER_GROUP), logits, ninf)
    e1 = jnp.max(el, axis=-1, keepdims=True)
    i1 = first(el == e1)
    el2 = jnp.where(lane == i1, ninf, el)
    e2 = jnp.max(el2, axis=-1, keepdims=True)
    i2 = first(el2 == e2)
    t = jnp.exp(e2 - e1)
    w1 = p_group / (1.0 + t)
    w2 = p_group * t / (1.0 + t)
    gate_ref[...] = jnp.where(lane == 0, w1, jnp.where(lane == 1, w2, 0.0))

    tm = logits.shape[0]
    hit0, hit1 = lane == i1 - N_GROUPS, lane == i2 - N_GROUPS
    member = jnp.where(hit0 | hit1, 1.0, 0.0)
    member_b = member.astype(BF16)
    earlier = (lax.broadcasted_iota(jnp.int32, (tm, tm), 1) < lax.broadcasted_iota(jnp.int32, (tm, tm), 0))
    rank = jnp.dot(jnp.where(earlier, 1.0, 0.0).astype(BF16), member_b, preferred_element_type=F32)
    lower = (lax.broadcasted_iota(jnp.int32, (LANES, LANES), 0) < lax.broadcasted_iota(jnp.int32, (LANES, LANES), 1))
    run_start = jnp.sum(jnp.dot(member_b, jnp.where(lower, 1.0, 0.0).astype(BF16), preferred_element_type=F32),
                        axis=0, keepdims=True)
    yield
    pos = rank + run_start
    positions = [jnp.sum(jnp.where(hit, pos, 0.0), axis=-1, keepdims=True) for hit in (hit0, hit1)]
    out_lane = lax.broadcasted_iota(jnp.int32, (tm, TOP_K * tm), 1).astype(F32)
    digit_row = lax.broadcasted_iota(jnp.int32, (8, tm), 0)
    local_tok = lax.broadcasted_iota(jnp.int32, (8, tm), 1)
    digits = None
    for s, p in enumerate(positions):
        a = s * tm + local_tok
        lhs = jnp.where(digit_row == 0, a // ID_SPLIT, jnp.where(digit_row == 1, a % ID_SPLIT, 0))
        part = jnp.dot(lhs.astype(F32).astype(BF16), jnp.where(out_lane == p, 1.0, 0.0).astype(BF16),
                       preferred_element_type=F32)
        digits = part if digits is None else digits + part
    a_sorted = (digits[0:1] * ID_SPLIT + digits[1:2]).astype(jnp.int32)
    tok = tile_idx * tm + a_sorted % tm
    row8 = lax.broadcasted_iota(jnp.int32, (8, TOP_K * tm), 0)
    order_ref[0] = jnp.where(row8 == 0, tok, jnp.where(row8 == 1, (a_sorted // tm) * n_tokens + tok, 0))
    count_ref[0] = jnp.broadcast_to(jnp.sum(member, axis=0, keepdims=True), (8, LANES)).astype(jnp.int32)


def _router_params(w_rg, b_rg, w_re, b_re):
    n_log = N_GROUPS + N_EXPERTS
    w = jnp.zeros((D_MODEL, LANES), F32).at[:, :N_GROUPS].set(w_rg).at[:, N_GROUPS:n_log].set(w_re).astype(BF16)
    b = jnp.zeros((1, LANES), F32).at[0, :N_GROUPS].set(b_rg).at[0, N_GROUPS:n_log].set(b_re)
    return w, b


W_CAST_ROWS = 256


def _cast_weight(src_ref, dst_ref):
    def body(c, carry):
        rows = pl.ds(pl.multiple_of(c * W_CAST_ROWS, W_CAST_ROWS), W_CAST_ROWS)
        dst_ref[rows, :] = src_ref[rows, :].astype(BF16)
        return carry
    lax.fori_loop(0, src_ref.shape[0] // W_CAST_ROWS, body, 0)


def _expert_changed(blk_e, blk):
    prev = blk_e[jnp.maximum(blk - 1, 0)]
    return (blk == 0) | (blk_e[blk] != prev)


WEIGHT_DMA_PRIORITY = 1


def _switch_expert(blk_e, nxt_e, blk, layer, w_hbms, stages, dsts, sem):
    def copies(e):
        return [pltpu.make_async_copy(w.at[layer, e], st, sem.at[i])
                for i, (w, st) in enumerate(zip(w_hbms, stages))]

    @pl.when(blk == 0)
    def _():
        for cp in copies(blk_e[0]):
            cp.start(priority=WEIGHT_DMA_PRIORITY)

    @pl.when(_expert_changed(blk_e, blk))
    def _():
        for cp, st, dst in zip(copies(blk_e[blk]), stages, dsts):
            cp.wait()
            _cast_weight(st, dst)

        @pl.when(nxt_e[blk] >= 0)
        def _():
            for cp in copies(nxt_e[blk]):
                cp.start(priority=WEIGHT_DMA_PRIORITY)


GATHER_AHEAD = 3
GATHER_SLOTS = GATHER_AHEAD + 1
MOE_CHUNKS = 4


SCATTER_SLOTS = 3


def _experts_body(blk_e, n_real, nxt_e, *refs, n_assign, layer):
    tok_refs = refs[:GATHER_AHEAD + 1]
    (dst_ref, dst_prev_ref, hn_hbm, wg_hbm, wu_hbm, wd_hbm, out_hbm, xs, xb, hid, ys,
     wg_stage, wu_stage, wd_stage, wgb, wub, wdb, gsem, ssem, wsem) = refs[GATHER_AHEAD + 1:]
    blk = pl.program_id(0)
    dummy_slot = SCATTER_SLOTS - 1

    def gather_row(idx_ref, slot, r):
        return pltpu.make_async_copy(hn_hbm.at[pl.ds(idx_ref[0, 0, r], 1), :], xs.at[slot, pl.ds(r, 1), :],
                                     gsem.at[slot])

    def scatter_row(idx_ref, slot, r):
        return pltpu.make_async_copy(ys.at[slot, pl.ds(r, 1), :], out_hbm.at[pl.ds(idx_ref[0, 0, r], 1), :],
                                     ssem.at[slot])

    def wait_gather(slot):
        pltpu.make_async_copy(hn_hbm.at[pl.ds(0, MOE_BLOCK), :], xs.at[slot], gsem.at[slot]).wait()

    def wait_scatter(slot):
        pltpu.make_async_copy(ys.at[slot], out_hbm.at[pl.ds(0, MOE_BLOCK), :], ssem.at[slot]).wait()

    @pl.when(blk == 0)
    def _():
        def body(r, carry):
            for s in range(GATHER_AHEAD):
                gather_row(tok_refs[s], s, r).start()
            return carry
        lax.fori_loop(0, MOE_BLOCK, body, 0)
        ys[dummy_slot] = jnp.zeros((MOE_BLOCK, HALF), jnp.uint32)
        for s in range(SCATTER_SLOTS):
            trash = pltpu.make_async_copy(
                ys.at[dummy_slot], out_hbm.at[pl.ds(n_assign + s * MOE_BLOCK, MOE_BLOCK), :], ssem.at[dummy_slot])
            trash.start()
            trash.wait()

    @pl.when(blk < n_real[0])
    def _():
        in_slot = blk % GATHER_SLOTS
        ahead_slot = (blk + GATHER_AHEAD) % GATHER_SLOTS
        out_slot = blk % SCATTER_SLOTS
        prev_slot = (blk + SCATTER_SLOTS - 1) % SCATTER_SLOTS

        _switch_expert(blk_e, nxt_e, blk, layer, (wg_hbm, wu_hbm, wd_hbm), (wg_stage, wu_stage, wd_stage),
                       (wgb, wub, wdb), wsem)
        wait_gather(in_slot)

        @pl.when(blk >= SCATTER_SLOTS - 1)
        def _():
            wait_scatter(out_slot)

        lo, hi = _unpack_halves(xs[in_slot])
        xb[:, :HALF] = lo.astype(BF16)
        xb[:, HALF:] = hi.astype(BF16)
        for r in range(MOE_BLOCK):
            scatter_row(dst_prev_ref, prev_slot, r).start(priority=r % 2)
            gather_row(tok_refs[GATHER_AHEAD], ahead_slot, r).start()
        cw = D_FF_EXPERT // MOE_CHUNKS
        for c in range(MOE_CHUNKS):
            cols = slice(c * cw, (c + 1) * cw)
            a = jnp.dot(xb[...], wgb[:, cols], preferred_element_type=F32)
            u = jnp.dot(xb[...], wub[:, cols], preferred_element_type=F32)
            hid[:, cols] = (a * jax.nn.sigmoid(a) * u).astype(BF16)
        cw = HALF // MOE_CHUNKS
        for c in range(MOE_CHUNKS):
            cols = slice(c * cw, (c + 1) * cw)
            hi_cols = slice(HALF + c * cw, HALF + (c + 1) * cw)
            ys[out_slot, :, cols] = pltpu.pack_elementwise(
                [jnp.dot(hid[...], wdb[:, cols], preferred_element_type=F32),
                 jnp.dot(hid[...], wdb[:, hi_cols], preferred_element_type=F32)], packed_dtype=BF16)

        @pl.when(blk == n_real[0] - 1)
        def _():
            def body(r, carry):
                scatter_row(dst_ref, out_slot, r).start()
                return carry
            lax.fori_loop(0, MOE_BLOCK, body, 0)
            for s in range(1, GATHER_SLOTS):
                wait_gather((blk + s) % GATHER_SLOTS)
            for s in range(SCATTER_SLOTS):
                wait_scatter(s)


def _moe_forward(hn, order, counts, w_gate, w_up, w_down, layer, n_tokens):
    A = n_tokens * TOP_K
    out_rows = A + SCATTER_SLOTS * MOE_BLOCK
    R = A + N_EXPERTS * MOE_BLOCK
    n_blk = R // MOE_BLOCK
    i32 = jnp.int32
    n_tiles, per_tile = order.shape[0], order.shape[2]
    experts = jnp.arange(N_EXPERTS, dtype=i32)
    n = counts[:, 0, :N_EXPERTS]
    total = jnp.sum(n, axis=0)
    padded = ((total + MOE_BLOCK - 1) // MOE_BLOCK) * MOE_BLOCK
    pend = jnp.cumsum(padded)
    pstart = pend - padded
    n_real = (pend[-1] // MOE_BLOCK).astype(i32).reshape(1)
    blk = jnp.arange(n_blk, dtype=i32)
    blk_e = jnp.minimum(jnp.sum((pend[None, :] <= (blk * MOE_BLOCK)[:, None]).astype(i32), axis=1), N_EXPERTS - 1)
    is_e = (blk_e[:, None] == experts[None, :]).astype(i32)
    of_block = lambda per_expert: jnp.sum(is_e * per_expert[None, :], axis=1)
    cum_incl = jnp.cumsum(n, axis=0)
    run_shift = (jnp.cumsum(n, axis=1) - n) - (cum_incl - n)
    per_tile_of_block = lambda m: jnp.sum(is_e[:, None, :] * m[None, :, :], axis=2)
    cum_b, shift_b = per_tile_of_block(cum_incl), per_tile_of_block(run_shift)
    in_blk = jnp.arange(MOE_BLOCK, dtype=i32)
    k = (blk * MOE_BLOCK - of_block(pstart))[:, None] + in_blk[None, :]
    tile_of = jnp.minimum(jnp.sum((cum_b[:, None, :] <= k[:, :, None]).astype(i32), axis=2), n_tiles - 1)
    is_tile = (tile_of[:, :, None] == jnp.arange(n_tiles, dtype=i32)[None, None, :]).astype(i32)
    entry = tile_of * per_tile + k + jnp.sum(is_tile * shift_b[:, None, :], axis=2)
    pad_entry = A + (blk % SCATTER_SLOTS)[:, None] * MOE_BLOCK + in_blk[None, :]
    entry = jnp.where(k < of_block(total)[:, None], entry, pad_entry)
    n_pad = SCATTER_SLOTS * MOE_BLOCK
    tok_table = jnp.concatenate([order[:, 0, :].reshape(A), jnp.zeros((n_pad,), i32)])
    dst_table = jnp.concatenate([order[:, 1, :].reshape(A), A + jnp.arange(n_pad, dtype=i32)])
    row_tok = tok_table[entry].reshape(n_blk, 1, MOE_BLOCK)
    row_dst = dst_table[entry].reshape(n_blk, 1, MOE_BLOCK)
    row_dst_prev = jnp.concatenate(
        [(A + (SCATTER_SLOTS - 1) * MOE_BLOCK + in_blk).reshape(1, 1, MOE_BLOCK), row_dst[:-1]], axis=0)

    run_end = pend[blk_e] // MOE_BLOCK
    nxt_e = jnp.where(run_end < n_real[0], blk_e[jnp.minimum(run_end, n_blk - 1)], -1).astype(jnp.int32)

    clamp = lambda i, nr: jnp.minimum(i, nr[0] - 1)
    idx_spec = lambda shift: pl.BlockSpec((1, 1, MOE_BLOCK), lambda i, be, nr, nx: (clamp(i + shift, nr), 0, 0),
                                          memory_space=pltpu.SMEM)
    hbm = pl.BlockSpec(memory_space=pl.ANY)
    up_shape, down_shape = (D_MODEL, D_FF_EXPERT), (D_FF_EXPERT, D_MODEL)
    return pl.pallas_call(
        functools.partial(_experts_body, n_assign=A, layer=layer),
        grid_spec=pltpu.PrefetchScalarGridSpec(
            num_scalar_prefetch=3, grid=(n_blk,),
            in_specs=[idx_spec(s) for s in range(GATHER_AHEAD + 1)] + [idx_spec(0), idx_spec(0)] + [hbm] * 4,
            out_specs=hbm,
            scratch_shapes=[pltpu.VMEM((GATHER_SLOTS, MOE_BLOCK, HALF), jnp.uint32),
                            pltpu.VMEM((MOE_BLOCK, D_MODEL), BF16),
                            pltpu.VMEM((MOE_BLOCK, D_FF_EXPERT), BF16),
                            pltpu.VMEM((SCATTER_SLOTS, MOE_BLOCK, HALF), jnp.uint32),
                            pltpu.VMEM(up_shape, F32), pltpu.VMEM(up_shape, F32), pltpu.VMEM(down_shape, F32),
                            pltpu.VMEM(up_shape, BF16), pltpu.VMEM(up_shape, BF16), pltpu.VMEM(down_shape, BF16),
                            pltpu.SemaphoreType.DMA((GATHER_SLOTS,)), pltpu.SemaphoreType.DMA((SCATTER_SLOTS,)),
                            pltpu.SemaphoreType.DMA((3,))]),
        out_shape=jax.ShapeDtypeStruct((out_rows, HALF), jnp.uint32),
        compiler_params=_params(("arbitrary",)), name="moe_experts",
    )(blk_e, n_real, nxt_e, *([row_tok] * (GATHER_AHEAD + 1)), row_dst, row_dst_prev, hn, w_gate, w_up, w_down)


FINAL_TM = 512


def _final_body(h_ref, y0_ref, y1_ref, gate_ref, g_ref, o_ref):
    gate = gate_ref[...]
    y0, y1 = _unpack_halves(y0_ref[...]), _unpack_halves(y1_ref[...])
    halves = [h_ref[:, i * HALF:(i + 1) * HALF] + gate[:, 0:1] * y0[i] + gate[:, 1:2] * y1[i] for i in range(2)]
    mean_sq = sum(jnp.sum(hh * hh, axis=-1, keepdims=True) for hh in halves) / D_MODEL
    inv = lax.rsqrt(mean_sq + RMS_EPS)
    for i, hh in enumerate(halves):
        o_ref[:, i * HALF:(i + 1) * HALF] = hh * inv * g_ref[:, i * HALF:(i + 1) * HALF]


def _final(h, ys, gate, gain):
    rows = h.shape[0]
    n_tiles = rows // FINAL_TM
    tile = lambda width: pl.BlockSpec((FINAL_TM, width), lambda i: (i, 0))
    g2 = gain.reshape(1, D_MODEL).astype(F32)
    return pl.pallas_call(
        _final_body, grid=(n_tiles,),
        in_specs=[tile(D_MODEL), tile(HALF), pl.BlockSpec((FINAL_TM, HALF), lambda i: (i + n_tiles, 0)),
                  tile(LANES), pl.BlockSpec((1, D_MODEL), lambda i: (0, 0))],
        out_specs=tile(D_MODEL), out_shape=jax.ShapeDtypeStruct((rows, D_MODEL), F32),
        compiler_params=_params(("parallel",)), name="combine_final_norm")(h, ys, ys, gate, g2)


def kernel(x, mem, norm_mix, w_in, attn_out_norm, sink_logit, w_gla_gf, b_gla_gf, w_gla_gb, b_gla_gb, gla_out_norm, w_out, norm_cross, norm_mem, w_cq, w_ck, w_cv, w_co, norm_ffn, w_router_group, b_router_group, w_router_expert, b_router_expert, w_gate, w_up, w_down, norm_final):
    batch, seq, _ = x.shape
    mem_len = mem.shape[1]
    n_tokens = batch * seq
    h = x.reshape(n_tokens, D_MODEL)
    memf = mem.reshape(batch * mem_len, D_MODEL)
    assert norm_mix.shape[0] == 1, "the combine step is fused with the final norm: single-layer stacks only"
    for l in range(norm_mix.shape[0]):
        w_lr = jnp.zeros((D_MODEL, LANES), F32).at[:, :2 * GLA_LOWRANK].set(w_in[l][:, MAIN_COLS:]).astype(BF16)
        proj, lr = _dense([h], w_in, l, n_cols=MAIN_COLS, out_dtype=BF16, gain=norm_mix[l], extra_w=w_lr,
                          name="in_proj")
        o_a = _window_attention(proj, sink_logit[l], attn_out_norm[l], batch, seq).reshape(n_tokens, ATT_Q)
        o_g = _gla(proj, lr, w_gla_gf[l], b_gla_gf[l], w_gla_gb[l], b_gla_gb[l], gla_out_norm[l],
                   batch, seq).reshape(n_tokens, GLA_V)
        h = _dense([o_a, o_g], w_out, l, n_cols=D_MODEL, out_dtype=F32, res=h, name="out_proj")
        kx = _dense([memf], w_ck, l, n_cols=D_MODEL, out_dtype=BF16, gain=norm_mem[l], name="mem_k_proj")
        vx = _dense([memf], w_cv, l, n_cols=D_MODEL, out_dtype=BF16, gain=norm_mem[l], name="mem_v_proj")
        router_w, router_b = _router_params(w_router_group[l], b_router_group[l],
                                            w_router_expert[l], b_router_expert[l])
        h, hn, gate, order, counts = _cross_block(h, norm_cross[l], w_cq, kx, vx, w_co, norm_ffn[l],
                                                  router_w, router_b, l, batch, seq, mem_len)
        ys = _moe_forward(hn, order, counts, w_gate, w_up, w_down, l, n_tokens)
    return _final(h, ys, gate, norm_final).reshape(batch, seq, D_MODEL)
```

```python
import functools

import jax
import jax.numpy as jnp
from jax import lax
from jax.experimental import pallas as pl
from jax.experimental.pallas import tpu as pltpu

F32 = jnp.float32
BF16 = jnp.bfloat16

D_MODEL = 2048
N_Q_HEADS = 8
N_KV_HEADS = 2
Q_PER_KV = N_Q_HEADS // N_KV_HEADS
HEAD_DIM = 128
WINDOW = 128
WBLK = 128
GLA_HEADS = 4
GLA_DK = 128
GLA_DV = 256
GLA_LOWRANK = 16
GLA_TAU = 16.0
GLA_CHUNK = 64
ATT_Q = N_Q_HEADS * HEAD_DIM
ATT_KV = N_KV_HEADS * HEAD_DIM
GLA_QK = GLA_HEADS * GLA_DK
GLA_V = GLA_HEADS * GLA_DV
MAIN_COLS = ATT_Q + 2 * ATT_KV + 2 * GLA_QK + 2 * GLA_V
COL_KA = ATT_Q
COL_VA = COL_KA + ATT_KV
COL_QG = COL_VA + ATT_KV
COL_KG = COL_QG + GLA_QK
COL_VG = COL_KG + GLA_QK
COL_RG = COL_VG + GLA_V
X_HEADS = 4
X_HEAD_DIM = D_MODEL // X_HEADS
N_GROUPS = 4
EXPERTS_PER_GROUP = 8
N_EXPERTS = N_GROUPS * EXPERTS_PER_GROUP
TOP_K = 2
D_FF_EXPERT = D_MODEL // 2
MOE_BLOCK = 256
RMS_EPS = 1e-6
NEG_INF = -1e30

LANES = 128
VMEM_LIMIT = 56 * 1024 * 1024
W_STAGE_ROWS = 128


def _params(sem):
    return pltpu.CompilerParams(dimension_semantics=sem, vmem_limit_bytes=VMEM_LIMIT)


def _nt(a, b):
    return lax.dot_general(a, b, (((1,), (1,)), ((), ())), preferred_element_type=F32)


def _tn(a, b):
    return lax.dot_general(a, b, (((0,), (0,)), ((), ())), preferred_element_type=F32)


def _rms(x, gain):
    return x * lax.rsqrt(jnp.mean(x * x, axis=-1, keepdims=True) + RMS_EPS) * gain


def _load_weight_bf16(w_hbm, layer, wb, stage, sem, k_rows, n_cols):
    n_chunks = k_rows // W_STAGE_ROWS

    def copy(c):
        return pltpu.make_async_copy(
            w_hbm.at[layer, pl.ds(c * W_STAGE_ROWS, W_STAGE_ROWS), pl.ds(0, n_cols)],
            stage.at[c % 2], sem.at[c % 2])

    copy(0).start()
    for c in range(n_chunks):
        if c + 1 < n_chunks:
            copy(c + 1).start()
        copy(c).wait()
        wb[c * W_STAGE_ROWS:(c + 1) * W_STAGE_ROWS, :] = stage[c % 2].astype(BF16)


def _dense_body(*refs, part_widths, has_norm, has_extra, has_res, n_cols, n_chunk, layer):
    it = iter(refs)
    x_refs = [next(it) for _ in part_widths]
    g_ref = next(it) if has_norm else None
    w_hbm = next(it)
    ew_ref = next(it) if has_extra else None
    res_ref = next(it) if has_res else None
    o_ref = next(it)
    eo_ref = next(it) if has_extra else None
    wb, stage, sem = next(it), next(it), next(it)
    u_ref = next(it) if has_norm else None
    k_rows = sum(part_widths)

    @pl.when(pl.program_id(0) == 0)
    def _():
        _load_weight_bf16(w_hbm, layer, wb, stage, sem, k_rows, n_cols)

    if has_norm:
        u_ref[...] = _rms(x_refs[0][...], g_ref[...]).astype(BF16)
        lhs = [(u_ref, 0, k_rows)]
    else:
        lhs, off = [], 0
        for r, kw in zip(x_refs, part_widths):
            lhs.append((r, off, kw))
            off += kw
    for n0 in range(0, n_cols, n_chunk):
        acc = None
        for r, off, kw in lhs:
            d = jnp.dot(r[...], wb[off:off + kw, n0:n0 + n_chunk], preferred_element_type=F32)
            acc = d if acc is None else acc + d
        if has_res:
            acc = acc + res_ref[:, n0:n0 + n_chunk]
        o_ref[:, n0:n0 + n_chunk] = acc.astype(o_ref.dtype)
    if has_extra:
        eo_ref[...] = jnp.dot(u_ref[...], ew_ref[...], preferred_element_type=F32)


def _dense(xs, w, layer, *, n_cols, out_dtype, gain=None, extra_w=None, res=None, tm=512, n_chunk=512, name):
    rows = xs[0].shape[0]
    part_widths = tuple(x.shape[1] for x in xs)
    k_rows = sum(part_widths)
    has_norm, has_extra, has_res = gain is not None, extra_w is not None, res is not None
    row_spec = lambda width: pl.BlockSpec((tm, width), lambda i: (i, 0))
    full_spec = lambda a: pl.BlockSpec(a.shape, lambda i: (0, 0))
    args, in_specs = list(xs), [row_spec(kw) for kw in part_widths]
    if has_norm:
        args.append(gain.reshape(1, k_rows).astype(F32))
        in_specs.append(full_spec(args[-1]))
    args.append(w)
    in_specs.append(pl.BlockSpec(memory_space=pl.ANY))
    if has_extra:
        args.append(extra_w)
        in_specs.append(full_spec(extra_w))
    if has_res:
        args.append(res)
        in_specs.append(row_spec(n_cols))
    out_shape = [jax.ShapeDtypeStruct((rows, n_cols), out_dtype)]
    out_specs = [row_spec(n_cols)]
    if has_extra:
        out_shape.append(jax.ShapeDtypeStruct((rows, extra_w.shape[1]), F32))
        out_specs.append(row_spec(extra_w.shape[1]))
    scratch = [pltpu.VMEM((k_rows, n_cols), BF16),
               pltpu.VMEM((2, W_STAGE_ROWS, n_cols), F32),
               pltpu.SemaphoreType.DMA((2,))]
    if has_norm:
        scratch.append(pltpu.VMEM((tm, k_rows), BF16))
    body = functools.partial(_dense_body, part_widths=part_widths, has_norm=has_norm, has_extra=has_extra,
                             has_res=has_res, n_cols=n_cols, n_chunk=n_chunk, layer=layer)
    outs = pl.pallas_call(
        body, grid=(rows // tm,), in_specs=in_specs, out_specs=out_specs, out_shape=out_shape,
        scratch_shapes=scratch, compiler_params=_params(("arbitrary",)), name=name)(*args)
    return outs if has_extra else outs[0]


ATT_TQ = 512


def _winattn_body(sink_ref, q_ref, kp_ref, km_ref, kn_ref, vp_ref, vm_ref, vn_ref, g_ref, o_ref,
                  kcat, vcat, obuf, *, seq):
    s0 = pl.program_id(1) * ATT_TQ
    kcat[0:WBLK, :] = kp_ref[0]
    kcat[WBLK:WBLK + ATT_TQ, :] = km_ref[0]
    kcat[WBLK + ATT_TQ:, :] = kn_ref[0]
    vcat[0:WBLK, :] = vp_ref[0]
    vcat[WBLK:WBLK + ATT_TQ, :] = vm_ref[0]
    vcat[WBLK + ATT_TQ:, :] = vn_ref[0]
    qi = lax.broadcasted_iota(jnp.int32, (WBLK, 3 * WBLK), 0) + WBLK
    ki = lax.broadcasted_iota(jnp.int32, (WBLK, 3 * WBLK), 1)
    dist_i = jnp.abs(ki - qi)
    neg_dist = jnp.where(dist_i <= WINDOW, -dist_i.astype(F32), NEG_INF)
    scale = HEAD_DIM ** -0.5
    for qb in range(ATT_TQ // WBLK):
        kabs = s0 + (qb - 1) * WBLK + ki
        bias_unit = jnp.where((kabs >= 0) & (kabs < seq), neg_dist, NEG_INF)
        head_cols = lambda j: slice(j * HEAD_DIM, (j + 1) * HEAD_DIM)
        kv_cols = lambda j: head_cols(j // Q_PER_KV)
        key_rows = slice(qb * WBLK, (qb + 3) * WBLK)
        scores = []
        for j in range(N_Q_HEADS):
            slope = 2.0 ** (-8.0 * (j + 1) / N_Q_HEADS)
            q = q_ref[0, qb * WBLK:(qb + 1) * WBLK, head_cols(j)]
            scores.append(_nt(q, kcat[key_rows, kv_cols(j)]) * scale + slope * bias_unit)
        probs, denoms = [], []
        for j in range(N_Q_HEADS):
            sink = sink_ref[j]
            m = jnp.maximum(jnp.max(scores[j], axis=-1, keepdims=True), sink)
            p = jnp.exp(scores[j] - m)
            denoms.append(jnp.sum(p, axis=-1, keepdims=True) + jnp.exp(sink - m))
            probs.append(p.astype(BF16))
        for j in range(N_Q_HEADS):
            pv = jnp.dot(probs[j], vcat[key_rows, kv_cols(j)], preferred_element_type=F32)
            obuf[qb, :, head_cols(j)] = pv / denoms[j]
        o_ref[0, qb * WBLK:(qb + 1) * WBLK, :] = _rms(obuf[qb], g_ref[...]).astype(o_ref.dtype)


def _window_attention(proj, sink_logit, gain, batch, seq):
    p3 = proj.reshape(batch, seq, MAIN_COLS)
    nq = seq // ATT_TQ
    per = ATT_TQ // WBLK
    last = seq // WBLK - 1
    main = lambda col: pl.BlockSpec((1, ATT_TQ, ATT_KV), lambda b, i: (b, i, col))
    prev = lambda col: pl.BlockSpec((1, WBLK, ATT_KV), lambda b, i: (b, jnp.maximum(i * per - 1, 0), col))
    nxt = lambda col: pl.BlockSpec((1, WBLK, ATT_KV), lambda b, i: (b, jnp.minimum(i * per + per, last), col))
    ck, cv = COL_KA // ATT_KV, COL_VA // ATT_KV
    return pl.pallas_call(
        functools.partial(_winattn_body, seq=seq),
        grid=(batch, nq),
        in_specs=[pl.BlockSpec(memory_space=pltpu.SMEM),
                  pl.BlockSpec((1, ATT_TQ, ATT_Q), lambda b, i: (b, i, 0)),
                  prev(ck), main(ck), nxt(ck), prev(cv), main(cv), nxt(cv),
                  pl.BlockSpec((1, ATT_Q), lambda b, i: (0, 0))],
        out_specs=pl.BlockSpec((1, ATT_TQ, ATT_Q), lambda b, i: (b, i, 0)),
        out_shape=jax.ShapeDtypeStruct((batch, seq, ATT_Q), BF16),
        scratch_shapes=[pltpu.VMEM((ATT_TQ + 2 * WBLK, ATT_KV), BF16),
                        pltpu.VMEM((ATT_TQ + 2 * WBLK, ATT_KV), BF16),
                        pltpu.VMEM((ATT_TQ // WBLK, WBLK, ATT_Q), F32)],
        compiler_params=_params(("parallel", "parallel")), name="window_attention",
    )(sink_logit.astype(F32), p3, p3, p3, p3, p3, p3, p3, gain.reshape(1, ATT_Q).astype(F32))


GLA_SCAN_ROWS = 256
GLA_SCAN_UNROLL = 2
GLA_EPI_ROWS = 512
GLA_UNROLL = 8


def _split2(x):
    hi = x.astype(BF16)
    lo = (x - hi.astype(F32)).astype(BF16)
    return hi, lo


def _gla_body(q_ref, k_ref, v_ref, r_ref, lr_ref, wf_ref, wb_ref, bf_ref, bb_ref, gn_ref, o_ref,
              cum_f, cum_b, acc, st_f, st_b, *, seq):
    C = GLA_CHUNK
    n_chunks = seq // C
    ri = lax.broadcasted_iota(jnp.int32, (GLA_SCAN_ROWS, GLA_SCAN_ROWS), 0)
    ci = lax.broadcasted_iota(jnp.int32, (GLA_SCAN_ROWS, GLA_SCAN_ROWS), 1)
    same = (ri // C) == (ci // C)
    tri_f = jnp.where(same & (ci <= ri), 1.0, 0.0).astype(BF16)
    tri_b = jnp.where(same & (ci >= ri), 1.0, 0.0).astype(BF16)

    def scan_body(t, carry):
        row_sets = [pl.ds(pl.multiple_of((t * GLA_SCAN_UNROLL + u) * GLA_SCAN_ROWS, GLA_SCAN_ROWS), GLA_SCAN_ROWS)
                    for u in range(GLA_SCAN_UNROLL)]
        lrs = [lr_ref[0, rows, :].astype(BF16) for rows in row_sets]
        zs = [jnp.dot(lr, w_ref[...], preferred_element_type=F32) + b_ref[...]
              for lr in lrs for w_ref, b_ref in ((wf_ref, bf_ref), (wb_ref, bb_ref))]
        gs = [_split2((jnp.minimum(z, 0.0) - jnp.log1p(jnp.exp(-jnp.abs(z)))) / GLA_TAU) for z in zs]
        for n, g3 in enumerate(gs):
            tri, dst = ((tri_f, cum_f), (tri_b, cum_b))[n % 2]
            dst[row_sets[n // 2], :] = sum(jnp.dot(tri, part, preferred_element_type=F32) for part in g3)
        return carry

    lax.fori_loop(0, seq // (GLA_SCAN_ROWS * GLA_SCAN_UNROLL), scan_body, 0)

    acc[...] = jnp.zeros_like(acc)
    st_f[...] = jnp.zeros_like(st_f)
    st_b[...] = jnp.zeros_like(st_b)
    rr = lax.broadcasted_iota(jnp.int32, (C, C), 0)
    cc = lax.broadcasted_iota(jnp.int32, (C, C), 1)
    scale = GLA_DK ** -0.5

    def chunk_body(i, carry):
        jobs = []
        for u in range(GLA_UNROLL):
            c = i * GLA_UNROLL + u
            jobs += [(c, cum_f, True), (n_chunks - 1 - c, cum_b, False)]
        prep = []
        for c, cum, forward in jobs:
            rows = pl.ds(pl.multiple_of(c * C, C), C)
            b = cum[rows, :]
            b_end = b[C - 1:C, :] if forward else b[0:1, :]
            q = q_ref[0, rows, :].astype(F32) * scale
            k = k_ref[0, rows, :].astype(F32)
            v = v_ref[0, rows, :]
            q_dec = (q * jnp.exp(b)).astype(BF16)
            k_inc = (k * jnp.exp(-b)).astype(BF16)
            k_dec = (k * jnp.exp(b_end - b)).astype(BF16)
            prep.append((rows, v, q_dec, k_inc, k_dec, jnp.exp(b_end)))
        attn = [_nt(q_dec, k_inc) for _, _, q_dec, k_inc, _, _ in prep]
        kv_t = [_tn(v, k_dec) for _, v, _, _, k_dec, _ in prep]
        state_t = {True: st_f[...], False: st_b[...]}
        o_inter = []
        for (_, _, forward), (_, _, q_dec, _, _, decay), kv in zip(jobs, prep, kv_t):
            o_inter.append(_nt(q_dec, state_t[forward].astype(BF16)))
            state_t[forward] = state_t[forward] * decay + kv
        st_f[...] = state_t[True]
        st_b[...] = state_t[False]
        for (_, _, forward), (rows, v, _, _, _, _), a, oi in zip(jobs, prep, attn, o_inter):
            a = jnp.where((cc <= rr) if forward else (cc >= rr), a, 0.0)
            o = jnp.dot(a.astype(BF16), v, preferred_element_type=F32) + oi
            acc[rows, :] = acc[rows, :] + o
        return carry

    lax.fori_loop(0, n_chunks // GLA_UNROLL, chunk_body, 0)

    def epi_body(t, carry):
        rows = pl.ds(pl.multiple_of(t * GLA_EPI_ROWS, GLA_EPI_ROWS), GLA_EPI_ROWS)
        r = r_ref[0, rows, :].astype(F32)
        o_ref[0, rows, :] = (_rms(acc[rows, :], gn_ref[...]) * (r * jax.nn.sigmoid(r))).astype(o_ref.dtype)
        return carry

    lax.fori_loop(0, seq // GLA_EPI_ROWS, epi_body, 0)


def _gla(proj, lr, w_gf, b_gf, w_gb, b_gb, gain, batch, seq):
    p3 = proj.reshape(batch, seq, MAIN_COLS)
    lr3 = lr.reshape(batch, seq, LANES)
    wf = jnp.zeros((LANES, GLA_QK), F32).at[:GLA_LOWRANK].set(w_gf).astype(BF16)
    wb = jnp.zeros((LANES, GLA_QK), F32).at[GLA_LOWRANK:2 * GLA_LOWRANK].set(w_gb).astype(BF16)
    seq_blk = lambda width, col0: pl.BlockSpec((1, seq, width), lambda b, h: (b, 0, col0 // width + h))
    head_w = pl.BlockSpec((LANES, GLA_DK), lambda b, h: (0, h))
    head_b = pl.BlockSpec((1, GLA_DK), lambda b, h: (0, h))
    return pl.pallas_call(
        functools.partial(_gla_body, seq=seq),
        grid=(batch, GLA_HEADS),
        in_specs=[seq_blk(GLA_DK, COL_QG), seq_blk(GLA_DK, COL_KG), seq_blk(GLA_DV, COL_VG),
                  seq_blk(GLA_DV, COL_RG),
                  pl.BlockSpec((1, seq, LANES), lambda b, h: (b, 0, 0)),
                  head_w, head_w, head_b, head_b,
                  pl.BlockSpec((1, GLA_DV), lambda b, h: (0, 0))],
        out_specs=pl.BlockSpec((1, seq, GLA_DV), lambda b, h: (b, 0, h)),
        out_shape=jax.ShapeDtypeStruct((batch, seq, GLA_V), BF16),
        scratch_shapes=[pltpu.VMEM((seq, GLA_DK), F32), pltpu.VMEM((seq, GLA_DK), F32),
                        pltpu.VMEM((seq, GLA_DV), F32),
                        pltpu.VMEM((GLA_DV, GLA_DK), F32), pltpu.VMEM((GLA_DV, GLA_DK), F32)],
        compiler_params=_params(("parallel", "parallel")), name="gla",
    )(p3, p3, p3, p3, lr3, wf, wb, b_gf.reshape(1, GLA_QK).astype(F32), b_gb.reshape(1, GLA_QK).astype(F32),
      gain.reshape(1, GLA_DV).astype(F32))


MEM_TM = 512
MEM_K_CHUNK = 256


def _mem_kv_body(m_ref, g_ref, wk_ref, wv_ref, k_ref, v_ref, u, acc_k, acc_v):
    step = pl.program_id(1)
    n_chunks = D_MODEL // MEM_K_CHUNK

    @pl.when(step == 0)
    def _():
        normed = _rms(m_ref[...], g_ref[...]).astype(BF16)
        for c in range(n_chunks):
            u[c] = normed[:, c * MEM_K_CHUNK:(c + 1) * MEM_K_CHUNK]
        acc_k[...] = jnp.zeros_like(acc_k)
        acc_v[...] = jnp.zeros_like(acc_v)

    lhs = u[step]
    acc_k[...] = acc_k[...] + jnp.dot(lhs, wk_ref[...].astype(BF16), preferred_element_type=F32)
    acc_v[...] = acc_v[...] + jnp.dot(lhs, wv_ref[...].astype(BF16), preferred_element_type=F32)

    @pl.when(step == n_chunks - 1)
    def _():
        k_ref[...] = acc_k[...].astype(k_ref.dtype)
        v_ref[...] = acc_v[...].astype(v_ref.dtype)


def _mem_kv(mem_rows, gain, w_ck, w_cv, layer):
    rows = mem_rows.shape[0]
    n_chunks = D_MODEL // MEM_K_CHUNK
    tile = pl.BlockSpec((MEM_TM, D_MODEL), lambda i, c: (i, 0))
    w_spec = pl.BlockSpec((None, MEM_K_CHUNK, D_MODEL), lambda i, c: (layer, c, 0))
    g2 = gain.reshape(1, D_MODEL).astype(F32)
    out = jax.ShapeDtypeStruct((rows, D_MODEL), BF16)
    return pl.pallas_call(
        _mem_kv_body, grid=(rows // MEM_TM, n_chunks),
        in_specs=[tile, pl.BlockSpec((1, D_MODEL), lambda i, c: (0, 0)), w_spec, w_spec],
        out_specs=[tile, tile], out_shape=[out, out],
        scratch_shapes=[pltpu.VMEM((n_chunks, MEM_TM, MEM_K_CHUNK), BF16),
                        pltpu.VMEM((MEM_TM, D_MODEL), F32), pltpu.VMEM((MEM_TM, D_MODEL), F32)],
        compiler_params=_params(("parallel", "arbitrary")), name="mem_kv_proj")(mem_rows, g2, w_ck, w_cv)


XATT_TQ = 512


XATT_CHUNK = 512


def _cross_body(h_ref, g_ref, wq_hbm, k_ref, v_ref, wo_hbm, gr_ref, wr_ref, br_ref,
                o_ref, hn_ref, gate_ref, order_ref, count_ref,
                wqb, wob, stage, sem, u_scr, q_scr, a_scr, prev, *, layer, n_tokens):
    step = pl.program_id(0)

    @pl.when(step == 0)
    def _():
        _load_weight_bf16(wq_hbm, layer, wqb, stage, sem, D_MODEL, D_MODEL)
        _load_weight_bf16(wo_hbm, layer, wob, stage, sem, D_MODEL, D_MODEL)
        prev[...] = jnp.zeros_like(prev)

    router = _route_tile(prev[...], gr_ref[...], wr_ref, br_ref, hn_ref, gate_ref, order_ref, count_ref,
                         jnp.maximum(step - 1, 0), n_tokens)

    scale = X_HEAD_DIM ** -0.5
    heads = [slice(h * X_HEAD_DIM, (h + 1) * X_HEAD_DIM) for h in range(X_HEADS)]
    chunks = [slice(n0, n0 + XATT_CHUNK) for n0 in range(0, D_MODEL, XATT_CHUNK)]

    def norm(rows):
        u_scr[rows, :] = _rms(h_ref[rows, :], g_ref[...]).astype(BF16)

    def q_proj(rows):
        for cols in chunks:
            q_scr[rows, cols] = jnp.dot(u_scr[rows, :], wqb[:, cols], preferred_element_type=F32).astype(BF16)

    def scores(rows):
        return [_nt(q_scr[rows, hd], k_ref[0, :, hd]) * scale for hd in heads]

    def softmax(s_list):
        out = []
        for s in s_list:
            p = jnp.exp(s - jnp.max(s, axis=-1, keepdims=True))
            out.append((p.astype(BF16), jnp.sum(p, axis=-1, keepdims=True)))
        return out

    def attend(rows, probs):
        for hd, (p, denom) in zip(heads, probs):
            a_scr[rows, hd] = (jnp.dot(p, v_ref[0, :, hd], preferred_element_type=F32) / denom).astype(BF16)

    def o_proj(rows):
        for cols in chunks:
            out = h_ref[rows, cols] + jnp.dot(a_scr[rows, :], wob[:, cols], preferred_element_type=F32)
            o_ref[rows, cols] = out
            prev[rows, cols] = out

    half = XATT_TQ // 2
    first, second = slice(0, half), slice(half, XATT_TQ)
    norm(first)
    q_proj(first)
    next(router)
    norm(second)
    s_first = scores(first)
    q_proj(second)
    next(router)
    p_first = softmax(s_first)
    attend(first, p_first)
    s_second = scores(second)
    o_proj(first)
    for _ in router:
        pass
    p_second = softmax(s_second)
    attend(second, p_second)
    o_proj(second)


def _cross_block(h, gain, w_cq, kx, vx, w_co, router_gain, router_w, router_b, layer, batch, seq, mem_len):
    rows = h.shape[0]
    n_tiles = rows // XATT_TQ
    tiles_per_batch = seq // XATT_TQ
    k3 = kx.reshape(batch, mem_len, D_MODEL)
    v3 = vx.reshape(batch, mem_len, D_MODEL)
    as_row = lambda g: g.reshape(1, D_MODEL).astype(F32)
    cur = lambda i: jnp.minimum(i, n_tiles - 1)
    lag = lambda i: jnp.maximum(i - 1, 0)
    tile = pl.BlockSpec((XATT_TQ, D_MODEL), lambda i: (cur(i), 0))
    mem_spec = pl.BlockSpec((1, mem_len, D_MODEL), lambda i: (cur(i) // tiles_per_batch, 0, 0))
    full = lambda a: pl.BlockSpec(a.shape, lambda i: (0, 0))
    lag_rows = lambda width: pl.BlockSpec((XATT_TQ, width), lambda i: (lag(i), 0))
    lag_tile = lambda width: pl.BlockSpec((1, 8, width), lambda i: (lag(i), 0, 0))
    hbm = pl.BlockSpec(memory_space=pl.ANY)
    g_cross, g_router = as_row(gain), as_row(router_gain)
    return pl.pallas_call(
        functools.partial(_cross_body, layer=layer, n_tokens=rows), grid=(n_tiles + 1,),
        in_specs=[tile, full(g_cross), hbm, mem_spec, mem_spec, hbm, full(g_router), full(router_w), full(router_b)],
        out_specs=[tile, lag_rows(HALF), lag_rows(LANES), lag_tile(TOP_K * XATT_TQ), lag_tile(LANES)],
        out_shape=[jax.ShapeDtypeStruct((rows, D_MODEL), F32),
                   jax.ShapeDtypeStruct((rows, HALF), jnp.uint32),
                   jax.ShapeDtypeStruct((rows, LANES), F32),
                   jax.ShapeDtypeStruct((n_tiles, 8, TOP_K * XATT_TQ), jnp.int32),
                   jax.ShapeDtypeStruct((n_tiles, 8, LANES), jnp.int32)],
        scratch_shapes=[pltpu.VMEM((D_MODEL, D_MODEL), BF16), pltpu.VMEM((D_MODEL, D_MODEL), BF16),
                        pltpu.VMEM((2, W_STAGE_ROWS, D_MODEL), F32), pltpu.SemaphoreType.DMA((2,)),
                        pltpu.VMEM((XATT_TQ, D_MODEL), BF16), pltpu.VMEM((XATT_TQ, D_MODEL), BF16),
                        pltpu.VMEM((XATT_TQ, D_MODEL), BF16), pltpu.VMEM((XATT_TQ, D_MODEL), F32)],
        compiler_params=_params(("arbitrary",)), name="cross_block",
    )(h, g_cross, w_cq, k3, v3, w_co, g_router, router_w, router_b)


HALF = D_MODEL // 2


def _pack_halves(x):
    return pltpu.pack_elementwise([x[:, :HALF], x[:, HALF:]], packed_dtype=BF16)


def _unpack_halves(words):
    return [pltpu.unpack_elementwise(words, index=i, packed_dtype=BF16, unpacked_dtype=F32) for i in range(2)]


ID_SPLIT = 32


def _route_tile(h, gain, w_ref, b_ref, hn_ref, gate_ref, order_ref, count_ref, tile_idx, n_tokens):
    hn = _rms(h, gain)
    hn_ref[...] = _pack_halves(hn)
    logits = jnp.dot(hn.astype(BF16), w_ref[...], preferred_element_type=F32) + b_ref[...]
    yield
    lane = lax.broadcasted_iota(jnp.int32, logits.shape, 1).astype(F32)
    ninf = -jnp.inf
    first = lambda hit: jnp.min(jnp.where(hit, lane, float(LANES)), axis=-1, keepdims=True)
    in_groups = lane < N_GROUPS
    gl = jnp.where(in_groups, logits, ninf)
    gmax = jnp.max(gl, axis=-1, keepdims=True)
    g_idx = first(gl == gmax)
    p_group = 1.0 / jnp.sum(jnp.where(in_groups, jnp.exp(logits - gmax), 0.0), axis=-1, keepdims=True)
    lo = N_GROUPS + EXPERTS_PER_GROUP * g_idx
    el = jnp.where((lane >= lo) & (lane < lo + EXPERTS_PER_GROUP), logits, ninf)
    e1 = jnp.max(el, axis=-1, keepdims=True)
    i1 = first(el == e1)
    el2 = jnp.where(lane == i1, ninf, el)
    e2 = jnp.max(el2, axis=-1, keepdims=True)
    i2 = first(el2 == e2)
    t = jnp.exp(e2 - e1)
    w1 = p_group / (1.0 + t)
    w2 = p_group * t / (1.0 + t)
    gate_ref[...] = jnp.where(lane == 0, w1, jnp.where(lane == 1, w2, 0.0))

    tm = logits.shape[0]
    hit0, hit1 = lane == i1 - N_GROUPS, lane == i2 - N_GROUPS
    member = jnp.where(hit0 | hit1, 1.0, 0.0)
    member_b = member.astype(BF16)
    earlier = (lax.broadcasted_iota(jnp.int32, (tm, tm), 1) < lax.broadcasted_iota(jnp.int32, (tm, tm), 0))
    rank = jnp.dot(jnp.where(earlier, 1.0, 0.0).astype(BF16), member_b, preferred_element_type=F32)
    lower = (lax.broadcasted_iota(jnp.int32, (LANES, LANES), 0) < lax.broadcasted_iota(jnp.int32, (LANES, LANES), 1))
    run_start = jnp.sum(jnp.dot(member_b, jnp.where(lower, 1.0, 0.0).astype(BF16), preferred_element_type=F32),
                        axis=0, keepdims=True)
    yield
    pos = rank + run_start
    positions = [jnp.sum(jnp.where(hit, pos, 0.0), axis=-1, keepdims=True) for hit in (hit0, hit1)]
    out_lane = lax.broadcasted_iota(jnp.int32, (tm, TOP_K * tm), 1).astype(F32)
    digit_row = lax.broadcasted_iota(jnp.int32, (8, tm), 0)
    local_tok = lax.broadcasted_iota(jnp.int32, (8, tm), 1)
    digits = None
    for s, p in enumerate(positions):
        a = s * tm + local_tok
        lhs = jnp.where(digit_row == 0, a // ID_SPLIT, jnp.where(digit_row == 1, a % ID_SPLIT, 0))
        part = jnp.dot(lhs.astype(F32).astype(BF16), jnp.where(out_lane == p, 1.0, 0.0).astype(BF16),
                       preferred_element_type=F32)
        digits = part if digits is None else digits + part
    a_sorted = (digits[0:1] * ID_SPLIT + digits[1:2]).astype(jnp.int32)
    tok = tile_idx * tm + a_sorted % tm
    row8 = lax.broadcasted_iota(jnp.int32, (8, TOP_K * tm), 0)
    order_ref[0] = jnp.where(row8 == 0, tok, jnp.where(row8 == 1, (a_sorted // tm) * n_tokens + tok, 0))
    count_ref[0] = jnp.broadcast_to(jnp.sum(member, axis=0, keepdims=True), (8, LANES)).astype(jnp.int32)


def _router_params(w_rg, b_rg, w_re, b_re):
    n_log = N_GROUPS + N_EXPERTS
    w = jnp.zeros((D_MODEL, LANES), F32).at[:, :N_GROUPS].set(w_rg).at[:, N_GROUPS:n_log].set(w_re).astype(BF16)
    b = jnp.zeros((1, LANES), F32).at[0, :N_GROUPS].set(b_rg).at[0, N_GROUPS:n_log].set(b_re)
    return w, b


W_CAST_ROWS = 256


def _cast_weight(src_ref, dst_ref):
    def body(c, carry):
        rows = pl.ds(pl.multiple_of(c * W_CAST_ROWS, W_CAST_ROWS), W_CAST_ROWS)
        dst_ref[rows, :] = src_ref[rows, :].astype(BF16)
        return carry
    lax.fori_loop(0, src_ref.shape[0] // W_CAST_ROWS, body, 0)


def _expert_changed(blk_e, blk):
    prev = blk_e[jnp.maximum(blk - 1, 0)]
    return (blk == 0) | (blk_e[blk] != prev)


WEIGHT_DMA_PRIORITY = 1


def _switch_expert(blk_e, nxt_e, blk, layer, w_hbms, stages, dsts, sem):
    def copies(e):
        return [pltpu.make_async_copy(w.at[layer, e], st, sem.at[i])
                for i, (w, st) in enumerate(zip(w_hbms, stages))]

    @pl.when(blk == 0)
    def _():
        for cp in copies(blk_e[0]):
            cp.start(priority=WEIGHT_DMA_PRIORITY)

    @pl.when(_expert_changed(blk_e, blk))
    def _():
        for cp, st, dst in zip(copies(blk_e[blk]), stages, dsts):
            cp.wait()
            _cast_weight(st, dst)

        @pl.when(nxt_e[blk] >= 0)
        def _():
            for cp in copies(nxt_e[blk]):
                cp.start(priority=WEIGHT_DMA_PRIORITY)


GATHER_AHEAD = 3
GATHER_SLOTS = GATHER_AHEAD + 1
MOE_CHUNKS = 4


SCATTER_SLOTS = 3


IDX_ROWS = 8
IDX_DST = GATHER_AHEAD + 1
IDX_DST_PREV = GATHER_AHEAD + 2


def _experts_body(blk_e, n_real, nxt_e, idx_ref, hn_hbm, wg_hbm, wu_hbm, wd_hbm, out_hbm, xs, xb, hid, ys,
                  wg_stage, wu_stage, wd_stage, wgb, wub, wdb, gsem, ssem, wsem, *, n_assign, layer):
    blk = pl.program_id(0)
    dummy_slot = SCATTER_SLOTS - 1

    def gather_row(idx_row, slot, r):
        return pltpu.make_async_copy(hn_hbm.at[pl.ds(idx_ref[0, idx_row, r], 1), :], xs.at[slot, pl.ds(r, 1), :],
                                     gsem.at[slot])

    def scatter_row(idx_row, slot, r):
        return pltpu.make_async_copy(ys.at[slot, pl.ds(r, 1), :], out_hbm.at[pl.ds(idx_ref[0, idx_row, r], 1), :],
                                     ssem.at[slot])

    def wait_gather(slot):
        pltpu.make_async_copy(hn_hbm.at[pl.ds(0, MOE_BLOCK), :], xs.at[slot], gsem.at[slot]).wait()

    def wait_scatter(slot):
        pltpu.make_async_copy(ys.at[slot], out_hbm.at[pl.ds(0, MOE_BLOCK), :], ssem.at[slot]).wait()

    @pl.when(blk == 0)
    def _():
        def body(r, carry):
            for s in range(GATHER_AHEAD):
                gather_row(s, s, r).start()
            return carry
        lax.fori_loop(0, MOE_BLOCK, body, 0)
        ys[dummy_slot] = jnp.zeros((MOE_BLOCK, HALF), jnp.uint32)
        for s in range(SCATTER_SLOTS):
            trash = pltpu.make_async_copy(
                ys.at[dummy_slot], out_hbm.at[pl.ds(n_assign + s * MOE_BLOCK, MOE_BLOCK), :], ssem.at[dummy_slot])
            trash.start()
            trash.wait()

    @pl.when(blk < n_real[0])
    def _():
        in_slot = blk % GATHER_SLOTS
        ahead_slot = (blk + GATHER_AHEAD) % GATHER_SLOTS
        out_slot = blk % SCATTER_SLOTS
        prev_slot = (blk + SCATTER_SLOTS - 1) % SCATTER_SLOTS

        _switch_expert(blk_e, nxt_e, blk, layer, (wg_hbm, wu_hbm, wd_hbm), (wg_stage, wu_stage, wd_stage),
                       (wgb, wub, wdb), wsem)
        wait_gather(in_slot)

        @pl.when(blk >= SCATTER_SLOTS - 1)
        def _():
            wait_scatter(out_slot)

        lo, hi = _unpack_halves(xs[in_slot])
        xb[:, :HALF] = lo.astype(BF16)
        xb[:, HALF:] = hi.astype(BF16)
        for r in range(MOE_BLOCK):
            scatter_row(IDX_DST_PREV, prev_slot, r).start(priority=r % 2)
            gather_row(GATHER_AHEAD, ahead_slot, r).start()
        cw = D_FF_EXPERT // MOE_CHUNKS
        for c in range(MOE_CHUNKS):
            cols = slice(c * cw, (c + 1) * cw)
            a = jnp.dot(xb[...], wgb[:, cols], preferred_element_type=F32)
            u = jnp.dot(xb[...], wub[:, cols], preferred_element_type=F32)
            hid[:, cols] = (a * jax.nn.sigmoid(a) * u).astype(BF16)
        cw = HALF // MOE_CHUNKS
        for c in range(MOE_CHUNKS):
            cols = slice(c * cw, (c + 1) * cw)
            hi_cols = slice(HALF + c * cw, HALF + (c + 1) * cw)
            ys[out_slot, :, cols] = pltpu.pack_elementwise(
                [jnp.dot(hid[...], wdb[:, cols], preferred_element_type=F32),
                 jnp.dot(hid[...], wdb[:, hi_cols], preferred_element_type=F32)], packed_dtype=BF16)

        @pl.when(blk == n_real[0] - 1)
        def _():
            def body(r, carry):
                scatter_row(IDX_DST, out_slot, r).start()
                return carry
            lax.fori_loop(0, MOE_BLOCK, body, 0)
            for s in range(1, GATHER_SLOTS):
                wait_gather((blk + s) % GATHER_SLOTS)
            for s in range(SCATTER_SLOTS):
                wait_scatter(s)


def _moe_forward(hn, order, counts, w_gate, w_up, w_down, layer, n_tokens):
    A = n_tokens * TOP_K
    out_rows = A + SCATTER_SLOTS * MOE_BLOCK
    R = A + N_EXPERTS * MOE_BLOCK
    n_blk = R // MOE_BLOCK
    i32 = jnp.int32
    n_tiles, per_tile = order.shape[0], order.shape[2]
    experts = jnp.arange(N_EXPERTS, dtype=i32)
    n = counts[:, 0, :N_EXPERTS]
    total = jnp.sum(n, axis=0)
    padded = ((total + MOE_BLOCK - 1) // MOE_BLOCK) * MOE_BLOCK
    pend = jnp.cumsum(padded)
    pstart = pend - padded
    n_real = (pend[-1] // MOE_BLOCK).astype(i32).reshape(1)
    blk = jnp.arange(n_blk, dtype=i32)
    blk_e = jnp.minimum(jnp.sum((pend[None, :] <= (blk * MOE_BLOCK)[:, None]).astype(i32), axis=1), N_EXPERTS - 1)
    is_e = (blk_e[:, None] == experts[None, :]).astype(i32)
    of_block = lambda per_expert: jnp.sum(is_e * per_expert[None, :], axis=1)
    cum_incl = jnp.cumsum(n, axis=0)
    run_shift = (jnp.cumsum(n, axis=1) - n) - (cum_incl - n)
    per_tile_of_block = lambda m: jnp.sum(is_e[:, None, :] * m[None, :, :], axis=2)
    cum_b, shift_b = per_tile_of_block(cum_incl), per_tile_of_block(run_shift)
    in_blk = jnp.arange(MOE_BLOCK, dtype=i32)
    k = (blk * MOE_BLOCK - of_block(pstart))[:, None] + in_blk[None, :]
    tile_of = jnp.minimum(jnp.sum((cum_b[:, None, :] <= k[:, :, None]).astype(i32), axis=2), n_tiles - 1)
    is_tile = (tile_of[:, :, None] == jnp.arange(n_tiles, dtype=i32)[None, None, :]).astype(i32)
    entry = tile_of * per_tile + k + jnp.sum(is_tile * shift_b[:, None, :], axis=2)
    pad_entry = A + (blk % SCATTER_SLOTS)[:, None] * MOE_BLOCK + in_blk[None, :]
    entry = jnp.where(k < of_block(total)[:, None], entry, pad_entry)
    n_pad = SCATTER_SLOTS * MOE_BLOCK
    tok_table = jnp.concatenate([order[:, 0, :].reshape(A), jnp.zeros((n_pad,), i32)])
    dst_table = jnp.concatenate([order[:, 1, :].reshape(A), A + jnp.arange(n_pad, dtype=i32)])
    row_tok = tok_table[entry]
    row_dst = dst_table[entry]
    row_dst_prev = jnp.concatenate([(A + (SCATTER_SLOTS - 1) * MOE_BLOCK + in_blk)[None, :], row_dst[:-1]], axis=0)
    last = n_real[0] - 1
    last_tok = lax.dynamic_slice_in_dim(row_tok, last, 1, axis=0)
    ahead = [jnp.where((blk + s <= last)[:, None],
                       jnp.concatenate([row_tok[s:], jnp.zeros((s, MOE_BLOCK), i32)], axis=0), last_tok)
             for s in range(GATHER_AHEAD + 1)]
    idx_rows = ahead + [row_dst, row_dst_prev]
    idx_rows += [jnp.zeros_like(row_dst)] * (IDX_ROWS - len(idx_rows))
    idx_all = jnp.stack(idx_rows, axis=1)

    run_end = pend[blk_e] // MOE_BLOCK
    nxt_e = jnp.where(run_end < n_real[0], blk_e[jnp.minimum(run_end, n_blk - 1)], -1).astype(jnp.int32)

    idx_spec = pl.BlockSpec((1, IDX_ROWS, MOE_BLOCK), lambda i, be, nr, nx: (jnp.minimum(i, nr[0] - 1), 0, 0),
                            memory_space=pltpu.SMEM)
    hbm = pl.BlockSpec(memory_space=pl.ANY)
    up_shape, down_shape = (D_MODEL, D_FF_EXPERT), (D_FF_EXPERT, D_MODEL)
    return pl.pallas_call(
        functools.partial(_experts_body, n_assign=A, layer=layer),
        grid_spec=pltpu.PrefetchScalarGridSpec(
            num_scalar_prefetch=3, grid=(n_blk,),
            in_specs=[idx_spec] + [hbm] * 4,
            out_specs=hbm,
            scratch_shapes=[pltpu.VMEM((GATHER_SLOTS, MOE_BLOCK, HALF), jnp.uint32),
                            pltpu.VMEM((MOE_BLOCK, D_MODEL), BF16),
                            pltpu.VMEM((MOE_BLOCK, D_FF_EXPERT), BF16),
                            pltpu.VMEM((SCATTER_SLOTS, MOE_BLOCK, HALF), jnp.uint32),
                            pltpu.VMEM(up_shape, F32), pltpu.VMEM(up_shape, F32), pltpu.VMEM(down_shape, F32),
                            pltpu.VMEM(up_shape, BF16), pltpu.VMEM(up_shape, BF16), pltpu.VMEM(down_shape, BF16),
                            pltpu.SemaphoreType.DMA((GATHER_SLOTS,)), pltpu.SemaphoreType.DMA((SCATTER_SLOTS,)),
                            pltpu.SemaphoreType.DMA((3,))]),
        out_shape=jax.ShapeDtypeStruct((out_rows, HALF), jnp.uint32),
        compiler_params=_params(("arbitrary",)), name="moe_experts",
    )(blk_e, n_real, nxt_e, idx_all, hn, w_gate, w_up, w_down)


FINAL_TM = 512


def _final_body(h_ref, y0_ref, y1_ref, gate_ref, g_ref, o_ref):
    gate = gate_ref[...]
    y0, y1 = _unpack_halves(y0_ref[...]), _unpack_halves(y1_ref[...])
    halves = [h_ref[:, i * HALF:(i + 1) * HALF] + gate[:, 0:1] * y0[i] + gate[:, 1:2] * y1[i] for i in range(2)]
    mean_sq = sum(jnp.sum(hh * hh, axis=-1, keepdims=True) for hh in halves) / D_MODEL
    inv = lax.rsqrt(mean_sq + RMS_EPS)
    for i, hh in enumerate(halves):
        o_ref[:, i * HALF:(i + 1) * HALF] = hh * inv * g_ref[:, i * HALF:(i + 1) * HALF]


def _final(h, ys, gate, gain):
    rows = h.shape[0]
    n_tiles = rows // FINAL_TM
    tile = lambda width: pl.BlockSpec((FINAL_TM, width), lambda i: (i, 0))
    g2 = gain.reshape(1, D_MODEL).astype(F32)
    return pl.pallas_call(
        _final_body, grid=(n_tiles,),
        in_specs=[tile(D_MODEL), tile(HALF), pl.BlockSpec((FINAL_TM, HALF), lambda i: (i + n_tiles, 0)),
                  tile(LANES), pl.BlockSpec((1, D_MODEL), lambda i: (0, 0))],
        out_specs=tile(D_MODEL), out_shape=jax.ShapeDtypeStruct((rows, D_MODEL), F32),
        compiler_params=_params(("parallel",)), name="combine_final_norm")(h, ys, ys, gate, g2)


def kernel(x, mem, norm_mix, w_in, attn_out_norm, sink_logit, w_gla_gf, b_gla_gf, w_gla_gb, b_gla_gb, gla_out_norm, w_out, norm_cross, norm_mem, w_cq, w_ck, w_cv, w_co, norm_ffn, w_router_group, b_router_group, w_router_expert, b_router_expert, w_gate, w_up, w_down, norm_final):
    batch, seq, _ = x.shape
    mem_len = mem.shape[1]
    n_tokens = batch * seq
    h = x.reshape(n_tokens, D_MODEL)
    memf = mem.reshape(batch * mem_len, D_MODEL)
    assert norm_mix.shape[0] == 1, "the combine step is fused with the final norm: single-layer stacks only"
    for l in range(norm_mix.shape[0]):
        w_lr = jnp.zeros((D_MODEL, LANES), F32).at[:, :2 * GLA_LOWRANK].set(w_in[l][:, MAIN_COLS:]).astype(BF16)
        proj, lr = _dense([h], w_in, l, n_cols=MAIN_COLS, out_dtype=BF16, gain=norm_mix[l], extra_w=w_lr,
                          name="in_proj")
        o_a = _window_attention(proj, sink_logit[l], attn_out_norm[l], batch, seq).reshape(n_tokens, ATT_Q)
        o_g = _gla(proj, lr, w_gla_gf[l], b_gla_gf[l], w_gla_gb[l], b_gla_gb[l], gla_out_norm[l],
                   batch, seq).reshape(n_tokens, GLA_V)
        h = _dense([o_a, o_g], w_out, l, n_cols=D_MODEL, out_dtype=F32, res=h, name="out_proj")
        kx, vx = _mem_kv(memf, norm_mem[l], w_ck, w_cv, l)
        router_w, router_b = _router_params(w_router_group[l], b_router_group[l],
                                            w_router_expert[l], b_router_expert[l])
        h, hn, gate, order, counts = _cross_block(h, norm_cross[l], w_cq, kx, vx, w_co, norm_ffn[l],
                                                  router_w, router_b, l, batch, seq, mem_len)
        ys = _moe_forward(hn, order, counts, w_gate, w_up, w_down, l, n_tokens)
    return _final(h, ys, gate, norm_final).reshape(batch, seq, D_MODEL)
```

```python
import functools

import jax
import jax.numpy as jnp
from jax import lax
from jax.experimental import pallas as pl
from jax.experimental.pallas import tpu as pltpu

F32 = jnp.float32
BF16 = jnp.bfloat16

D_MODEL = 2048
N_Q_HEADS = 8
N_KV_HEADS = 2
Q_PER_KV = N_Q_HEADS // N_KV_HEADS
HEAD_DIM = 128
WINDOW = 128
WBLK = 128
GLA_HEADS = 4
GLA_DK = 128
GLA_DV = 256
GLA_LOWRANK = 16
GLA_TAU = 16.0
GLA_CHUNK = 64
ATT_Q = N_Q_HEADS * HEAD_DIM
ATT_KV = N_KV_HEADS * HEAD_DIM
GLA_QK = GLA_HEADS * GLA_DK
GLA_V = GLA_HEADS * GLA_DV
MAIN_COLS = ATT_Q + 2 * ATT_KV + 2 * GLA_QK + 2 * GLA_V
COL_KA = ATT_Q
COL_VA = COL_KA + ATT_KV
COL_QG = COL_VA + ATT_KV
COL_KG = COL_QG + GLA_QK
COL_VG = COL_KG + GLA_QK
COL_RG = COL_VG + GLA_V
X_HEADS = 4
X_HEAD_DIM = D_MODEL // X_HEADS
N_GROUPS = 4
EXPERTS_PER_GROUP = 8
N_EXPERTS = N_GROUPS * EXPERTS_PER_GROUP
TOP_K = 2
D_FF_EXPERT = D_MODEL // 2
MOE_BLOCK = 256
RMS_EPS = 1e-6
NEG_INF = -1e30

LANES = 128
VMEM_LIMIT = 56 * 1024 * 1024
W_STAGE_ROWS = 128


def _params(sem):
    return pltpu.CompilerParams(dimension_semantics=sem, vmem_limit_bytes=VMEM_LIMIT)


def _nt(a, b):
    return lax.dot_general(a, b, (((1,), (1,)), ((), ())), preferred_element_type=F32)


def _tn(a, b):
    return lax.dot_general(a, b, (((0,), (0,)), ((), ())), preferred_element_type=F32)


def _rms(x, gain):
    return x * lax.rsqrt(jnp.mean(x * x, axis=-1, keepdims=True) + RMS_EPS) * gain


def _load_weight_bf16(w_hbm, layer, wb, stage, sem, k_rows, n_cols):
    n_chunks = k_rows // W_STAGE_ROWS

    def copy(c):
        return pltpu.make_async_copy(
            w_hbm.at[layer, pl.ds(c * W_STAGE_ROWS, W_STAGE_ROWS), pl.ds(0, n_cols)],
            stage.at[c % 2], sem.at[c % 2])

    copy(0).start()
    for c in range(n_chunks):
        if c + 1 < n_chunks:
            copy(c + 1).start()
        copy(c).wait()
        wb[c * W_STAGE_ROWS:(c + 1) * W_STAGE_ROWS, :] = stage[c % 2].astype(BF16)


def _dense_body(*refs, part_widths, has_norm, has_extra, has_res, n_cols, n_chunk, layer):
    it = iter(refs)
    x_refs = [next(it) for _ in part_widths]
    g_ref = next(it) if has_norm else None
    w_hbm = next(it)
    ew_ref = next(it) if has_extra else None
    res_ref = next(it) if has_res else None
    o_ref = next(it)
    eo_ref = next(it) if has_extra else None
    wb, stage, sem = next(it), next(it), next(it)
    u_ref = next(it) if has_norm else None
    k_rows = sum(part_widths)

    @pl.when(pl.program_id(0) == 0)
    def _():
        _load_weight_bf16(w_hbm, layer, wb, stage, sem, k_rows, n_cols)

    if has_norm:
        u_ref[...] = _rms(x_refs[0][...], g_ref[...]).astype(BF16)
        lhs = [(u_ref, 0, k_rows)]
    else:
        lhs, off = [], 0
        for r, kw in zip(x_refs, part_widths):
            lhs.append((r, off, kw))
            off += kw
    for n0 in range(0, n_cols, n_chunk):
        acc = None
        for r, off, kw in lhs:
            d = jnp.dot(r[...], wb[off:off + kw, n0:n0 + n_chunk], preferred_element_type=F32)
            acc = d if acc is None else acc + d
        if has_res:
            acc = acc + res_ref[:, n0:n0 + n_chunk]
        o_ref[:, n0:n0 + n_chunk] = acc.astype(o_ref.dtype)
    if has_extra:
        eo_ref[...] = jnp.dot(u_ref[...], ew_ref[...], preferred_element_type=F32)


def _dense(xs, w, layer, *, n_cols, out_dtype, gain=None, extra_w=None, res=None, tm=512, n_chunk=512, name):
    rows = xs[0].shape[0]
    part_widths = tuple(x.shape[1] for x in xs)
    k_rows = sum(part_widths)
    has_norm, has_extra, has_res = gain is not None, extra_w is not None, res is not None
    row_spec = lambda width: pl.BlockSpec((tm, width), lambda i: (i, 0))
    full_spec = lambda a: pl.BlockSpec(a.shape, lambda i: (0, 0))
    args, in_specs = list(xs), [row_spec(kw) for kw in part_widths]
    if has_norm:
        args.append(gain.reshape(1, k_rows).astype(F32))
        in_specs.append(full_spec(args[-1]))
    args.append(w)
    in_specs.append(pl.BlockSpec(memory_space=pl.ANY))
    if has_extra:
        args.append(extra_w)
        in_specs.append(full_spec(extra_w))
    if has_res:
        args.append(res)
        in_specs.append(row_spec(n_cols))
    out_shape = [jax.ShapeDtypeStruct((rows, n_cols), out_dtype)]
    out_specs = [row_spec(n_cols)]
    if has_extra:
        out_shape.append(jax.ShapeDtypeStruct((rows, extra_w.shape[1]), F32))
        out_specs.append(row_spec(extra_w.shape[1]))
    scratch = [pltpu.VMEM((k_rows, n_cols), BF16),
               pltpu.VMEM((2, W_STAGE_ROWS, n_cols), F32),
               pltpu.SemaphoreType.DMA((2,))]
    if has_norm:
        scratch.append(pltpu.VMEM((tm, k_rows), BF16))
    body = functools.partial(_dense_body, part_widths=part_widths, has_norm=has_norm, has_extra=has_extra,
                             has_res=has_res, n_cols=n_cols, n_chunk=n_chunk, layer=layer)
    outs = pl.pallas_call(
        body, grid=(rows // tm,), in_specs=in_specs, out_specs=out_specs, out_shape=out_shape,
        scratch_shapes=scratch, compiler_params=_params(("arbitrary",)), name=name)(*args)
    return outs if has_extra else outs[0]


ATT_TQ = 512


def _winattn_body(sink_ref, q_ref, kp_ref, km_ref, kn_ref, vp_ref, vm_ref, vn_ref, g_ref, o_ref,
                  kcat, vcat, obuf, *, seq):
    s0 = pl.program_id(1) * ATT_TQ
    kcat[0:WBLK, :] = kp_ref[0]
    kcat[WBLK:WBLK + ATT_TQ, :] = km_ref[0]
    kcat[WBLK + ATT_TQ:, :] = kn_ref[0]
    vcat[0:WBLK, :] = vp_ref[0]
    vcat[WBLK:WBLK + ATT_TQ, :] = vm_ref[0]
    vcat[WBLK + ATT_TQ:, :] = vn_ref[0]
    qi = lax.broadcasted_iota(jnp.int32, (WBLK, 3 * WBLK), 0) + WBLK
    ki = lax.broadcasted_iota(jnp.int32, (WBLK, 3 * WBLK), 1)
    dist_i = jnp.abs(ki - qi)
    neg_dist = jnp.where(dist_i <= WINDOW, -dist_i.astype(F32), NEG_INF)
    scale = HEAD_DIM ** -0.5
    for qb in range(ATT_TQ // WBLK):
        kabs = s0 + (qb - 1) * WBLK + ki
        bias_unit = jnp.where((kabs >= 0) & (kabs < seq), neg_dist, NEG_INF)
        head_cols = lambda j: slice(j * HEAD_DIM, (j + 1) * HEAD_DIM)
        kv_cols = lambda j: head_cols(j // Q_PER_KV)
        key_rows = slice(qb * WBLK, (qb + 3) * WBLK)
        scores = []
        for j in range(N_Q_HEADS):
            slope = 2.0 ** (-8.0 * (j + 1) / N_Q_HEADS)
            q = q_ref[0, qb * WBLK:(qb + 1) * WBLK, head_cols(j)]
            scores.append(_nt(q, kcat[key_rows, kv_cols(j)]) * scale + slope * bias_unit)
        probs, denoms = [], []
        for j in range(N_Q_HEADS):
            sink = sink_ref[j]
            m = jnp.maximum(jnp.max(scores[j], axis=-1, keepdims=True), sink)
            p = jnp.exp(scores[j] - m)
            denoms.append(jnp.sum(p, axis=-1, keepdims=True) + jnp.exp(sink - m))
            probs.append(p.astype(BF16))
        for j in range(N_Q_HEADS):
            pv = jnp.dot(probs[j], vcat[key_rows, kv_cols(j)], preferred_element_type=F32)
            obuf[qb, :, head_cols(j)] = pv / denoms[j]
        o_ref[0, qb * WBLK:(qb + 1) * WBLK, :] = _rms(obuf[qb], g_ref[...]).astype(o_ref.dtype)


def _window_attention(proj, sink_logit, gain, batch, seq):
    p3 = proj.reshape(batch, seq, MAIN_COLS)
    nq = seq // ATT_TQ
    per = ATT_TQ // WBLK
    last = seq // WBLK - 1
    main = lambda col: pl.BlockSpec((1, ATT_TQ, ATT_KV), lambda b, i: (b, i, col))
    prev = lambda col: pl.BlockSpec((1, WBLK, ATT_KV), lambda b, i: (b, jnp.maximum(i * per - 1, 0), col))
    nxt = lambda col: pl.BlockSpec((1, WBLK, ATT_KV), lambda b, i: (b, jnp.minimum(i * per + per, last), col))
    ck, cv = COL_KA // ATT_KV, COL_VA // ATT_KV
    return pl.pallas_call(
        functools.partial(_winattn_body, seq=seq),
        grid=(batch, nq),
        in_specs=[pl.BlockSpec(memory_space=pltpu.SMEM),
                  pl.BlockSpec((1, ATT_TQ, ATT_Q), lambda b, i: (b, i, 0)),
                  prev(ck), main(ck), nxt(ck), prev(cv), main(cv), nxt(cv),
                  pl.BlockSpec((1, ATT_Q), lambda b, i: (0, 0))],
        out_specs=pl.BlockSpec((1, ATT_TQ, ATT_Q), lambda b, i: (b, i, 0)),
        out_shape=jax.ShapeDtypeStruct((batch, seq, ATT_Q), BF16),
        scratch_shapes=[pltpu.VMEM((ATT_TQ + 2 * WBLK, ATT_KV), BF16),
                        pltpu.VMEM((ATT_TQ + 2 * WBLK, ATT_KV), BF16),
                        pltpu.VMEM((ATT_TQ // WBLK, WBLK, ATT_Q), F32)],
        compiler_params=_params(("parallel", "parallel")), name="window_attention",
    )(sink_logit.astype(F32), p3, p3, p3, p3, p3, p3, p3, gain.reshape(1, ATT_Q).astype(F32))


GLA_SCAN_ROWS = 256
GLA_SCAN_UNROLL = 2
GLA_EPI_ROWS = 512
GLA_UNROLL = 8


def _split2(x):
    hi = x.astype(BF16)
    lo = (x - hi.astype(F32)).astype(BF16)
    return hi, lo


def _gla_body(q_ref, k_ref, v_ref, r_ref, lr_ref, wf_ref, wb_ref, bf_ref, bb_ref, gn_ref, o_ref,
              cum_f, cum_b, acc, st_f, st_b, *, seq):
    C = GLA_CHUNK
    n_chunks = seq // C
    ri = lax.broadcasted_iota(jnp.int32, (GLA_SCAN_ROWS, GLA_SCAN_ROWS), 0)
    ci = lax.broadcasted_iota(jnp.int32, (GLA_SCAN_ROWS, GLA_SCAN_ROWS), 1)
    same = (ri // C) == (ci // C)
    tri_f = jnp.where(same & (ci <= ri), 1.0, 0.0).astype(BF16)
    tri_b = jnp.where(same & (ci >= ri), 1.0, 0.0).astype(BF16)

    def scan_body(t, carry):
        row_sets = [pl.ds(pl.multiple_of((t * GLA_SCAN_UNROLL + u) * GLA_SCAN_ROWS, GLA_SCAN_ROWS), GLA_SCAN_ROWS)
                    for u in range(GLA_SCAN_UNROLL)]
        lrs = [lr_ref[0, rows, :].astype(BF16) for rows in row_sets]
        zs = [jnp.dot(lr, w_ref[...], preferred_element_type=F32) + b_ref[...]
              for lr in lrs for w_ref, b_ref in ((wf_ref, bf_ref), (wb_ref, bb_ref))]
        gs = [_split2((jnp.minimum(z, 0.0) - jnp.log1p(jnp.exp(-jnp.abs(z)))) / GLA_TAU) for z in zs]
        for n, g3 in enumerate(gs):
            tri, dst = ((tri_f, cum_f), (tri_b, cum_b))[n % 2]
            dst[row_sets[n // 2], :] = sum(jnp.dot(tri, part, preferred_element_type=F32) for part in g3)
        return carry

    lax.fori_loop(0, seq // (GLA_SCAN_ROWS * GLA_SCAN_UNROLL), scan_body, 0)

    acc[...] = jnp.zeros_like(acc)
    st_f[...] = jnp.zeros_like(st_f)
    st_b[...] = jnp.zeros_like(st_b)
    rr = lax.broadcasted_iota(jnp.int32, (C, C), 0)
    cc = lax.broadcasted_iota(jnp.int32, (C, C), 1)
    scale = GLA_DK ** -0.5

    def chunk_body(i, carry):
        jobs = []
        for u in range(GLA_UNROLL):
            c = i * GLA_UNROLL + u
            jobs += [(c, cum_f, True), (n_chunks - 1 - c, cum_b, False)]
        prep = []
        for c, cum, forward in jobs:
            rows = pl.ds(pl.multiple_of(c * C, C), C)
            b = cum[rows, :]
            b_end = b[C - 1:C, :] if forward else b[0:1, :]
            q = q_ref[0, rows, :].astype(F32) * scale
            k = k_ref[0, rows, :].astype(F32)
            v = v_ref[0, rows, :]
            q_dec = (q * jnp.exp(b)).astype(BF16)
            k_inc = (k * jnp.exp(-b)).astype(BF16)
            k_dec = (k * jnp.exp(b_end - b)).astype(BF16)
            prep.append((rows, v, q_dec, k_inc, k_dec, jnp.exp(b_end)))
        attn = [_nt(q_dec, k_inc) for _, _, q_dec, k_inc, _, _ in prep]
        kv_t = [_tn(v, k_dec) for _, v, _, _, k_dec, _ in prep]
        state_t = {True: st_f[...], False: st_b[...]}
        o_inter = []
        for (_, _, forward), (_, _, q_dec, _, _, decay), kv in zip(jobs, prep, kv_t):
            o_inter.append(_nt(q_dec, state_t[forward].astype(BF16)))
            state_t[forward] = state_t[forward] * decay + kv
        st_f[...] = state_t[True]
        st_b[...] = state_t[False]
        for (_, _, forward), (rows, v, _, _, _, _), a, oi in zip(jobs, prep, attn, o_inter):
            a = jnp.where((cc <= rr) if forward else (cc >= rr), a, 0.0)
            o = jnp.dot(a.astype(BF16), v, preferred_element_type=F32) + oi
            acc[rows, :] = acc[rows, :] + o
        return carry

    lax.fori_loop(0, n_chunks // GLA_UNROLL, chunk_body, 0)

    def epi_body(t, carry):
        rows = pl.ds(pl.multiple_of(t * GLA_EPI_ROWS, GLA_EPI_ROWS), GLA_EPI_ROWS)
        r = r_ref[0, rows, :].astype(F32)
        o_ref[0, rows, :] = (_rms(acc[rows, :], gn_ref[...]) * (r * jax.nn.sigmoid(r))).astype(o_ref.dtype)
        return carry

    lax.fori_loop(0, seq // GLA_EPI_ROWS, epi_body, 0)


def _gla(proj, lr, w_gf, b_gf, w_gb, b_gb, gain, batch, seq):
    p3 = proj.reshape(batch, seq, MAIN_COLS)
    lr3 = lr.reshape(batch, seq, LANES)
    wf = jnp.zeros((LANES, GLA_QK), F32).at[:GLA_LOWRANK].set(w_gf).astype(BF16)
    wb = jnp.zeros((LANES, GLA_QK), F32).at[GLA_LOWRANK:2 * GLA_LOWRANK].set(w_gb).astype(BF16)
    seq_blk = lambda width, col0: pl.BlockSpec((1, seq, width), lambda b, h: (b, 0, col0 // width + h))
    head_w = pl.BlockSpec((LANES, GLA_DK), lambda b, h: (0, h))
    head_b = pl.BlockSpec((1, GLA_DK), lambda b, h: (0, h))
    return pl.pallas_call(
        functools.partial(_gla_body, seq=seq),
        grid=(batch, GLA_HEADS),
        in_specs=[seq_blk(GLA_DK, COL_QG), seq_blk(GLA_DK, COL_KG), seq_blk(GLA_DV, COL_VG),
                  seq_blk(GLA_DV, COL_RG),
                  pl.BlockSpec((1, seq, LANES), lambda b, h: (b, 0, 0)),
                  head_w, head_w, head_b, head_b,
                  pl.BlockSpec((1, GLA_DV), lambda b, h: (0, 0))],
        out_specs=pl.BlockSpec((1, seq, GLA_DV), lambda b, h: (b, 0, h)),
        out_shape=jax.ShapeDtypeStruct((batch, seq, GLA_V), BF16),
        scratch_shapes=[pltpu.VMEM((seq, GLA_DK), F32), pltpu.VMEM((seq, GLA_DK), F32),
                        pltpu.VMEM((seq, GLA_DV), F32),
                        pltpu.VMEM((GLA_DV, GLA_DK), F32), pltpu.VMEM((GLA_DV, GLA_DK), F32)],
        compiler_params=_params(("parallel", "parallel")), name="gla",
    )(p3, p3, p3, p3, lr3, wf, wb, b_gf.reshape(1, GLA_QK).astype(F32), b_gb.reshape(1, GLA_QK).astype(F32),
      gain.reshape(1, GLA_DV).astype(F32))


MEM_TM = 512
MEM_K_CHUNK = 256


def _mem_kv_body(m_ref, g_ref, wk_ref, wv_ref, k_ref, v_ref, u, acc_k, acc_v):
    step = pl.program_id(1)
    n_chunks = D_MODEL // MEM_K_CHUNK

    @pl.when(step == 0)
    def _():
        normed = _rms(m_ref[...], g_ref[...]).astype(BF16)
        for c in range(n_chunks):
            u[c] = normed[:, c * MEM_K_CHUNK:(c + 1) * MEM_K_CHUNK]
        acc_k[...] = jnp.zeros_like(acc_k)
        acc_v[...] = jnp.zeros_like(acc_v)

    lhs = u[step]
    acc_k[...] = acc_k[...] + jnp.dot(lhs, wk_ref[...].astype(BF16), preferred_element_type=F32)
    acc_v[...] = acc_v[...] + jnp.dot(lhs, wv_ref[...].astype(BF16), preferred_element_type=F32)

    @pl.when(step == n_chunks - 1)
    def _():
        k_ref[...] = acc_k[...].astype(k_ref.dtype)
        v_ref[...] = acc_v[...].astype(v_ref.dtype)


def _mem_kv(mem_rows, gain, w_ck, w_cv, layer):
    rows = mem_rows.shape[0]
    n_chunks = D_MODEL // MEM_K_CHUNK
    tile = pl.BlockSpec((MEM_TM, D_MODEL), lambda i, c: (i, 0))
    w_spec = pl.BlockSpec((None, MEM_K_CHUNK, D_MODEL), lambda i, c: (layer, c, 0))
    g2 = gain.reshape(1, D_MODEL).astype(F32)
    out = jax.ShapeDtypeStruct((rows, D_MODEL), BF16)
    return pl.pallas_call(
        _mem_kv_body, grid=(rows // MEM_TM, n_chunks),
        in_specs=[tile, pl.BlockSpec((1, D_MODEL), lambda i, c: (0, 0)), w_spec, w_spec],
        out_specs=[tile, tile], out_shape=[out, out],
        scratch_shapes=[pltpu.VMEM((n_chunks, MEM_TM, MEM_K_CHUNK), BF16),
                        pltpu.VMEM((MEM_TM, D_MODEL), F32), pltpu.VMEM((MEM_TM, D_MODEL), F32)],
        compiler_params=_params(("parallel", "arbitrary")), name="mem_kv_proj")(mem_rows, g2, w_ck, w_cv)


XATT_TQ = 512


XATT_CHUNK = 512


def _cross_body(h_ref, g_ref, wq_hbm, k_ref, v_ref, wo_hbm, gr_ref, wr_ref, br_ref,
                o_ref, hn_ref, gate_ref, order_ref, count_ref,
                wqb, wob, stage, sem, u_scr, q_scr, a_scr, prev, *, layer, n_tokens):
    step = pl.program_id(0)

    @pl.when(step == 0)
    def _():
        _load_weight_bf16(wq_hbm, layer, wqb, stage, sem, D_MODEL, D_MODEL)
        _load_weight_bf16(wo_hbm, layer, wob, stage, sem, D_MODEL, D_MODEL)
        prev[...] = jnp.zeros_like(prev)

    router = _route_tile(prev[...], gr_ref[...], wr_ref, br_ref, hn_ref, gate_ref, order_ref, count_ref,
                         jnp.maximum(step - 1, 0), n_tokens)

    scale = X_HEAD_DIM ** -0.5
    heads = [slice(h * X_HEAD_DIM, (h + 1) * X_HEAD_DIM) for h in range(X_HEADS)]
    chunks = [slice(n0, n0 + XATT_CHUNK) for n0 in range(0, D_MODEL, XATT_CHUNK)]

    def norm(rows):
        u_scr[rows, :] = _rms(h_ref[rows, :], g_ref[...]).astype(BF16)

    def q_proj(rows):
        for cols in chunks:
            q_scr[rows, cols] = jnp.dot(u_scr[rows, :], wqb[:, cols], preferred_element_type=F32).astype(BF16)

    def scores(rows):
        return [_nt(q_scr[rows, hd], k_ref[0, :, hd]) * scale for hd in heads]

    def softmax(s_list):
        out = []
        for s in s_list:
            p = jnp.exp(s - jnp.max(s, axis=-1, keepdims=True))
            out.append((p.astype(BF16), jnp.sum(p, axis=-1, keepdims=True)))
        return out

    def attend(rows, probs):
        for hd, (p, denom) in zip(heads, probs):
            a_scr[rows, hd] = (jnp.dot(p, v_ref[0, :, hd], preferred_element_type=F32) / denom).astype(BF16)

    def o_proj(rows):
        for cols in chunks:
            out = h_ref[rows, cols] + jnp.dot(a_scr[rows, :], wob[:, cols], preferred_element_type=F32)
            o_ref[rows, cols] = out
            prev[rows, cols] = out

    half = XATT_TQ // 2
    first, second = slice(0, half), slice(half, XATT_TQ)
    norm(first)
    q_proj(first)
    next(router)
    norm(second)
    s_first = scores(first)
    q_proj(second)
    next(router)
    p_first = softmax(s_first)
    attend(first, p_first)
    s_second = scores(second)
    o_proj(first)
    for _ in router:
        pass
    p_second = softmax(s_second)
    attend(second, p_second)
    o_proj(second)


def _cross_block(h, gain, w_cq, kx, vx, w_co, router_gain, router_w, router_b, layer, batch, seq, mem_len):
    rows = h.shape[0]
    n_tiles = rows // XATT_TQ
    tiles_per_batch = seq // XATT_TQ
    k3 = kx.reshape(batch, mem_len, D_MODEL)
    v3 = vx.reshape(batch, mem_len, D_MODEL)
    as_row = lambda g: g.reshape(1, D_MODEL).astype(F32)
    cur = lambda i: jnp.minimum(i, n_tiles - 1)
    lag = lambda i: jnp.maximum(i - 1, 0)
    tile = pl.BlockSpec((XATT_TQ, D_MODEL), lambda i: (cur(i), 0))
    mem_spec = pl.BlockSpec((1, mem_len, D_MODEL), lambda i: (cur(i) // tiles_per_batch, 0, 0))
    full = lambda a: pl.BlockSpec(a.shape, lambda i: (0, 0))
    lag_rows = lambda width: pl.BlockSpec((XATT_TQ, width), lambda i: (lag(i), 0))
    lag_tile = lambda sublanes: pl.BlockSpec((1, sublanes, LANES), lambda i: (lag(i), 0, 0))
    assert TOP_K * XATT_TQ == ID_SPLIT * ID_SPLIT, "the router's sort grid holds one tile's assignments"
    hbm = pl.BlockSpec(memory_space=pl.ANY)
    g_cross, g_router = as_row(gain), as_row(router_gain)
    return pl.pallas_call(
        functools.partial(_cross_body, layer=layer, n_tokens=rows), grid=(n_tiles + 1,),
        in_specs=[tile, full(g_cross), hbm, mem_spec, mem_spec, hbm, full(g_router), full(router_w), full(router_b)],
        out_specs=[tile, lag_rows(HALF), lag_rows(LANES), lag_tile(ID_SPLIT), lag_tile(8)],
        out_shape=[jax.ShapeDtypeStruct((rows, D_MODEL), F32),
                   jax.ShapeDtypeStruct((rows, HALF), jnp.uint32),
                   jax.ShapeDtypeStruct((rows, LANES), F32),
                   jax.ShapeDtypeStruct((n_tiles, ID_SPLIT, LANES), jnp.int32),
                   jax.ShapeDtypeStruct((n_tiles, 8, LANES), jnp.int32)],
        scratch_shapes=[pltpu.VMEM((D_MODEL, D_MODEL), BF16), pltpu.VMEM((D_MODEL, D_MODEL), BF16),
                        pltpu.VMEM((2, W_STAGE_ROWS, D_MODEL), F32), pltpu.SemaphoreType.DMA((2,)),
                        pltpu.VMEM((XATT_TQ, D_MODEL), BF16), pltpu.VMEM((XATT_TQ, D_MODEL), BF16),
                        pltpu.VMEM((XATT_TQ, D_MODEL), BF16), pltpu.VMEM((XATT_TQ, D_MODEL), F32)],
        compiler_params=_params(("arbitrary",)), name="cross_block",
    )(h, g_cross, w_cq, k3, v3, w_co, g_router, router_w, router_b)


HALF = D_MODEL // 2


def _pack_halves(x):
    return pltpu.pack_elementwise([x[:, :HALF], x[:, HALF:]], packed_dtype=BF16)


def _unpack_halves(words):
    return [pltpu.unpack_elementwise(words, index=i, packed_dtype=BF16, unpacked_dtype=F32) for i in range(2)]


ID_SPLIT = 32


def _route_tile(h, gain, w_ref, b_ref, hn_ref, gate_ref, order_ref, count_ref, tile_idx, n_tokens):
    hn = _rms(h, gain)
    hn_ref[...] = _pack_halves(hn)
    logits = jnp.dot(hn.astype(BF16), w_ref[...], preferred_element_type=F32) + b_ref[...]
    yield
    lane = lax.broadcasted_iota(jnp.int32, logits.shape, 1).astype(F32)
    ninf = -jnp.inf
    first = lambda hit: jnp.min(jnp.where(hit, lane, float(LANES)), axis=-1, keepdims=True)
    in_groups = lane < N_GROUPS
    gl = jnp.where(in_groups, logits, ninf)
    gmax = jnp.max(gl, axis=-1, keepdims=True)
    g_idx = first(gl == gmax)
    p_group = 1.0 / jnp.sum(jnp.where(in_groups, jnp.exp(logits - gmax), 0.0), axis=-1, keepdims=True)
    lo = N_GROUPS + EXPERTS_PER_GROUP * g_idx
    el = jnp.where((lane >= lo) & (lane < lo + EXPERTS_PER_GROUP), logits, ninf)
    e1 = jnp.max(el, axis=-1, keepdims=True)
    i1 = first(el == e1)
    el2 = jnp.where(lane == i1, ninf, el)
    e2 = jnp.max(el2, axis=-1, keepdims=True)
    i2 = first(el2 == e2)
    t = jnp.exp(e2 - e1)
    w1 = p_group / (1.0 + t)
    w2 = p_group * t / (1.0 + t)
    gate_ref[...] = jnp.where(lane == 0, w1, jnp.where(lane == 1, w2, 0.0))

    tm = logits.shape[0]
    hit0, hit1 = lane == i1 - N_GROUPS, lane == i2 - N_GROUPS
    member = jnp.where(hit0 | hit1, 1.0, 0.0)
    member_b = member.astype(BF16)
    earlier = (lax.broadcasted_iota(jnp.int32, (tm, tm), 1) < lax.broadcasted_iota(jnp.int32, (tm, tm), 0))
    rank = jnp.dot(jnp.where(earlier, 1.0, 0.0).astype(BF16), member_b, preferred_element_type=F32)
    lower = (lax.broadcasted_iota(jnp.int32, (LANES, LANES), 0) < lax.broadcasted_iota(jnp.int32, (LANES, LANES), 1))
    run_start = jnp.sum(jnp.dot(member_b, jnp.where(lower, 1.0, 0.0).astype(BF16), preferred_element_type=F32),
                        axis=0, keepdims=True)
    yield
    pos = rank + run_start
    positions = [jnp.sum(jnp.where(hit, pos, 0.0), axis=-1, keepdims=True) for hit in (hit0, hit1)]
    local_tok = lax.broadcasted_iota(jnp.int32, (tm, 1), 0)
    grids = [None, None]
    for s, p in enumerate(positions):
        p_hi = jnp.floor(p / ID_SPLIT)
        at_hi = jnp.where(lane == p_hi, 1.0, 0.0)
        at_lo = jnp.where(lane == p - ID_SPLIT * p_hi, 1.0, 0.0).astype(BF16)
        a = s * tm + local_tok
        for d, digit in enumerate((a // ID_SPLIT, a % ID_SPLIT)):
            part = _tn((at_hi * digit.astype(F32)).astype(BF16), at_lo)
            grids[d] = part if grids[d] is None else grids[d] + part
    a_sorted = (grids[0] * ID_SPLIT + grids[1]).astype(jnp.int32)
    dst = (a_sorted // tm) * n_tokens + tile_idx * tm + a_sorted % tm
    order_ref[0] = dst[:ID_SPLIT, :]
    count_ref[0] =jnp.broadcast_to(jnp.sum(member, axis=0, keepdims=True), (8, LANES)).astype(jnp.int32)


def _router_params(w_rg, b_rg, w_re, b_re):
    n_log = N_GROUPS + N_EXPERTS
    w = jnp.zeros((D_MODEL, LANES), F32).at[:, :N_GROUPS].set(w_rg).at[:, N_GROUPS:n_log].set(w_re).astype(BF16)
    b = jnp.zeros((1, LANES), F32).at[0, :N_GROUPS].set(b_rg).at[0, N_GROUPS:n_log].set(b_re)
    return w, b


W_CAST_ROWS = 256


def _cast_weight(src_ref, dst_ref):
    def body(c, carry):
        rows = pl.ds(pl.multiple_of(c * W_CAST_ROWS, W_CAST_ROWS), W_CAST_ROWS)
        dst_ref[rows, :] = src_ref[rows, :].astype(BF16)
        return carry
    lax.fori_loop(0, src_ref.shape[0] // W_CAST_ROWS, body, 0)


def _expert_changed(blk_e, blk):
    prev = blk_e[jnp.maximum(blk - 1, 0)]
    return (blk == 0) | (blk_e[blk] != prev)


WEIGHT_DMA_PRIORITY = 1


def _switch_expert(blk_e, nxt_e, blk, layer, w_hbms, stages, dsts, sem):
    def copies(e):
        return [pltpu.make_async_copy(w.at[layer, e], st, sem.at[i])
                for i, (w, st) in enumerate(zip(w_hbms, stages))]

    @pl.when(blk == 0)
    def _():
        for cp in copies(blk_e[0]):
            cp.start(priority=WEIGHT_DMA_PRIORITY)

    @pl.when(_expert_changed(blk_e, blk))
    def _():
        for cp, st, dst in zip(copies(blk_e[blk]), stages, dsts):
            cp.wait()
            _cast_weight(st, dst)

        @pl.when(nxt_e[blk] >= 0)
        def _():
            for cp in copies(nxt_e[blk]):
                cp.start(priority=WEIGHT_DMA_PRIORITY)


GATHER_AHEAD = 3
GATHER_SLOTS = GATHER_AHEAD + 1
MOE_CHUNKS = 4


SCATTER_SLOTS = 3


IDX_ROWS = 8
IDX_DST = GATHER_AHEAD + 1
IDX_DST_PREV = GATHER_AHEAD + 2


def _experts_body(blk_e, n_real, nxt_e, idx_ref, hn_hbm, wg_hbm, wu_hbm, wd_hbm, out_hbm, xs, xb, hid, ys,
                  wg_stage, wu_stage, wd_stage, wgb, wub, wdb, gsem, ssem, wsem, *, n_assign, layer):
    blk = pl.program_id(0)
    dummy_slot = SCATTER_SLOTS - 1

    def gather_row(idx_row, slot, r):
        return pltpu.make_async_copy(hn_hbm.at[pl.ds(idx_ref[0, idx_row, r], 1), :], xs.at[slot, pl.ds(r, 1), :],
                                     gsem.at[slot])

    def scatter_row(idx_row, slot, r):
        return pltpu.make_async_copy(ys.at[slot, pl.ds(r, 1), :], out_hbm.at[pl.ds(idx_ref[0, idx_row, r], 1), :],
                                     ssem.at[slot])

    def wait_gather(slot):
        pltpu.make_async_copy(hn_hbm.at[pl.ds(0, MOE_BLOCK), :], xs.at[slot], gsem.at[slot]).wait()

    def wait_scatter(slot):
        pltpu.make_async_copy(ys.at[slot], out_hbm.at[pl.ds(0, MOE_BLOCK), :], ssem.at[slot]).wait()

    @pl.when(blk == 0)
    def _():
        def body(r, carry):
            for s in range(GATHER_AHEAD):
                gather_row(s, s, r).start()
            return carry
        lax.fori_loop(0, MOE_BLOCK, body, 0)
        ys[dummy_slot] = jnp.zeros((MOE_BLOCK, HALF), jnp.uint32)
        for s in range(SCATTER_SLOTS):
            trash = pltpu.make_async_copy(
                ys.at[dummy_slot], out_hbm.at[pl.ds(n_assign + s * MOE_BLOCK, MOE_BLOCK), :], ssem.at[dummy_slot])
            trash.start()
            trash.wait()

    @pl.when(blk < n_real[0])
    def _():
        in_slot = blk % GATHER_SLOTS
        ahead_slot = (blk + GATHER_AHEAD) % GATHER_SLOTS
        out_slot = blk % SCATTER_SLOTS
        prev_slot = (blk + SCATTER_SLOTS - 1) % SCATTER_SLOTS

        _switch_expert(blk_e, nxt_e, blk, layer, (wg_hbm, wu_hbm, wd_hbm), (wg_stage, wu_stage, wd_stage),
                       (wgb, wub, wdb), wsem)
        wait_gather(in_slot)

        @pl.when(blk >= SCATTER_SLOTS - 1)
        def _():
            wait_scatter(out_slot)

        lo, hi = _unpack_halves(xs[in_slot])
        xb[:, :HALF] = lo.astype(BF16)
        xb[:, HALF:] = hi.astype(BF16)
        for r in range(MOE_BLOCK):
            scatter_row(IDX_DST_PREV, prev_slot, r).start(priority=r % 2)
            gather_row(GATHER_AHEAD, ahead_slot, r).start()
        cw = D_FF_EXPERT // MOE_CHUNKS
        for c in range(MOE_CHUNKS):
            cols = slice(c * cw, (c + 1) * cw)
            a = jnp.dot(xb[...], wgb[:, cols], preferred_element_type=F32)
            u = jnp.dot(xb[...], wub[:, cols], preferred_element_type=F32)
            hid[:, cols] = (a * jax.nn.sigmoid(a) * u).astype(BF16)
        cw = HALF // MOE_CHUNKS
        for c in range(MOE_CHUNKS):
            cols = slice(c * cw, (c + 1) * cw)
            hi_cols = slice(HALF + c * cw, HALF + (c + 1) * cw)
            ys[out_slot, :, cols] = pltpu.pack_elementwise(
                [jnp.dot(hid[...], wdb[:, cols], preferred_element_type=F32),
                 jnp.dot(hid[...], wdb[:, hi_cols], preferred_element_type=F32)], packed_dtype=BF16)

        @pl.when(blk == n_real[0] - 1)
        def _():
            def body(r, carry):
                scatter_row(IDX_DST, out_slot, r).start()
                return carry
            lax.fori_loop(0, MOE_BLOCK, body, 0)
            for s in range(1, GATHER_SLOTS):
                wait_gather((blk + s) % GATHER_SLOTS)
            for s in range(SCATTER_SLOTS):
                wait_scatter(s)


def _moe_forward(hn, order, counts, w_gate, w_up, w_down, layer, n_tokens):
    A = n_tokens * TOP_K
    out_rows = A + SCATTER_SLOTS * MOE_BLOCK
    R = A + N_EXPERTS * MOE_BLOCK
    n_blk = R // MOE_BLOCK
    i32 = jnp.int32
    n_tiles, per_tile = order.shape[0], ID_SPLIT * ID_SPLIT
    experts = jnp.arange(N_EXPERTS, dtype=i32)
    n = counts[:, 0, :N_EXPERTS]
    total = jnp.sum(n, axis=0)
    padded = ((total + MOE_BLOCK - 1) // MOE_BLOCK) * MOE_BLOCK
    pend = jnp.cumsum(padded)
    pstart = pend - padded
    n_real = (pend[-1] // MOE_BLOCK).astype(i32).reshape(1)
    blk = jnp.arange(n_blk, dtype=i32)
    blk_e = jnp.minimum(jnp.sum((pend[None, :] <= (blk * MOE_BLOCK)[:, None]).astype(i32), axis=1), N_EXPERTS - 1)
    is_e = (blk_e[:, None] == experts[None, :]).astype(i32)
    of_block = lambda per_expert: jnp.sum(is_e * per_expert[None, :], axis=1)
    cum_incl = jnp.cumsum(n, axis=0)
    run_shift = (jnp.cumsum(n, axis=1) - n) - (cum_incl - n)
    per_tile_of_block = lambda m: jnp.sum(is_e[:, None, :] * m[None, :, :], axis=2)
    cum_b, shift_b = per_tile_of_block(cum_incl), per_tile_of_block(run_shift)
    in_blk = jnp.arange(MOE_BLOCK, dtype=i32)
    k = (blk * MOE_BLOCK - of_block(pstart))[:, None] + in_blk[None, :]
    tile_of = jnp.minimum(jnp.sum((cum_b[:, None, :] <= k[:, :, None]).astype(i32), axis=2), n_tiles - 1)
    is_tile = (tile_of[:, :, None] == jnp.arange(n_tiles, dtype=i32)[None, None, :]).astype(i32)
    entry = tile_of * per_tile + k + jnp.sum(is_tile * shift_b[:, None, :], axis=2)
    pad_entry = A + (blk % SCATTER_SLOTS)[:, None] * MOE_BLOCK + in_blk[None, :]
    entry = jnp.where(k < of_block(total)[:, None], entry, pad_entry)
    n_pad = SCATTER_SLOTS * MOE_BLOCK
    dst_table = jnp.concatenate([order[:, :, :ID_SPLIT].reshape(A), A + jnp.arange(n_pad, dtype=i32)])
    row_dst = dst_table[entry]
    row_tok = row_dst % n_tokens
    row_dst_prev = jnp.concatenate([(A + (SCATTER_SLOTS - 1) * MOE_BLOCK + in_blk)[None, :], row_dst[:-1]], axis=0)
    last = n_real[0] - 1
    last_tok = lax.dynamic_slice_in_dim(row_tok, last, 1, axis=0)
    ahead = [jnp.where((blk + s <= last)[:, None],
                       jnp.concatenate([row_tok[s:], jnp.zeros((s, MOE_BLOCK), i32)], axis=0), last_tok)
             for s in range(GATHER_AHEAD + 1)]
    idx_rows = ahead + [row_dst, row_dst_prev]
    idx_rows += [jnp.zeros_like(row_dst)] * (IDX_ROWS - len(idx_rows))
    idx_all = jnp.stack(idx_rows, axis=1)

    run_end = pend[blk_e] // MOE_BLOCK
    nxt_e = jnp.where(run_end < n_real[0], blk_e[jnp.minimum(run_end, n_blk - 1)], -1).astype(jnp.int32)

    idx_spec = pl.BlockSpec((1, IDX_ROWS, MOE_BLOCK), lambda i, be, nr, nx: (jnp.minimum(i, nr[0] - 1), 0, 0),
                            memory_space=pltpu.SMEM)
    hbm = pl.BlockSpec(memory_space=pl.ANY)
    up_shape, down_shape = (D_MODEL, D_FF_EXPERT), (D_FF_EXPERT, D_MODEL)
    return pl.pallas_call(
        functools.partial(_experts_body, n_assign=A, layer=layer),
        grid_spec=pltpu.PrefetchScalarGridSpec(
            num_scalar_prefetch=3, grid=(n_blk,),
            in_specs=[idx_spec] + [hbm] * 4,
            out_specs=hbm,
            scratch_shapes=[pltpu.VMEM((GATHER_SLOTS, MOE_BLOCK, HALF), jnp.uint32),
                            pltpu.VMEM((MOE_BLOCK, D_MODEL), BF16),
                            pltpu.VMEM((MOE_BLOCK, D_FF_EXPERT), BF16),
                            pltpu.VMEM((SCATTER_SLOTS, MOE_BLOCK, HALF), jnp.uint32),
                            pltpu.VMEM(up_shape, F32), pltpu.VMEM(up_shape, F32), pltpu.VMEM(down_shape, F32),
                            pltpu.VMEM(up_shape, BF16), pltpu.VMEM(up_shape, BF16), pltpu.VMEM(down_shape, BF16),
                            pltpu.SemaphoreType.DMA((GATHER_SLOTS,)), pltpu.SemaphoreType.DMA((SCATTER_SLOTS,)),
                            pltpu.SemaphoreType.DMA((3,))]),
        out_shape=jax.ShapeDtypeStruct((out_rows, HALF), jnp.uint32),
        compiler_params=_params(("arbitrary",)), name="moe_experts",
    )(blk_e, n_real, nxt_e, idx_all, hn, w_gate, w_up, w_down)


FINAL_TM = 512


def _final_body(h_ref, y0_ref, y1_ref, gate_ref, g_ref, o_ref):
    gate = gate_ref[...]
    y0, y1 = _unpack_halves(y0_ref[...]), _unpack_halves(y1_ref[...])
    halves = [h_ref[:, i * HALF:(i + 1) * HALF] + gate[:, 0:1] * y0[i] + gate[:, 1:2] * y1[i] for i in range(2)]
    mean_sq = sum(jnp.sum(hh * hh, axis=-1, keepdims=True) for hh in halves) / D_MODEL
    inv = lax.rsqrt(mean_sq + RMS_EPS)
    for i, hh in enumerate(halves):
        o_ref[:, i * HALF:(i + 1) * HALF] = hh * inv * g_ref[:, i * HALF:(i + 1) * HALF]


def _final(h, ys, gate, gain):
    rows = h.shape[0]
    n_tiles = rows // FINAL_TM
    tile = lambda width: pl.BlockSpec((FINAL_TM, width), lambda i: (i, 0))
    g2 = gain.reshape(1, D_MODEL).astype(F32)
    return pl.pallas_call(
        _final_body, grid=(n_tiles,),
        in_specs=[tile(D_MODEL), tile(HALF), pl.BlockSpec((FINAL_TM, HALF), lambda i: (i + n_tiles, 0)),
                  tile(LANES), pl.BlockSpec((1, D_MODEL), lambda i: (0, 0))],
        out_specs=tile(D_MODEL), out_shape=jax.ShapeDtypeStruct((rows, D_MODEL), F32),
        compiler_params=_params(("parallel",)), name="combine_final_norm")(h, ys, ys, gate, g2)


def kernel(x, mem, norm_mix, w_in, attn_out_norm, sink_logit, w_gla_gf, b_gla_gf, w_gla_gb, b_gla_gb, gla_out_norm, w_out, norm_cross, norm_mem, w_cq, w_ck, w_cv, w_co, norm_ffn, w_router_group, b_router_group, w_router_expert, b_router_expert, w_gate, w_up, w_down, norm_final):
    batch, seq, _ = x.shape
    mem_len = mem.shape[1]
    n_tokens = batch * seq
    h = x.reshape(n_tokens, D_MODEL)
    memf = mem.reshape(batch * mem_len, D_MODEL)
    assert norm_mix.shape[0] == 1, "the combine step is fused with the final norm: single-layer stacks only"
    for l in range(norm_mix.shape[0]):
        w_lr = jnp.zeros((D_MODEL, LANES), F32).at[:, :2 * GLA_LOWRANK].set(w_in[l][:, MAIN_COLS:]).astype(BF16)
        proj, lr = _dense([h], w_in, l, n_cols=MAIN_COLS, out_dtype=BF16, gain=norm_mix[l], extra_w=w_lr,
                          name="in_proj")
        o_a = _window_attention(proj, sink_logit[l], attn_out_norm[l], batch, seq).reshape(n_tokens, ATT_Q)
        o_g = _gla(proj, lr, w_gla_gf[l], b_gla_gf[l], w_gla_gb[l], b_gla_gb[l], gla_out_norm[l],
                   batch, seq).reshape(n_tokens, GLA_V)
        h = _dense([o_a, o_g], w_out, l, n_cols=D_MODEL, out_dtype=F32, res=h, name="out_proj")
        kx, vx = _mem_kv(memf, norm_mem[l], w_ck, w_cv, l)
        router_w, router_b = _router_params(w_router_group[l], b_router_group[l],
                                            w_router_expert[l], b_router_expert[l])
        h, hn, gate, order, counts = _cross_block(h, norm_cross[l], w_cq, kx, vx, w_co, norm_ffn[l],
                                                  router_w, router_b, l, batch, seq, mem_len)
        ys = _moe_forward(hn, order, counts, w_gate, w_up, w_down, l, n_tokens)
    return _final(h, ys, gate, norm_final).reshape(batch, seq, D_MODEL)
```

```python
import functools

import jax
import jax.numpy as jnp
from jax import lax
from jax.experimental import pallas as pl
from jax.experimental.pallas import tpu as pltpu

F32 = jnp.float32
BF16 = jnp.bfloat16

D_MODEL = 2048
N_Q_HEADS = 8
N_KV_HEADS = 2
Q_PER_KV = N_Q_HEADS // N_KV_HEADS
HEAD_DIM = 128
WINDOW = 128
WBLK = 128
GLA_HEADS = 4
GLA_DK = 128
GLA_DV = 256
GLA_LOWRANK = 16
GLA_TAU = 16.0
GLA_CHUNK = 64
ATT_Q = N_Q_HEADS * HEAD_DIM
ATT_KV = N_KV_HEADS * HEAD_DIM
GLA_QK = GLA_HEADS * GLA_DK
GLA_V = GLA_HEADS * GLA_DV
MAIN_COLS = ATT_Q + 2 * ATT_KV + 2 * GLA_QK + 2 * GLA_V
COL_KA = ATT_Q
COL_VA = COL_KA + ATT_KV
COL_QG = COL_VA + ATT_KV
COL_KG = COL_QG + GLA_QK
COL_VG = COL_KG + GLA_QK
COL_RG = COL_VG + GLA_V
X_HEADS = 4
X_HEAD_DIM = D_MODEL // X_HEADS
N_GROUPS = 4
EXPERTS_PER_GROUP = 8
N_EXPERTS = N_GROUPS * EXPERTS_PER_GROUP
TOP_K = 2
D_FF_EXPERT = D_MODEL // 2
MOE_BLOCK = 256
RMS_EPS = 1e-6
NEG_INF = -1e30

LANES = 128
VMEM_LIMIT = 56 * 1024 * 1024
W_STAGE_ROWS = 128


def _params(sem):
    return pltpu.CompilerParams(dimension_semantics=sem, vmem_limit_bytes=VMEM_LIMIT)


def _nt(a, b):
    return lax.dot_general(a, b, (((1,), (1,)), ((), ())), preferred_element_type=F32)


def _tn(a, b):
    return lax.dot_general(a, b, (((0,), (0,)), ((), ())), preferred_element_type=F32)


def _rms(x, gain):
    return x * lax.rsqrt(jnp.mean(x * x, axis=-1, keepdims=True) + RMS_EPS) * gain


def _load_weight_bf16(w_hbm, layer, wb, stage, sem, k_rows, n_cols):
    n_chunks = k_rows // W_STAGE_ROWS

    def copy(c):
        return pltpu.make_async_copy(
            w_hbm.at[layer, pl.ds(c * W_STAGE_ROWS, W_STAGE_ROWS), pl.ds(0, n_cols)],
            stage.at[c % 2], sem.at[c % 2])

    copy(0).start()
    for c in range(n_chunks):
        if c + 1 < n_chunks:
            copy(c + 1).start()
        copy(c).wait()
        wb[c * W_STAGE_ROWS:(c + 1) * W_STAGE_ROWS, :] = stage[c % 2].astype(BF16)


def _dense_body(*refs, part_widths, has_norm, has_extra, has_res, n_cols, n_chunk, layer):
    it = iter(refs)
    x_refs = [next(it) for _ in part_widths]
    g_ref = next(it) if has_norm else None
    w_hbm = next(it)
    ew_ref = next(it) if has_extra else None
    res_ref = next(it) if has_res else None
    o_ref = next(it)
    eo_ref = next(it) if has_extra else None
    wb, stage, sem = next(it), next(it), next(it)
    u_ref = next(it) if has_norm else None
    k_rows = sum(part_widths)

    @pl.when(pl.program_id(0) == 0)
    def _():
        _load_weight_bf16(w_hbm, layer, wb, stage, sem, k_rows, n_cols)

    if has_norm:
        u_ref[...] = _rms(x_refs[0][...], g_ref[...]).astype(BF16)
        lhs = [(u_ref, 0, k_rows)]
    else:
        lhs, off = [], 0
        for r, kw in zip(x_refs, part_widths):
            lhs.append((r, off, kw))
            off += kw
    for n0 in range(0, n_cols, n_chunk):
        acc = None
        for r, off, kw in lhs:
            d = jnp.dot(r[...], wb[off:off + kw, n0:n0 + n_chunk], preferred_element_type=F32)
            acc = d if acc is None else acc + d
        if has_res:
            acc = acc + res_ref[:, n0:n0 + n_chunk]
        o_ref[:, n0:n0 + n_chunk] = acc.astype(o_ref.dtype)
    if has_extra:
        eo_ref[...] = jnp.dot(u_ref[...], ew_ref[...], preferred_element_type=F32)


def _dense(xs, w, layer, *, n_cols, out_dtype, gain=None, extra_w=None, res=None, tm=512, n_chunk=512, name):
    rows = xs[0].shape[0]
    part_widths = tuple(x.shape[1] for x in xs)
    k_rows = sum(part_widths)
    has_norm, has_extra, has_res = gain is not None, extra_w is not None, res is not None
    row_spec = lambda width: pl.BlockSpec((tm, width), lambda i: (i, 0))
    full_spec = lambda a: pl.BlockSpec(a.shape, lambda i: (0, 0))
    args, in_specs = list(xs), [row_spec(kw) for kw in part_widths]
    if has_norm:
        args.append(gain.reshape(1, k_rows).astype(F32))
        in_specs.append(full_spec(args[-1]))
    args.append(w)
    in_specs.append(pl.BlockSpec(memory_space=pl.ANY))
    if has_extra:
        args.append(extra_w)
        in_specs.append(full_spec(extra_w))
    if has_res:
        args.append(res)
        in_specs.append(row_spec(n_cols))
    out_shape = [jax.ShapeDtypeStruct((rows, n_cols), out_dtype)]
    out_specs = [row_spec(n_cols)]
    if has_extra:
        out_shape.append(jax.ShapeDtypeStruct((rows, extra_w.shape[1]), F32))
        out_specs.append(row_spec(extra_w.shape[1]))
    scratch = [pltpu.VMEM((k_rows, n_cols), BF16),
               pltpu.VMEM((2, W_STAGE_ROWS, n_cols), F32),
               pltpu.SemaphoreType.DMA((2,))]
    if has_norm:
        scratch.append(pltpu.VMEM((tm, k_rows), BF16))
    body = functools.partial(_dense_body, part_widths=part_widths, has_norm=has_norm, has_extra=has_extra,
                             has_res=has_res, n_cols=n_cols, n_chunk=n_chunk, layer=layer)
    outs = pl.pallas_call(
        body, grid=(rows // tm,), in_specs=in_specs, out_specs=out_specs, out_shape=out_shape,
        scratch_shapes=scratch, compiler_params=_params(("arbitrary",)), name=name)(*args)
    return outs if has_extra else outs[0]


ATT_TQ = 512


def _winattn_body(sink_ref, q_ref, kp_ref, km_ref, kn_ref, vp_ref, vm_ref, vn_ref, g_ref, o_ref,
                  kcat, vcat, obuf, *, seq):
    s0 = pl.program_id(1) * ATT_TQ
    kcat[0:WBLK, :] = kp_ref[0]
    kcat[WBLK:WBLK + ATT_TQ, :] = km_ref[0]
    kcat[WBLK + ATT_TQ:, :] = kn_ref[0]
    vcat[0:WBLK, :] = vp_ref[0]
    vcat[WBLK:WBLK + ATT_TQ, :] = vm_ref[0]
    vcat[WBLK + ATT_TQ:, :] = vn_ref[0]
    qi = lax.broadcasted_iota(jnp.int32, (WBLK, 3 * WBLK), 0) + WBLK
    ki = lax.broadcasted_iota(jnp.int32, (WBLK, 3 * WBLK), 1)
    dist_i = jnp.abs(ki - qi)
    neg_dist = jnp.where(dist_i <= WINDOW, -dist_i.astype(F32), NEG_INF)
    scale = HEAD_DIM ** -0.5
    for qb in range(ATT_TQ // WBLK):
        kabs = s0 + (qb - 1) * WBLK + ki
        bias_unit = jnp.where((kabs >= 0) & (kabs < seq), neg_dist, NEG_INF)
        head_cols = lambda j: slice(j * HEAD_DIM, (j + 1) * HEAD_DIM)
        kv_cols = lambda j: head_cols(j // Q_PER_KV)
        key_rows = slice(qb * WBLK, (qb + 3) * WBLK)
        scores = []
        for j in range(N_Q_HEADS):
            slope = 2.0 ** (-8.0 * (j + 1) / N_Q_HEADS)
            q = q_ref[0, qb * WBLK:(qb + 1) * WBLK, head_cols(j)]
            scores.append(_nt(q, kcat[key_rows, kv_cols(j)]) * scale + slope * bias_unit)
        probs, denoms = [], []
        for j in range(N_Q_HEADS):
            sink = sink_ref[j]
            m = jnp.maximum(jnp.max(scores[j], axis=-1, keepdims=True), sink)
            p = jnp.exp(scores[j] - m)
            denoms.append(jnp.sum(p, axis=-1, keepdims=True) + jnp.exp(sink - m))
            probs.append(p.astype(BF16))
        for j in range(N_Q_HEADS):
            pv = jnp.dot(probs[j], vcat[key_rows, kv_cols(j)], preferred_element_type=F32)
            obuf[qb, :, head_cols(j)] = pv / denoms[j]
        o_ref[0, qb * WBLK:(qb + 1) * WBLK, :] = _rms(obuf[qb], g_ref[...]).astype(o_ref.dtype)


def _window_attention(proj, sink_logit, gain, batch, seq):
    p3 = proj.reshape(batch, seq, MAIN_COLS)
    nq = seq // ATT_TQ
    per = ATT_TQ // WBLK
    last = seq // WBLK - 1
    main = lambda col: pl.BlockSpec((1, ATT_TQ, ATT_KV), lambda b, i: (b, i, col))
    prev = lambda col: pl.BlockSpec((1, WBLK, ATT_KV), lambda b, i: (b, jnp.maximum(i * per - 1, 0), col))
    nxt = lambda col: pl.BlockSpec((1, WBLK, ATT_KV), lambda b, i: (b, jnp.minimum(i * per + per, last), col))
    ck, cv = COL_KA // ATT_KV, COL_VA // ATT_KV
    return pl.pallas_call(
        functools.partial(_winattn_body, seq=seq),
        grid=(batch, nq),
        in_specs=[pl.BlockSpec(memory_space=pltpu.SMEM),
                  pl.BlockSpec((1, ATT_TQ, ATT_Q), lambda b, i: (b, i, 0)),
                  prev(ck), main(ck), nxt(ck), prev(cv), main(cv), nxt(cv),
                  pl.BlockSpec((1, ATT_Q), lambda b, i: (0, 0))],
        out_specs=pl.BlockSpec((1, ATT_TQ, ATT_Q), lambda b, i: (b, i, 0)),
        out_shape=jax.ShapeDtypeStruct((batch, seq, ATT_Q), BF16),
        scratch_shapes=[pltpu.VMEM((ATT_TQ + 2 * WBLK, ATT_KV), BF16),
                        pltpu.VMEM((ATT_TQ + 2 * WBLK, ATT_KV), BF16),
                        pltpu.VMEM((ATT_TQ // WBLK, WBLK, ATT_Q), F32)],
        compiler_params=_params(("parallel", "parallel")), name="window_attention",
    )(sink_logit.astype(F32), p3, p3, p3, p3, p3, p3, p3, gain.reshape(1, ATT_Q).astype(F32))


GLA_SCAN_ROWS = 256
GLA_SCAN_UNROLL = 2
GLA_EPI_ROWS = 512
GLA_UNROLL = 8


def _split2(x):
    hi = x.astype(BF16)
    lo = (x - hi.astype(F32)).astype(BF16)
    return hi, lo


def _gla_body(q_ref, k_ref, v_ref, r_ref, lr_ref, wf_ref, wb_ref, bf_ref, bb_ref, gn_ref, o_ref,
              cum_f, cum_b, acc, st_f, st_b, *, seq):
    C = GLA_CHUNK
    n_chunks = seq // C
    ri = lax.broadcasted_iota(jnp.int32, (GLA_SCAN_ROWS, GLA_SCAN_ROWS), 0)
    ci = lax.broadcasted_iota(jnp.int32, (GLA_SCAN_ROWS, GLA_SCAN_ROWS), 1)
    same = (ri // C) == (ci // C)
    tri_f = jnp.where(same & (ci <= ri), 1.0, 0.0).astype(BF16)
    tri_b = jnp.where(same & (ci >= ri), 1.0, 0.0).astype(BF16)

    def scan_body(t, carry):
        row_sets = [pl.ds(pl.multiple_of((t * GLA_SCAN_UNROLL + u) * GLA_SCAN_ROWS, GLA_SCAN_ROWS), GLA_SCAN_ROWS)
                    for u in range(GLA_SCAN_UNROLL)]
        lrs = [lr_ref[0, rows, :].astype(BF16) for rows in row_sets]
        zs = [jnp.dot(lr, w_ref[...], preferred_element_type=F32) + b_ref[...]
              for lr in lrs for w_ref, b_ref in ((wf_ref, bf_ref), (wb_ref, bb_ref))]
        gs = [_split2((jnp.minimum(z, 0.0) - jnp.log1p(jnp.exp(-jnp.abs(z)))) / GLA_TAU) for z in zs]
        for n, g3 in enumerate(gs):
            tri, dst = ((tri_f, cum_f), (tri_b, cum_b))[n % 2]
            dst[row_sets[n // 2], :] = sum(jnp.dot(tri, part, preferred_element_type=F32) for part in g3)
        return carry

    lax.fori_loop(0, seq // (GLA_SCAN_ROWS * GLA_SCAN_UNROLL), scan_body, 0)

    acc[...] = jnp.zeros_like(acc)
    st_f[...] = jnp.zeros_like(st_f)
    st_b[...] = jnp.zeros_like(st_b)
    rr = lax.broadcasted_iota(jnp.int32, (C, C), 0)
    cc = lax.broadcasted_iota(jnp.int32, (C, C), 1)
    scale = GLA_DK ** -0.5

    def chunk_body(i, carry):
        jobs = []
        for u in range(GLA_UNROLL):
            c = i * GLA_UNROLL + u
            jobs += [(c, cum_f, True), (n_chunks - 1 - c, cum_b, False)]
        prep = []
        for c, cum, forward in jobs:
            rows = pl.ds(pl.multiple_of(c * C, C), C)
            b = cum[rows, :]
            b_end = b[C - 1:C, :] if forward else b[0:1, :]
            q = q_ref[0, rows, :].astype(F32) * scale
            k = k_ref[0, rows, :].astype(F32)
            v = v_ref[0, rows, :]
            q_dec = (q * jnp.exp(b)).astype(BF16)
            k_inc = (k * jnp.exp(-b)).astype(BF16)
            k_dec = (k * jnp.exp(b_end - b)).astype(BF16)
            prep.append((rows, v, q_dec, k_inc, k_dec, jnp.exp(b_end)))
        attn = [_nt(q_dec, k_inc) for _, _, q_dec, k_inc, _, _ in prep]
        kv_t = [_tn(v, k_dec) for _, v, _, _, k_dec, _ in prep]
        state_t = {True: st_f[...], False: st_b[...]}
        o_inter = []
        for (_, _, forward), (_, _, q_dec, _, _, decay), kv in zip(jobs, prep, kv_t):
            o_inter.append(_nt(q_dec, state_t[forward].astype(BF16)))
            state_t[forward] = state_t[forward] * decay + kv
        st_f[...] = state_t[True]
        st_b[...] = state_t[False]
        for (_, _, forward), (rows, v, _, _, _, _), a, oi in zip(jobs, prep, attn, o_inter):
            a = jnp.where((cc <= rr) if forward else (cc >= rr), a, 0.0)
            o = jnp.dot(a.astype(BF16), v, preferred_element_type=F32) + oi
            acc[rows, :] = acc[rows, :] + o
        return carry

    lax.fori_loop(0, n_chunks // GLA_UNROLL, chunk_body, 0)

    def epi_body(t, carry):
        rows = pl.ds(pl.multiple_of(t * GLA_EPI_ROWS, GLA_EPI_ROWS), GLA_EPI_ROWS)
        r = r_ref[0, rows, :].astype(F32)
        o_ref[0, rows, :] = (_rms(acc[rows, :], gn_ref[...]) * (r * jax.nn.sigmoid(r))).astype(o_ref.dtype)
        return carry

    lax.fori_loop(0, seq // GLA_EPI_ROWS, epi_body, 0)


def _gla(proj, lr, w_gf, b_gf, w_gb, b_gb, gain, batch, seq):
    p3 = proj.reshape(batch, seq, MAIN_COLS)
    lr3 = lr.reshape(batch, seq, LANES)
    wf = jnp.zeros((LANES, GLA_QK), F32).at[:GLA_LOWRANK].set(w_gf).astype(BF16)
    wb = jnp.zeros((LANES, GLA_QK), F32).at[GLA_LOWRANK:2 * GLA_LOWRANK].set(w_gb).astype(BF16)
    seq_blk = lambda width, col0: pl.BlockSpec((1, seq, width), lambda b, h: (b, 0, col0 // width + h))
    head_w = pl.BlockSpec((LANES, GLA_DK), lambda b, h: (0, h))
    head_b = pl.BlockSpec((1, GLA_DK), lambda b, h: (0, h))
    return pl.pallas_call(
        functools.partial(_gla_body, seq=seq),
        grid=(batch, GLA_HEADS),
        in_specs=[seq_blk(GLA_DK, COL_QG), seq_blk(GLA_DK, COL_KG), seq_blk(GLA_DV, COL_VG),
                  seq_blk(GLA_DV, COL_RG),
                  pl.BlockSpec((1, seq, LANES), lambda b, h: (b, 0, 0)),
                  head_w, head_w, head_b, head_b,
                  pl.BlockSpec((1, GLA_DV), lambda b, h: (0, 0))],
        out_specs=pl.BlockSpec((1, seq, GLA_DV), lambda b, h: (b, 0, h)),
        out_shape=jax.ShapeDtypeStruct((batch, seq, GLA_V), BF16),
        scratch_shapes=[pltpu.VMEM((seq, GLA_DK), F32), pltpu.VMEM((seq, GLA_DK), F32),
                        pltpu.VMEM((seq, GLA_DV), F32),
                        pltpu.VMEM((GLA_DV, GLA_DK), F32), pltpu.VMEM((GLA_DV, GLA_DK), F32)],
        compiler_params=_params(("parallel", "parallel")), name="gla",
    )(p3, p3, p3, p3, lr3, wf, wb, b_gf.reshape(1, GLA_QK).astype(F32), b_gb.reshape(1, GLA_QK).astype(F32),
      gain.reshape(1, GLA_DV).astype(F32))


MEM_TM = 512
MEM_K_CHUNK = 256


def _mem_kv_body(m_ref, g_ref, wk_ref, wv_ref, k_ref, v_ref, u, acc_k, acc_v):
    step = pl.program_id(1)
    n_chunks = D_MODEL // MEM_K_CHUNK

    @pl.when(step == 0)
    def _():
        normed = _rms(m_ref[...], g_ref[...]).astype(BF16)
        for c in range(n_chunks):
            u[c] = normed[:, c * MEM_K_CHUNK:(c + 1) * MEM_K_CHUNK]
        acc_k[...] = jnp.zeros_like(acc_k)
        acc_v[...] = jnp.zeros_like(acc_v)

    lhs = u[step]
    acc_k[...] = acc_k[...] + jnp.dot(lhs, wk_ref[...].astype(BF16), preferred_element_type=F32)
    acc_v[...] = acc_v[...] + jnp.dot(lhs, wv_ref[...].astype(BF16), preferred_element_type=F32)

    @pl.when(step == n_chunks - 1)
    def _():
        k_ref[...] = acc_k[...].astype(k_ref.dtype)
        v_ref[...] = acc_v[...].astype(v_ref.dtype)


def _mem_kv(mem_rows, gain, w_ck, w_cv, layer):
    rows = mem_rows.shape[0]
    n_chunks = D_MODEL // MEM_K_CHUNK
    tile = pl.BlockSpec((MEM_TM, D_MODEL), lambda i, c: (i, 0))
    w_spec = pl.BlockSpec((None, MEM_K_CHUNK, D_MODEL), lambda i, c: (layer, c, 0))
    g2 = gain.reshape(1, D_MODEL).astype(F32)
    out = jax.ShapeDtypeStruct((rows, D_MODEL), BF16)
    return pl.pallas_call(
        _mem_kv_body, grid=(rows // MEM_TM, n_chunks),
        in_specs=[tile, pl.BlockSpec((1, D_MODEL), lambda i, c: (0, 0)), w_spec, w_spec],
        out_specs=[tile, tile], out_shape=[out, out],
        scratch_shapes=[pltpu.VMEM((n_chunks, MEM_TM, MEM_K_CHUNK), BF16),
                        pltpu.VMEM((MEM_TM, D_MODEL), F32), pltpu.VMEM((MEM_TM, D_MODEL), F32)],
        compiler_params=_params(("parallel", "arbitrary")), name="mem_kv_proj")(mem_rows, g2, w_ck, w_cv)


XATT_TQ = 512


XATT_CHUNK = 512


def _cross_body(h_ref, g_ref, wq_hbm, k_ref, v_ref, wo_hbm, gr_ref, wr_ref, br_ref,
                o_ref, hn_ref, gate_ref, order_ref, count_ref,
                wqb, wob, stage, sem, u_scr, q_scr, a_scr, prev, *, layer, n_tokens):
    step = pl.program_id(0)

    @pl.when(step == 0)
    def _():
        _load_weight_bf16(wq_hbm, layer, wqb, stage, sem, D_MODEL, D_MODEL)
        _load_weight_bf16(wo_hbm, layer, wob, stage, sem, D_MODEL, D_MODEL)
        prev[...] = jnp.zeros_like(prev)

    router = _route_tile(prev[...], gr_ref[...], wr_ref, br_ref, hn_ref, gate_ref, order_ref, count_ref,
                         jnp.maximum(step - 1, 0), n_tokens)

    scale = X_HEAD_DIM ** -0.5
    heads = [slice(h * X_HEAD_DIM, (h + 1) * X_HEAD_DIM) for h in range(X_HEADS)]
    chunks = [slice(n0, n0 + XATT_CHUNK) for n0 in range(0, D_MODEL, XATT_CHUNK)]

    def norm(rows):
        u_scr[rows, :] = _rms(h_ref[rows, :], g_ref[...]).astype(BF16)

    def q_proj(rows):
        for cols in chunks:
            q_scr[rows, cols] = jnp.dot(u_scr[rows, :], wqb[:, cols], preferred_element_type=F32).astype(BF16)

    def scores(rows):
        return [_nt(q_scr[rows, hd], k_ref[0, :, hd]) * scale for hd in heads]

    def softmax(s_list):
        out = []
        for s in s_list:
            p = jnp.exp(s - jnp.max(s, axis=-1, keepdims=True))
            out.append((p.astype(BF16), jnp.sum(p, axis=-1, keepdims=True)))
        return out

    def attend(rows, probs):
        for hd, (p, denom) in zip(heads, probs):
            a_scr[rows, hd] = (jnp.dot(p, v_ref[0, :, hd], preferred_element_type=F32) / denom).astype(BF16)

    def o_proj(rows):
        for cols in chunks:
            out = h_ref[rows, cols] + jnp.dot(a_scr[rows, :], wob[:, cols], preferred_element_type=F32)
            o_ref[rows, cols] = out
            prev[rows, cols] = out

    half = XATT_TQ // 2
    first, second = slice(0, half), slice(half, XATT_TQ)
    norm(first)
    q_proj(first)
    next(router)
    norm(second)
    s_first = scores(first)
    q_proj(second)
    next(router)
    p_first = softmax(s_first)
    attend(first, p_first)
    s_second = scores(second)
    o_proj(first)
    for _ in router:
        pass
    p_second = softmax(s_second)
    attend(second, p_second)
    o_proj(second)


def _cross_block(h, gain, w_cq, kx, vx, w_co, router_gain, router_w, router_b, layer, batch, seq, mem_len):
    rows = h.shape[0]
    n_tiles = rows // XATT_TQ
    tiles_per_batch = seq // XATT_TQ
    k3 = kx.reshape(batch, mem_len, D_MODEL)
    v3 = vx.reshape(batch, mem_len, D_MODEL)
    as_row = lambda g: g.reshape(1, D_MODEL).astype(F32)
    cur = lambda i: jnp.minimum(i, n_tiles - 1)
    lag = lambda i: jnp.maximum(i - 1, 0)
    tile = pl.BlockSpec((XATT_TQ, D_MODEL), lambda i: (cur(i), 0))
    mem_spec = pl.BlockSpec((1, mem_len, D_MODEL), lambda i: (cur(i) // tiles_per_batch, 0, 0))
    full = lambda a: pl.BlockSpec(a.shape, lambda i: (0, 0))
    lag_rows = lambda width: pl.BlockSpec((XATT_TQ, width), lambda i: (lag(i), 0))
    lag_tile = lambda sublanes: pl.BlockSpec((1, sublanes, LANES), lambda i: (lag(i), 0, 0))
    assert TOP_K * XATT_TQ == ID_SPLIT * ID_SPLIT, "the router's sort grid holds one tile's assignments"
    hbm = pl.BlockSpec(memory_space=pl.ANY)
    g_cross, g_router = as_row(gain), as_row(router_gain)
    return pl.pallas_call(
        functools.partial(_cross_body, layer=layer, n_tokens=rows), grid=(n_tiles + 1,),
        in_specs=[tile, full(g_cross), hbm, mem_spec, mem_spec, hbm, full(g_router), full(router_w), full(router_b)],
        out_specs=[tile, lag_rows(HALF), lag_rows(LANES), lag_tile(ID_SPLIT), lag_tile(8)],
        out_shape=[jax.ShapeDtypeStruct((rows, D_MODEL), F32),
                   jax.ShapeDtypeStruct((rows, HALF), jnp.uint32),
                   jax.ShapeDtypeStruct((rows, LANES), F32),
                   jax.ShapeDtypeStruct((n_tiles, ID_SPLIT, LANES), jnp.int32),
                   jax.ShapeDtypeStruct((n_tiles, 8, LANES), jnp.int32)],
        scratch_shapes=[pltpu.VMEM((D_MODEL, D_MODEL), BF16), pltpu.VMEM((D_MODEL, D_MODEL), BF16),
                        pltpu.VMEM((2, W_STAGE_ROWS, D_MODEL), F32), pltpu.SemaphoreType.DMA((2,)),
                        pltpu.VMEM((XATT_TQ, D_MODEL), BF16), pltpu.VMEM((XATT_TQ, D_MODEL), BF16),
                        pltpu.VMEM((XATT_TQ, D_MODEL), BF16), pltpu.VMEM((XATT_TQ, D_MODEL), F32)],
        compiler_params=_params(("arbitrary",)), name="cross_block",
    )(h, g_cross, w_cq, k3, v3, w_co, g_router, router_w, router_b)


HALF = D_MODEL // 2


def _pack_halves(x):
    return pltpu.pack_elementwise([x[:, :HALF], x[:, HALF:]], packed_dtype=BF16)


def _unpack_halves(words):
    return [pltpu.unpack_elementwise(words, index=i, packed_dtype=BF16, unpacked_dtype=F32) for i in range(2)]


ID_SPLIT = 32


def _route_tile(h, gain, w_ref, b_ref, hn_ref, gate_ref, order_ref, count_ref, tile_idx, n_tokens):
    hn = _rms(h, gain)
    hn_ref[...] = _pack_halves(hn)
    logits = jnp.dot(hn.astype(BF16), w_ref[...], preferred_element_type=F32) + b_ref[...]
    yield
    lane = lax.broadcasted_iota(jnp.int32, logits.shape, 1).astype(F32)
    ninf = -jnp.inf
    first = lambda hit: jnp.min(jnp.where(hit, lane, float(LANES)), axis=-1, keepdims=True)
    in_groups = lane < N_GROUPS
    gl = jnp.where(in_groups, logits, ninf)
    gmax = jnp.max(gl, axis=-1, keepdims=True)
    g_idx = first(gl == gmax)
    p_group = 1.0 / jnp.sum(jnp.where(in_groups, jnp.exp(logits - gmax), 0.0), axis=-1, keepdims=True)
    lo = N_GROUPS + EXPERTS_PER_GROUP * g_idx
    el = jnp.where((lane >= lo) & (lane < lo + EXPERTS_PER_GROUP), logits, ninf)
    e1 = jnp.max(el, axis=-1, keepdims=True)
    i1 = first(el == e1)
    el2 = jnp.where(lane == i1, ninf, el)
    e2 = jnp.max(el2, axis=-1, keepdims=True)
    i2 = first(el2 == e2)
    t = jnp.exp(e2 - e1)
    w1 = p_group / (1.0 + t)
    w2 = p_group * t / (1.0 + t)
    gate_ref[...] = jnp.where(lane == 0, w1, jnp.where(lane == 1, w2, 0.0))

    tm = logits.shape[0]
    hit0, hit1 = lane == i1 - N_GROUPS, lane == i2 - N_GROUPS
    member = jnp.where(hit0 | hit1, 1.0, 0.0)
    member_b = member.astype(BF16)
    earlier = (lax.broadcasted_iota(jnp.int32, (tm, tm), 1) < lax.broadcasted_iota(jnp.int32, (tm, tm), 0))
    rank = jnp.dot(jnp.where(earlier, 1.0, 0.0).astype(BF16), member_b, preferred_element_type=F32)
    lower = (lax.broadcasted_iota(jnp.int32, (LANES, LANES), 0) < lax.broadcasted_iota(jnp.int32, (LANES, LANES), 1))
    run_start = jnp.sum(jnp.dot(member_b, jnp.where(lower, 1.0, 0.0).astype(BF16), preferred_element_type=F32),
                        axis=0, keepdims=True)
    yield
    pos = rank + run_start
    positions = [jnp.sum(jnp.where(hit, pos, 0.0), axis=-1, keepdims=True) for hit in (hit0, hit1)]
    local_tok = lax.broadcasted_iota(jnp.int32, (tm, 1), 0)
    grids = [None, None]
    for s, p in enumerate(positions):
        p_hi = jnp.floor(p / ID_SPLIT)
        at_hi = jnp.where(lane == p_hi, 1.0, 0.0)
        at_lo = jnp.where(lane == p - ID_SPLIT * p_hi, 1.0, 0.0).astype(BF16)
        a = s * tm + local_tok
        for d, digit in enumerate((a // ID_SPLIT, a % ID_SPLIT)):
            part = _tn((at_hi * digit.astype(F32)).astype(BF16), at_lo)
            grids[d] = part if grids[d] is None else grids[d] + part
    a_sorted = (grids[0] * ID_SPLIT + grids[1]).astype(jnp.int32)
    dst = (a_sorted // tm) * n_tokens + tile_idx * tm + a_sorted % tm
    order_ref[0] = dst[:ID_SPLIT, :]
    count_ref[0] =jnp.broadcast_to(jnp.sum(member, axis=0, keepdims=True), (8, LANES)).astype(jnp.int32)


def _router_params(w_rg, b_rg, w_re, b_re):
    n_log = N_GROUPS + N_EXPERTS
    w = jnp.zeros((D_MODEL, LANES), F32).at[:, :N_GROUPS].set(w_rg).at[:, N_GROUPS:n_log].set(w_re).astype(BF16)
    b = jnp.zeros((1, LANES), F32).at[0, :N_GROUPS].set(b_rg).at[0, N_GROUPS:n_log].set(b_re)
    return w, b


def _expert_changed(blk_e, blk):
    prev = blk_e[jnp.maximum(blk - 1, 0)]
    return (blk == 0) | (blk_e[blk] != prev)


WEIGHT_DMA_PRIORITY = 1


def _expert_weight_copies(layer, w_hbms, stages, sem):
    def copies(e):
        return [pltpu.make_async_copy(w.at[layer, e], st, sem.at[i])
                for i, (w, st) in enumerate(zip(w_hbms, stages))]
    return copies


GATHER_AHEAD = 3
GATHER_SLOTS = GATHER_AHEAD + 1
MOE_CHUNKS = 2


SCATTER_SLOTS = 3


IDX_ROWS = 8
IDX_DST = GATHER_AHEAD + 1
IDX_DST_PREV = GATHER_AHEAD + 2


def _experts_body(blk_e, n_real, nxt_e, idx_ref, hn_hbm, wg_hbm, wu_hbm, wd_hbm, out_hbm, xs, xb, hid, ys,
                  wg_stage, wu_stage, wd_stage, wgb, wub, wdb, gsem, ssem, wsem, *, n_assign, layer):
    blk = pl.program_id(0)
    dummy_slot = SCATTER_SLOTS - 1

    def gather_row(idx_row, slot, r):
        return pltpu.make_async_copy(hn_hbm.at[pl.ds(idx_ref[0, idx_row, r], 1), :], xs.at[slot, pl.ds(r, 1), :],
                                     gsem.at[slot])

    def scatter_row(idx_row, slot, r):
        return pltpu.make_async_copy(ys.at[slot, pl.ds(r, 1), :], out_hbm.at[pl.ds(idx_ref[0, idx_row, r], 1), :],
                                     ssem.at[slot])

    def wait_gather(slot):
        pltpu.make_async_copy(hn_hbm.at[pl.ds(0, MOE_BLOCK), :], xs.at[slot], gsem.at[slot]).wait()

    def wait_scatter(slot):
        pltpu.make_async_copy(ys.at[slot], out_hbm.at[pl.ds(0, MOE_BLOCK), :], ssem.at[slot]).wait()

    @pl.when(blk == 0)
    def _():
        def body(r, carry):
            for s in range(GATHER_AHEAD):
                gather_row(s, s, r).start()
            return carry
        lax.fori_loop(0, MOE_BLOCK, body, 0)
        ys[dummy_slot] = jnp.zeros((MOE_BLOCK, HALF), jnp.uint32)
        for s in range(SCATTER_SLOTS):
            trash = pltpu.make_async_copy(
                ys.at[dummy_slot], out_hbm.at[pl.ds(n_assign + s * MOE_BLOCK, MOE_BLOCK), :], ssem.at[dummy_slot])
            trash.start()
            trash.wait()

    @pl.when(blk < n_real[0])
    def _():
        in_slot = blk % GATHER_SLOTS
        ahead_slot = (blk + GATHER_AHEAD) % GATHER_SLOTS
        out_slot = blk % SCATTER_SLOTS
        prev_slot = (blk + SCATTER_SLOTS - 1) % SCATTER_SLOTS

        copies = _expert_weight_copies(layer, (wg_hbm, wu_hbm, wd_hbm), (wg_stage, wu_stage, wd_stage), wsem)
        first_of_expert = _expert_changed(blk_e, blk)

        @pl.when(blk == 0)
        def _():
            for cp in copies(blk_e[0]):
                cp.start(priority=WEIGHT_DMA_PRIORITY)

        wait_gather(in_slot)

        @pl.when(blk >= SCATTER_SLOTS - 1)
        def _():
            wait_scatter(out_slot)

        def compute(fresh_weights):
            lo, hi = _unpack_halves(xs[in_slot])
            xb[:, :HALF] = lo.astype(BF16)
            xb[:, HALF:] = hi.astype(BF16)
            for r in range(MOE_BLOCK):
                scatter_row(IDX_DST_PREV, prev_slot, r).start(priority=r % 2)
                gather_row(GATHER_AHEAD, ahead_slot, r).start()
            cw = D_FF_EXPERT // MOE_CHUNKS
            for c in range(MOE_CHUNKS):
                cols = slice(c * cw, (c + 1) * cw)
                if fresh_weights:
                    wgb[:, cols] = wg_stage[:, cols].astype(BF16)
                    wub[:, cols] = wu_stage[:, cols].astype(BF16)
                a = jnp.dot(xb[...], wgb[:, cols], preferred_element_type=F32)
                u = jnp.dot(xb[...], wub[:, cols], preferred_element_type=F32)
                hid[:, cols] = (a * jax.nn.sigmoid(a) * u).astype(BF16)
            cw = HALF // MOE_CHUNKS
            for c in range(MOE_CHUNKS):
                cols = slice(c * cw, (c + 1) * cw)
                hi_cols = slice(HALF + c * cw, HALF + (c + 1) * cw)
                if fresh_weights:
                    wdb[:, cols] = wd_stage[:, cols].astype(BF16)
                    wdb[:, hi_cols] = wd_stage[:, hi_cols].astype(BF16)
                ys[out_slot, :, cols] = pltpu.pack_elementwise(
                    [jnp.dot(hid[...], wdb[:, cols], preferred_element_type=F32),
                     jnp.dot(hid[...], wdb[:, hi_cols], preferred_element_type=F32)], packed_dtype=BF16)

        @pl.when(first_of_expert)
        def _():
            for cp in copies(blk_e[blk]):
                cp.wait()
            compute(True)

            @pl.when(nxt_e[blk] >= 0)
            def _():
                for cp in copies(nxt_e[blk]):
                    cp.start(priority=WEIGHT_DMA_PRIORITY)

        @pl.when(jnp.logical_not(first_of_expert))
        def _():
            compute(False)

        @pl.when(blk == n_real[0] - 1)
        def _():
            def body(r, carry):
                scatter_row(IDX_DST, out_slot, r).start()
                return carry
            lax.fori_loop(0, MOE_BLOCK, body, 0)
            for s in range(1, GATHER_SLOTS):
                wait_gather((blk + s) % GATHER_SLOTS)
            for s in range(SCATTER_SLOTS):
                wait_scatter(s)


def _moe_forward(hn, order, counts, w_gate, w_up, w_down, layer, n_tokens):
    A = n_tokens * TOP_K
    out_rows = A + SCATTER_SLOTS * MOE_BLOCK
    R = A + N_EXPERTS * MOE_BLOCK
    n_blk = R // MOE_BLOCK
    i32 = jnp.int32
    n_tiles, per_tile = order.shape[0], ID_SPLIT * ID_SPLIT
    experts = jnp.arange(N_EXPERTS, dtype=i32)
    n = counts[:, 0, :N_EXPERTS]
    total = jnp.sum(n, axis=0)
    padded = ((total + MOE_BLOCK - 1) // MOE_BLOCK) * MOE_BLOCK
    pend = jnp.cumsum(padded)
    pstart = pend - padded
    n_real = (pend[-1] // MOE_BLOCK).astype(i32).reshape(1)
    blk = jnp.arange(n_blk, dtype=i32)
    blk_e = jnp.minimum(jnp.sum((pend[None, :] <= (blk * MOE_BLOCK)[:, None]).astype(i32), axis=1), N_EXPERTS - 1)
    is_e = (blk_e[:, None] == experts[None, :]).astype(i32)
    of_block = lambda per_expert: jnp.sum(is_e * per_expert[None, :], axis=1)
    cum_incl = jnp.cumsum(n, axis=0)
    run_shift = (jnp.cumsum(n, axis=1) - n) - (cum_incl - n)
    per_tile_of_block = lambda m: jnp.sum(is_e[:, None, :] * m[None, :, :], axis=2)
    cum_b, shift_b = per_tile_of_block(cum_incl), per_tile_of_block(run_shift)
    in_blk = jnp.arange(MOE_BLOCK, dtype=i32)
    k = (blk * MOE_BLOCK - of_block(pstart))[:, None] + in_blk[None, :]
    tile_of = jnp.minimum(jnp.sum((cum_b[:, None, :] <= k[:, :, None]).astype(i32), axis=2), n_tiles - 1)
    is_tile = (tile_of[:, :, None] == jnp.arange(n_tiles, dtype=i32)[None, None, :]).astype(i32)
    entry = tile_of * per_tile + k + jnp.sum(is_tile * shift_b[:, None, :], axis=2)
    pad_entry = A + (blk % SCATTER_SLOTS)[:, None] * MOE_BLOCK + in_blk[None, :]
    entry = jnp.where(k < of_block(total)[:, None], entry, pad_entry)
    n_pad = SCATTER_SLOTS * MOE_BLOCK
    dst_table = jnp.concatenate([order[:, :, :ID_SPLIT].reshape(A), A + jnp.arange(n_pad, dtype=i32)])
    row_dst = dst_table[entry]
    row_tok = row_dst % n_tokens
    row_dst_prev = jnp.concatenate([(A + (SCATTER_SLOTS - 1) * MOE_BLOCK + in_blk)[None, :], row_dst[:-1]], axis=0)
    last = n_real[0] - 1
    last_tok = lax.dynamic_slice_in_dim(row_tok, last, 1, axis=0)
    ahead = [jnp.where((blk + s <= last)[:, None],
                       jnp.concatenate([row_tok[s:], jnp.zeros((s, MOE_BLOCK), i32)], axis=0), last_tok)
             for s in range(GATHER_AHEAD + 1)]
    idx_rows = ahead + [row_dst, row_dst_prev]
    idx_rows += [jnp.zeros_like(row_dst)] * (IDX_ROWS - len(idx_rows))
    idx_all = jnp.stack(idx_rows, axis=1)

    run_end = pend[blk_e] // MOE_BLOCK
    nxt_e = jnp.where(run_end < n_real[0], blk_e[jnp.minimum(run_end, n_blk - 1)], -1).astype(jnp.int32)

    idx_spec = pl.BlockSpec((1, IDX_ROWS, MOE_BLOCK), lambda i, be, nr, nx: (jnp.minimum(i, nr[0] - 1), 0, 0),
                            memory_space=pltpu.SMEM)
    hbm = pl.BlockSpec(memory_space=pl.ANY)
    up_shape, down_shape = (D_MODEL, D_FF_EXPERT), (D_FF_EXPERT, D_MODEL)
    return pl.pallas_call(
        functools.partial(_experts_body, n_assign=A, layer=layer),
        grid_spec=pltpu.PrefetchScalarGridSpec(
            num_scalar_prefetch=3, grid=(n_blk,),
            in_specs=[idx_spec] + [hbm] * 4,
            out_specs=hbm,
            scratch_shapes=[pltpu.VMEM((GATHER_SLOTS, MOE_BLOCK, HALF), jnp.uint32),
                            pltpu.VMEM((MOE_BLOCK, D_MODEL), BF16),
                            pltpu.VMEM((MOE_BLOCK, D_FF_EXPERT), BF16),
                            pltpu.VMEM((SCATTER_SLOTS, MOE_BLOCK, HALF), jnp.uint32),
                            pltpu.VMEM(up_shape, F32), pltpu.VMEM(up_shape, F32), pltpu.VMEM(down_shape, F32),
                            pltpu.VMEM(up_shape, BF16), pltpu.VMEM(up_shape, BF16), pltpu.VMEM(down_shape, BF16),
                            pltpu.SemaphoreType.DMA((GATHER_SLOTS,)), pltpu.SemaphoreType.DMA((SCATTER_SLOTS,)),
                            pltpu.SemaphoreType.DMA((3,))]),
        out_shape=jax.ShapeDtypeStruct((out_rows, HALF), jnp.uint32),
        compiler_params=_params(("arbitrary",)), name="moe_experts",
    )(blk_e, n_real, nxt_e, idx_all, hn, w_gate, w_up, w_down)


FINAL_TM = 512


def _final_body(h_ref, y0_ref, y1_ref, gate_ref, g_ref, o_ref):
    gate = gate_ref[...]
    y0, y1 = _unpack_halves(y0_ref[...]), _unpack_halves(y1_ref[...])
    halves = [h_ref[:, i * HALF:(i + 1) * HALF] + gate[:, 0:1] * y0[i] + gate[:, 1:2] * y1[i] for i in range(2)]
    mean_sq = sum(jnp.sum(hh * hh, axis=-1, keepdims=True) for hh in halves) / D_MODEL
    inv = lax.rsqrt(mean_sq + RMS_EPS)
    for i, hh in enumerate(halves):
        o_ref[:, i * HALF:(i + 1) * HALF] = hh * inv * g_ref[:, i * HALF:(i + 1) * HALF]


def _final(h, ys, gate, gain):
    rows = h.shape[0]
    n_tiles = rows // FINAL_TM
    tile = lambda width: pl.BlockSpec((FINAL_TM, width), lambda i: (i, 0))
    g2 = gain.reshape(1, D_MODEL).astype(F32)
    return pl.pallas_call(
        _final_body, grid=(n_tiles,),
        in_specs=[tile(D_MODEL), tile(HALF), pl.BlockSpec((FINAL_TM, HALF), lambda i: (i + n_tiles, 0)),
                  tile(LANES), pl.BlockSpec((1, D_MODEL), lambda i: (0, 0))],
        out_specs=tile(D_MODEL), out_shape=jax.ShapeDtypeStruct((rows, D_MODEL), F32),
        compiler_params=_params(("parallel",)), name="combine_final_norm")(h, ys, ys, gate, g2)


def kernel(x, mem, norm_mix, w_in, attn_out_norm, sink_logit, w_gla_gf, b_gla_gf, w_gla_gb, b_gla_gb, gla_out_norm, w_out, norm_cross, norm_mem, w_cq, w_ck, w_cv, w_co, norm_ffn, w_router_group, b_router_group, w_router_expert, b_router_expert, w_gate, w_up, w_down, norm_final):
    batch, seq, _ = x.shape
    mem_len = mem.shape[1]
    n_tokens = batch * seq
    h = x.reshape(n_tokens, D_MODEL)
    memf = mem.reshape(batch * mem_len, D_MODEL)
    assert norm_mix.shape[0] == 1, "the combine step is fused with the final norm: single-layer stacks only"
    for l in range(norm_mix.shape[0]):
        w_lr = jnp.zeros((D_MODEL, LANES), F32).at[:, :2 * GLA_LOWRANK].set(w_in[l][:, MAIN_COLS:]).astype(BF16)
        proj, lr = _dense([h], w_in, l, n_cols=MAIN_COLS, out_dtype=BF16, gain=norm_mix[l], extra_w=w_lr,
                          name="in_proj")
        o_a = _window_attention(proj, sink_logit[l], attn_out_norm[l], batch, seq).reshape(n_tokens, ATT_Q)
        o_g = _gla(proj, lr, w_gla_gf[l], b_gla_gf[l], w_gla_gb[l], b_gla_gb[l], gla_out_norm[l],
                   batch, seq).reshape(n_tokens, GLA_V)
        h = _dense([o_a, o_g], w_out, l, n_cols=D_MODEL, out_dtype=F32, res=h, name="out_proj")
        kx, vx = _mem_kv(memf, norm_mem[l], w_ck, w_cv, l)
        router_w, router_b = _router_params(w_router_group[l], b_router_group[l],
                                            w_router_expert[l], b_router_expert[l])
        h, hn, gate, order, counts = _cross_block(h, norm_cross[l], w_cq, kx, vx, w_co, norm_ffn[l],
                                                  router_w, router_b, l, batch, seq, mem_len)
        ys = _moe_forward(hn, order, counts, w_gate, w_up, w_down, l, n_tokens)
    return _final(h, ys, gate, norm_final).reshape(batch, seq, D_MODEL)
```

```python
import functools

import jax
import jax.numpy as jnp
from jax import lax
from jax.experimental import pallas as pl
from jax.experimental.pallas import tpu as pltpu

F32 = jnp.float32
BF16 = jnp.bfloat16

D_MODEL = 2048
N_Q_HEADS = 8
N_KV_HEADS = 2
Q_PER_KV = N_Q_HEADS // N_KV_HEADS
HEAD_DIM = 128
WINDOW = 128
WBLK = 128
GLA_HEADS = 4
GLA_DK = 128
GLA_DV = 256
GLA_LOWRANK = 16
GLA_TAU = 16.0
GLA_CHUNK = 64
ATT_Q = N_Q_HEADS * HEAD_DIM
ATT_KV = N_KV_HEADS * HEAD_DIM
GLA_QK = GLA_HEADS * GLA_DK
GLA_V = GLA_HEADS * GLA_DV
MAIN_COLS = ATT_Q + 2 * ATT_KV + 2 * GLA_QK + 2 * GLA_V
COL_KA = ATT_Q
COL_VA = COL_KA + ATT_KV
COL_QG = COL_VA + ATT_KV
COL_KG = COL_QG + GLA_QK
COL_VG = COL_KG + GLA_QK
COL_RG = COL_VG + GLA_V
X_HEADS = 4
X_HEAD_DIM = D_MODEL // X_HEADS
N_GROUPS = 4
EXPERTS_PER_GROUP = 8
N_EXPERTS = N_GROUPS * EXPERTS_PER_GROUP
TOP_K = 2
D_FF_EXPERT = D_MODEL // 2
MOE_BLOCK = 256
RMS_EPS = 1e-6
NEG_INF = -1e30

LANES = 128
VMEM_LIMIT = 56 * 1024 * 1024
W_STAGE_ROWS = 128


def _params(sem):
    return pltpu.CompilerParams(dimension_semantics=sem, vmem_limit_bytes=VMEM_LIMIT)


def _nt(a, b):
    return lax.dot_general(a, b, (((1,), (1,)), ((), ())), preferred_element_type=F32)


def _tn(a, b):
    return lax.dot_general(a, b, (((0,), (0,)), ((), ())), preferred_element_type=F32)


def _rms(x, gain):
    return x * lax.rsqrt(jnp.mean(x * x, axis=-1, keepdims=True) + RMS_EPS) * gain


def _load_weight_bf16(w_hbm, layer, wb, stage, sem, k_rows, n_cols):
    n_chunks = k_rows // W_STAGE_ROWS

    def copy(c):
        return pltpu.make_async_copy(
            w_hbm.at[layer, pl.ds(c * W_STAGE_ROWS, W_STAGE_ROWS), pl.ds(0, n_cols)],
            stage.at[c % 2], sem.at[c % 2])

    copy(0).start()
    for c in range(n_chunks):
        if c + 1 < n_chunks:
            copy(c + 1).start()
        copy(c).wait()
        wb[c * W_STAGE_ROWS:(c + 1) * W_STAGE_ROWS, :] = stage[c % 2].astype(BF16)


def _dense_body(*refs, part_widths, has_norm, has_extra, has_res, n_cols, n_chunk, layer, lead_scale):
    it = iter(refs)
    x_refs = [next(it) for _ in part_widths]
    g_ref = next(it) if has_norm else None
    w_hbm = next(it)
    ew_ref = next(it) if has_extra else None
    res_ref = next(it) if has_res else None
    o_ref = next(it)
    eo_ref = next(it) if has_extra else None
    wb, stage, sem = next(it), next(it), next(it)
    u_ref = next(it) if has_norm else None
    k_rows = sum(part_widths)

    @pl.when(pl.program_id(0) == 0)
    def _():
        if layer is None:
            whole = pltpu.make_async_copy(w_hbm.at[:, pl.ds(0, n_cols)], wb, sem.at[0])
            whole.start()
            whole.wait()
        else:
            _load_weight_bf16(w_hbm, layer, wb, stage, sem, k_rows, n_cols)

    if has_norm:
        u_ref[...] = _rms(x_refs[0][...], g_ref[...]).astype(BF16)
        lhs = [(u_ref, 0, k_rows)]
    else:
        lhs, off = [], 0
        for r, kw in zip(x_refs, part_widths):
            lhs.append((r, off, kw))
            off += kw
    for n0 in range(0, n_cols, n_chunk):
        acc = None
        for r, off, kw in lhs:
            d = jnp.dot(r[...], wb[off:off + kw, n0:n0 + n_chunk], preferred_element_type=F32)
            acc = d if acc is None else acc + d
        if has_res:
            acc = acc + res_ref[:, n0:n0 + n_chunk]
        if lead_scale is not None and n0 < lead_scale[0]:
            acc = acc * lead_scale[1]
        o_ref[:, n0:n0 + n_chunk] = acc.astype(o_ref.dtype)
    if has_extra:
        eo_ref[...] = jnp.dot(u_ref[...], ew_ref[...], preferred_element_type=F32)


def _dense(xs, w, layer, *, n_cols, out_dtype, gain=None, extra_w=None, res=None, tm=512, n_chunk=512,
           lead_scale=None, name):
    assert lead_scale is None or lead_scale[0] % n_chunk == 0
    rows = xs[0].shape[0]
    part_widths = tuple(x.shape[1] for x in xs)
    k_rows = sum(part_widths)
    has_norm, has_extra, has_res = gain is not None, extra_w is not None, res is not None
    row_spec = lambda width: pl.BlockSpec((tm, width), lambda i: (i, 0))
    full_spec = lambda a: pl.BlockSpec(a.shape, lambda i: (0, 0))
    args, in_specs = list(xs), [row_spec(kw) for kw in part_widths]
    if has_norm:
        args.append(gain.reshape(1, k_rows).astype(F32))
        in_specs.append(full_spec(args[-1]))
    args.append(w)
    in_specs.append(pl.BlockSpec(memory_space=pl.ANY))
    if has_extra:
        args.append(extra_w)
        in_specs.append(full_spec(extra_w))
    if has_res:
        args.append(res)
        in_specs.append(row_spec(n_cols))
    out_shape = [jax.ShapeDtypeStruct((rows, n_cols), out_dtype)]
    out_specs = [row_spec(n_cols)]
    if has_extra:
        out_shape.append(jax.ShapeDtypeStruct((rows, extra_w.shape[1]), F32))
        out_specs.append(row_spec(extra_w.shape[1]))
    stage_cols = LANES if layer is None else n_cols
    scratch = [pltpu.VMEM((k_rows, n_cols), BF16),
               pltpu.VMEM((2, W_STAGE_ROWS, stage_cols), F32),
               pltpu.SemaphoreType.DMA((2,))]
    if has_norm:
        scratch.append(pltpu.VMEM((tm, k_rows), BF16))
    body = functools.partial(_dense_body, part_widths=part_widths, has_norm=has_norm, has_extra=has_extra,
                             has_res=has_res, n_cols=n_cols, n_chunk=n_chunk, layer=layer, lead_scale=lead_scale)
    outs = pl.pallas_call(
        body, grid=(rows // tm,), in_specs=in_specs, out_specs=out_specs, out_shape=out_shape,
        scratch_shapes=scratch, compiler_params=_params(("arbitrary",)), name=name)(*args)
    return outs if has_extra else outs[0]


ATT_TQ = 512
LOG2_E = 1.4426950408889634
ATT_Q_SCALE = LOG2_E * HEAD_DIM ** -0.5


def _winattn_body(sink_ref, q_ref, kp_ref, km_ref, kn_ref, vp_ref, vm_ref, vn_ref, g_ref, o_ref,
                  kcat, vcat, obuf, *, seq):
    s0 = pl.program_id(1) * ATT_TQ
    kcat[0:WBLK, :] = kp_ref[0]
    kcat[WBLK:WBLK + ATT_TQ, :] = km_ref[0]
    kcat[WBLK + ATT_TQ:, :] = kn_ref[0]
    vcat[0:WBLK, :] = vp_ref[0]
    vcat[WBLK:WBLK + ATT_TQ, :] = vm_ref[0]
    vcat[WBLK + ATT_TQ:, :] = vn_ref[0]
    qi = lax.broadcasted_iota(jnp.int32, (WBLK, 3 * WBLK), 0) + WBLK
    ki = lax.broadcasted_iota(jnp.int32, (WBLK, 3 * WBLK), 1)
    dist_i = jnp.abs(ki - qi)
    neg_dist = jnp.where(dist_i <= WINDOW, -dist_i.astype(F32), NEG_INF)
    for qb in range(ATT_TQ // WBLK):
        kabs = s0 + (qb - 1) * WBLK + ki
        bias_unit = jnp.where((kabs >= 0) & (kabs < seq), neg_dist, NEG_INF)
        head_cols = lambda j: slice(j * HEAD_DIM, (j + 1) * HEAD_DIM)
        kv_cols = lambda j: head_cols(j // Q_PER_KV)
        key_rows = slice(qb * WBLK, (qb + 3) * WBLK)
        scores = []
        for j in range(N_Q_HEADS):
            slope = 2.0 ** (-8.0 * (j + 1) / N_Q_HEADS)
            q = q_ref[0, qb * WBLK:(qb + 1) * WBLK, head_cols(j)]
            scores.append(_nt(q, kcat[key_rows, kv_cols(j)]) + (slope * LOG2_E) * bias_unit)
        probs, denoms = [], []
        for j in range(N_Q_HEADS):
            sink = sink_ref[j] * LOG2_E
            m = jnp.maximum(jnp.max(scores[j], axis=-1, keepdims=True), sink)
            p = jnp.exp2(scores[j] - m)
            denoms.append(jnp.sum(p, axis=-1, keepdims=True) + jnp.exp2(sink - m))
            probs.append(p.astype(BF16))
        for j in range(N_Q_HEADS):
            pv = jnp.dot(probs[j], vcat[key_rows, kv_cols(j)], preferred_element_type=F32)
            obuf[qb, :, head_cols(j)] = pv / denoms[j]
        o_ref[0, qb * WBLK:(qb + 1) * WBLK, :] = _rms(obuf[qb], g_ref[...]).astype(o_ref.dtype)


def _window_attention(proj, sink_logit, gain, batch, seq):
    p3 = proj.reshape(batch, seq, MAIN_COLS)
    nq = seq // ATT_TQ
    per = ATT_TQ // WBLK
    last = seq // WBLK - 1
    main = lambda col: pl.BlockSpec((1, ATT_TQ, ATT_KV), lambda b, i: (b, i, col))
    prev = lambda col: pl.BlockSpec((1, WBLK, ATT_KV), lambda b, i: (b, jnp.maximum(i * per - 1, 0), col))
    nxt = lambda col: pl.BlockSpec((1, WBLK, ATT_KV), lambda b, i: (b, jnp.minimum(i * per + per, last), col))
    ck, cv = COL_KA // ATT_KV, COL_VA // ATT_KV
    return pl.pallas_call(
        functools.partial(_winattn_body, seq=seq),
        grid=(batch, nq),
        in_specs=[pl.BlockSpec(memory_space=pltpu.SMEM),
                  pl.BlockSpec((1, ATT_TQ, ATT_Q), lambda b, i: (b, i, 0)),
                  prev(ck), main(ck), nxt(ck), prev(cv), main(cv), nxt(cv),
                  pl.BlockSpec((1, ATT_Q), lambda b, i: (0, 0))],
        out_specs=pl.BlockSpec((1, ATT_TQ, ATT_Q), lambda b, i: (b, i, 0)),
        out_shape=jax.ShapeDtypeStruct((batch, seq, ATT_Q), BF16),
        scratch_shapes=[pltpu.VMEM((ATT_TQ + 2 * WBLK, ATT_KV), BF16),
                        pltpu.VMEM((ATT_TQ + 2 * WBLK, ATT_KV), BF16),
                        pltpu.VMEM((ATT_TQ // WBLK, WBLK, ATT_Q), F32)],
        compiler_params=_params(("parallel", "parallel")), name="window_attention",
    )(sink_logit.astype(F32), p3, p3, p3, p3, p3, p3, p3, gain.reshape(1, ATT_Q).astype(F32))


GLA_SCAN_ROWS = 256
GLA_SCAN_UNROLL = 2
GLA_EPI_ROWS = 512
GLA_UNROLL = 8


def _split2(x):
    hi = x.astype(BF16)
    lo = (x - hi.astype(F32)).astype(BF16)
    return hi, lo


def _gla_body(q_ref, k_ref, v_ref, r_ref, lr_ref, wf_ref, wb_ref, bf_ref, bb_ref, gn_ref, o_ref,
              cum_f, cum_b, acc, st_f, st_b, *, seq):
    C = GLA_CHUNK
    n_chunks = seq // C
    ri = lax.broadcasted_iota(jnp.int32, (GLA_SCAN_ROWS, GLA_SCAN_ROWS), 0)
    ci = lax.broadcasted_iota(jnp.int32, (GLA_SCAN_ROWS, GLA_SCAN_ROWS), 1)
    same = (ri // C) == (ci // C)
    tri_f = jnp.where(same & (ci <= ri), 1.0, 0.0).astype(BF16)
    tri_b = jnp.where(same & (ci >= ri), 1.0, 0.0).astype(BF16)

    def scan_body(t, carry):
        row_sets = [pl.ds(pl.multiple_of((t * GLA_SCAN_UNROLL + u) * GLA_SCAN_ROWS, GLA_SCAN_ROWS), GLA_SCAN_ROWS)
                    for u in range(GLA_SCAN_UNROLL)]
        lrs = [lr_ref[0, rows, :].astype(BF16) for rows in row_sets]
        zs = [jnp.dot(lr, w_ref[...], preferred_element_type=F32) + b_ref[...]
              for lr in lrs for w_ref, b_ref in ((wf_ref, bf_ref), (wb_ref, bb_ref))]
        gs = [_split2((jnp.minimum(z, 0.0) - jnp.log1p(jnp.exp(-jnp.abs(z)))) / GLA_TAU) for z in zs]
        for n, g3 in enumerate(gs):
            tri, dst = ((tri_f, cum_f), (tri_b, cum_b))[n % 2]
            dst[row_sets[n // 2], :] = sum(jnp.dot(tri, part, preferred_element_type=F32) for part in g3)
        return carry

    lax.fori_loop(0, seq // (GLA_SCAN_ROWS * GLA_SCAN_UNROLL), scan_body, 0)

    acc[...] = jnp.zeros_like(acc)
    st_f[...] = jnp.zeros_like(st_f)
    st_b[...] = jnp.zeros_like(st_b)
    rr = lax.broadcasted_iota(jnp.int32, (C, C), 0)
    cc = lax.broadcasted_iota(jnp.int32, (C, C), 1)
    scale = GLA_DK ** -0.5

    def chunk_body(i, carry):
        jobs = []
        for u in range(GLA_UNROLL):
            c = i * GLA_UNROLL + u
            jobs += [(c, cum_f, True), (n_chunks - 1 - c, cum_b, False)]
        prep = []
        for c, cum, forward in jobs:
            rows = pl.ds(pl.multiple_of(c * C, C), C)
            b = cum[rows, :]
            b_end = b[C - 1:C, :] if forward else b[0:1, :]
            q = q_ref[0, rows, :].astype(F32) * scale
            k = k_ref[0, rows, :].astype(F32)
            v = v_ref[0, rows, :]
            q_dec = (q * jnp.exp(b)).astype(BF16)
            k_inc = (k * jnp.exp(-b)).astype(BF16)
            k_dec = (k * jnp.exp(b_end - b)).astype(BF16)
            prep.append((rows, v, q_dec, k_inc, k_dec, jnp.exp(b_end)))
        attn = [_nt(q_dec, k_inc) for _, _, q_dec, k_inc, _, _ in prep]
        kv_t = [_tn(v, k_dec) for _, v, _, _, k_dec, _ in prep]
        state_t = {True: st_f[...], False: st_b[...]}
        o_inter = []
        for (_, _, forward), (_, _, q_dec, _, _, decay), kv in zip(jobs, prep, kv_t):
            o_inter.append(_nt(q_dec, state_t[forward].astype(BF16)))
            state_t[forward] = state_t[forward] * decay + kv
        st_f[...] = state_t[True]
        st_b[...] = state_t[False]
        for (_, _, forward), (rows, v, _, _, _, _), a, oi in zip(jobs, prep, attn, o_inter):
            a = jnp.where((cc <= rr) if forward else (cc >= rr), a, 0.0)
            o = jnp.dot(a.astype(BF16), v, preferred_element_type=F32) + oi
            acc[rows, :] = acc[rows, :] + o
        return carry

    lax.fori_loop(0, n_chunks // GLA_UNROLL, chunk_body, 0)

    def epi_body(t, carry):
        rows = pl.ds(pl.multiple_of(t * GLA_EPI_ROWS, GLA_EPI_ROWS), GLA_EPI_ROWS)
        r = r_ref[0, rows, :].astype(F32)
        o_ref[0, rows, :] = (_rms(acc[rows, :], gn_ref[...]) * (r * jax.nn.sigmoid(r))).astype(o_ref.dtype)
        return carry

    lax.fori_loop(0, seq // GLA_EPI_ROWS, epi_body, 0)


def _gla(proj, lr, w_gf, b_gf, w_gb, b_gb, gain, batch, seq):
    p3 = proj.reshape(batch, seq, MAIN_COLS)
    lr3 = lr.reshape(batch, seq, LANES)
    wf = jnp.zeros((LANES, GLA_QK), F32).at[:GLA_LOWRANK].set(w_gf).astype(BF16)
    wb = jnp.zeros((LANES, GLA_QK), F32).at[GLA_LOWRANK:2 * GLA_LOWRANK].set(w_gb).astype(BF16)
    seq_blk = lambda width, col0: pl.BlockSpec((1, seq, width), lambda b, h: (b, 0, col0 // width + h))
    head_w = pl.BlockSpec((LANES, GLA_DK), lambda b, h: (0, h))
    head_b = pl.BlockSpec((1, GLA_DK), lambda b, h: (0, h))
    return pl.pallas_call(
        functools.partial(_gla_body, seq=seq),
        grid=(batch, GLA_HEADS),
        in_specs=[seq_blk(GLA_DK, COL_QG), seq_blk(GLA_DK, COL_KG), seq_blk(GLA_DV, COL_VG),
                  seq_blk(GLA_DV, COL_RG),
                  pl.BlockSpec((1, seq, LANES), lambda b, h: (b, 0, 0)),
                  head_w, head_w, head_b, head_b,
                  pl.BlockSpec((1, GLA_DV), lambda b, h: (0, 0))],
        out_specs=pl.BlockSpec((1, seq, GLA_DV), lambda b, h: (b, 0, h)),
        out_shape=jax.ShapeDtypeStruct((batch, seq, GLA_V), BF16),
        scratch_shapes=[pltpu.VMEM((seq, GLA_DK), F32), pltpu.VMEM((seq, GLA_DK), F32),
                        pltpu.VMEM((seq, GLA_DV), F32),
                        pltpu.VMEM((GLA_DV, GLA_DK), F32), pltpu.VMEM((GLA_DV, GLA_DK), F32)],
        compiler_params=_params(("parallel", "parallel")), name="gla",
    )(p3, p3, p3, p3, lr3, wf, wb, b_gf.reshape(1, GLA_QK).astype(F32), b_gb.reshape(1, GLA_QK).astype(F32),
      gain.reshape(1, GLA_DV).astype(F32))


MEM_TM = 512
MEM_K_CHUNK = 512


def _mem_kv_body(m_ref, g_ref, wk_ref, wv_ref, k_ref, v_ref, u, acc_k, acc_v):
    step = pl.program_id(1)
    n_chunks = D_MODEL // MEM_K_CHUNK

    @pl.when(step == 0)
    def _():
        normed = _rms(m_ref[...], g_ref[...]).astype(BF16)
        for c in range(n_chunks):
            u[c] = normed[:, c * MEM_K_CHUNK:(c + 1) * MEM_K_CHUNK]
        acc_k[...] = jnp.zeros_like(acc_k)
        acc_v[...] = jnp.zeros_like(acc_v)

    lhs = u[step]
    acc_k[...] = acc_k[...] + jnp.dot(lhs, wk_ref[...].astype(BF16), preferred_element_type=F32)
    acc_v[...] = acc_v[...] + jnp.dot(lhs, wv_ref[...].astype(BF16), preferred_element_type=F32)

    @pl.when(step == n_chunks - 1)
    def _():
        k_ref[...] = acc_k[...].astype(k_ref.dtype)
        v_ref[...] = acc_v[...].astype(v_ref.dtype)


def _mem_kv(mem_rows, gain, w_ck, w_cv, layer):
    rows = mem_rows.shape[0]
    n_chunks = D_MODEL // MEM_K_CHUNK
    tile = pl.BlockSpec((MEM_TM, D_MODEL), lambda i, c: (i, 0))
    w_spec = pl.BlockSpec((None, MEM_K_CHUNK, D_MODEL), lambda i, c: (layer, c, 0))
    g2 = gain.reshape(1, D_MODEL).astype(F32)
    out = jax.ShapeDtypeStruct((rows, D_MODEL), BF16)
    return pl.pallas_call(
        _mem_kv_body, grid=(rows // MEM_TM, n_chunks),
        in_specs=[tile, pl.BlockSpec((1, D_MODEL), lambda i, c: (0, 0)), w_spec, w_spec],
        out_specs=[tile, tile], out_shape=[out, out],
        scratch_shapes=[pltpu.VMEM((n_chunks, MEM_TM, MEM_K_CHUNK), BF16),
                        pltpu.VMEM((MEM_TM, D_MODEL), F32), pltpu.VMEM((MEM_TM, D_MODEL), F32)],
        compiler_params=_params(("parallel", "arbitrary")), name="mem_kv_proj")(mem_rows, g2, w_ck, w_cv)


XATT_TQ = 512


XATT_CHUNK = 512


def _cross_body(h_ref, g_ref, wq_hbm, k_ref, v_ref, wo_hbm, gr_ref, wr_ref, br_ref,
                o_ref, hn_ref, gate_ref, order_ref, count_ref,
                wqb, wob, stage, sem, u_scr, q_scr, a_scr, prev, *, layer, n_tokens):
    step = pl.program_id(0)

    @pl.when(step == 0)
    def _():
        _load_weight_bf16(wq_hbm, layer, wqb, stage, sem, D_MODEL, D_MODEL)
        _load_weight_bf16(wo_hbm, layer, wob, stage, sem, D_MODEL, D_MODEL)
        prev[...] = jnp.zeros_like(prev)

    router = _route_tile(prev[...], gr_ref[...], wr_ref, br_ref, hn_ref, gate_ref, order_ref, count_ref,
                         jnp.maximum(step - 1, 0), n_tokens)

    scale = LOG2_E * X_HEAD_DIM ** -0.5
    heads = [slice(h * X_HEAD_DIM, (h + 1) * X_HEAD_DIM) for h in range(X_HEADS)]
    chunks = [slice(n0, n0 + XATT_CHUNK) for n0 in range(0, D_MODEL, XATT_CHUNK)]

    def norm(rows):
        u_scr[rows, :] = _rms(h_ref[rows, :], g_ref[...]).astype(BF16)

    def q_proj(rows):
        for cols in chunks:
            q = jnp.dot(u_scr[rows, :], wqb[:, cols], preferred_element_type=F32)
            q_scr[rows, cols] = (q * scale).astype(BF16)

    def scores(rows):
        return [_nt(q_scr[rows, hd], k_ref[0, :, hd]) for hd in heads]

    def softmax(s_list):
        out = []
        for s in s_list:
            p = jnp.exp2(s - jnp.max(s, axis=-1, keepdims=True))
            out.append((p.astype(BF16), jnp.sum(p, axis=-1, keepdims=True)))
        return out

    def attend(rows, probs):
        for hd, (p, denom) in zip(heads, probs):
            a_scr[rows, hd] = (jnp.dot(p, v_ref[0, :, hd], preferred_element_type=F32) / denom).astype(BF16)

    def o_proj(rows):
        for cols in chunks:
            out = h_ref[rows, cols] + jnp.dot(a_scr[rows, :], wob[:, cols], preferred_element_type=F32)
            o_ref[rows, cols] = out
            prev[rows, cols] = out

    half = XATT_TQ // 2
    first, second = slice(0, half), slice(half, XATT_TQ)
    norm(first)
    q_proj(first)
    next(router)
    norm(second)
    s_first = scores(first)
    q_proj(second)
    next(router)
    p_first = softmax(s_first)
    attend(first, p_first)
    s_second = scores(second)
    o_proj(first)
    for _ in router:
        pass
    p_second = softmax(s_second)
    attend(second, p_second)
    o_proj(second)


def _cross_block(h, gain, w_cq, kx, vx, w_co, router_gain, router_w, router_b, layer, batch, seq, mem_len):
    rows = h.shape[0]
    n_tiles = rows // XATT_TQ
    tiles_per_batch = seq // XATT_TQ
    k3 = kx.reshape(batch, mem_len, D_MODEL)
    v3 = vx.reshape(batch, mem_len, D_MODEL)
    as_row = lambda g: g.reshape(1, D_MODEL).astype(F32)
    cur = lambda i: jnp.minimum(i, n_tiles - 1)
    lag = lambda i: jnp.maximum(i - 1, 0)
    tile = pl.BlockSpec((XATT_TQ, D_MODEL), lambda i: (cur(i), 0))
    mem_spec = pl.BlockSpec((1, mem_len, D_MODEL), lambda i: (cur(i) // tiles_per_batch, 0, 0))
    full = lambda a: pl.BlockSpec(a.shape, lambda i: (0, 0))
    lag_rows = lambda width: pl.BlockSpec((XATT_TQ, width), lambda i: (lag(i), 0))
    lag_tile = lambda sublanes: pl.BlockSpec((1, sublanes, LANES), lambda i: (lag(i), 0, 0))
    assert TOP_K * XATT_TQ == ID_SPLIT * ID_SPLIT, "the router's sort grid holds one tile's assignments"
    hbm = pl.BlockSpec(memory_space=pl.ANY)
    g_cross, g_router = as_row(gain), as_row(router_gain)
    return pl.pallas_call(
        functools.partial(_cross_body, layer=layer, n_tokens=rows), grid=(n_tiles + 1,),
        in_specs=[tile, full(g_cross), hbm, mem_spec, mem_spec, hbm, full(g_router), full(router_w), full(router_b)],
        out_specs=[tile, lag_rows(HALF), lag_rows(LANES), lag_tile(ID_SPLIT), lag_tile(8)],
        out_shape=[jax.ShapeDtypeStruct((rows, D_MODEL), F32),
                   jax.ShapeDtypeStruct((rows, HALF), jnp.uint32),
                   jax.ShapeDtypeStruct((rows, LANES), F32),
                   jax.ShapeDtypeStruct((n_tiles, ID_SPLIT, LANES), jnp.int32),
                   jax.ShapeDtypeStruct((n_tiles, 8, LANES), jnp.int32)],
        scratch_shapes=[pltpu.VMEM((D_MODEL, D_MODEL), BF16), pltpu.VMEM((D_MODEL, D_MODEL), BF16),
                        pltpu.VMEM((2, W_STAGE_ROWS, D_MODEL), F32), pltpu.SemaphoreType.DMA((2,)),
                        pltpu.VMEM((XATT_TQ, D_MODEL), BF16), pltpu.VMEM((XATT_TQ, D_MODEL), BF16),
                        pltpu.VMEM((XATT_TQ, D_MODEL), BF16), pltpu.VMEM((XATT_TQ, D_MODEL), F32)],
        compiler_params=_params(("arbitrary",)), name="cross_block",
    )(h, g_cross, w_cq, k3, v3, w_co, g_router, router_w, router_b)


HALF = D_MODEL // 2


def _pack_halves(x):
    return pltpu.pack_elementwise([x[:, :HALF], x[:, HALF:]], packed_dtype=BF16)


def _unpack_halves(words):
    return [pltpu.unpack_elementwise(words, index=i, packed_dtype=BF16, unpacked_dtype=F32) for i in range(2)]


ID_SPLIT = 32


def _route_tile(h, gain, w_ref, b_ref, hn_ref, gate_ref, order_ref, count_ref, tile_idx, n_tokens):
    hn = _rms(h, gain)
    hn_ref[...] = _pack_halves(hn)
    logits = jnp.dot(hn.astype(BF16), w_ref[...], preferred_element_type=F32) + b_ref[...]
    yield
    lane = lax.broadcasted_iota(jnp.int32, logits.shape, 1).astype(F32)
    ninf = -jnp.inf
    first = lambda hit: jnp.min(jnp.where(hit, lane, float(LANES)), axis=-1, keepdims=True)
    in_groups = lane < N_GROUPS
    gl = jnp.where(in_groups, logits, ninf)
    gmax = jnp.max(gl, axis=-1, keepdims=True)
    g_idx = first(gl == gmax)
    p_group = 1.0 / jnp.sum(jnp.where(in_groups, jnp.exp(logits - gmax), 0.0), axis=-1, keepdims=True)
    lo = N_GROUPS + EXPERTS_PER_GROUP * g_idx
    el = jnp.where((lane >= lo) & (lane < lo + EXPERTS_PER_GROUP), logits, ninf)
    e1 = jnp.max(el, axis=-1, keepdims=True)
    i1 = first(el == e1)
    el2 = jnp.where(lane == i1, ninf, el)
    e2 = jnp.max(el2, axis=-1, keepdims=True)
    i2 = first(el2 == e2)
    t = jnp.exp(e2 - e1)
    w1 = p_group / (1.0 + t)
    w2 = p_group * t / (1.0 + t)
    gate_ref[...] = jnp.where(lane == 0, w1, jnp.where(lane == 1, w2, 0.0))

    tm = logits.shape[0]
    hit0, hit1 = lane == i1 - N_GROUPS, lane == i2 - N_GROUPS
    member = jnp.where(hit0 | hit1, 1.0, 0.0)
    member_b = member.astype(BF16)
    earlier = (lax.broadcasted_iota(jnp.int32, (tm, tm), 1) < lax.broadcasted_iota(jnp.int32, (tm, tm), 0))
    rank = jnp.dot(jnp.where(earlier, 1.0, 0.0).astype(BF16), member_b, preferred_element_type=F32)
    lower = (lax.broadcasted_iota(jnp.int32, (LANES, LANES), 0) < lax.broadcasted_iota(jnp.int32, (LANES, LANES), 1))
    run_start = jnp.sum(jnp.dot(member_b, jnp.where(lower, 1.0, 0.0).astype(BF16), preferred_element_type=F32),
                        axis=0, keepdims=True)
    yield
    pos = rank + run_start
    positions = [jnp.sum(jnp.where(hit, pos, 0.0), axis=-1, keepdims=True) for hit in (hit0, hit1)]
    local_tok = lax.broadcasted_iota(jnp.int32, (tm, 1), 0)
    grids = [None, None]
    for s, p in enumerate(positions):
        p_hi = jnp.floor(p / ID_SPLIT)
        at_hi = jnp.where(lane == p_hi, 1.0, 0.0)
        at_lo = jnp.where(lane == p - ID_SPLIT * p_hi, 1.0, 0.0).astype(BF16)
        a = s * tm + local_tok
        for d, digit in enumerate((a // ID_SPLIT, a % ID_SPLIT)):
            part = _tn((at_hi * digit.astype(F32)).astype(BF16), at_lo)
            grids[d] = part if grids[d] is None else grids[d] + part
    a_sorted = (grids[0] * ID_SPLIT + grids[1]).astype(jnp.int32)
    dst = (a_sorted // tm) * n_tokens + tile_idx * tm + a_sorted % tm
    order_ref[0] = dst[:ID_SPLIT, :]
    count_ref[0] =jnp.broadcast_to(jnp.sum(member, axis=0, keepdims=True), (8, LANES)).astype(jnp.int32)


def _router_params(w_rg, b_rg, w_re, b_re):
    n_log = N_GROUPS + N_EXPERTS
    w = jnp.zeros((D_MODEL, LANES), F32).at[:, :N_GROUPS].set(w_rg).at[:, N_GROUPS:n_log].set(w_re).astype(BF16)
    b = jnp.zeros((1, LANES), F32).at[0, :N_GROUPS].set(b_rg).at[0, N_GROUPS:n_log].set(b_re)
    return w, b


def _expert_changed(blk_e, blk):
    prev = blk_e[jnp.maximum(blk - 1, 0)]
    return (blk == 0) | (blk_e[blk] != prev)


WEIGHT_DMA_PRIORITY = 1


def _expert_weight_copies(layer, w_hbms, stages, sem):
    def copies(e):
        return [pltpu.make_async_copy(w.at[layer, e], st, sem.at[i])
                for i, (w, st) in enumerate(zip(w_hbms, stages))]
    return copies


GATHER_AHEAD = 3
GATHER_SLOTS = GATHER_AHEAD + 1
MOE_CHUNKS = 2


SCATTER_SLOTS = 3


IDX_ROWS = 8
IDX_DST = GATHER_AHEAD + 1
IDX_DST_PREV = GATHER_AHEAD + 2


def _experts_body(blk_e, n_real, nxt_e, idx_ref, hn_hbm, wg_hbm, wu_hbm, wd_hbm, out_hbm, xs, xb, hid, ys,
                  wg_stage, wu_stage, wd_stage, wgb, wub, wdb, gsem, ssem, wsem, *, n_assign, layer):
    blk = pl.program_id(0)
    dummy_slot = SCATTER_SLOTS - 1

    def gather_row(idx_row, slot, r):
        return pltpu.make_async_copy(hn_hbm.at[pl.ds(idx_ref[0, idx_row, r], 1), :], xs.at[slot, pl.ds(r, 1), :],
                                     gsem.at[slot])

    def scatter_row(idx_row, slot, r):
        return pltpu.make_async_copy(ys.at[slot, pl.ds(r, 1), :], out_hbm.at[pl.ds(idx_ref[0, idx_row, r], 1), :],
                                     ssem.at[slot])

    def wait_gather(slot):
        pltpu.make_async_copy(hn_hbm.at[pl.ds(0, MOE_BLOCK), :], xs.at[slot], gsem.at[slot]).wait()

    def wait_scatter(slot):
        pltpu.make_async_copy(ys.at[slot], out_hbm.at[pl.ds(0, MOE_BLOCK), :], ssem.at[slot]).wait()

    @pl.when(blk == 0)
    def _():
        def body(r, carry):
            for s in range(GATHER_AHEAD):
                gather_row(s, s, r).start()
            return carry
        lax.fori_loop(0, MOE_BLOCK, body, 0)
        ys[dummy_slot] = jnp.zeros((MOE_BLOCK, HALF), jnp.uint32)
        for s in range(SCATTER_SLOTS):
            trash = pltpu.make_async_copy(
                ys.at[dummy_slot], out_hbm.at[pl.ds(n_assign + s * MOE_BLOCK, MOE_BLOCK), :], ssem.at[dummy_slot])
            trash.start()
            trash.wait()

    @pl.when(blk < n_real[0])
    def _():
        in_slot = blk % GATHER_SLOTS
        ahead_slot = (blk + GATHER_AHEAD) % GATHER_SLOTS
        out_slot = blk % SCATTER_SLOTS
        prev_slot = (blk + SCATTER_SLOTS - 1) % SCATTER_SLOTS

        copies = _expert_weight_copies(layer, (wg_hbm, wu_hbm, wd_hbm), (wg_stage, wu_stage, wd_stage), wsem)
        first_of_expert = _expert_changed(blk_e, blk)

        @pl.when(blk == 0)
        def _():
            for cp in copies(blk_e[0]):
                cp.start(priority=WEIGHT_DMA_PRIORITY)

        wait_gather(in_slot)

        @pl.when(blk >= SCATTER_SLOTS - 1)
        def _():
            wait_scatter(out_slot)

        def compute(fresh_weights):
            lo, hi = _unpack_halves(xs[in_slot])
            xb[:, :HALF] = lo.astype(BF16)
            xb[:, HALF:] = hi.astype(BF16)
            for r in range(MOE_BLOCK):
                scatter_row(IDX_DST_PREV, prev_slot, r).start(priority=r % 2)
                gather_row(GATHER_AHEAD, ahead_slot, r).start()
            cw = D_FF_EXPERT // MOE_CHUNKS
            for c in range(MOE_CHUNKS):
                cols = slice(c * cw, (c + 1) * cw)
                if fresh_weights:
                    wgb[:, cols] = wg_stage[:, cols].astype(BF16)
                    wub[:, cols] = wu_stage[:, cols].astype(BF16)
                a = jnp.dot(xb[...], wgb[:, cols], preferred_element_type=F32)
                u = jnp.dot(xb[...], wub[:, cols], preferred_element_type=F32)
                hid[:, cols] = (a * jax.nn.sigmoid(a) * u).astype(BF16)
            cw = HALF // MOE_CHUNKS
            for c in range(MOE_CHUNKS):
                cols = slice(c * cw, (c + 1) * cw)
                hi_cols = slice(HALF + c * cw, HALF + (c + 1) * cw)
                if fresh_weights:
                    wdb[:, cols] = wd_stage[:, cols].astype(BF16)
                    wdb[:, hi_cols] = wd_stage[:, hi_cols].astype(BF16)
                ys[out_slot, :, cols] = pltpu.pack_elementwise(
                    [jnp.dot(hid[...], wdb[:, cols], preferred_element_type=F32),
                     jnp.dot(hid[...], wdb[:, hi_cols], preferred_element_type=F32)], packed_dtype=BF16)

        @pl.when(first_of_expert)
        def _():
            for cp in copies(blk_e[blk]):
                cp.wait()
            compute(True)

            @pl.when(nxt_e[blk] >= 0)
            def _():
                for cp in copies(nxt_e[blk]):
                    cp.start(priority=WEIGHT_DMA_PRIORITY)

        @pl.when(jnp.logical_not(first_of_expert))
        def _():
            compute(False)

        @pl.when(blk == n_real[0] - 1)
        def _():
            def body(r, carry):
                scatter_row(IDX_DST, out_slot, r).start()
                return carry
            lax.fori_loop(0, MOE_BLOCK, body, 0)
            for s in range(1, GATHER_SLOTS):
                wait_gather((blk + s) % GATHER_SLOTS)
            for s in range(SCATTER_SLOTS):
                wait_scatter(s)


def _moe_forward(hn, order, counts, w_gate, w_up, w_down, layer, n_tokens):
    A = n_tokens * TOP_K
    out_rows = A + SCATTER_SLOTS * MOE_BLOCK
    R = A + N_EXPERTS * MOE_BLOCK
    n_blk = R // MOE_BLOCK
    i32 = jnp.int32
    n_tiles, per_tile = order.shape[0], ID_SPLIT * ID_SPLIT
    experts = jnp.arange(N_EXPERTS, dtype=i32)
    n = counts[:, 0, :N_EXPERTS]
    total = jnp.sum(n, axis=0)
    padded = ((total + MOE_BLOCK - 1) // MOE_BLOCK) * MOE_BLOCK
    pend = jnp.cumsum(padded)
    pstart = pend - padded
    n_real = (pend[-1] // MOE_BLOCK).astype(i32).reshape(1)
    blk = jnp.arange(n_blk, dtype=i32)
    blk_e = jnp.minimum(jnp.sum((pend[None, :] <= (blk * MOE_BLOCK)[:, None]).astype(i32), axis=1), N_EXPERTS - 1)
    is_e = (blk_e[:, None] == experts[None, :]).astype(i32)
    of_block = lambda per_expert: jnp.sum(is_e * per_expert[None, :], axis=1)
    cum_incl = jnp.cumsum(n, axis=0)
    run_shift = (jnp.cumsum(n, axis=1) - n) - (cum_incl - n)
    per_tile_of_block = lambda m: jnp.sum(is_e[:, None, :] * m[None, :, :], axis=2)
    cum_b, shift_b = per_tile_of_block(cum_incl), per_tile_of_block(run_shift)
    in_blk = jnp.arange(MOE_BLOCK, dtype=i32)
    k = (blk * MOE_BLOCK - of_block(pstart))[:, None] + in_blk[None, :]
    tile_of = jnp.minimum(jnp.sum((cum_b[:, None, :] <= k[:, :, None]).astype(i32), axis=2), n_tiles - 1)
    is_tile = (tile_of[:, :, None] == jnp.arange(n_tiles, dtype=i32)[None, None, :]).astype(i32)
    entry = tile_of * per_tile + k + jnp.sum(is_tile * shift_b[:, None, :], axis=2)
    pad_entry = A + (blk % SCATTER_SLOTS)[:, None] * MOE_BLOCK + in_blk[None, :]
    entry = jnp.where(k < of_block(total)[:, None], entry, pad_entry)
    n_pad = SCATTER_SLOTS * MOE_BLOCK
    dst_table = jnp.concatenate([order[:, :, :ID_SPLIT].reshape(A), A + jnp.arange(n_pad, dtype=i32)])
    row_dst = dst_table[entry]
    row_tok = row_dst % n_tokens
    row_dst_prev = jnp.concatenate([(A + (SCATTER_SLOTS - 1) * MOE_BLOCK + in_blk)[None, :], row_dst[:-1]], axis=0)
    last = n_real[0] - 1
    last_tok = lax.dynamic_slice_in_dim(row_tok, last, 1, axis=0)
    ahead = [jnp.where((blk + s <= last)[:, None],
                       jnp.concatenate([row_tok[s:], jnp.zeros((s, MOE_BLOCK), i32)], axis=0), last_tok)
             for s in range(GATHER_AHEAD + 1)]
    idx_rows = ahead + [row_dst, row_dst_prev]
    idx_rows += [jnp.zeros_like(row_dst)] * (IDX_ROWS - len(idx_rows))
    idx_all = jnp.stack(idx_rows, axis=1)

    run_end = pend[blk_e] // MOE_BLOCK
    nxt_e = jnp.where(run_end < n_real[0], blk_e[jnp.minimum(run_end, n_blk - 1)], -1).astype(jnp.int32)

    idx_spec = pl.BlockSpec((1, IDX_ROWS, MOE_BLOCK), lambda i, be, nr, nx: (jnp.minimum(i, nr[0] - 1), 0, 0),
                            memory_space=pltpu.SMEM)
    hbm = pl.BlockSpec(memory_space=pl.ANY)
    up_shape, down_shape = (D_MODEL, D_FF_EXPERT), (D_FF_EXPERT, D_MODEL)
    return pl.pallas_call(
        functools.partial(_experts_body, n_assign=A, layer=layer),
        grid_spec=pltpu.PrefetchScalarGridSpec(
            num_scalar_prefetch=3, grid=(n_blk,),
            in_specs=[idx_spec] + [hbm] * 4,
            out_specs=hbm,
            scratch_shapes=[pltpu.VMEM((GATHER_SLOTS, MOE_BLOCK, HALF), jnp.uint32),
                            pltpu.VMEM((MOE_BLOCK, D_MODEL), BF16),
                            pltpu.VMEM((MOE_BLOCK, D_FF_EXPERT), BF16),
                            pltpu.VMEM((SCATTER_SLOTS, MOE_BLOCK, HALF), jnp.uint32),
                            pltpu.VMEM(up_shape, F32), pltpu.VMEM(up_shape, F32), pltpu.VMEM(down_shape, F32),
                            pltpu.VMEM(up_shape, BF16), pltpu.VMEM(up_shape, BF16), pltpu.VMEM(down_shape, BF16),
                            pltpu.SemaphoreType.DMA((GATHER_SLOTS,)), pltpu.SemaphoreType.DMA((SCATTER_SLOTS,)),
                            pltpu.SemaphoreType.DMA((3,))]),
        out_shape=jax.ShapeDtypeStruct((out_rows, HALF), jnp.uint32),
        compiler_params=_params(("arbitrary",)), name="moe_experts",
    )(blk_e, n_real, nxt_e, idx_all, hn, w_gate, w_up, w_down)


FINAL_TM = 512


def _final_body(h_ref, y0_ref, y1_ref, gate_ref, g_ref, o_ref):
    gate = gate_ref[...]
    y0, y1 = _unpack_halves(y0_ref[...]), _unpack_halves(y1_ref[...])
    halves = [h_ref[:, i * HALF:(i + 1) * HALF] + gate[:, 0:1] * y0[i] + gate[:, 1:2] * y1[i] for i in range(2)]
    mean_sq = sum(jnp.sum(hh * hh, axis=-1, keepdims=True) for hh in halves) / D_MODEL
    inv = lax.rsqrt(mean_sq + RMS_EPS)
    for i, hh in enumerate(halves):
        o_ref[:, i * HALF:(i + 1) * HALF] = hh * inv * g_ref[:, i * HALF:(i + 1) * HALF]


def _final(h, ys, gate, gain):
    rows = h.shape[0]
    n_tiles = rows // FINAL_TM
    tile = lambda width: pl.BlockSpec((FINAL_TM, width), lambda i: (i, 0))
    g2 = gain.reshape(1, D_MODEL).astype(F32)
    return pl.pallas_call(
        _final_body, grid=(n_tiles,),
        in_specs=[tile(D_MODEL), tile(HALF), pl.BlockSpec((FINAL_TM, HALF), lambda i: (i + n_tiles, 0)),
                  tile(LANES), pl.BlockSpec((1, D_MODEL), lambda i: (0, 0))],
        out_specs=tile(D_MODEL), out_shape=jax.ShapeDtypeStruct((rows, D_MODEL), F32),
        compiler_params=_params(("parallel",)), name="combine_final_norm")(h, ys, ys, gate, g2)


def kernel(x, mem, norm_mix, w_in, attn_out_norm, sink_logit, w_gla_gf, b_gla_gf, w_gla_gb, b_gla_gb, gla_out_norm, w_out, norm_cross, norm_mem, w_cq, w_ck, w_cv, w_co, norm_ffn, w_router_group, b_router_group, w_router_expert, b_router_expert, w_gate, w_up, w_down, norm_final):
    batch, seq, _ = x.shape
    mem_len = mem.shape[1]
    n_tokens = batch * seq
    h = x.reshape(n_tokens, D_MODEL)
    memf = mem.reshape(batch * mem_len, D_MODEL)
    assert norm_mix.shape[0] == 1, "the combine step is fused with the final norm: single-layer stacks only"
    for l in range(norm_mix.shape[0]):
        w_lr = jnp.zeros((D_MODEL, LANES), F32).at[:, :2 * GLA_LOWRANK].set(w_in[l][:, MAIN_COLS:]).astype(BF16)
        w_main = w_in[l].astype(BF16)
        proj, lr = _dense([h], w_main, None, n_cols=MAIN_COLS, out_dtype=BF16, gain=norm_mix[l], extra_w=w_lr,
                          lead_scale=(ATT_Q, ATT_Q_SCALE), name="in_proj")
        o_a = _window_attention(proj, sink_logit[l], attn_out_norm[l], batch, seq).reshape(n_tokens, ATT_Q)
        o_g = _gla(proj, lr, w_gla_gf[l], b_gla_gf[l], w_gla_gb[l], b_gla_gb[l], gla_out_norm[l],
                   batch, seq).reshape(n_tokens, GLA_V)
        h = _dense([o_a, o_g], w_out, l, n_cols=D_MODEL, out_dtype=F32, res=h, name="out_proj")
        kx, vx = _mem_kv(memf, norm_mem[l], w_ck, w_cv, l)
        router_w, router_b = _router_params(w_router_group[l], b_router_group[l],
                                            w_router_expert[l], b_router_expert[l])
        h, hn, gate, order, counts = _cross_block(h, norm_cross[l], w_cq, kx, vx, w_co, norm_ffn[l],
                                                  router_w, router_b, l, batch, seq, mem_len)
        ys = _moe_forward(hn, order, counts, w_gate, w_up, w_down, l, n_tokens)
    return _final(h, ys, gate, norm_final).reshape(batch, seq, D_MODEL)
```

```python
import functools

import jax
import jax.numpy as jnp
from jax import lax
from jax.experimental import pallas as pl
from jax.experimental.pallas import tpu as pltpu

F32 = jnp.float32
BF16 = jnp.bfloat16

D_MODEL = 2048
N_Q_HEADS = 8
N_KV_HEADS = 2
Q_PER_KV = N_Q_HEADS // N_KV_HEADS
HEAD_DIM = 128
WINDOW = 128
WBLK = 128
GLA_HEADS = 4
GLA_DK = 128
GLA_DV = 256
GLA_LOWRANK = 16
GLA_TAU = 16.0
GLA_CHUNK = 64
ATT_Q = N_Q_HEADS * HEAD_DIM
ATT_KV = N_KV_HEADS * HEAD_DIM
GLA_QK = GLA_HEADS * GLA_DK
GLA_V = GLA_HEADS * GLA_DV
MAIN_COLS = ATT_Q + 2 * ATT_KV + 2 * GLA_QK + 2 * GLA_V
COL_KA = ATT_Q
COL_VA = COL_KA + ATT_KV
COL_QG = COL_VA + ATT_KV
COL_KG = COL_QG + GLA_QK
COL_VG = COL_KG + GLA_QK
COL_RG = COL_VG + GLA_V
X_HEADS = 4
X_HEAD_DIM = D_MODEL // X_HEADS
N_GROUPS = 4
EXPERTS_PER_GROUP = 8
N_EXPERTS = N_GROUPS * EXPERTS_PER_GROUP
TOP_K = 2
D_FF_EXPERT = D_MODEL // 2
MOE_BLOCK = 256
RMS_EPS = 1e-6
NEG_INF = -1e30

LANES = 128
VMEM_LIMIT = 56 * 1024 * 1024
W_STAGE_ROWS = 128


def _params(sem):
    return pltpu.CompilerParams(dimension_semantics=sem, vmem_limit_bytes=VMEM_LIMIT)


def _nt(a, b):
    return lax.dot_general(a, b, (((1,), (1,)), ((), ())), preferred_element_type=F32)


def _tn(a, b):
    return lax.dot_general(a, b, (((0,), (0,)), ((), ())), preferred_element_type=F32)


def _rms(x, gain):
    return x * lax.rsqrt(jnp.mean(x * x, axis=-1, keepdims=True) + RMS_EPS) * gain


def _load_weight_bf16(w_hbm, layer, wb, stage, sem, k_rows, n_cols):
    n_chunks = k_rows // W_STAGE_ROWS

    def copy(c):
        return pltpu.make_async_copy(
            w_hbm.at[layer, pl.ds(c * W_STAGE_ROWS, W_STAGE_ROWS), pl.ds(0, n_cols)],
            stage.at[c % 2], sem.at[c % 2])

    copy(0).start()
    for c in range(n_chunks):
        if c + 1 < n_chunks:
            copy(c + 1).start()
        copy(c).wait()
        wb[c * W_STAGE_ROWS:(c + 1) * W_STAGE_ROWS, :] = stage[c % 2].astype(BF16)


def _dense_body(*refs, part_widths, has_norm, has_extra, has_res, n_cols, n_chunk, layer, lead_scale):
    it = iter(refs)
    x_refs = [next(it) for _ in part_widths]
    g_ref = next(it) if has_norm else None
    w_hbm = next(it)
    ew_ref = next(it) if has_extra else None
    res_ref = next(it) if has_res else None
    o_ref = next(it)
    eo_ref = next(it) if has_extra else None
    wb, stage, sem = next(it), next(it), next(it)
    u_ref = next(it) if has_norm else None
    k_rows = sum(part_widths)

    @pl.when(pl.program_id(0) == 0)
    def _():
        if layer is None:
            whole = pltpu.make_async_copy(w_hbm.at[:, pl.ds(0, n_cols)], wb, sem.at[0])
            whole.start()
            whole.wait()
        else:
            _load_weight_bf16(w_hbm, layer, wb, stage, sem, k_rows, n_cols)

    if has_norm:
        u_ref[...] = _rms(x_refs[0][...], g_ref[...]).astype(BF16)
        lhs = [(u_ref, 0, k_rows)]
    else:
        lhs, off = [], 0
        for r, kw in zip(x_refs, part_widths):
            lhs.append((r, off, kw))
            off += kw
    for n0 in range(0, n_cols, n_chunk):
        acc = None
        for r, off, kw in lhs:
            d = jnp.dot(r[...], wb[off:off + kw, n0:n0 + n_chunk], preferred_element_type=F32)
            acc = d if acc is None else acc + d
        if has_res:
            acc = acc + res_ref[:, n0:n0 + n_chunk]
        if lead_scale is not None and n0 < lead_scale[0]:
            acc = acc * lead_scale[1]
        o_ref[:, n0:n0 + n_chunk] = acc.astype(o_ref.dtype)
    if has_extra:
        eo_ref[...] = jnp.dot(u_ref[...], ew_ref[...], preferred_element_type=F32)


def _dense(xs, w, layer, *, n_cols, out_dtype, gain=None, extra_w=None, res=None, tm=512, n_chunk=512,
           lead_scale=None, name):
    assert lead_scale is None or lead_scale[0] % n_chunk == 0
    rows = xs[0].shape[0]
    part_widths = tuple(x.shape[1] for x in xs)
    k_rows = sum(part_widths)
    has_norm, has_extra, has_res = gain is not None, extra_w is not None, res is not None
    row_spec = lambda width: pl.BlockSpec((tm, width), lambda i: (i, 0))
    full_spec = lambda a: pl.BlockSpec(a.shape, lambda i: (0, 0))
    args, in_specs = list(xs), [row_spec(kw) for kw in part_widths]
    if has_norm:
        args.append(gain.reshape(1, k_rows).astype(F32))
        in_specs.append(full_spec(args[-1]))
    args.append(w)
    in_specs.append(pl.BlockSpec(memory_space=pl.ANY))
    if has_extra:
        args.append(extra_w)
        in_specs.append(full_spec(extra_w))
    if has_res:
        args.append(res)
        in_specs.append(row_spec(n_cols))
    out_shape = [jax.ShapeDtypeStruct((rows, n_cols), out_dtype)]
    out_specs = [row_spec(n_cols)]
    if has_extra:
        out_shape.append(jax.ShapeDtypeStruct((rows, extra_w.shape[1]), F32))
        out_specs.append(row_spec(extra_w.shape[1]))
    stage_cols = LANES if layer is None else n_cols
    scratch = [pltpu.VMEM((k_rows, n_cols), BF16),
               pltpu.VMEM((2, W_STAGE_ROWS, stage_cols), F32),
               pltpu.SemaphoreType.DMA((2,))]
    if has_norm:
        scratch.append(pltpu.VMEM((tm, k_rows), BF16))
    body = functools.partial(_dense_body, part_widths=part_widths, has_norm=has_norm, has_extra=has_extra,
                             has_res=has_res, n_cols=n_cols, n_chunk=n_chunk, layer=layer, lead_scale=lead_scale)
    outs = pl.pallas_call(
        body, grid=(rows // tm,), in_specs=in_specs, out_specs=out_specs, out_shape=out_shape,
        scratch_shapes=scratch, compiler_params=_params(("arbitrary",)), name=name)(*args)
    return outs if has_extra else outs[0]


ATT_TQ = 512
LOG2_E = 1.4426950408889634
ATT_Q_SCALE = LOG2_E * HEAD_DIM ** -0.5


def _winattn_body(sink_ref, q_ref, kp_ref, km_ref, kn_ref, vp_ref, vm_ref, vn_ref, g_ref, o_ref,
                  kcat, vcat, obuf, *, seq):
    s0 = pl.program_id(1) * ATT_TQ
    kcat[0:WBLK, :] = kp_ref[0]
    kcat[WBLK:WBLK + ATT_TQ, :] = km_ref[0]
    kcat[WBLK + ATT_TQ:, :] = kn_ref[0]
    vcat[0:WBLK, :] = vp_ref[0]
    vcat[WBLK:WBLK + ATT_TQ, :] = vm_ref[0]
    vcat[WBLK + ATT_TQ:, :] = vn_ref[0]
    qi = lax.broadcasted_iota(jnp.int32, (WBLK, 3 * WBLK), 0) + WBLK
    ki = lax.broadcasted_iota(jnp.int32, (WBLK, 3 * WBLK), 1)
    dist_i = jnp.abs(ki - qi)
    neg_dist = jnp.where(dist_i <= WINDOW, -dist_i.astype(F32), NEG_INF)
    for qb in range(ATT_TQ // WBLK):
        kabs = s0 + (qb - 1) * WBLK + ki
        bias_unit = jnp.where((kabs >= 0) & (kabs < seq), neg_dist, NEG_INF)
        head_cols = lambda j: slice(j * HEAD_DIM, (j + 1) * HEAD_DIM)
        kv_cols = lambda j: head_cols(j // Q_PER_KV)
        key_rows = slice(qb * WBLK, (qb + 3) * WBLK)
        scores = []
        for j in range(N_Q_HEADS):
            slope = 2.0 ** (-8.0 * (j + 1) / N_Q_HEADS)
            q = q_ref[0, qb * WBLK:(qb + 1) * WBLK, head_cols(j)]
            scores.append(_nt(q, kcat[key_rows, kv_cols(j)]) + (slope * LOG2_E) * bias_unit)
        probs, denoms = [], []
        for j in range(N_Q_HEADS):
            sink = sink_ref[j] * LOG2_E
            m = jnp.maximum(jnp.max(scores[j], axis=-1, keepdims=True), sink)
            p = jnp.exp2(scores[j] - m)
            denoms.append(jnp.sum(p, axis=-1, keepdims=True) + jnp.exp2(sink - m))
            probs.append(p.astype(BF16))
        for j in range(N_Q_HEADS):
            pv = jnp.dot(probs[j], vcat[key_rows, kv_cols(j)], preferred_element_type=F32)
            obuf[qb, :, head_cols(j)] = pv / denoms[j]
        o_ref[0, qb * WBLK:(qb + 1) * WBLK, :] = _rms(obuf[qb], g_ref[...]).astype(o_ref.dtype)


def _window_attention(proj, sink_logit, gain, batch, seq):
    p3 = proj.reshape(batch, seq, MAIN_COLS)
    nq = seq // ATT_TQ
    per = ATT_TQ // WBLK
    last = seq // WBLK - 1
    main = lambda col: pl.BlockSpec((1, ATT_TQ, ATT_KV), lambda b, i: (b, i, col))
    prev = lambda col: pl.BlockSpec((1, WBLK, ATT_KV), lambda b, i: (b, jnp.maximum(i * per - 1, 0), col))
    nxt = lambda col: pl.BlockSpec((1, WBLK, ATT_KV), lambda b, i: (b, jnp.minimum(i * per + per, last), col))
    ck, cv = COL_KA // ATT_KV, COL_VA // ATT_KV
    return pl.pallas_call(
        functools.partial(_winattn_body, seq=seq),
        grid=(batch, nq),
        in_specs=[pl.BlockSpec(memory_space=pltpu.SMEM),
                  pl.BlockSpec((1, ATT_TQ, ATT_Q), lambda b, i: (b, i, 0)),
                  prev(ck), main(ck), nxt(ck), prev(cv), main(cv), nxt(cv),
                  pl.BlockSpec((1, ATT_Q), lambda b, i: (0, 0))],
        out_specs=pl.BlockSpec((1, ATT_TQ, ATT_Q), lambda b, i: (b, i, 0)),
        out_shape=jax.ShapeDtypeStruct((batch, seq, ATT_Q), BF16),
        scratch_shapes=[pltpu.VMEM((ATT_TQ + 2 * WBLK, ATT_KV), BF16),
                        pltpu.VMEM((ATT_TQ + 2 * WBLK, ATT_KV), BF16),
                        pltpu.VMEM((ATT_TQ // WBLK, WBLK, ATT_Q), F32)],
        compiler_params=_params(("parallel", "parallel")), name="window_attention",
    )(sink_logit.astype(F32), p3, p3, p3, p3, p3, p3, p3, gain.reshape(1, ATT_Q).astype(F32))


GLA_SCAN_ROWS = 256
GLA_SCAN_UNROLL = 2
GLA_EPI_ROWS = 512
GLA_UNROLL = 8


def _split2(x):
    hi = x.astype(BF16)
    lo = (x - hi.astype(F32)).astype(BF16)
    return hi, lo


def _gla_body(q_ref, k_ref, v_ref, r_ref, lr_ref, wf_ref, wb_ref, bf_ref, bb_ref, gn_ref, o_ref,
              cum_f, cum_b, acc, st_f, st_b, *, seq):
    C = GLA_CHUNK
    n_chunks = seq // C
    ri = lax.broadcasted_iota(jnp.int32, (GLA_SCAN_ROWS, GLA_SCAN_ROWS), 0)
    ci = lax.broadcasted_iota(jnp.int32, (GLA_SCAN_ROWS, GLA_SCAN_ROWS), 1)
    same = (ri // C) == (ci // C)
    tri_f = jnp.where(same & (ci <= ri), 1.0, 0.0).astype(BF16)
    tri_b = jnp.where(same & (ci >= ri), 1.0, 0.0).astype(BF16)

    def scan_body(t, carry):
        row_sets = [pl.ds(pl.multiple_of((t * GLA_SCAN_UNROLL + u) * GLA_SCAN_ROWS, GLA_SCAN_ROWS), GLA_SCAN_ROWS)
                    for u in range(GLA_SCAN_UNROLL)]
        lrs = [lr_ref[0, rows, :].astype(BF16) for rows in row_sets]
        zs = [jnp.dot(lr, w_ref[...], preferred_element_type=F32) + b_ref[...]
              for lr in lrs for w_ref, b_ref in ((wf_ref, bf_ref), (wb_ref, bb_ref))]
        gs = [_split2((jnp.minimum(z, 0.0) - jnp.log1p(jnp.exp(-jnp.abs(z)))) * (LOG2_E / GLA_TAU)) for z in zs]
        for n, g3 in enumerate(gs):
            tri, dst = ((tri_f, cum_f), (tri_b, cum_b))[n % 2]
            dst[row_sets[n // 2], :] = sum(jnp.dot(tri, part, preferred_element_type=F32) for part in g3)
        return carry

    lax.fori_loop(0, seq // (GLA_SCAN_ROWS * GLA_SCAN_UNROLL), scan_body, 0)

    acc[...] = jnp.zeros_like(acc)
    st_f[...] = jnp.zeros_like(st_f)
    st_b[...] = jnp.zeros_like(st_b)
    rr = lax.broadcasted_iota(jnp.int32, (C, C), 0)
    cc = lax.broadcasted_iota(jnp.int32, (C, C), 1)
    scale = GLA_DK ** -0.5

    def chunk_body(i, carry):
        jobs = []
        for u in range(GLA_UNROLL):
            c = i * GLA_UNROLL + u
            jobs += [(c, cum_f, True), (n_chunks - 1 - c, cum_b, False)]
        prep = []
        for c, cum, forward in jobs:
            rows = pl.ds(pl.multiple_of(c * C, C), C)
            b = cum[rows, :]
            b_end = b[C - 1:C, :] if forward else b[0:1, :]
            q = q_ref[0, rows, :].astype(F32) * scale
            k = k_ref[0, rows, :].astype(F32)
            v = v_ref[0, rows, :]
            q_dec = (q * jnp.exp2(b)).astype(BF16)
            k_inc = (k * jnp.exp2(-b)).astype(BF16)
            k_dec = (k * jnp.exp2(b_end - b)).astype(BF16)
            prep.append((rows, v, q_dec, k_inc, k_dec, jnp.exp2(b_end)))
        attn = [_nt(q_dec, k_inc) for _, _, q_dec, k_inc, _, _ in prep]
        kv_t = [_tn(v, k_dec) for _, v, _, _, k_dec, _ in prep]
        state_t = {True: st_f[...], False: st_b[...]}
        o_inter = []
        for (_, _, forward), (_, _, q_dec, _, _, decay), kv in zip(jobs, prep, kv_t):
            o_inter.append(_nt(q_dec, state_t[forward].astype(BF16)))
            state_t[forward] = state_t[forward] * decay + kv
        st_f[...] = state_t[True]
        st_b[...] = state_t[False]
        for (_, _, forward), (rows, v, _, _, _, _), a, oi in zip(jobs, prep, attn, o_inter):
            a = jnp.where((cc <= rr) if forward else (cc >= rr), a, 0.0)
            o = jnp.dot(a.astype(BF16), v, preferred_element_type=F32) + oi
            acc[rows, :] = acc[rows, :] + o
        return carry

    lax.fori_loop(0, n_chunks // GLA_UNROLL, chunk_body, 0)

    def epi_body(t, carry):
        rows = pl.ds(pl.multiple_of(t * GLA_EPI_ROWS, GLA_EPI_ROWS), GLA_EPI_ROWS)
        r = r_ref[0, rows, :].astype(F32)
        o_ref[0, rows, :] = (_rms(acc[rows, :], gn_ref[...]) * (r * jax.nn.sigmoid(r))).astype(o_ref.dtype)
        return carry

    lax.fori_loop(0, seq // GLA_EPI_ROWS, epi_body, 0)


def _gla(proj, lr, w_gf, b_gf, w_gb, b_gb, gain, batch, seq):
    p3 = proj.reshape(batch, seq, MAIN_COLS)
    lr3 = lr.reshape(batch, seq, LANES)
    wf = jnp.zeros((LANES, GLA_QK), F32).at[:GLA_LOWRANK].set(w_gf).astype(BF16)
    wb = jnp.zeros((LANES, GLA_QK), F32).at[GLA_LOWRANK:2 * GLA_LOWRANK].set(w_gb).astype(BF16)
    seq_blk = lambda width, col0: pl.BlockSpec((1, seq, width), lambda b, h: (b, 0, col0 // width + h))
    head_w = pl.BlockSpec((LANES, GLA_DK), lambda b, h: (0, h))
    head_b = pl.BlockSpec((1, GLA_DK), lambda b, h: (0, h))
    return pl.pallas_call(
        functools.partial(_gla_body, seq=seq),
        grid=(batch, GLA_HEADS),
        in_specs=[seq_blk(GLA_DK, COL_QG), seq_blk(GLA_DK, COL_KG), seq_blk(GLA_DV, COL_VG),
                  seq_blk(GLA_DV, COL_RG),
                  pl.BlockSpec((1, seq, LANES), lambda b, h: (b, 0, 0)),
                  head_w, head_w, head_b, head_b,
                  pl.BlockSpec((1, GLA_DV), lambda b, h: (0, 0))],
        out_specs=pl.BlockSpec((1, seq, GLA_DV), lambda b, h: (b, 0, h)),
        out_shape=jax.ShapeDtypeStruct((batch, seq, GLA_V), BF16),
        scratch_shapes=[pltpu.VMEM((seq, GLA_DK), F32), pltpu.VMEM((seq, GLA_DK), F32),
                        pltpu.VMEM((seq, GLA_DV), F32),
                        pltpu.VMEM((GLA_DV, GLA_DK), F32), pltpu.VMEM((GLA_DV, GLA_DK), F32)],
        compiler_params=_params(("parallel", "parallel")), name="gla",
    )(p3, p3, p3, p3, lr3, wf, wb, b_gf.reshape(1, GLA_QK).astype(F32), b_gb.reshape(1, GLA_QK).astype(F32),
      gain.reshape(1, GLA_DV).astype(F32))


MEM_TM = 512
MEM_K_CHUNK = 512


def _mem_kv_body(m_ref, g_ref, wk_ref, wv_ref, k_ref, v_ref, u, acc_k, acc_v):
    step = pl.program_id(1)
    n_chunks = D_MODEL // MEM_K_CHUNK

    @pl.when(step == 0)
    def _():
        normed = _rms(m_ref[...], g_ref[...]).astype(BF16)
        for c in range(n_chunks):
            u[c] = normed[:, c * MEM_K_CHUNK:(c + 1) * MEM_K_CHUNK]
        acc_k[...] = jnp.zeros_like(acc_k)
        acc_v[...] = jnp.zeros_like(acc_v)

    lhs = u[step]
    acc_k[...] = acc_k[...] + jnp.dot(lhs, wk_ref[...].astype(BF16), preferred_element_type=F32)
    acc_v[...] = acc_v[...] + jnp.dot(lhs, wv_ref[...].astype(BF16), preferred_element_type=F32)

    @pl.when(step == n_chunks - 1)
    def _():
        k_ref[...] = acc_k[...].astype(k_ref.dtype)
        v_ref[...] = acc_v[...].astype(v_ref.dtype)


def _mem_kv(mem_rows, gain, w_ck, w_cv, layer):
    rows = mem_rows.shape[0]
    n_chunks = D_MODEL // MEM_K_CHUNK
    tile = pl.BlockSpec((MEM_TM, D_MODEL), lambda i, c: (i, 0))
    w_spec = pl.BlockSpec((None, MEM_K_CHUNK, D_MODEL), lambda i, c: (layer, c, 0))
    g2 = gain.reshape(1, D_MODEL).astype(F32)
    out = jax.ShapeDtypeStruct((rows, D_MODEL), BF16)
    return pl.pallas_call(
        _mem_kv_body, grid=(rows // MEM_TM, n_chunks),
        in_specs=[tile, pl.BlockSpec((1, D_MODEL), lambda i, c: (0, 0)), w_spec, w_spec],
        out_specs=[tile, tile], out_shape=[out, out],
        scratch_shapes=[pltpu.VMEM((n_chunks, MEM_TM, MEM_K_CHUNK), BF16),
                        pltpu.VMEM((MEM_TM, D_MODEL), F32), pltpu.VMEM((MEM_TM, D_MODEL), F32)],
        compiler_params=_params(("parallel", "arbitrary")), name="mem_kv_proj")(mem_rows, g2, w_ck, w_cv)


XATT_TQ = 512


XATT_CHUNK = 512


def _cross_body(h_ref, g_ref, wq_hbm, k_ref, v_ref, wo_hbm, gr_ref, wr_ref, br_ref,
                o_ref, hn_ref, gate_ref, order_ref, count_ref,
                wqb, wob, stage, sem, u_scr, q_scr, a_scr, prev, *, layer, n_tokens):
    step = pl.program_id(0)

    @pl.when(step == 0)
    def _():
        _load_weight_bf16(wq_hbm, layer, wqb, stage, sem, D_MODEL, D_MODEL)
        _load_weight_bf16(wo_hbm, layer, wob, stage, sem, D_MODEL, D_MODEL)
        prev[...] = jnp.zeros_like(prev)

    router = _route_tile(prev[...], gr_ref[...], wr_ref, br_ref, hn_ref, gate_ref, order_ref, count_ref,
                         jnp.maximum(step - 1, 0), n_tokens)

    scale = LOG2_E * X_HEAD_DIM ** -0.5
    heads = [slice(h * X_HEAD_DIM, (h + 1) * X_HEAD_DIM) for h in range(X_HEADS)]
    chunks = [slice(n0, n0 + XATT_CHUNK) for n0 in range(0, D_MODEL, XATT_CHUNK)]

    def norm(rows):
        u_scr[rows, :] = _rms(h_ref[rows, :], g_ref[...]).astype(BF16)

    def q_proj(rows):
        for cols in chunks:
            q = jnp.dot(u_scr[rows, :], wqb[:, cols], preferred_element_type=F32)
            q_scr[rows, cols] = (q * scale).astype(BF16)

    def scores(rows):
        return [_nt(q_scr[rows, hd], k_ref[0, :, hd]) for hd in heads]

    def softmax(s_list):
        out = []
        for s in s_list:
            p = jnp.exp2(s - jnp.max(s, axis=-1, keepdims=True))
            out.append((p.astype(BF16), jnp.sum(p, axis=-1, keepdims=True)))
        return out

    def attend(rows, probs):
        for hd, (p, denom) in zip(heads, probs):
            a_scr[rows, hd] = (jnp.dot(p, v_ref[0, :, hd], preferred_element_type=F32) / denom).astype(BF16)

    def o_proj(rows):
        for cols in chunks:
            out = h_ref[rows, cols] + jnp.dot(a_scr[rows, :], wob[:, cols], preferred_element_type=F32)
            o_ref[rows, cols] = out
            prev[rows, cols] = out

    half = XATT_TQ // 2
    first, second = slice(0, half), slice(half, XATT_TQ)
    norm(first)
    q_proj(first)
    next(router)
    norm(second)
    s_first = scores(first)
    q_proj(second)
    next(router)
    p_first = softmax(s_first)
    attend(first, p_first)
    s_second = scores(second)
    o_proj(first)
    for _ in router:
        pass
    p_second = softmax(s_second)
    attend(second, p_second)
    o_proj(second)


def _cross_block(h, gain, w_cq, kx, vx, w_co, router_gain, router_w, router_b, layer, batch, seq, mem_len):
    rows = h.shape[0]
    n_tiles = rows // XATT_TQ
    tiles_per_batch = seq // XATT_TQ
    k3 = kx.reshape(batch, mem_len, D_MODEL)
    v3 = vx.reshape(batch, mem_len, D_MODEL)
    as_row = lambda g: g.reshape(1, D_MODEL).astype(F32)
    cur = lambda i: jnp.minimum(i, n_tiles - 1)
    lag = lambda i: jnp.maximum(i - 1, 0)
    tile = pl.BlockSpec((XATT_TQ, D_MODEL), lambda i: (cur(i), 0))
    mem_spec = pl.BlockSpec((1, mem_len, D_MODEL), lambda i: (cur(i) // tiles_per_batch, 0, 0))
    full = lambda a: pl.BlockSpec(a.shape, lambda i: (0, 0))
    lag_rows = lambda width: pl.BlockSpec((XATT_TQ, width), lambda i: (lag(i), 0))
    lag_tile = lambda sublanes: pl.BlockSpec((1, sublanes, LANES), lambda i: (lag(i), 0, 0))
    assert TOP_K * XATT_TQ == ID_SPLIT * ID_SPLIT, "the router's sort grid holds one tile's assignments"
    hbm = pl.BlockSpec(memory_space=pl.ANY)
    g_cross, g_router = as_row(gain), as_row(router_gain)
    return pl.pallas_call(
        functools.partial(_cross_body, layer=layer, n_tokens=rows), grid=(n_tiles + 1,),
        in_specs=[tile, full(g_cross), hbm, mem_spec, mem_spec, hbm, full(g_router), full(router_w), full(router_b)],
        out_specs=[tile, lag_rows(HALF), lag_rows(LANES), lag_tile(ID_SPLIT), lag_tile(8)],
        out_shape=[jax.ShapeDtypeStruct((rows, D_MODEL), F32),
                   jax.ShapeDtypeStruct((rows, HALF), jnp.uint32),
                   jax.ShapeDtypeStruct((rows, LANES), F32),
                   jax.ShapeDtypeStruct((n_tiles, ID_SPLIT, LANES), jnp.int32),
                   jax.ShapeDtypeStruct((n_tiles, 8, LANES), jnp.int32)],
        scratch_shapes=[pltpu.VMEM((D_MODEL, D_MODEL), BF16), pltpu.VMEM((D_MODEL, D_MODEL), BF16),
                        pltpu.VMEM((2, W_STAGE_ROWS, D_MODEL), F32), pltpu.SemaphoreType.DMA((2,)),
                        pltpu.VMEM((XATT_TQ, D_MODEL), BF16), pltpu.VMEM((XATT_TQ, D_MODEL), BF16),
                        pltpu.VMEM((XATT_TQ, D_MODEL), BF16), pltpu.VMEM((XATT_TQ, D_MODEL), F32)],
        compiler_params=_params(("arbitrary",)), name="cross_block",
    )(h, g_cross, w_cq, k3, v3, w_co, g_router, router_w, router_b)


HALF = D_MODEL // 2


def _pack_halves(x):
    return pltpu.pack_elementwise([x[:, :HALF], x[:, HALF:]], packed_dtype=BF16)


def _unpack_halves(words):
    return [pltpu.unpack_elementwise(words, index=i, packed_dtype=BF16, unpacked_dtype=F32) for i in range(2)]


ID_SPLIT = 32


def _route_tile(h, gain, w_ref, b_ref, hn_ref, gate_ref, order_ref, count_ref, tile_idx, n_tokens):
    hn = _rms(h, gain)
    hn_ref[...] = _pack_halves(hn)
    logits = jnp.dot(hn.astype(BF16), w_ref[...], preferred_element_type=F32) + b_ref[...]
    yield
    lane = lax.broadcasted_iota(jnp.int32, logits.shape, 1).astype(F32)
    ninf = -jnp.inf
    first = lambda hit: jnp.min(jnp.where(hit, lane, float(LANES)), axis=-1, keepdims=True)
    in_groups = lane < N_GROUPS
    gl = jnp.where(in_groups, logits, ninf)
    gmax = jnp.max(gl, axis=-1, keepdims=True)
    g_idx = first(gl == gmax)
    p_group = 1.0 / jnp.sum(jnp.where(in_groups, jnp.exp(logits - gmax), 0.0), axis=-1, keepdims=True)
    lo = N_GROUPS + EXPERTS_PER_GROUP * g_idx
    el = jnp.where((lane >= lo) & (lane < lo + EXPERTS_PER_GROUP), logits, ninf)
    e1 = jnp.max(el, axis=-1, keepdims=True)
    i1 = first(el == e1)
    el2 = jnp.where(lane == i1, ninf, el)
    e2 = jnp.max(el2, axis=-1, keepdims=True)
    i2 = first(el2 == e2)
    t = jnp.exp(e2 - e1)
    w1 = p_group / (1.0 + t)
    w2 = p_group * t / (1.0 + t)
    gate_ref[...] = jnp.where(lane == 0, w1, jnp.where(lane == 1, w2, 0.0))

    tm = logits.shape[0]
    hit0, hit1 = lane == i1 - N_GROUPS, lane == i2 - N_GROUPS
    member = jnp.where(hit0 | hit1, 1.0, 0.0)
    member_b = member.astype(BF16)
    earlier = (lax.broadcasted_iota(jnp.int32, (tm, tm), 1) < lax.broadcasted_iota(jnp.int32, (tm, tm), 0))
    rank = jnp.dot(jnp.where(earlier, 1.0, 0.0).astype(BF16), member_b, preferred_element_type=F32)
    lower = (lax.broadcasted_iota(jnp.int32, (LANES, LANES), 0) < lax.broadcasted_iota(jnp.int32, (LANES, LANES), 1))
    run_start = jnp.sum(jnp.dot(member_b, jnp.where(lower, 1.0, 0.0).astype(BF16), preferred_element_type=F32),
                        axis=0, keepdims=True)
    yield
    pos = rank + run_start
    positions = [jnp.sum(jnp.where(hit, pos, 0.0), axis=-1, keepdims=True) for hit in (hit0, hit1)]
    local_tok = lax.broadcasted_iota(jnp.int32, (tm, 1), 0)
    grids = [None, None]
    for s, p in enumerate(positions):
        p_hi = jnp.floor(p / ID_SPLIT)
        at_hi = jnp.where(lane == p_hi, 1.0, 0.0)
        at_lo = jnp.where(lane == p - ID_SPLIT * p_hi, 1.0, 0.0).astype(BF16)
        a = s * tm + local_tok
        for d, digit in enumerate((a // ID_SPLIT, a % ID_SPLIT)):
            part = _tn((at_hi * digit.astype(F32)).astype(BF16), at_lo)
            grids[d] = part if grids[d] is None else grids[d] + part
    a_sorted = (grids[0] * ID_SPLIT + grids[1]).astype(jnp.int32)
    dst = (a_sorted // tm) * n_tokens + tile_idx * tm + a_sorted % tm
    order_ref[0] = dst[:ID_SPLIT, :]
    count_ref[0] =jnp.broadcast_to(jnp.sum(member, axis=0, keepdims=True), (8, LANES)).astype(jnp.int32)


def _router_params(w_rg, b_rg, w_re, b_re):
    n_log = N_GROUPS + N_EXPERTS
    w = jnp.zeros((D_MODEL, LANES), F32).at[:, :N_GROUPS].set(w_rg).at[:, N_GROUPS:n_log].set(w_re).astype(BF16)
    b = jnp.zeros((1, LANES), F32).at[0, :N_GROUPS].set(b_rg).at[0, N_GROUPS:n_log].set(b_re)
    return w, b


def _expert_changed(blk_e, blk):
    prev = blk_e[jnp.maximum(blk - 1, 0)]
    return (blk == 0) | (blk_e[blk] != prev)


WEIGHT_DMA_PRIORITY = 1


def _expert_weight_copies(layer, w_hbms, stages, sem):
    def copies(e):
        return [pltpu.make_async_copy(w.at[layer, e], st, sem.at[i])
                for i, (w, st) in enumerate(zip(w_hbms, stages))]
    return copies


GATHER_AHEAD = 3
GATHER_SLOTS = GATHER_AHEAD + 1
MOE_CHUNKS = 2


SCATTER_SLOTS = 3


IDX_ROWS = 8
IDX_DST = GATHER_AHEAD + 1
IDX_DST_PREV = GATHER_AHEAD + 2


def _experts_body(blk_e, n_real, nxt_e, idx_ref, hn_hbm, wg_hbm, wu_hbm, wd_hbm, out_hbm, xs, xb, hid, ys,
                  wg_stage, wu_stage, wd_stage, wgb, wub, wdb, gsem, ssem, wsem, *, n_assign, layer):
    blk = pl.program_id(0)
    dummy_slot = SCATTER_SLOTS - 1

    def gather_row(idx_row, slot, r):
        return pltpu.make_async_copy(hn_hbm.at[pl.ds(idx_ref[0, idx_row, r], 1), :], xs.at[slot, pl.ds(r, 1), :],
                                     gsem.at[slot])

    def scatter_row(idx_row, slot, r):
        return pltpu.make_async_copy(ys.at[slot, pl.ds(r, 1), :], out_hbm.at[pl.ds(idx_ref[0, idx_row, r], 1), :],
                                     ssem.at[slot])

    def wait_gather(slot):
        pltpu.make_async_copy(hn_hbm.at[pl.ds(0, MOE_BLOCK), :], xs.at[slot], gsem.at[slot]).wait()

    def wait_scatter(slot):
        pltpu.make_async_copy(ys.at[slot], out_hbm.at[pl.ds(0, MOE_BLOCK), :], ssem.at[slot]).wait()

    @pl.when(blk == 0)
    def _():
        def body(r, carry):
            for s in range(GATHER_AHEAD):
                gather_row(s, s, r).start()
            return carry
        lax.fori_loop(0, MOE_BLOCK, body, 0)
        ys[dummy_slot] = jnp.zeros((MOE_BLOCK, HALF), jnp.uint32)
        for s in range(SCATTER_SLOTS):
            trash = pltpu.make_async_copy(
                ys.at[dummy_slot], out_hbm.at[pl.ds(n_assign + s * MOE_BLOCK, MOE_BLOCK), :], ssem.at[dummy_slot])
            trash.start()
            trash.wait()

    @pl.when(blk < n_real[0])
    def _():
        in_slot = blk % GATHER_SLOTS
        ahead_slot = (blk + GATHER_AHEAD) % GATHER_SLOTS
        out_slot = blk % SCATTER_SLOTS
        prev_slot = (blk + SCATTER_SLOTS - 1) % SCATTER_SLOTS

        copies = _expert_weight_copies(layer, (wg_hbm, wu_hbm, wd_hbm), (wg_stage, wu_stage, wd_stage), wsem)
        first_of_expert = _expert_changed(blk_e, blk)

        @pl.when(blk == 0)
        def _():
            for cp in copies(blk_e[0]):
                cp.start(priority=WEIGHT_DMA_PRIORITY)

        wait_gather(in_slot)

        @pl.when(blk >= SCATTER_SLOTS - 1)
        def _():
            wait_scatter(out_slot)

        def compute(fresh_weights):
            lo, hi = _unpack_halves(xs[in_slot])
            xb[:, :HALF] = lo.astype(BF16)
            xb[:, HALF:] = hi.astype(BF16)
            for r in range(MOE_BLOCK):
                scatter_row(IDX_DST_PREV, prev_slot, r).start(priority=r % 2)
                gather_row(GATHER_AHEAD, ahead_slot, r).start()
            cw = D_FF_EXPERT // MOE_CHUNKS
            for c in range(MOE_CHUNKS):
                cols = slice(c * cw, (c + 1) * cw)
                if fresh_weights:
                    wgb[:, cols] = wg_stage[:, cols].astype(BF16)
                    wub[:, cols] = wu_stage[:, cols].astype(BF16)
                a = jnp.dot(xb[...], wgb[:, cols], preferred_element_type=F32)
                u = jnp.dot(xb[...], wub[:, cols], preferred_element_type=F32)
                hid[:, cols] = (a * jax.nn.sigmoid(a) * u).astype(BF16)
            cw = HALF // MOE_CHUNKS
            for c in range(MOE_CHUNKS):
                cols = slice(c * cw, (c + 1) * cw)
                hi_cols = slice(HALF + c * cw, HALF + (c + 1) * cw)
                if fresh_weights:
                    wdb[:, cols] = wd_stage[:, cols].astype(BF16)
                    wdb[:, hi_cols] = wd_stage[:, hi_cols].astype(BF16)
                ys[out_slot, :, cols] = pltpu.pack_elementwise(
                    [jnp.dot(hid[...], wdb[:, cols], preferred_element_type=F32),
                     jnp.dot(hid[...], wdb[:, hi_cols], preferred_element_type=F32)], packed_dtype=BF16)

        @pl.when(first_of_expert)
        def _():
            for cp in copies(blk_e[blk]):
                cp.wait()
            compute(True)

            @pl.when(nxt_e[blk] >= 0)
            def _():
                for cp in copies(nxt_e[blk]):
                    cp.start(priority=WEIGHT_DMA_PRIORITY)

        @pl.when(jnp.logical_not(first_of_expert))
        def _():
            compute(False)

        @pl.when(blk == n_real[0] - 1)
        def _():
            def body(r, carry):
                scatter_row(IDX_DST, out_slot, r).start()
                return carry
            lax.fori_loop(0, MOE_BLOCK, body, 0)
            for s in range(1, GATHER_SLOTS):
                wait_gather((blk + s) % GATHER_SLOTS)
            for s in range(SCATTER_SLOTS):
                wait_scatter(s)


def _moe_forward(hn, order, counts, w_gate, w_up, w_down, layer, n_tokens):
    A = n_tokens * TOP_K
    out_rows = A + SCATTER_SLOTS * MOE_BLOCK
    R = A + N_EXPERTS * MOE_BLOCK
    n_blk = R // MOE_BLOCK
    i32 = jnp.int32
    n_tiles, per_tile = order.shape[0], ID_SPLIT * ID_SPLIT
    experts = jnp.arange(N_EXPERTS, dtype=i32)
    n = counts[:, 0, :N_EXPERTS]
    total = jnp.sum(n, axis=0)
    padded = ((total + MOE_BLOCK - 1) // MOE_BLOCK) * MOE_BLOCK
    pend = jnp.cumsum(padded)
    pstart = pend - padded
    n_real = (pend[-1] // MOE_BLOCK).astype(i32).reshape(1)
    blk = jnp.arange(n_blk, dtype=i32)
    blk_e = jnp.minimum(jnp.sum((pend[None, :] <= (blk * MOE_BLOCK)[:, None]).astype(i32), axis=1), N_EXPERTS - 1)
    is_e = (blk_e[:, None] == experts[None, :]).astype(i32)
    of_block = lambda per_expert: jnp.sum(is_e * per_expert[None, :], axis=1)
    cum_incl = jnp.cumsum(n, axis=0)
    run_shift = (jnp.cumsum(n, axis=1) - n) - (cum_incl - n)
    per_tile_of_block = lambda m: jnp.sum(is_e[:, None, :] * m[None, :, :], axis=2)
    cum_b, shift_b = per_tile_of_block(cum_incl), per_tile_of_block(run_shift)
    in_blk = jnp.arange(MOE_BLOCK, dtype=i32)
    k = (blk * MOE_BLOCK - of_block(pstart))[:, None] + in_blk[None, :]
    tile_of = jnp.minimum(jnp.sum((cum_b[:, None, :] <= k[:, :, None]).astype(i32), axis=2), n_tiles - 1)
    is_tile = (tile_of[:, :, None] == jnp.arange(n_tiles, dtype=i32)[None, None, :]).astype(i32)
    entry = tile_of * per_tile + k + jnp.sum(is_tile * shift_b[:, None, :], axis=2)
    pad_entry = A + (blk % SCATTER_SLOTS)[:, None] * MOE_BLOCK + in_blk[None, :]
    entry = jnp.where(k < of_block(total)[:, None], entry, pad_entry)
    n_pad = SCATTER_SLOTS * MOE_BLOCK
    dst_table = jnp.concatenate([order[:, :, :ID_SPLIT].reshape(A), A + jnp.arange(n_pad, dtype=i32)])
    row_dst = dst_table[entry]
    row_tok = row_dst % n_tokens
    row_dst_prev = jnp.concatenate([(A + (SCATTER_SLOTS - 1) * MOE_BLOCK + in_blk)[None, :], row_dst[:-1]], axis=0)
    last = n_real[0] - 1
    last_tok = lax.dynamic_slice_in_dim(row_tok, last, 1, axis=0)
    ahead = [jnp.where((blk + s <= last)[:, None],
                       jnp.concatenate([row_tok[s:], jnp.zeros((s, MOE_BLOCK), i32)], axis=0), last_tok)
             for s in range(GATHER_AHEAD + 1)]
    idx_rows = ahead + [row_dst, row_dst_prev]
    idx_rows += [jnp.zeros_like(row_dst)] * (IDX_ROWS - len(idx_rows))
    idx_all = jnp.stack(idx_rows, axis=1)

    run_end = pend[blk_e] // MOE_BLOCK
    nxt_e = jnp.where(run_end < n_real[0], blk_e[jnp.minimum(run_end, n_blk - 1)], -1).astype(jnp.int32)

    idx_spec = pl.BlockSpec((1, IDX_ROWS, MOE_BLOCK), lambda i, be, nr, nx: (jnp.minimum(i, nr[0] - 1), 0, 0),
                            memory_space=pltpu.SMEM)
    hbm = pl.BlockSpec(memory_space=pl.ANY)
    up_shape, down_shape = (D_MODEL, D_FF_EXPERT), (D_FF_EXPERT, D_MODEL)
    return pl.pallas_call(
        functools.partial(_experts_body, n_assign=A, layer=layer),
        grid_spec=pltpu.PrefetchScalarGridSpec(
            num_scalar_prefetch=3, grid=(n_blk,),
            in_specs=[idx_spec] + [hbm] * 4,
            out_specs=hbm,
            scratch_shapes=[pltpu.VMEM((GATHER_SLOTS, MOE_BLOCK, HALF), jnp.uint32),
                            pltpu.VMEM((MOE_BLOCK, D_MODEL), BF16),
                            pltpu.VMEM((MOE_BLOCK, D_FF_EXPERT), BF16),
                            pltpu.VMEM((SCATTER_SLOTS, MOE_BLOCK, HALF), jnp.uint32),
                            pltpu.VMEM(up_shape, F32), pltpu.VMEM(up_shape, F32), pltpu.VMEM(down_shape, F32),
                            pltpu.VMEM(up_shape, BF16), pltpu.VMEM(up_shape, BF16), pltpu.VMEM(down_shape, BF16),
                            pltpu.SemaphoreType.DMA((GATHER_SLOTS,)), pltpu.SemaphoreType.DMA((SCATTER_SLOTS,)),
                            pltpu.SemaphoreType.DMA((3,))]),
        out_shape=jax.ShapeDtypeStruct((out_rows, HALF), jnp.uint32),
        compiler_params=_params(("arbitrary",)), name="moe_experts",
    )(blk_e, n_real, nxt_e, idx_all, hn, w_gate, w_up, w_down)


FINAL_TM = 512


def _final_body(h_ref, y0_ref, y1_ref, gate_ref, g_ref, o_ref):
    gate = gate_ref[...]
    y0, y1 = _unpack_halves(y0_ref[...]), _unpack_halves(y1_ref[...])
    halves = [h_ref[:, i * HALF:(i + 1) * HALF] + gate[:, 0:1] * y0[i] + gate[:, 1:2] * y1[i] for i in range(2)]
    mean_sq = sum(jnp.sum(hh * hh, axis=-1, keepdims=True) for hh in halves) / D_MODEL
    inv = lax.rsqrt(mean_sq + RMS_EPS)
    for i, hh in enumerate(halves):
        o_ref[:, i * HALF:(i + 1) * HALF] = hh * inv * g_ref[:, i * HALF:(i + 1) * HALF]


def _final(h, ys, gate, gain):
    rows = h.shape[0]
    n_tiles = rows // FINAL_TM
    tile = lambda width: pl.BlockSpec((FINAL_TM, width), lambda i: (i, 0))
    g2 = gain.reshape(1, D_MODEL).astype(F32)
    return pl.pallas_call(
        _final_body, grid=(n_tiles,),
        in_specs=[tile(D_MODEL), tile(HALF), pl.BlockSpec((FINAL_TM, HALF), lambda i: (i + n_tiles, 0)),
                  tile(LANES), pl.BlockSpec((1, D_MODEL), lambda i: (0, 0))],
        out_specs=tile(D_MODEL), out_shape=jax.ShapeDtypeStruct((rows, D_MODEL), F32),
        compiler_params=_params(("parallel",)), name="combine_final_norm")(h, ys, ys, gate, g2)


def kernel(x, mem, norm_mix, w_in, attn_out_norm, sink_logit, w_gla_gf, b_gla_gf, w_gla_gb, b_gla_gb, gla_out_norm, w_out, norm_cross, norm_mem, w_cq, w_ck, w_cv, w_co, norm_ffn, w_router_group, b_router_group, w_router_expert, b_router_expert, w_gate, w_up, w_down, norm_final):
    batch, seq, _ = x.shape
    mem_len = mem.shape[1]
    n_tokens = batch * seq
    h = x.reshape(n_tokens, D_MODEL)
    memf = mem.reshape(batch * mem_len, D_MODEL)
    assert norm_mix.shape[0] == 1, "the combine step is fused with the final norm: single-layer stacks only"
    for l in range(norm_mix.shape[0]):
        w_lr = jnp.zeros((D_MODEL, LANES), F32).at[:, :2 * GLA_LOWRANK].set(w_in[l][:, MAIN_COLS:]).astype(BF16)
        w_main = w_in[l].astype(BF16)
        proj, lr = _dense([h], w_main, None, n_cols=MAIN_COLS, out_dtype=BF16, gain=norm_mix[l], extra_w=w_lr,
                          lead_scale=(ATT_Q, ATT_Q_SCALE), name="in_proj")
        o_a = _window_attention(proj, sink_logit[l], attn_out_norm[l], batch, seq).reshape(n_tokens, ATT_Q)
        o_g = _gla(proj, lr, w_gla_gf[l], b_gla_gf[l], w_gla_gb[l], b_gla_gb[l], gla_out_norm[l],
                   batch, seq).reshape(n_tokens, GLA_V)
        h = _dense([o_a, o_g], w_out, l, n_cols=D_MODEL, out_dtype=F32, res=h, name="out_proj")
        kx, vx = _mem_kv(memf, norm_mem[l], w_ck, w_cv, l)
        router_w, router_b = _router_params(w_router_group[l], b_router_group[l],
                                            w_router_expert[l], b_router_expert[l])
        h, hn, gate, order, counts = _cross_block(h, norm_cross[l], w_cq, kx, vx, w_co, norm_ffn[l],
                                                  router_w, router_b, l, batch, seq, mem_len)
        ys = _moe_forward(hn, order, counts, w_gate, w_up, w_down, l, n_tokens)
    return _final(h, ys, gate, norm_final).reshape(batch, seq, D_MODEL)
```

```python
import functools

import jax
import jax.numpy as jnp
from jax import lax
from jax.experimental import pallas as pl
from jax.experimental.pallas import tpu as pltpu

F32 = jnp.float32
BF16 = jnp.bfloat16

D_MODEL = 2048
N_Q_HEADS = 8
N_KV_HEADS = 2
Q_PER_KV = N_Q_HEADS // N_KV_HEADS
HEAD_DIM = 128
WINDOW = 128
WBLK = 128
GLA_HEADS = 4
GLA_DK = 128
GLA_DV = 256
GLA_LOWRANK = 16
GLA_TAU = 16.0
GLA_CHUNK = 64
ATT_Q = N_Q_HEADS * HEAD_DIM
ATT_KV = N_KV_HEADS * HEAD_DIM
GLA_QK = GLA_HEADS * GLA_DK
GLA_V = GLA_HEADS * GLA_DV
MAIN_COLS = ATT_Q + 2 * ATT_KV + 2 * GLA_QK + 2 * GLA_V
COL_KA = ATT_Q
COL_VA = COL_KA + ATT_KV
COL_QG = COL_VA + ATT_KV
COL_KG = COL_QG + GLA_QK
COL_VG = COL_KG + GLA_QK
COL_RG = COL_VG + GLA_V
X_HEADS = 4
X_HEAD_DIM = D_MODEL // X_HEADS
N_GROUPS = 4
EXPERTS_PER_GROUP = 8
N_EXPERTS = N_GROUPS * EXPERTS_PER_GROUP
TOP_K = 2
D_FF_EXPERT = D_MODEL // 2
MOE_BLOCK = 256
RMS_EPS = 1e-6
NEG_INF = -1e30

LANES = 128
VMEM_LIMIT = 56 * 1024 * 1024
W_STAGE_ROWS = 128


def _params(sem):
    return pltpu.CompilerParams(dimension_semantics=sem, vmem_limit_bytes=VMEM_LIMIT)


def _nt(a, b):
    return lax.dot_general(a, b, (((1,), (1,)), ((), ())), preferred_element_type=F32)


def _tn(a, b):
    return lax.dot_general(a, b, (((0,), (0,)), ((), ())), preferred_element_type=F32)


def _rms(x, gain):
    return x * lax.rsqrt(jnp.mean(x * x, axis=-1, keepdims=True) + RMS_EPS) * gain


def _load_weight_bf16(w_hbm, layer, wb, stage, sem, k_rows, n_cols):
    n_chunks = k_rows // W_STAGE_ROWS

    def copy(c):
        return pltpu.make_async_copy(
            w_hbm.at[layer, pl.ds(c * W_STAGE_ROWS, W_STAGE_ROWS), pl.ds(0, n_cols)],
            stage.at[c % 2], sem.at[c % 2])

    copy(0).start()
    for c in range(n_chunks):
        if c + 1 < n_chunks:
            copy(c + 1).start()
        copy(c).wait()
        wb[c * W_STAGE_ROWS:(c + 1) * W_STAGE_ROWS, :] = stage[c % 2].astype(BF16)


def _dense_body(*refs, part_widths, has_norm, has_extra, has_res, n_cols, n_chunk, layer, lead_scale):
    it = iter(refs)
    x_refs = [next(it) for _ in part_widths]
    g_ref = next(it) if has_norm else None
    w_hbm = next(it)
    ew_ref = next(it) if has_extra else None
    res_ref = next(it) if has_res else None
    o_ref = next(it)
    eo_ref = next(it) if has_extra else None
    wb, stage, sem = next(it), next(it), next(it)
    u_ref = next(it) if has_norm else None
    k_rows = sum(part_widths)

    @pl.when(pl.program_id(0) == 0)
    def _():
        if layer is None:
            whole = pltpu.make_async_copy(w_hbm.at[:, pl.ds(0, n_cols)], wb, sem.at[0])
            whole.start()
            whole.wait()
        else:
            _load_weight_bf16(w_hbm, layer, wb, stage, sem, k_rows, n_cols)

    if has_norm:
        u_ref[...] = _rms(x_refs[0][...], g_ref[...]).astype(BF16)
        lhs = [(u_ref, 0, k_rows)]
    else:
        lhs, off = [], 0
        for r, kw in zip(x_refs, part_widths):
            lhs.append((r, off, kw))
            off += kw
    for n0 in range(0, n_cols, n_chunk):
        acc = None
        for r, off, kw in lhs:
            d = jnp.dot(r[...], wb[off:off + kw, n0:n0 + n_chunk], preferred_element_type=F32)
            acc = d if acc is None else acc + d
        if has_res:
            acc = acc + res_ref[:, n0:n0 + n_chunk]
        if lead_scale is not None and n0 < lead_scale[0]:
            acc = acc * lead_scale[1]
        o_ref[:, n0:n0 + n_chunk] = acc.astype(o_ref.dtype)
    if has_extra:
        eo_ref[...] = jnp.dot(u_ref[...], ew_ref[...], preferred_element_type=F32)


def _dense(xs, w, layer, *, n_cols, out_dtype, gain=None, extra_w=None, res=None, tm=512, n_chunk=512,
           lead_scale=None, name):
    assert lead_scale is None or lead_scale[0] % n_chunk == 0
    rows = xs[0].shape[0]
    part_widths = tuple(x.shape[1] for x in xs)
    k_rows = sum(part_widths)
    has_norm, has_extra, has_res = gain is not None, extra_w is not None, res is not None
    row_spec = lambda width: pl.BlockSpec((tm, width), lambda i: (i, 0))
    full_spec = lambda a: pl.BlockSpec(a.shape, lambda i: (0, 0))
    args, in_specs = list(xs), [row_spec(kw) for kw in part_widths]
    if has_norm:
        args.append(gain.reshape(1, k_rows).astype(F32))
        in_specs.append(full_spec(args[-1]))
    args.append(w)
    in_specs.append(pl.BlockSpec(memory_space=pl.ANY))
    if has_extra:
        args.append(extra_w)
        in_specs.append(full_spec(extra_w))
    if has_res:
        args.append(res)
        in_specs.append(row_spec(n_cols))
    out_shape = [jax.ShapeDtypeStruct((rows, n_cols), out_dtype)]
    out_specs = [row_spec(n_cols)]
    if has_extra:
        out_shape.append(jax.ShapeDtypeStruct((rows, extra_w.shape[1]), F32))
        out_specs.append(row_spec(extra_w.shape[1]))
    stage_cols = LANES if layer is None else n_cols
    scratch = [pltpu.VMEM((k_rows, n_cols), BF16),
               pltpu.VMEM((2, W_STAGE_ROWS, stage_cols), F32),
               pltpu.SemaphoreType.DMA((2,))]
    if has_norm:
        scratch.append(pltpu.VMEM((tm, k_rows), BF16))
    body = functools.partial(_dense_body, part_widths=part_widths, has_norm=has_norm, has_extra=has_extra,
                             has_res=has_res, n_cols=n_cols, n_chunk=n_chunk, layer=layer, lead_scale=lead_scale)
    outs = pl.pallas_call(
        body, grid=(rows // tm,), in_specs=in_specs, out_specs=out_specs, out_shape=out_shape,
        scratch_shapes=scratch, compiler_params=_params(("arbitrary",)), name=name)(*args)
    return outs if has_extra else outs[0]


ATT_TQ = 512
LOG2_E = 1.4426950408889634
ATT_Q_SCALE = LOG2_E * HEAD_DIM ** -0.5


def _winattn_body(sink_ref, q_ref, kp_ref, km_ref, kn_ref, vp_ref, vm_ref, vn_ref, g_ref, o_ref,
                  kcat, vcat, obuf, *, seq):
    s0 = pl.program_id(1) * ATT_TQ
    kcat[0:WBLK, :] = kp_ref[0]
    kcat[WBLK:WBLK + ATT_TQ, :] = km_ref[0]
    kcat[WBLK + ATT_TQ:, :] = kn_ref[0]
    vcat[0:WBLK, :] = vp_ref[0]
    vcat[WBLK:WBLK + ATT_TQ, :] = vm_ref[0]
    vcat[WBLK + ATT_TQ:, :] = vn_ref[0]
    qi = lax.broadcasted_iota(jnp.int32, (WBLK, 3 * WBLK), 0) + WBLK
    ki = lax.broadcasted_iota(jnp.int32, (WBLK, 3 * WBLK), 1)
    dist_i = jnp.abs(ki - qi)
    neg_dist = jnp.where(dist_i <= WINDOW, -dist_i.astype(F32), NEG_INF)
    for qb in range(ATT_TQ // WBLK):
        kabs = s0 + (qb - 1) * WBLK + ki
        bias_unit = jnp.where((kabs >= 0) & (kabs < seq), neg_dist, NEG_INF)
        head_cols = lambda j: slice(j * HEAD_DIM, (j + 1) * HEAD_DIM)
        kv_cols = lambda j: head_cols(j // Q_PER_KV)
        key_rows = slice(qb * WBLK, (qb + 3) * WBLK)
        scores = []
        for j in range(N_Q_HEADS):
            slope = 2.0 ** (-8.0 * (j + 1) / N_Q_HEADS)
            q = q_ref[0, qb * WBLK:(qb + 1) * WBLK, head_cols(j)]
            scores.append(_nt(q, kcat[key_rows, kv_cols(j)]) + (slope * LOG2_E) * bias_unit)
        probs, denoms = [], []
        for j in range(N_Q_HEADS):
            sink = sink_ref[j] * LOG2_E
            m = jnp.maximum(jnp.max(scores[j], axis=-1, keepdims=True), sink)
            p = jnp.exp2(scores[j] - m)
            denoms.append(jnp.sum(p, axis=-1, keepdims=True) + jnp.exp2(sink - m))
            probs.append(p.astype(BF16))
        for j in range(N_Q_HEADS):
            pv = jnp.dot(probs[j], vcat[key_rows, kv_cols(j)], preferred_element_type=F32)
            obuf[qb, :, head_cols(j)] = pv / denoms[j]
        o_ref[0, qb * WBLK:(qb + 1) * WBLK, :] = _rms(obuf[qb], g_ref[...]).astype(o_ref.dtype)


def _window_attention(proj, sink_logit, gain, batch, seq):
    p3 = proj.reshape(batch, seq, MAIN_COLS)
    nq = seq // ATT_TQ
    per = ATT_TQ // WBLK
    last = seq // WBLK - 1
    main = lambda col: pl.BlockSpec((1, ATT_TQ, ATT_KV), lambda b, i: (b, i, col))
    prev = lambda col: pl.BlockSpec((1, WBLK, ATT_KV), lambda b, i: (b, jnp.maximum(i * per - 1, 0), col))
    nxt = lambda col: pl.BlockSpec((1, WBLK, ATT_KV), lambda b, i: (b, jnp.minimum(i * per + per, last), col))
    ck, cv = COL_KA // ATT_KV, COL_VA // ATT_KV
    return pl.pallas_call(
        functools.partial(_winattn_body, seq=seq),
        grid=(batch, nq),
        in_specs=[pl.BlockSpec(memory_space=pltpu.SMEM),
                  pl.BlockSpec((1, ATT_TQ, ATT_Q), lambda b, i: (b, i, 0)),
                  prev(ck), main(ck), nxt(ck), prev(cv), main(cv), nxt(cv),
                  pl.BlockSpec((1, ATT_Q), lambda b, i: (0, 0))],
        out_specs=pl.BlockSpec((1, ATT_TQ, ATT_Q), lambda b, i: (b, i, 0)),
        out_shape=jax.ShapeDtypeStruct((batch, seq, ATT_Q), BF16),
        scratch_shapes=[pltpu.VMEM((ATT_TQ + 2 * WBLK, ATT_KV), BF16),
                        pltpu.VMEM((ATT_TQ + 2 * WBLK, ATT_KV), BF16),
                        pltpu.VMEM((ATT_TQ // WBLK, WBLK, ATT_Q), F32)],
        compiler_params=_params(("parallel", "parallel")), name="window_attention",
    )(sink_logit.astype(F32), p3, p3, p3, p3, p3, p3, p3, gain.reshape(1, ATT_Q).astype(F32))


GLA_SCAN_ROWS = 256
GLA_SCAN_UNROLL = 2
GLA_EPI_ROWS = 512
GLA_UNROLL = 8


def _split2(x):
    hi = x.astype(BF16)
    lo = (x - hi.astype(F32)).astype(BF16)
    return hi, lo


def _gla_body(q_ref, k_ref, v_ref, r_ref, lr_ref, wf_ref, wb_ref, bf_ref, bb_ref, gn_ref, o_ref,
              cum_f, cum_b, acc, st_f, st_b, *, seq):
    C = GLA_CHUNK
    n_chunks = seq // C
    ri = lax.broadcasted_iota(jnp.int32, (GLA_SCAN_ROWS, GLA_SCAN_ROWS), 0)
    ci = lax.broadcasted_iota(jnp.int32, (GLA_SCAN_ROWS, GLA_SCAN_ROWS), 1)
    same = (ri // C) == (ci // C)
    tri_f = jnp.where(same & (ci <= ri), 1.0, 0.0).astype(BF16)
    tri_b = jnp.where(same & (ci >= ri), 1.0, 0.0).astype(BF16)

    def scan_body(t, carry):
        row_sets = [pl.ds(pl.multiple_of((t * GLA_SCAN_UNROLL + u) * GLA_SCAN_ROWS, GLA_SCAN_ROWS), GLA_SCAN_ROWS)
                    for u in range(GLA_SCAN_UNROLL)]
        lrs = [lr_ref[0, rows, :].astype(BF16) for rows in row_sets]
        zs = [jnp.dot(lr, w_ref[...], preferred_element_type=F32) + b_ref[...]
              for lr in lrs for w_ref, b_ref in ((wf_ref, bf_ref), (wb_ref, bb_ref))]
        gs = [_split2((jnp.minimum(z, 0.0) - jnp.log(1.0 + jnp.exp(-jnp.abs(z)))) * (LOG2_E / GLA_TAU)) for z in zs]
        for n, g3 in enumerate(gs):
            tri, dst = ((tri_f, cum_f), (tri_b, cum_b))[n % 2]
            dst[row_sets[n // 2], :] = sum(jnp.dot(tri, part, preferred_element_type=F32) for part in g3)
        return carry

    lax.fori_loop(0, seq // (GLA_SCAN_ROWS * GLA_SCAN_UNROLL), scan_body, 0)

    acc[...] = jnp.zeros_like(acc)
    st_f[...] = jnp.zeros_like(st_f)
    st_b[...] = jnp.zeros_like(st_b)
    rr = lax.broadcasted_iota(jnp.int32, (C, C), 0)
    cc = lax.broadcasted_iota(jnp.int32, (C, C), 1)
    scale = GLA_DK ** -0.5

    def chunk_body(i, carry):
        jobs = []
        for u in range(GLA_UNROLL):
            c = i * GLA_UNROLL + u
            jobs += [(c, cum_f, True), (n_chunks - 1 - c, cum_b, False)]
        prep = []
        for c, cum, forward in jobs:
            rows = pl.ds(pl.multiple_of(c * C, C), C)
            b = cum[rows, :]
            b_end = b[C - 1:C, :] if forward else b[0:1, :]
            q = q_ref[0, rows, :].astype(F32) * scale
            k = k_ref[0, rows, :].astype(F32)
            v = v_ref[0, rows, :]
            q_dec = (q * jnp.exp2(b)).astype(BF16)
            k_inc = (k * jnp.exp2(-b)).astype(BF16)
            k_dec = (k * jnp.exp2(b_end - b)).astype(BF16)
            prep.append((rows, v, q_dec, k_inc, k_dec, jnp.exp2(b_end)))
        attn = [_nt(q_dec, k_inc) for _, _, q_dec, k_inc, _, _ in prep]
        kv_t = [_tn(v, k_dec) for _, v, _, _, k_dec, _ in prep]
        state_t = {True: st_f[...], False: st_b[...]}
        o_inter = []
        for (_, _, forward), (_, _, q_dec, _, _, decay), kv in zip(jobs, prep, kv_t):
            o_inter.append(_nt(q_dec, state_t[forward].astype(BF16)))
            state_t[forward] = state_t[forward] * decay + kv
        st_f[...] = state_t[True]
        st_b[...] = state_t[False]
        for (_, _, forward), (rows, v, _, _, _, _), a, oi in zip(jobs, prep, attn, o_inter):
            a = jnp.where((cc <= rr) if forward else (cc >= rr), a, 0.0)
            o = jnp.dot(a.astype(BF16), v, preferred_element_type=F32) + oi
            acc[rows, :] = acc[rows, :] + o
        return carry

    lax.fori_loop(0, n_chunks // GLA_UNROLL, chunk_body, 0)

    def epi_body(t, carry):
        rows = pl.ds(pl.multiple_of(t * GLA_EPI_ROWS, GLA_EPI_ROWS), GLA_EPI_ROWS)
        r = r_ref[0, rows, :].astype(F32)
        o_ref[0, rows, :] = (_rms(acc[rows, :], gn_ref[...]) * (r * jax.nn.sigmoid(r))).astype(o_ref.dtype)
        return carry

    lax.fori_loop(0, seq // GLA_EPI_ROWS, epi_body, 0)


def _gla(proj, lr, w_gf, b_gf, w_gb, b_gb, gain, batch, seq):
    p3 = proj.reshape(batch, seq, MAIN_COLS)
    lr3 = lr.reshape(batch, seq, LANES)
    wf = jnp.zeros((LANES, GLA_QK), F32).at[:GLA_LOWRANK].set(w_gf).astype(BF16)
    wb = jnp.zeros((LANES, GLA_QK), F32).at[GLA_LOWRANK:2 * GLA_LOWRANK].set(w_gb).astype(BF16)
    seq_blk = lambda width, col0: pl.BlockSpec((1, seq, width), lambda b, h: (b, 0, col0 // width + h))
    head_w = pl.BlockSpec((LANES, GLA_DK), lambda b, h: (0, h))
    head_b = pl.BlockSpec((1, GLA_DK), lambda b, h: (0, h))
    return pl.pallas_call(
        functools.partial(_gla_body, seq=seq),
        grid=(batch, GLA_HEADS),
        in_specs=[seq_blk(GLA_DK, COL_QG), seq_blk(GLA_DK, COL_KG), seq_blk(GLA_DV, COL_VG),
                  seq_blk(GLA_DV, COL_RG),
                  pl.BlockSpec((1, seq, LANES), lambda b, h: (b, 0, 0)),
                  head_w, head_w, head_b, head_b,
                  pl.BlockSpec((1, GLA_DV), lambda b, h: (0, 0))],
        out_specs=pl.BlockSpec((1, seq, GLA_DV), lambda b, h: (b, 0, h)),
        out_shape=jax.ShapeDtypeStruct((batch, seq, GLA_V), BF16),
        scratch_shapes=[pltpu.VMEM((seq, GLA_DK), F32), pltpu.VMEM((seq, GLA_DK), F32),
                        pltpu.VMEM((seq, GLA_DV), F32),
                        pltpu.VMEM((GLA_DV, GLA_DK), F32), pltpu.VMEM((GLA_DV, GLA_DK), F32)],
        compiler_params=_params(("parallel", "parallel")), name="gla",
    )(p3, p3, p3, p3, lr3, wf, wb, b_gf.reshape(1, GLA_QK).astype(F32), b_gb.reshape(1, GLA_QK).astype(F32),
      gain.reshape(1, GLA_DV).astype(F32))


MEM_TM = 512
MEM_K_CHUNK = 512


def _mem_kv_body(m_ref, g_ref, wk_ref, wv_ref, k_ref, v_ref, u, acc_k, acc_v):
    step = pl.program_id(1)
    n_chunks = D_MODEL // MEM_K_CHUNK

    @pl.when(step == 0)
    def _():
        normed = _rms(m_ref[...], g_ref[...]).astype(BF16)
        for c in range(n_chunks):
            u[c] = normed[:, c * MEM_K_CHUNK:(c + 1) * MEM_K_CHUNK]
        acc_k[...] = jnp.zeros_like(acc_k)
        acc_v[...] = jnp.zeros_like(acc_v)

    lhs = u[step]
    acc_k[...] = acc_k[...] + jnp.dot(lhs, wk_ref[...].astype(BF16), preferred_element_type=F32)
    acc_v[...] = acc_v[...] + jnp.dot(lhs, wv_ref[...].astype(BF16), preferred_element_type=F32)

    @pl.when(step == n_chunks - 1)
    def _():
        k_ref[...] = acc_k[...].astype(k_ref.dtype)
        v_ref[...] = acc_v[...].astype(v_ref.dtype)


def _mem_kv(mem_rows, gain, w_ck, w_cv, layer):
    rows = mem_rows.shape[0]
    n_chunks = D_MODEL // MEM_K_CHUNK
    tile = pl.BlockSpec((MEM_TM, D_MODEL), lambda i, c: (i, 0))
    w_spec = pl.BlockSpec((None, MEM_K_CHUNK, D_MODEL), lambda i, c: (layer, c, 0))
    g2 = gain.reshape(1, D_MODEL).astype(F32)
    out = jax.ShapeDtypeStruct((rows, D_MODEL), BF16)
    return pl.pallas_call(
        _mem_kv_body, grid=(rows // MEM_TM, n_chunks),
        in_specs=[tile, pl.BlockSpec((1, D_MODEL), lambda i, c: (0, 0)), w_spec, w_spec],
        out_specs=[tile, tile], out_shape=[out, out],
        scratch_shapes=[pltpu.VMEM((n_chunks, MEM_TM, MEM_K_CHUNK), BF16),
                        pltpu.VMEM((MEM_TM, D_MODEL), F32), pltpu.VMEM((MEM_TM, D_MODEL), F32)],
        compiler_params=_params(("parallel", "arbitrary")), name="mem_kv_proj")(mem_rows, g2, w_ck, w_cv)


XATT_TQ = 512


XATT_CHUNK = 512


def _cross_body(h_ref, g_ref, wq_hbm, k_ref, v_ref, wo_hbm, gr_ref, wr_ref, br_ref,
                o_ref, hn_ref, gate_ref, order_ref, count_ref,
                wqb, wob, stage, sem, u_scr, q_scr, a_scr, prev, *, layer, n_tokens):
    step = pl.program_id(0)

    @pl.when(step == 0)
    def _():
        _load_weight_bf16(wq_hbm, layer, wqb, stage, sem, D_MODEL, D_MODEL)
        _load_weight_bf16(wo_hbm, layer, wob, stage, sem, D_MODEL, D_MODEL)
        prev[...] = jnp.zeros_like(prev)

    router = _route_tile(prev[...], gr_ref[...], wr_ref, br_ref, hn_ref, gate_ref, order_ref, count_ref,
                         jnp.maximum(step - 1, 0), n_tokens)

    scale = LOG2_E * X_HEAD_DIM ** -0.5
    heads = [slice(h * X_HEAD_DIM, (h + 1) * X_HEAD_DIM) for h in range(X_HEADS)]
    chunks = [slice(n0, n0 + XATT_CHUNK) for n0 in range(0, D_MODEL, XATT_CHUNK)]

    def norm(rows):
        u_scr[rows, :] = _rms(h_ref[rows, :], g_ref[...]).astype(BF16)

    def q_proj(rows):
        for cols in chunks:
            q = jnp.dot(u_scr[rows, :], wqb[:, cols], preferred_element_type=F32)
            q_scr[rows, cols] = (q * scale).astype(BF16)

    def scores(rows):
        return [_nt(q_scr[rows, hd], k_ref[0, :, hd]) for hd in heads]

    def softmax(s_list):
        out = []
        for s in s_list:
            p = jnp.exp2(s - jnp.max(s, axis=-1, keepdims=True))
            out.append((p.astype(BF16), jnp.sum(p, axis=-1, keepdims=True)))
        return out

    def attend(rows, probs):
        for hd, (p, denom) in zip(heads, probs):
            a_scr[rows, hd] = (jnp.dot(p, v_ref[0, :, hd], preferred_element_type=F32) / denom).astype(BF16)

    def o_proj(rows):
        for cols in chunks:
            out = h_ref[rows, cols] + jnp.dot(a_scr[rows, :], wob[:, cols], preferred_element_type=F32)
            o_ref[rows, cols] = out
            prev[rows, cols] = out

    half = XATT_TQ // 2
    first, second = slice(0, half), slice(half, XATT_TQ)
    norm(first)
    q_proj(first)
    next(router)
    norm(second)
    s_first = scores(first)
    q_proj(second)
    next(router)
    p_first = softmax(s_first)
    attend(first, p_first)
    s_second = scores(second)
    o_proj(first)
    for _ in router:
        pass
    p_second = softmax(s_second)
    attend(second, p_second)
    o_proj(second)


def _cross_block(h, gain, w_cq, kx, vx, w_co, router_gain, router_w, router_b, layer, batch, seq, mem_len):
    rows = h.shape[0]
    n_tiles = rows // XATT_TQ
    tiles_per_batch = seq // XATT_TQ
    k3 = kx.reshape(batch, mem_len, D_MODEL)
    v3 = vx.reshape(batch, mem_len, D_MODEL)
    as_row = lambda g: g.reshape(1, D_MODEL).astype(F32)
    cur = lambda i: jnp.minimum(i, n_tiles - 1)
    lag = lambda i: jnp.maximum(i - 1, 0)
    tile = pl.BlockSpec((XATT_TQ, D_MODEL), lambda i: (cur(i), 0))
    mem_spec = pl.BlockSpec((1, mem_len, D_MODEL), lambda i: (cur(i) // tiles_per_batch, 0, 0))
    full = lambda a: pl.BlockSpec(a.shape, lambda i: (0, 0))
    lag_rows = lambda width: pl.BlockSpec((XATT_TQ, width), lambda i: (lag(i), 0))
    lag_tile = lambda sublanes: pl.BlockSpec((1, sublanes, LANES), lambda i: (lag(i), 0, 0))
    assert TOP_K * XATT_TQ == ID_SPLIT * ID_SPLIT, "the router's sort grid holds one tile's assignments"
    hbm = pl.BlockSpec(memory_space=pl.ANY)
    g_cross, g_router = as_row(gain), as_row(router_gain)
    return pl.pallas_call(
        functools.partial(_cross_body, layer=layer, n_tokens=rows), grid=(n_tiles + 1,),
        in_specs=[tile, full(g_cross), hbm, mem_spec, mem_spec, hbm, full(g_router), full(router_w), full(router_b)],
        out_specs=[tile, lag_rows(HALF), lag_rows(LANES), lag_tile(ID_SPLIT), lag_tile(8)],
        out_shape=[jax.ShapeDtypeStruct((rows, D_MODEL), F32),
                   jax.ShapeDtypeStruct((rows, HALF), jnp.uint32),
                   jax.ShapeDtypeStruct((rows, LANES), F32),
                   jax.ShapeDtypeStruct((n_tiles, ID_SPLIT, LANES), jnp.int32),
                   jax.ShapeDtypeStruct((n_tiles, 8, LANES), jnp.int32)],
        scratch_shapes=[pltpu.VMEM((D_MODEL, D_MODEL), BF16), pltpu.VMEM((D_MODEL, D_MODEL), BF16),
                        pltpu.VMEM((2, W_STAGE_ROWS, D_MODEL), F32), pltpu.SemaphoreType.DMA((2,)),
                        pltpu.VMEM((XATT_TQ, D_MODEL), BF16), pltpu.VMEM((XATT_TQ, D_MODEL), BF16),
                        pltpu.VMEM((XATT_TQ, D_MODEL), BF16), pltpu.VMEM((XATT_TQ, D_MODEL), F32)],
        compiler_params=_params(("arbitrary",)), name="cross_block",
    )(h, g_cross, w_cq, k3, v3, w_co, g_router, router_w, router_b)


HALF = D_MODEL // 2


def _pack_halves(x):
    return pltpu.pack_elementwise([x[:, :HALF], x[:, HALF:]], packed_dtype=BF16)


def _unpack_halves(words):
    return [pltpu.unpack_elementwise(words, index=i, packed_dtype=BF16, unpacked_dtype=F32) for i in range(2)]


ID_SPLIT = 32


def _route_tile(h, gain, w_ref, b_ref, hn_ref, gate_ref, order_ref, count_ref, tile_idx, n_tokens):
    hn = _rms(h, gain)
    hn_ref[...] = _pack_halves(hn)
    logits = jnp.dot(hn.astype(BF16), w_ref[...], preferred_element_type=F32) + b_ref[...]
    yield
    lane = lax.broadcasted_iota(jnp.int32, logits.shape, 1).astype(F32)
    ninf = -jnp.inf
    first = lambda hit: jnp.min(jnp.where(hit, lane, float(LANES)), axis=-1, keepdims=True)
    in_groups = lane < N_GROUPS
    gl = jnp.where(in_groups, logits, ninf)
    gmax = jnp.max(gl, axis=-1, keepdims=True)
    g_idx = first(gl == gmax)
    p_group = 1.0 / jnp.sum(jnp.where(in_groups, jnp.exp(logits - gmax), 0.0), axis=-1, keepdims=True)
    lo = N_GROUPS + EXPERTS_PER_GROUP * g_idx
    el = jnp.where((lane >= lo) & (lane < lo + EXPERTS_PER_GROUP), logits, ninf)
    e1 = jnp.max(el, axis=-1, keepdims=True)
    i1 = first(el == e1)
    el2 = jnp.where(lane == i1, ninf, el)
    e2 = jnp.max(el2, axis=-1, keepdims=True)
    i2 = first(el2 == e2)
    t = jnp.exp(e2 - e1)
    w1 = p_group / (1.0 + t)
    w2 = p_group * t / (1.0 + t)
    gate_ref[...] = jnp.where(lane == 0, w1, jnp.where(lane == 1, w2, 0.0))

    tm = logits.shape[0]
    hit0, hit1 = lane == i1 - N_GROUPS, lane == i2 - N_GROUPS
    member = jnp.where(hit0 | hit1, 1.0, 0.0)
    member_b = member.astype(BF16)
    earlier = (lax.broadcasted_iota(jnp.int32, (tm, tm), 1) < lax.broadcasted_iota(jnp.int32, (tm, tm), 0))
    rank = jnp.dot(jnp.where(earlier, 1.0, 0.0).astype(BF16), member_b, preferred_element_type=F32)
    lower = (lax.broadcasted_iota(jnp.int32, (LANES, LANES), 0) < lax.broadcasted_iota(jnp.int32, (LANES, LANES), 1))
    run_start = jnp.sum(jnp.dot(member_b, jnp.where(lower, 1.0, 0.0).astype(BF16), preferred_element_type=F32),
                        axis=0, keepdims=True)
    yield
    pos = rank + run_start
    positions = [jnp.sum(jnp.where(hit, pos, 0.0), axis=-1, keepdims=True) for hit in (hit0, hit1)]
    local_tok = lax.broadcasted_iota(jnp.int32, (tm, 1), 0)
    grids = [None, None]
    for s, p in enumerate(positions):
        p_hi = jnp.floor(p / ID_SPLIT)
        at_hi = jnp.where(lane == p_hi, 1.0, 0.0)
        at_lo = jnp.where(lane == p - ID_SPLIT * p_hi, 1.0, 0.0).astype(BF16)
        a = s * tm + local_tok
        for d, digit in enumerate((a // ID_SPLIT, a % ID_SPLIT)):
            part = _tn((at_hi * digit.astype(F32)).astype(BF16), at_lo)
            grids[d] = part if grids[d] is None else grids[d] + part
    a_sorted = (grids[0] * ID_SPLIT + grids[1]).astype(jnp.int32)
    dst = (a_sorted // tm) * n_tokens + tile_idx * tm + a_sorted % tm
    order_ref[0] = dst[:ID_SPLIT, :]
    count_ref[0] =jnp.broadcast_to(jnp.sum(member, axis=0, keepdims=True), (8, LANES)).astype(jnp.int32)


def _router_params(w_rg, b_rg, w_re, b_re):
    n_log = N_GROUPS + N_EXPERTS
    w = jnp.zeros((D_MODEL, LANES), F32).at[:, :N_GROUPS].set(w_rg).at[:, N_GROUPS:n_log].set(w_re).astype(BF16)
    b = jnp.zeros((1, LANES), F32).at[0, :N_GROUPS].set(b_rg).at[0, N_GROUPS:n_log].set(b_re)
    return w, b


def _expert_changed(blk_e, blk):
    prev = blk_e[jnp.maximum(blk - 1, 0)]
    return (blk == 0) | (blk_e[blk] != prev)


WEIGHT_DMA_PRIORITY = 1


def _expert_weight_copies(layer, w_hbms, stages, sem):
    def copies(e):
        return [pltpu.make_async_copy(w.at[layer, e], st, sem.at[i])
                for i, (w, st) in enumerate(zip(w_hbms, stages))]
    return copies


GATHER_AHEAD = 3
GATHER_SLOTS = GATHER_AHEAD + 1
MOE_CHUNKS = 2


SCATTER_SLOTS = 3


IDX_ROWS = 8
IDX_DST = GATHER_AHEAD + 1
IDX_DST_PREV = GATHER_AHEAD + 2


def _experts_body(blk_e, n_real, nxt_e, idx_ref, hn_hbm, wg_hbm, wu_hbm, wd_hbm, out_hbm, xs, xb, hid, ys,
                  wg_stage, wu_stage, wd_stage, wgb, wub, wdb, gsem, ssem, wsem, *, n_assign, layer):
    blk = pl.program_id(0)
    dummy_slot = SCATTER_SLOTS - 1

    def gather_row(idx_row, slot, r):
        return pltpu.make_async_copy(hn_hbm.at[pl.ds(idx_ref[0, idx_row, r], 1), :], xs.at[slot, pl.ds(r, 1), :],
                                     gsem.at[slot])

    def scatter_row(idx_row, slot, r):
        return pltpu.make_async_copy(ys.at[slot, pl.ds(r, 1), :], out_hbm.at[pl.ds(idx_ref[0, idx_row, r], 1), :],
                                     ssem.at[slot])

    def wait_gather(slot):
        pltpu.make_async_copy(hn_hbm.at[pl.ds(0, MOE_BLOCK), :], xs.at[slot], gsem.at[slot]).wait()

    def wait_scatter(slot):
        pltpu.make_async_copy(ys.at[slot], out_hbm.at[pl.ds(0, MOE_BLOCK), :], ssem.at[slot]).wait()

    @pl.when(blk == 0)
    def _():
        def body(r, carry):
            for s in range(GATHER_AHEAD):
                gather_row(s, s, r).start()
            return carry
        lax.fori_loop(0, MOE_BLOCK, body, 0)
        ys[dummy_slot] = jnp.zeros((MOE_BLOCK, HALF), jnp.uint32)
        for s in range(SCATTER_SLOTS):
            trash = pltpu.make_async_copy(
                ys.at[dummy_slot], out_hbm.at[pl.ds(n_assign + s * MOE_BLOCK, MOE_BLOCK), :], ssem.at[dummy_slot])
            trash.start()
            trash.wait()

    @pl.when(blk < n_real[0])
    def _():
        in_slot = blk % GATHER_SLOTS
        ahead_slot = (blk + GATHER_AHEAD) % GATHER_SLOTS
        out_slot = blk % SCATTER_SLOTS
        prev_slot = (blk + SCATTER_SLOTS - 1) % SCATTER_SLOTS

        copies = _expert_weight_copies(layer, (wg_hbm, wu_hbm, wd_hbm), (wg_stage, wu_stage, wd_stage), wsem)
        first_of_expert = _expert_changed(blk_e, blk)

        @pl.when(blk == 0)
        def _():
            for cp in copies(blk_e[0]):
                cp.start(priority=WEIGHT_DMA_PRIORITY)

        wait_gather(in_slot)

        @pl.when(blk >= SCATTER_SLOTS - 1)
        def _():
            wait_scatter(out_slot)

        def compute(fresh_weights):
            lo, hi = _unpack_halves(xs[in_slot])
            xb[:, :HALF] = lo.astype(BF16)
            xb[:, HALF:] = hi.astype(BF16)
            for r in range(MOE_BLOCK):
                scatter_row(IDX_DST_PREV, prev_slot, r).start(priority=r % 2)
                gather_row(GATHER_AHEAD, ahead_slot, r).start()
            cw = D_FF_EXPERT // MOE_CHUNKS
            for c in range(MOE_CHUNKS):
                cols = slice(c * cw, (c + 1) * cw)
                if fresh_weights:
                    wgb[:, cols] = wg_stage[:, cols].astype(BF16)
                    wub[:, cols] = wu_stage[:, cols].astype(BF16)
                a = jnp.dot(xb[...], wgb[:, cols], preferred_element_type=F32)
                u = jnp.dot(xb[...], wub[:, cols], preferred_element_type=F32)
                hid[:, cols] = (a * jax.nn.sigmoid(a) * u).astype(BF16)
            cw = HALF // MOE_CHUNKS
            for c in range(MOE_CHUNKS):
                cols = slice(c * cw, (c + 1) * cw)
                hi_cols = slice(HALF + c * cw, HALF + (c + 1) * cw)
                if fresh_weights:
                    wdb[:, cols] = wd_stage[:, cols].astype(BF16)
                    wdb[:, hi_cols] = wd_stage[:, hi_cols].astype(BF16)
                ys[out_slot, :, cols] = pltpu.pack_elementwise(
                    [jnp.dot(hid[...], wdb[:, cols], preferred_element_type=F32),
                     jnp.dot(hid[...], wdb[:, hi_cols], preferred_element_type=F32)], packed_dtype=BF16)

        @pl.when(first_of_expert)
        def _():
            for cp in copies(blk_e[blk]):
                cp.wait()
            compute(True)

            @pl.when(nxt_e[blk] >= 0)
            def _():
                for cp in copies(nxt_e[blk]):
                    cp.start(priority=WEIGHT_DMA_PRIORITY)

        @pl.when(jnp.logical_not(first_of_expert))
        def _():
            compute(False)

        @pl.when(blk == n_real[0] - 1)
        def _():
            def body(r, carry):
                scatter_row(IDX_DST, out_slot, r).start()
                return carry
            lax.fori_loop(0, MOE_BLOCK, body, 0)
            for s in range(1, GATHER_SLOTS):
                wait_gather((blk + s) % GATHER_SLOTS)
            for s in range(SCATTER_SLOTS):
                wait_scatter(s)


def _moe_forward(hn, order, counts, w_gate, w_up, w_down, layer, n_tokens):
    A = n_tokens * TOP_K
    out_rows = A + SCATTER_SLOTS * MOE_BLOCK
    R = A + N_EXPERTS * MOE_BLOCK
    n_blk = R // MOE_BLOCK
    i32 = jnp.int32
    n_tiles, per_tile = order.shape[0], ID_SPLIT * ID_SPLIT
    experts = jnp.arange(N_EXPERTS, dtype=i32)
    n = counts[:, 0, :N_EXPERTS]
    total = jnp.sum(n, axis=0)
    padded = ((total + MOE_BLOCK - 1) // MOE_BLOCK) * MOE_BLOCK
    pend = jnp.cumsum(padded)
    pstart = pend - padded
    n_real = (pend[-1] // MOE_BLOCK).astype(i32).reshape(1)
    blk = jnp.arange(n_blk, dtype=i32)
    blk_e = jnp.minimum(jnp.sum((pend[None, :] <= (blk * MOE_BLOCK)[:, None]).astype(i32), axis=1), N_EXPERTS - 1)
    is_e = (blk_e[:, None] == experts[None, :]).astype(i32)
    of_block = lambda per_expert: jnp.sum(is_e * per_expert[None, :], axis=1)
    cum_incl = jnp.cumsum(n, axis=0)
    run_shift = (jnp.cumsum(n, axis=1) - n) - (cum_incl - n)
    per_tile_of_block = lambda m: jnp.sum(is_e[:, None, :] * m[None, :, :], axis=2)
    cum_b, shift_b = per_tile_of_block(cum_incl), per_tile_of_block(run_shift)
    in_blk = jnp.arange(MOE_BLOCK, dtype=i32)
    k = (blk * MOE_BLOCK - of_block(pstart))[:, None] + in_blk[None, :]
    tile_of = jnp.minimum(jnp.sum((cum_b[:, None, :] <= k[:, :, None]).astype(i32), axis=2), n_tiles - 1)
    is_tile = (tile_of[:, :, None] == jnp.arange(n_tiles, dtype=i32)[None, None, :]).astype(i32)
    entry = tile_of * per_tile + k + jnp.sum(is_tile * shift_b[:, None, :], axis=2)
    pad_entry = A + (blk % SCATTER_SLOTS)[:, None] * MOE_BLOCK + in_blk[None, :]
    entry = jnp.where(k < of_block(total)[:, None], entry, pad_entry)
    n_pad = SCATTER_SLOTS * MOE_BLOCK
    dst_table = jnp.concatenate([order[:, :, :ID_SPLIT].reshape(A), A + jnp.arange(n_pad, dtype=i32)])
    row_dst = dst_table[entry]
    row_tok = row_dst % n_tokens
    row_dst_prev = jnp.concatenate([(A + (SCATTER_SLOTS - 1) * MOE_BLOCK + in_blk)[None, :], row_dst[:-1]], axis=0)
    last = n_real[0] - 1
    last_tok = lax.dynamic_slice_in_dim(row_tok, last, 1, axis=0)
    ahead = [jnp.where((blk + s <= last)[:, None],
                       jnp.concatenate([row_tok[s:], jnp.zeros((s, MOE_BLOCK), i32)], axis=0), last_tok)
             for s in range(GATHER_AHEAD + 1)]
    idx_rows = ahead + [row_dst, row_dst_prev]
    idx_rows += [jnp.zeros_like(row_dst)] * (IDX_ROWS - len(idx_rows))
    idx_all = jnp.stack(idx_rows, axis=1)

    run_end = pend[blk_e] // MOE_BLOCK
    nxt_e = jnp.where(run_end < n_real[0], blk_e[jnp.minimum(run_end, n_blk - 1)], -1).astype(jnp.int32)

    idx_spec = pl.BlockSpec((1, IDX_ROWS, MOE_BLOCK), lambda i, be, nr, nx: (jnp.minimum(i, nr[0] - 1), 0, 0),
                            memory_space=pltpu.SMEM)
    hbm = pl.BlockSpec(memory_space=pl.ANY)
    up_shape, down_shape = (D_MODEL, D_FF_EXPERT), (D_FF_EXPERT, D_MODEL)
    return pl.pallas_call(
        functools.partial(_experts_body, n_assign=A, layer=layer),
        grid_spec=pltpu.PrefetchScalarGridSpec(
            num_scalar_prefetch=3, grid=(n_blk,),
            in_specs=[idx_spec] + [hbm] * 4,
            out_specs=hbm,
            scratch_shapes=[pltpu.VMEM((GATHER_SLOTS, MOE_BLOCK, HALF), jnp.uint32),
                            pltpu.VMEM((MOE_BLOCK, D_MODEL), BF16),
                            pltpu.VMEM((MOE_BLOCK, D_FF_EXPERT), BF16),
                            pltpu.VMEM((SCATTER_SLOTS, MOE_BLOCK, HALF), jnp.uint32),
                            pltpu.VMEM(up_shape, F32), pltpu.VMEM(up_shape, F32), pltpu.VMEM(down_shape, F32),
                            pltpu.VMEM(up_shape, BF16), pltpu.VMEM(up_shape, BF16), pltpu.VMEM(down_shape, BF16),
                            pltpu.SemaphoreType.DMA((GATHER_SLOTS,)), pltpu.SemaphoreType.DMA((SCATTER_SLOTS,)),
                            pltpu.SemaphoreType.DMA((3,))]),
        out_shape=jax.ShapeDtypeStruct((out_rows, HALF), jnp.uint32),
        compiler_params=_params(("arbitrary",)), name="moe_experts",
    )(blk_e, n_real, nxt_e, idx_all, hn, w_gate, w_up, w_down)


FINAL_TM = 512


def _final_body(h_ref, y0_ref, y1_ref, gate_ref, g_ref, o_ref):
    gate = gate_ref[...]
    y0, y1 = _unpack_halves(y0_ref[...]), _unpack_halves(y1_ref[...])
    halves = [h_ref[:, i * HALF:(i + 1) * HALF] + gate[:, 0:1] * y0[i] + gate[:, 1:2] * y1[i] for i in range(2)]
    mean_sq = sum(jnp.sum(hh * hh, axis=-1, keepdims=True) for hh in halves) / D_MODEL
    inv = lax.rsqrt(mean_sq + RMS_EPS)
    for i, hh in enumerate(halves):
        o_ref[:, i * HALF:(i + 1) * HALF] = hh * inv * g_ref[:, i * HALF:(i + 1) * HALF]


def _final(h, ys, gate, gain):
    rows = h.shape[0]
    n_tiles = rows // FINAL_TM
    tile = lambda width: pl.BlockSpec((FINAL_TM, width), lambda i: (i, 0))
    g2 = gain.reshape(1, D_MODEL).astype(F32)
    return pl.pallas_call(
        _final_body, grid=(n_tiles,),
        in_specs=[tile(D_MODEL), tile(HALF), pl.BlockSpec((FINAL_TM, HALF), lambda i: (i + n_tiles, 0)),
                  tile(LANES), pl.BlockSpec((1, D_MODEL), lambda i: (0, 0))],
        out_specs=tile(D_MODEL), out_shape=jax.ShapeDtypeStruct((rows, D_MODEL), F32),
        compiler_params=_params(("parallel",)), name="combine_final_norm")(h, ys, ys, gate, g2)


def kernel(x, mem, norm_mix, w_in, attn_out_norm, sink_logit, w_gla_gf, b_gla_gf, w_gla_gb, b_gla_gb, gla_out_norm, w_out, norm_cross, norm_mem, w_cq, w_ck, w_cv, w_co, norm_ffn, w_router_group, b_router_group, w_router_expert, b_router_expert, w_gate, w_up, w_down, norm_final):
    batch, seq, _ = x.shape
    mem_len = mem.shape[1]
    n_tokens = batch * seq
    h = x.reshape(n_tokens, D_MODEL)
    memf = mem.reshape(batch * mem_len, D_MODEL)
    assert norm_mix.shape[0] == 1, "the combine step is fused with the final norm: single-layer stacks only"
    for l in range(norm_mix.shape[0]):
        w_lr = jnp.zeros((D_MODEL, LANES), F32).at[:, :2 * GLA_LOWRANK].set(w_in[l][:, MAIN_COLS:]).astype(BF16)
        w_main = w_in[l].astype(BF16)
        proj, lr = _dense([h], w_main, None, n_cols=MAIN_COLS, out_dtype=BF16, gain=norm_mix[l], extra_w=w_lr,
                          lead_scale=(ATT_Q, ATT_Q_SCALE), name="in_proj")
        o_a = _window_attention(proj, sink_logit[l], attn_out_norm[l], batch, seq).reshape(n_tokens, ATT_Q)
        o_g = _gla(proj, lr, w_gla_gf[l], b_gla_gf[l], w_gla_gb[l], b_gla_gb[l], gla_out_norm[l],
                   batch, seq).reshape(n_tokens, GLA_V)
        h = _dense([o_a, o_g], w_out, l, n_cols=D_MODEL, out_dtype=F32, res=h, name="out_proj")
        kx, vx = _mem_kv(memf, norm_mem[l], w_ck, w_cv, l)
        router_w, router_b = _router_params(w_router_group[l], b_router_group[l],
                                            w_router_expert[l], b_router_expert[l])
        h, hn, gate, order, counts = _cross_block(h, norm_cross[l], w_cq, kx, vx, w_co, norm_ffn[l],
                                                  router_w, router_b, l, batch, seq, mem_len)
        ys = _moe_forward(hn, order, counts, w_gate, w_up, w_down, l, n_tokens)
    return _final(h, ys, gate, norm_final).reshape(batch, seq, D_MODEL)
```

```python
import functools

import jax
import jax.numpy as jnp
from jax import lax
from jax.experimental import pallas as pl
from jax.experimental.pallas import tpu as pltpu

F32 = jnp.float32
BF16 = jnp.bfloat16

D_MODEL = 2048
N_Q_HEADS = 8
N_KV_HEADS = 2
Q_PER_KV = N_Q_HEADS // N_KV_HEADS
HEAD_DIM = 128
WINDOW = 128
WBLK = 128
GLA_HEADS = 4
GLA_DK = 128
GLA_DV = 256
GLA_LOWRANK = 16
GLA_TAU = 16.0
GLA_CHUNK = 64
ATT_Q = N_Q_HEADS * HEAD_DIM
ATT_KV = N_KV_HEADS * HEAD_DIM
GLA_QK = GLA_HEADS * GLA_DK
GLA_V = GLA_HEADS * GLA_DV
MAIN_COLS = ATT_Q + 2 * ATT_KV + 2 * GLA_QK + 2 * GLA_V
COL_KA = ATT_Q
COL_VA = COL_KA + ATT_KV
COL_QG = COL_VA + ATT_KV
COL_KG = COL_QG + GLA_QK
COL_VG = COL_KG + GLA_QK
COL_RG = COL_VG + GLA_V
X_HEADS = 4
X_HEAD_DIM = D_MODEL // X_HEADS
N_GROUPS = 4
EXPERTS_PER_GROUP = 8
N_EXPERTS = N_GROUPS * EXPERTS_PER_GROUP
TOP_K = 2
D_FF_EXPERT = D_MODEL // 2
MOE_BLOCK = 256
RMS_EPS = 1e-6
NEG_INF = -1e30

LANES = 128
VMEM_LIMIT = 56 * 1024 * 1024
W_STAGE_ROWS = 128


def _params(sem):
    return pltpu.CompilerParams(dimension_semantics=sem, vmem_limit_bytes=VMEM_LIMIT)


def _nt(a, b):
    return lax.dot_general(a, b, (((1,), (1,)), ((), ())), preferred_element_type=F32)


def _tn(a, b):
    return lax.dot_general(a, b, (((0,), (0,)), ((), ())), preferred_element_type=F32)


def _rms(x, gain):
    return x * lax.rsqrt(jnp.mean(x * x, axis=-1, keepdims=True) + RMS_EPS) * gain


def _load_weight_bf16(w_hbm, layer, wb, stage, sem, k_rows, n_cols):
    n_chunks = k_rows // W_STAGE_ROWS

    def copy(c):
        return pltpu.make_async_copy(
            w_hbm.at[layer, pl.ds(c * W_STAGE_ROWS, W_STAGE_ROWS), pl.ds(0, n_cols)],
            stage.at[c % 2], sem.at[c % 2])

    copy(0).start()
    for c in range(n_chunks):
        if c + 1 < n_chunks:
            copy(c + 1).start()
        copy(c).wait()
        wb[c * W_STAGE_ROWS:(c + 1) * W_STAGE_ROWS, :] = stage[c % 2].astype(BF16)


def _dense_body(*refs, part_widths, has_norm, has_extra, has_res, n_cols, n_chunk, layer, lead_scale):
    it = iter(refs)
    x_refs = [next(it) for _ in part_widths]
    g_ref = next(it) if has_norm else None
    w_hbm = next(it)
    ew_ref = next(it) if has_extra else None
    res_ref = next(it) if has_res else None
    o_ref = next(it)
    eo_ref = next(it) if has_extra else None
    wb, stage, sem = next(it), next(it), next(it)
    u_ref = next(it) if has_norm else None
    k_rows = sum(part_widths)

    @pl.when(pl.program_id(0) == 0)
    def _():
        if layer is None:
            whole = pltpu.make_async_copy(w_hbm.at[:, pl.ds(0, n_cols)], wb, sem.at[0])
            whole.start()
            whole.wait()
        else:
            _load_weight_bf16(w_hbm, layer, wb, stage, sem, k_rows, n_cols)

    if has_norm:
        u_ref[...] = _rms(x_refs[0][...], g_ref[...]).astype(BF16)
        lhs = [(u_ref, 0, k_rows)]
    else:
        lhs, off = [], 0
        for r, kw in zip(x_refs, part_widths):
            lhs.append((r, off, kw))
            off += kw
    for n0 in range(0, n_cols, n_chunk):
        acc = None
        for r, off, kw in lhs:
            d = jnp.dot(r[...], wb[off:off + kw, n0:n0 + n_chunk], preferred_element_type=F32)
            acc = d if acc is None else acc + d
        if has_res:
            acc = acc + res_ref[:, n0:n0 + n_chunk]
        if lead_scale is not None and n0 < lead_scale[0]:
            acc = acc * lead_scale[1]
        o_ref[:, n0:n0 + n_chunk] = acc.astype(o_ref.dtype)
    if has_extra:
        eo_ref[...] = jnp.dot(u_ref[...], ew_ref[...], preferred_element_type=F32)


def _dense(xs, w, layer, *, n_cols, out_dtype, gain=None, extra_w=None, res=None, tm=512, n_chunk=512,
           lead_scale=None, name):
    assert lead_scale is None or lead_scale[0] % n_chunk == 0
    rows = xs[0].shape[0]
    part_widths = tuple(x.shape[1] for x in xs)
    k_rows = sum(part_widths)
    has_norm, has_extra, has_res = gain is not None, extra_w is not None, res is not None
    row_spec = lambda width: pl.BlockSpec((tm, width), lambda i: (i, 0))
    full_spec = lambda a: pl.BlockSpec(a.shape, lambda i: (0, 0))
    args, in_specs = list(xs), [row_spec(kw) for kw in part_widths]
    if has_norm:
        args.append(gain.reshape(1, k_rows).astype(F32))
        in_specs.append(full_spec(args[-1]))
    args.append(w)
    in_specs.append(pl.BlockSpec(memory_space=pl.ANY))
    if has_extra:
        args.append(extra_w)
        in_specs.append(full_spec(extra_w))
    if has_res:
        args.append(res)
        in_specs.append(row_spec(n_cols))
    out_shape = [jax.ShapeDtypeStruct((rows, n_cols), out_dtype)]
    out_specs = [row_spec(n_cols)]
    if has_extra:
        out_shape.append(jax.ShapeDtypeStruct((rows, extra_w.shape[1]), F32))
        out_specs.append(row_spec(extra_w.shape[1]))
    stage_cols = LANES if layer is None else n_cols
    scratch = [pltpu.VMEM((k_rows, n_cols), BF16),
               pltpu.VMEM((2, W_STAGE_ROWS, stage_cols), F32),
               pltpu.SemaphoreType.DMA((2,))]
    if has_norm:
        scratch.append(pltpu.VMEM((tm, k_rows), BF16))
    body = functools.partial(_dense_body, part_widths=part_widths, has_norm=has_norm, has_extra=has_extra,
                             has_res=has_res, n_cols=n_cols, n_chunk=n_chunk, layer=layer, lead_scale=lead_scale)
    outs = pl.pallas_call(
        body, grid=(rows // tm,), in_specs=in_specs, out_specs=out_specs, out_shape=out_shape,
        scratch_shapes=scratch, compiler_params=_params(("arbitrary",)), name=name)(*args)
    return outs if has_extra else outs[0]


ATT_TQ = 512
LOG2_E = 1.4426950408889634
ATT_Q_SCALE = LOG2_E * HEAD_DIM ** -0.5


def _winattn_body(sink_ref, q_ref, kp_ref, km_ref, kn_ref, vp_ref, vm_ref, vn_ref, g_ref, o_ref,
                  kcat, vcat, obuf, *, seq):
    s0 = pl.program_id(1) * ATT_TQ
    kcat[0:WBLK, :] = kp_ref[0]
    kcat[WBLK:WBLK + ATT_TQ, :] = km_ref[0]
    kcat[WBLK + ATT_TQ:, :] = kn_ref[0]
    vcat[0:WBLK, :] = vp_ref[0]
    vcat[WBLK:WBLK + ATT_TQ, :] = vm_ref[0]
    vcat[WBLK + ATT_TQ:, :] = vn_ref[0]
    qi = lax.broadcasted_iota(jnp.int32, (WBLK, 3 * WBLK), 0) + WBLK
    ki = lax.broadcasted_iota(jnp.int32, (WBLK, 3 * WBLK), 1)
    dist_i = jnp.abs(ki - qi)
    neg_dist = jnp.where(dist_i <= WINDOW, -dist_i.astype(F32), NEG_INF)
    for qb in range(ATT_TQ // WBLK):
        kabs = s0 + (qb - 1) * WBLK + ki
        bias_unit = jnp.where((kabs >= 0) & (kabs < seq), neg_dist, NEG_INF)
        head_cols = lambda j: slice(j * HEAD_DIM, (j + 1) * HEAD_DIM)
        kv_cols = lambda j: head_cols(j // Q_PER_KV)
        key_rows = slice(qb * WBLK, (qb + 3) * WBLK)
        scores = []
        for j in range(N_Q_HEADS):
            slope = 2.0 ** (-8.0 * (j + 1) / N_Q_HEADS)
            q = q_ref[0, qb * WBLK:(qb + 1) * WBLK, head_cols(j)]
            scores.append(_nt(q, kcat[key_rows, kv_cols(j)]) + (slope * LOG2_E) * bias_unit)
        probs, denoms = [], []
        for j in range(N_Q_HEADS):
            sink = sink_ref[j] * LOG2_E
            m = jnp.maximum(jnp.max(scores[j], axis=-1, keepdims=True), sink)
            p = jnp.exp2(scores[j] - m)
            denoms.append(jnp.sum(p, axis=-1, keepdims=True) + jnp.exp2(sink - m))
            probs.append(p.astype(BF16))
        for j in range(N_Q_HEADS):
            pv = jnp.dot(probs[j], vcat[key_rows, kv_cols(j)], preferred_element_type=F32)
            obuf[qb, :, head_cols(j)] = pv / denoms[j]
        o_ref[0, qb * WBLK:(qb + 1) * WBLK, :] = _rms(obuf[qb], g_ref[...]).astype(o_ref.dtype)


def _window_attention(proj, sink_logit, gain, batch, seq):
    p3 = proj.reshape(batch, seq, MAIN_COLS)
    nq = seq // ATT_TQ
    per = ATT_TQ // WBLK
    last = seq // WBLK - 1
    main = lambda col: pl.BlockSpec((1, ATT_TQ, ATT_KV), lambda b, i: (b, i, col))
    prev = lambda col: pl.BlockSpec((1, WBLK, ATT_KV), lambda b, i: (b, jnp.maximum(i * per - 1, 0), col))
    nxt = lambda col: pl.BlockSpec((1, WBLK, ATT_KV), lambda b, i: (b, jnp.minimum(i * per + per, last), col))
    ck, cv = COL_KA // ATT_KV, COL_VA // ATT_KV
    return pl.pallas_call(
        functools.partial(_winattn_body, seq=seq),
        grid=(batch, nq),
        in_specs=[pl.BlockSpec(memory_space=pltpu.SMEM),
                  pl.BlockSpec((1, ATT_TQ, ATT_Q), lambda b, i: (b, i, 0)),
                  prev(ck), main(ck), nxt(ck), prev(cv), main(cv), nxt(cv),
                  pl.BlockSpec((1, ATT_Q), lambda b, i: (0, 0))],
        out_specs=pl.BlockSpec((1, ATT_TQ, ATT_Q), lambda b, i: (b, i, 0)),
        out_shape=jax.ShapeDtypeStruct((batch, seq, ATT_Q), BF16),
        scratch_shapes=[pltpu.VMEM((ATT_TQ + 2 * WBLK, ATT_KV), BF16),
                        pltpu.VMEM((ATT_TQ + 2 * WBLK, ATT_KV), BF16),
                        pltpu.VMEM((ATT_TQ // WBLK, WBLK, ATT_Q), F32)],
        compiler_params=_params(("parallel", "parallel")), name="window_attention",
    )(sink_logit.astype(F32), p3, p3, p3, p3, p3, p3, p3, gain.reshape(1, ATT_Q).astype(F32))


GLA_SCAN_ROWS = 256
GLA_SCAN_UNROLL = 4
GLA_EPI_ROWS = 512
GLA_UNROLL = 8


def _split2(x):
    hi = x.astype(BF16)
    lo = (x - hi.astype(F32)).astype(BF16)
    return hi, lo


def _gla_body(q_ref, k_ref, v_ref, r_ref, lr_ref, wf_ref, wb_ref, bf_ref, bb_ref, gn_ref, o_ref,
              cum_f, cum_b, acc, st_f, st_b, *, seq):
    C = GLA_CHUNK
    n_chunks = seq // C
    ri = lax.broadcasted_iota(jnp.int32, (GLA_SCAN_ROWS, GLA_SCAN_ROWS), 0)
    ci = lax.broadcasted_iota(jnp.int32, (GLA_SCAN_ROWS, GLA_SCAN_ROWS), 1)
    same = (ri // C) == (ci // C)
    tri_f = jnp.where(same & (ci <= ri), 1.0, 0.0).astype(BF16)
    tri_b = jnp.where(same & (ci >= ri), 1.0, 0.0).astype(BF16)

    def scan_body(t, carry):
        row_sets = [pl.ds(pl.multiple_of((t * GLA_SCAN_UNROLL + u) * GLA_SCAN_ROWS, GLA_SCAN_ROWS), GLA_SCAN_ROWS)
                    for u in range(GLA_SCAN_UNROLL)]
        lrs = [lr_ref[0, rows, :].astype(BF16) for rows in row_sets]
        zs = [jnp.dot(lr, w_ref[...], preferred_element_type=F32) + b_ref[...]
              for lr in lrs for w_ref, b_ref in ((wf_ref, bf_ref), (wb_ref, bb_ref))]
        gs = [_split2((jnp.minimum(z, 0.0) - jnp.log(1.0 + jnp.exp(-jnp.abs(z)))) * (LOG2_E / GLA_TAU)) for z in zs]
        for n, g3 in enumerate(gs):
            tri, dst = ((tri_f, cum_f), (tri_b, cum_b))[n % 2]
            dst[row_sets[n // 2], :] = sum(jnp.dot(tri, part, preferred_element_type=F32) for part in g3)
        return carry

    lax.fori_loop(0, seq // (GLA_SCAN_ROWS * GLA_SCAN_UNROLL), scan_body, 0)

    acc[...] = jnp.zeros_like(acc)
    st_f[...] = jnp.zeros_like(st_f)
    st_b[...] = jnp.zeros_like(st_b)
    rr = lax.broadcasted_iota(jnp.int32, (C, C), 0)
    cc = lax.broadcasted_iota(jnp.int32, (C, C), 1)
    scale = GLA_DK ** -0.5

    def chunk_body(i, carry):
        jobs = []
        for u in range(GLA_UNROLL):
            c = i * GLA_UNROLL + u
            jobs += [(c, cum_f, True), (n_chunks - 1 - c, cum_b, False)]
        prep = []
        for c, cum, forward in jobs:
            rows = pl.ds(pl.multiple_of(c * C, C), C)
            b = cum[rows, :]
            b_end = b[C - 1:C, :] if forward else b[0:1, :]
            q = q_ref[0, rows, :].astype(F32) * scale
            k = k_ref[0, rows, :].astype(F32)
            v = v_ref[0, rows, :]
            q_dec = (q * jnp.exp2(b)).astype(BF16)
            k_inc = (k * jnp.exp2(-b)).astype(BF16)
            k_dec = (k * jnp.exp2(b_end - b)).astype(BF16)
            prep.append((rows, v, q_dec, k_inc, k_dec, jnp.exp2(b_end)))
        attn = [_nt(q_dec, k_inc) for _, _, q_dec, k_inc, _, _ in prep]
        kv_t = [_tn(v, k_dec) for _, v, _, _, k_dec, _ in prep]
        state_t = {True: st_f[...], False: st_b[...]}
        o_inter = []
        for (_, _, forward), (_, _, q_dec, _, _, decay), kv in zip(jobs, prep, kv_t):
            o_inter.append(_nt(q_dec, state_t[forward].astype(BF16)))
            state_t[forward] = state_t[forward] * decay + kv
        st_f[...] = state_t[True]
        st_b[...] = state_t[False]
        for (_, _, forward), (rows, v, _, _, _, _), a, oi in zip(jobs, prep, attn, o_inter):
            a = jnp.where((cc <= rr) if forward else (cc >= rr), a, 0.0)
            o = jnp.dot(a.astype(BF16), v, preferred_element_type=F32) + oi
            acc[rows, :] = acc[rows, :] + o
        return carry

    lax.fori_loop(0, n_chunks // GLA_UNROLL, chunk_body, 0)

    def epi_body(t, carry):
        rows = pl.ds(pl.multiple_of(t * GLA_EPI_ROWS, GLA_EPI_ROWS), GLA_EPI_ROWS)
        r = r_ref[0, rows, :].astype(F32)
        o_ref[0, rows, :] = (_rms(acc[rows, :], gn_ref[...]) * (r * jax.nn.sigmoid(r))).astype(o_ref.dtype)
        return carry

    lax.fori_loop(0, seq // GLA_EPI_ROWS, epi_body, 0)


def _gla(proj, lr, w_gf, b_gf, w_gb, b_gb, gain, batch, seq):
    p3 = proj.reshape(batch, seq, MAIN_COLS)
    lr3 = lr.reshape(batch, seq, LANES)
    wf = jnp.zeros((LANES, GLA_QK), F32).at[:GLA_LOWRANK].set(w_gf).astype(BF16)
    wb = jnp.zeros((LANES, GLA_QK), F32).at[GLA_LOWRANK:2 * GLA_LOWRANK].set(w_gb).astype(BF16)
    seq_blk = lambda width, col0: pl.BlockSpec((1, seq, width), lambda b, h: (b, 0, col0 // width + h))
    head_w = pl.BlockSpec((LANES, GLA_DK), lambda b, h: (0, h))
    head_b = pl.BlockSpec((1, GLA_DK), lambda b, h: (0, h))
    return pl.pallas_call(
        functools.partial(_gla_body, seq=seq),
        grid=(batch, GLA_HEADS),
        in_specs=[seq_blk(GLA_DK, COL_QG), seq_blk(GLA_DK, COL_KG), seq_blk(GLA_DV, COL_VG),
                  seq_blk(GLA_DV, COL_RG),
                  pl.BlockSpec((1, seq, LANES), lambda b, h: (b, 0, 0)),
                  head_w, head_w, head_b, head_b,
                  pl.BlockSpec((1, GLA_DV), lambda b, h: (0, 0))],
        out_specs=pl.BlockSpec((1, seq, GLA_DV), lambda b, h: (b, 0, h)),
        out_shape=jax.ShapeDtypeStruct((batch, seq, GLA_V), BF16),
        scratch_shapes=[pltpu.VMEM((seq, GLA_DK), F32), pltpu.VMEM((seq, GLA_DK), F32),
                        pltpu.VMEM((seq, GLA_DV), F32),
                        pltpu.VMEM((GLA_DV, GLA_DK), F32), pltpu.VMEM((GLA_DV, GLA_DK), F32)],
        compiler_params=_params(("parallel", "parallel")), name="gla",
    )(p3, p3, p3, p3, lr3, wf, wb, b_gf.reshape(1, GLA_QK).astype(F32), b_gb.reshape(1, GLA_QK).astype(F32),
      gain.reshape(1, GLA_DV).astype(F32))


MEM_TM = 512
MEM_K_CHUNK = 512


def _mem_kv_body(m_ref, g_ref, wk_ref, wv_ref, k_ref, v_ref, u, acc_k, acc_v):
    step = pl.program_id(1)
    n_chunks = D_MODEL // MEM_K_CHUNK

    @pl.when(step == 0)
    def _():
        normed = _rms(m_ref[...], g_ref[...]).astype(BF16)
        for c in range(n_chunks):
            u[c] = normed[:, c * MEM_K_CHUNK:(c + 1) * MEM_K_CHUNK]
        acc_k[...] = jnp.zeros_like(acc_k)
        acc_v[...] = jnp.zeros_like(acc_v)

    lhs = u[step]
    acc_k[...] = acc_k[...] + jnp.dot(lhs, wk_ref[...].astype(BF16), preferred_element_type=F32)
    acc_v[...] = acc_v[...] + jnp.dot(lhs, wv_ref[...].astype(BF16), preferred_element_type=F32)

    @pl.when(step == n_chunks - 1)
    def _():
        k_ref[...] = acc_k[...].astype(k_ref.dtype)
        v_ref[...] = acc_v[...].astype(v_ref.dtype)


def _mem_kv(mem_rows, gain, w_ck, w_cv, layer):
    rows = mem_rows.shape[0]
    n_chunks = D_MODEL // MEM_K_CHUNK
    tile = pl.BlockSpec((MEM_TM, D_MODEL), lambda i, c: (i, 0))
    w_spec = pl.BlockSpec((None, MEM_K_CHUNK, D_MODEL), lambda i, c: (layer, c, 0))
    g2 = gain.reshape(1, D_MODEL).astype(F32)
    out = jax.ShapeDtypeStruct((rows, D_MODEL), BF16)
    return pl.pallas_call(
        _mem_kv_body, grid=(rows // MEM_TM, n_chunks),
        in_specs=[tile, pl.BlockSpec((1, D_MODEL), lambda i, c: (0, 0)), w_spec, w_spec],
        out_specs=[tile, tile], out_shape=[out, out],
        scratch_shapes=[pltpu.VMEM((n_chunks, MEM_TM, MEM_K_CHUNK), BF16),
                        pltpu.VMEM((MEM_TM, D_MODEL), F32), pltpu.VMEM((MEM_TM, D_MODEL), F32)],
        compiler_params=_params(("parallel", "arbitrary")), name="mem_kv_proj")(mem_rows, g2, w_ck, w_cv)


XATT_TQ = 512


XATT_CHUNK = 512


def _cross_body(h_ref, g_ref, wq_hbm, k_ref, v_ref, wo_hbm, gr_ref, wr_ref, br_ref,
                o_ref, hn_ref, gate_ref, order_ref, count_ref,
                wqb, wob, stage, sem, u_scr, q_scr, a_scr, prev, *, layer, n_tokens):
    step = pl.program_id(0)

    @pl.when(step == 0)
    def _():
        _load_weight_bf16(wq_hbm, layer, wqb, stage, sem, D_MODEL, D_MODEL)
        _load_weight_bf16(wo_hbm, layer, wob, stage, sem, D_MODEL, D_MODEL)
        prev[...] = jnp.zeros_like(prev)

    router = _route_tile(prev[...], gr_ref[...], wr_ref, br_ref, hn_ref, gate_ref, order_ref, count_ref,
                         jnp.maximum(step - 1, 0), n_tokens)

    scale = LOG2_E * X_HEAD_DIM ** -0.5
    heads = [slice(h * X_HEAD_DIM, (h + 1) * X_HEAD_DIM) for h in range(X_HEADS)]
    chunks = [slice(n0, n0 + XATT_CHUNK) for n0 in range(0, D_MODEL, XATT_CHUNK)]

    def norm(rows):
        u_scr[rows, :] = _rms(h_ref[rows, :], g_ref[...]).astype(BF16)

    def q_proj(rows):
        for cols in chunks:
            q = jnp.dot(u_scr[rows, :], wqb[:, cols], preferred_element_type=F32)
            q_scr[rows, cols] = (q * scale).astype(BF16)

    def scores(rows):
        return [_nt(q_scr[rows, hd], k_ref[0, :, hd]) for hd in heads]

    def softmax(s_list):
        out = []
        for s in s_list:
            p = jnp.exp2(s - jnp.max(s, axis=-1, keepdims=True))
            out.append((p.astype(BF16), jnp.sum(p, axis=-1, keepdims=True)))
        return out

    def attend(rows, probs):
        for hd, (p, denom) in zip(heads, probs):
            a_scr[rows, hd] = (jnp.dot(p, v_ref[0, :, hd], preferred_element_type=F32) / denom).astype(BF16)

    def o_proj(rows):
        for cols in chunks:
            out = h_ref[rows, cols] + jnp.dot(a_scr[rows, :], wob[:, cols], preferred_element_type=F32)
            o_ref[rows, cols] = out
            prev[rows, cols] = out

    half = XATT_TQ // 2
    first, second = slice(0, half), slice(half, XATT_TQ)
    norm(first)
    q_proj(first)
    next(router)
    norm(second)
    s_first = scores(first)
    q_proj(second)
    next(router)
    p_first = softmax(s_first)
    attend(first, p_first)
    s_second = scores(second)
    o_proj(first)
    for _ in router:
        pass
    p_second = softmax(s_second)
    attend(second, p_second)
    o_proj(second)


def _cross_block(h, gain, w_cq, kx, vx, w_co, router_gain, router_w, router_b, layer, batch, seq, mem_len):
    rows = h.shape[0]
    n_tiles = rows // XATT_TQ
    tiles_per_batch = seq // XATT_TQ
    k3 = kx.reshape(batch, mem_len, D_MODEL)
    v3 = vx.reshape(batch, mem_len, D_MODEL)
    as_row = lambda g: g.reshape(1, D_MODEL).astype(F32)
    cur = lambda i: jnp.minimum(i, n_tiles - 1)
    lag = lambda i: jnp.maximum(i - 1, 0)
    tile = pl.BlockSpec((XATT_TQ, D_MODEL), lambda i: (cur(i), 0))
    mem_spec = pl.BlockSpec((1, mem_len, D_MODEL), lambda i: (cur(i) // tiles_per_batch, 0, 0))
    full = lambda a: pl.BlockSpec(a.shape, lambda i: (0, 0))
    lag_rows = lambda width: pl.BlockSpec((XATT_TQ, width), lambda i: (lag(i), 0))
    lag_tile = lambda sublanes: pl.BlockSpec((1, sublanes, LANES), lambda i: (lag(i), 0, 0))
    assert TOP_K * XATT_TQ == ID_SPLIT * ID_SPLIT, "the router's sort grid holds one tile's assignments"
    hbm = pl.BlockSpec(memory_space=pl.ANY)
    g_cross, g_router = as_row(gain), as_row(router_gain)
    return pl.pallas_call(
        functools.partial(_cross_body, layer=layer, n_tokens=rows), grid=(n_tiles + 1,),
        in_specs=[tile, full(g_cross), hbm, mem_spec, mem_spec, hbm, full(g_router), full(router_w), full(router_b)],
        out_specs=[tile, lag_rows(HALF), lag_rows(LANES), lag_tile(ID_SPLIT), lag_tile(8)],
        out_shape=[jax.ShapeDtypeStruct((rows, D_MODEL), F32),
                   jax.ShapeDtypeStruct((rows, HALF), jnp.uint32),
                   jax.ShapeDtypeStruct((rows, LANES), F32),
                   jax.ShapeDtypeStruct((n_tiles, ID_SPLIT, LANES), jnp.int32),
                   jax.ShapeDtypeStruct((n_tiles, 8, LANES), jnp.int32)],
        scratch_shapes=[pltpu.VMEM((D_MODEL, D_MODEL), BF16), pltpu.VMEM((D_MODEL, D_MODEL), BF16),
                        pltpu.VMEM((2, W_STAGE_ROWS, D_MODEL), F32), pltpu.SemaphoreType.DMA((2,)),
                        pltpu.VMEM((XATT_TQ, D_MODEL), BF16), pltpu.VMEM((XATT_TQ, D_MODEL), BF16),
                        pltpu.VMEM((XATT_TQ, D_MODEL), BF16), pltpu.VMEM((XATT_TQ, D_MODEL), F32)],
        compiler_params=_params(("arbitrary",)), name="cross_block",
    )(h, g_cross, w_cq, k3, v3, w_co, g_router, router_w, router_b)


HALF = D_MODEL // 2


def _pack_halves(x):
    return pltpu.pack_elementwise([x[:, :HALF], x[:, HALF:]], packed_dtype=BF16)


def _unpack_halves(words):
    return [pltpu.unpack_elementwise(words, index=i, packed_dtype=BF16, unpacked_dtype=F32) for i in range(2)]


ID_SPLIT = 32


def _route_tile(h, gain, w_ref, b_ref, hn_ref, gate_ref, order_ref, count_ref, tile_idx, n_tokens):
    hn = _rms(h, gain)
    hn_ref[...] = _pack_halves(hn)
    logits = jnp.dot(hn.astype(BF16), w_ref[...], preferred_element_type=F32) + b_ref[...]
    yield
    lane = lax.broadcasted_iota(jnp.int32, logits.shape, 1).astype(F32)
    ninf = -jnp.inf
    first = lambda hit: jnp.min(jnp.where(hit, lane, float(LANES)), axis=-1, keepdims=True)
    in_groups = lane < N_GROUPS
    gl = jnp.where(in_groups, logits, ninf)
    gmax = jnp.max(gl, axis=-1, keepdims=True)
    g_idx = first(gl == gmax)
    p_group = 1.0 / jnp.sum(jnp.where(in_groups, jnp.exp(logits - gmax), 0.0), axis=-1, keepdims=True)
    lo = N_GROUPS + EXPERTS_PER_GROUP * g_idx
    el = jnp.where((lane >= lo) & (lane < lo + EXPERTS_PER_GROUP), logits, ninf)
    e1 = jnp.max(el, axis=-1, keepdims=True)
    i1 = first(el == e1)
    el2 = jnp.where(lane == i1, ninf, el)
    e2 = jnp.max(el2, axis=-1, keepdims=True)
    i2 = first(el2 == e2)
    t = jnp.exp(e2 - e1)
    w1 = p_group / (1.0 + t)
    w2 = p_group * t / (1.0 + t)
    gate_ref[...] = jnp.where(lane == 0, w1, jnp.where(lane == 1, w2, 0.0))

    tm = logits.shape[0]
    hit0, hit1 = lane == i1 - N_GROUPS, lane == i2 - N_GROUPS
    member = jnp.where(hit0 | hit1, 1.0, 0.0)
    member_b = member.astype(BF16)
    earlier = (lax.broadcasted_iota(jnp.int32, (tm, tm), 1) < lax.broadcasted_iota(jnp.int32, (tm, tm), 0))
    rank = jnp.dot(jnp.where(earlier, 1.0, 0.0).astype(BF16), member_b, preferred_element_type=F32)
    lower = (lax.broadcasted_iota(jnp.int32, (LANES, LANES), 0) < lax.broadcasted_iota(jnp.int32, (LANES, LANES), 1))
    run_start = jnp.sum(jnp.dot(member_b, jnp.where(lower, 1.0, 0.0).astype(BF16), preferred_element_type=F32),
                        axis=0, keepdims=True)
    yield
    pos = rank + run_start
    positions = [jnp.sum(jnp.where(hit, pos, 0.0), axis=-1, keepdims=True) for hit in (hit0, hit1)]
    local_tok = lax.broadcasted_iota(jnp.int32, (tm, 1), 0)
    grids = [None, None]
    for s, p in enumerate(positions):
        p_hi = jnp.floor(p / ID_SPLIT)
        at_hi = jnp.where(lane == p_hi, 1.0, 0.0)
        at_lo = jnp.where(lane == p - ID_SPLIT * p_hi, 1.0, 0.0).astype(BF16)
        a = s * tm + local_tok
        for d, digit in enumerate((a // ID_SPLIT, a % ID_SPLIT)):
            part = _tn((at_hi * digit.astype(F32)).astype(BF16), at_lo)
            grids[d] = part if grids[d] is None else grids[d] + part
    a_sorted = (grids[0] * ID_SPLIT + grids[1]).astype(jnp.int32)
    dst = (a_sorted // tm) * n_tokens + tile_idx * tm + a_sorted % tm
    order_ref[0] = dst[:ID_SPLIT, :]
    count_ref[0] =jnp.broadcast_to(jnp.sum(member, axis=0, keepdims=True), (8, LANES)).astype(jnp.int32)


def _router_params(w_rg, b_rg, w_re, b_re):
    n_log = N_GROUPS + N_EXPERTS
    w = jnp.zeros((D_MODEL, LANES), F32).at[:, :N_GROUPS].set(w_rg).at[:, N_GROUPS:n_log].set(w_re).astype(BF16)
    b = jnp.zeros((1, LANES), F32).at[0, :N_GROUPS].set(b_rg).at[0, N_GROUPS:n_log].set(b_re)
    return w, b


def _expert_changed(blk_e, blk):
    prev = blk_e[jnp.maximum(blk - 1, 0)]
    return (blk == 0) | (blk_e[blk] != prev)


WEIGHT_DMA_PRIORITY = 1


def _expert_weight_copies(layer, w_hbms, stages, sem):
    def copies(e):
        return [pltpu.make_async_copy(w.at[layer, e], st, sem.at[i])
                for i, (w, st) in enumerate(zip(w_hbms, stages))]
    return copies


GATHER_AHEAD = 3
GATHER_SLOTS = GATHER_AHEAD + 1
MOE_CHUNKS = 2


SCATTER_SLOTS = 3


IDX_ROWS = 8
IDX_DST = GATHER_AHEAD + 1
IDX_DST_PREV = GATHER_AHEAD + 2


def _experts_body(blk_e, n_real, nxt_e, idx_ref, hn_hbm, wg_hbm, wu_hbm, wd_hbm, out_hbm, xs, xb, hid, ys,
                  wg_stage, wu_stage, wd_stage, wgb, wub, wdb, gsem, ssem, wsem, *, n_assign, layer):
    blk = pl.program_id(0)
    dummy_slot = SCATTER_SLOTS - 1

    def gather_row(idx_row, slot, r):
        return pltpu.make_async_copy(hn_hbm.at[pl.ds(idx_ref[0, idx_row, r], 1), :], xs.at[slot, pl.ds(r, 1), :],
                                     gsem.at[slot])

    def scatter_row(idx_row, slot, r):
        return pltpu.make_async_copy(ys.at[slot, pl.ds(r, 1), :], out_hbm.at[pl.ds(idx_ref[0, idx_row, r], 1), :],
                                     ssem.at[slot])

    def wait_gather(slot):
        pltpu.make_async_copy(hn_hbm.at[pl.ds(0, MOE_BLOCK), :], xs.at[slot], gsem.at[slot]).wait()

    def wait_scatter(slot):
        pltpu.make_async_copy(ys.at[slot], out_hbm.at[pl.ds(0, MOE_BLOCK), :], ssem.at[slot]).wait()

    @pl.when(blk == 0)
    def _():
        def body(r, carry):
            for s in range(GATHER_AHEAD):
                gather_row(s, s, r).start()
            return carry
        lax.fori_loop(0, MOE_BLOCK, body, 0)
        ys[dummy_slot] = jnp.zeros((MOE_BLOCK, HALF), jnp.uint32)
        for s in range(SCATTER_SLOTS):
            trash = pltpu.make_async_copy(
                ys.at[dummy_slot], out_hbm.at[pl.ds(n_assign + s * MOE_BLOCK, MOE_BLOCK), :], ssem.at[dummy_slot])
            trash.start()
            trash.wait()

    @pl.when(blk < n_real[0])
    def _():
        in_slot = blk % GATHER_SLOTS
        ahead_slot = (blk + GATHER_AHEAD) % GATHER_SLOTS
        out_slot = blk % SCATTER_SLOTS
        prev_slot = (blk + SCATTER_SLOTS - 1) % SCATTER_SLOTS

        copies = _expert_weight_copies(layer, (wg_hbm, wu_hbm, wd_hbm), (wg_stage, wu_stage, wd_stage), wsem)
        first_of_expert = _expert_changed(blk_e, blk)

        @pl.when(blk == 0)
        def _():
            for cp in copies(blk_e[0]):
                cp.start(priority=WEIGHT_DMA_PRIORITY)

        wait_gather(in_slot)

        @pl.when(blk >= SCATTER_SLOTS - 1)
        def _():
            wait_scatter(out_slot)

        def compute(fresh_weights):
            lo, hi = _unpack_halves(xs[in_slot])
            xb[:, :HALF] = lo.astype(BF16)
            xb[:, HALF:] = hi.astype(BF16)
            for r in range(MOE_BLOCK):
                scatter_row(IDX_DST_PREV, prev_slot, r).start(priority=r % 2)
                gather_row(GATHER_AHEAD, ahead_slot, r).start()
            cw = D_FF_EXPERT // MOE_CHUNKS
            for c in range(MOE_CHUNKS):
                cols = slice(c * cw, (c + 1) * cw)
                if fresh_weights:
                    wgb[:, cols] = wg_stage[:, cols].astype(BF16)
                    wub[:, cols] = wu_stage[:, cols].astype(BF16)
                a = jnp.dot(xb[...], wgb[:, cols], preferred_element_type=F32)
                u = jnp.dot(xb[...], wub[:, cols], preferred_element_type=F32)
                hid[:, cols] = (a * jax.nn.sigmoid(a) * u).astype(BF16)
            cw = HALF // MOE_CHUNKS
            for c in range(MOE_CHUNKS):
                cols = slice(c * cw, (c + 1) * cw)
                hi_cols = slice(HALF + c * cw, HALF + (c + 1) * cw)
                if fresh_weights:
                    wdb[:, cols] = wd_stage[:, cols].astype(BF16)
                    wdb[:, hi_cols] = wd_stage[:, hi_cols].astype(BF16)
                ys[out_slot, :, cols] = pltpu.pack_elementwise(
                    [jnp.dot(hid[...], wdb[:, cols], preferred_element_type=F32),
                     jnp.dot(hid[...], wdb[:, hi_cols], preferred_element_type=F32)], packed_dtype=BF16)

        @pl.when(first_of_expert)
        def _():
            for cp in copies(blk_e[blk]):
                cp.wait()
            compute(True)

            @pl.when(nxt_e[blk] >= 0)
            def _():
                for cp in copies(nxt_e[blk]):
                    cp.start(priority=WEIGHT_DMA_PRIORITY)

        @pl.when(jnp.logical_not(first_of_expert))
        def _():
            compute(False)

        @pl.when(blk == n_real[0] - 1)
        def _():
            def body(r, carry):
                scatter_row(IDX_DST, out_slot, r).start()
                return carry
            lax.fori_loop(0, MOE_BLOCK, body, 0)
            for s in range(1, GATHER_SLOTS):
                wait_gather((blk + s) % GATHER_SLOTS)
            for s in range(SCATTER_SLOTS):
                wait_scatter(s)


def _moe_forward(hn, order, counts, w_gate, w_up, w_down, layer, n_tokens):
    A = n_tokens * TOP_K
    out_rows = A + SCATTER_SLOTS * MOE_BLOCK
    R = A + N_EXPERTS * MOE_BLOCK
    n_blk = R // MOE_BLOCK
    i32 = jnp.int32
    n_tiles, per_tile = order.shape[0], ID_SPLIT * ID_SPLIT
    experts = jnp.arange(N_EXPERTS, dtype=i32)
    n = counts[:, 0, :N_EXPERTS]
    total = jnp.sum(n, axis=0)
    padded = ((total + MOE_BLOCK - 1) // MOE_BLOCK) * MOE_BLOCK
    pend = jnp.cumsum(padded)
    pstart = pend - padded
    n_real = (pend[-1] // MOE_BLOCK).astype(i32).reshape(1)
    blk = jnp.arange(n_blk, dtype=i32)
    blk_e = jnp.minimum(jnp.sum((pend[None, :] <= (blk * MOE_BLOCK)[:, None]).astype(i32), axis=1), N_EXPERTS - 1)
    is_e = (blk_e[:, None] == experts[None, :]).astype(i32)
    of_block = lambda per_expert: jnp.sum(is_e * per_expert[None, :], axis=1)
    cum_incl = jnp.cumsum(n, axis=0)
    run_shift = (jnp.cumsum(n, axis=1) - n) - (cum_incl - n)
    per_tile_of_block = lambda m: jnp.sum(is_e[:, None, :] * m[None, :, :], axis=2)
    cum_b, shift_b = per_tile_of_block(cum_incl), per_tile_of_block(run_shift)
    in_blk = jnp.arange(MOE_BLOCK, dtype=i32)
    k = (blk * MOE_BLOCK - of_block(pstart))[:, None] + in_blk[None, :]
    tile_of = jnp.minimum(jnp.sum((cum_b[:, None, :] <= k[:, :, None]).astype(i32), axis=2), n_tiles - 1)
    is_tile = (tile_of[:, :, None] == jnp.arange(n_tiles, dtype=i32)[None, None, :]).astype(i32)
    entry = tile_of * per_tile + k + jnp.sum(is_tile * shift_b[:, None, :], axis=2)
    pad_entry = A + (blk % SCATTER_SLOTS)[:, None] * MOE_BLOCK + in_blk[None, :]
    entry = jnp.where(k < of_block(total)[:, None], entry, pad_entry)
    n_pad = SCATTER_SLOTS * MOE_BLOCK
    dst_table = jnp.concatenate([order[:, :, :ID_SPLIT].reshape(A), A + jnp.arange(n_pad, dtype=i32)])
    row_dst = dst_table[entry]
    row_tok = row_dst % n_tokens
    row_dst_prev = jnp.concatenate([(A + (SCATTER_SLOTS - 1) * MOE_BLOCK + in_blk)[None, :], row_dst[:-1]], axis=0)
    last = n_real[0] - 1
    last_tok = lax.dynamic_slice_in_dim(row_tok, last, 1, axis=0)
    ahead = [jnp.where((blk + s <= last)[:, None],
                       jnp.concatenate([row_tok[s:], jnp.zeros((s, MOE_BLOCK), i32)], axis=0), last_tok)
             for s in range(GATHER_AHEAD + 1)]
    idx_rows = ahead + [row_dst, row_dst_prev]
    idx_rows += [jnp.zeros_like(row_dst)] * (IDX_ROWS - len(idx_rows))
    idx_all = jnp.stack(idx_rows, axis=1)

    run_end = pend[blk_e] // MOE_BLOCK
    nxt_e = jnp.where(run_end < n_real[0], blk_e[jnp.minimum(run_end, n_blk - 1)], -1).astype(jnp.int32)

    idx_spec = pl.BlockSpec((1, IDX_ROWS, MOE_BLOCK), lambda i, be, nr, nx: (jnp.minimum(i, nr[0] - 1), 0, 0),
                            memory_space=pltpu.SMEM)
    hbm = pl.BlockSpec(memory_space=pl.ANY)
    up_shape, down_shape = (D_MODEL, D_FF_EXPERT), (D_FF_EXPERT, D_MODEL)
    return pl.pallas_call(
        functools.partial(_experts_body, n_assign=A, layer=layer),
        grid_spec=pltpu.PrefetchScalarGridSpec(
            num_scalar_prefetch=3, grid=(n_blk,),
            in_specs=[idx_spec] + [hbm] * 4,
            out_specs=hbm,
            scratch_shapes=[pltpu.VMEM((GATHER_SLOTS, MOE_BLOCK, HALF), jnp.uint32),
                            pltpu.VMEM((MOE_BLOCK, D_MODEL), BF16),
                            pltpu.VMEM((MOE_BLOCK, D_FF_EXPERT), BF16),
                            pltpu.VMEM((SCATTER_SLOTS, MOE_BLOCK, HALF), jnp.uint32),
                            pltpu.VMEM(up_shape, F32), pltpu.VMEM(up_shape, F32), pltpu.VMEM(down_shape, F32),
                            pltpu.VMEM(up_shape, BF16), pltpu.VMEM(up_shape, BF16), pltpu.VMEM(down_shape, BF16),
                            pltpu.SemaphoreType.DMA((GATHER_SLOTS,)), pltpu.SemaphoreType.DMA((SCATTER_SLOTS,)),
                            pltpu.SemaphoreType.DMA((3,))]),
        out_shape=jax.ShapeDtypeStruct((out_rows, HALF), jnp.uint32),
        compiler_params=_params(("arbitrary",)), name="moe_experts",
    )(blk_e, n_real, nxt_e, idx_all, hn, w_gate, w_up, w_down)


FINAL_TM = 512


def _final_body(h_ref, y0_ref, y1_ref, gate_ref, g_ref, o_ref):
    gate = gate_ref[...]
    y0, y1 = _unpack_halves(y0_ref[...]), _unpack_halves(y1_ref[...])
    halves = [h_ref[:, i * HALF:(i + 1) * HALF] + gate[:, 0:1] * y0[i] + gate[:, 1:2] * y1[i] for i in range(2)]
    mean_sq = sum(jnp.sum(hh * hh, axis=-1, keepdims=True) for hh in halves) / D_MODEL
    inv = lax.rsqrt(mean_sq + RMS_EPS)
    for i, hh in enumerate(halves):
        o_ref[:, i * HALF:(i + 1) * HALF] = hh * inv * g_ref[:, i * HALF:(i + 1) * HALF]


def _final(h, ys, gate, gain):
    rows = h.shape[0]
    n_tiles = rows // FINAL_TM
    tile = lambda width: pl.BlockSpec((FINAL_TM, width), lambda i: (i, 0))
    g2 = gain.reshape(1, D_MODEL).astype(F32)
    return pl.pallas_call(
        _final_body, grid=(n_tiles,),
        in_specs=[tile(D_MODEL), tile(HALF), pl.BlockSpec((FINAL_TM, HALF), lambda i: (i + n_tiles, 0)),
                  tile(LANES), pl.BlockSpec((1, D_MODEL), lambda i: (0, 0))],
        out_specs=tile(D_MODEL), out_shape=jax.ShapeDtypeStruct((rows, D_MODEL), F32),
        compiler_params=_params(("parallel",)), name="combine_final_norm")(h, ys, ys, gate, g2)


def kernel(x, mem, norm_mix, w_in, attn_out_norm, sink_logit, w_gla_gf, b_gla_gf, w_gla_gb, b_gla_gb, gla_out_norm, w_out, norm_cross, norm_mem, w_cq, w_ck, w_cv, w_co, norm_ffn, w_router_group, b_router_group, w_router_expert, b_router_expert, w_gate, w_up, w_down, norm_final):
    batch, seq, _ = x.shape
    mem_len = mem.shape[1]
    n_tokens = batch * seq
    h = x.reshape(n_tokens, D_MODEL)
    memf = mem.reshape(batch * mem_len, D_MODEL)
    assert norm_mix.shape[0] == 1, "the combine step is fused with the final norm: single-layer stacks only"
    for l in range(norm_mix.shape[0]):
        w_lr = jnp.zeros((D_MODEL, LANES), F32).at[:, :2 * GLA_LOWRANK].set(w_in[l][:, MAIN_COLS:]).astype(BF16)
        w_main = w_in[l].astype(BF16)
        proj, lr = _dense([h], w_main, None, n_cols=MAIN_COLS, out_dtype=BF16, gain=norm_mix[l], extra_w=w_lr,
                          lead_scale=(ATT_Q, ATT_Q_SCALE), name="in_proj")
        o_a = _window_attention(proj, sink_logit[l], attn_out_norm[l], batch, seq).reshape(n_tokens, ATT_Q)
        o_g = _gla(proj, lr, w_gla_gf[l], b_gla_gf[l], w_gla_gb[l], b_gla_gb[l], gla_out_norm[l],
                   batch, seq).reshape(n_tokens, GLA_V)
        h = _dense([o_a, o_g], w_out, l, n_cols=D_MODEL, out_dtype=F32, res=h, name="out_proj")
        kx, vx = _mem_kv(memf, norm_mem[l], w_ck, w_cv, l)
        router_w, router_b = _router_params(w_router_group[l], b_router_group[l],
                                            w_router_expert[l], b_router_expert[l])
        h, hn, gate, order, counts = _cross_block(h, norm_cross[l], w_cq, kx, vx, w_co, norm_ffn[l],
                                                  router_w, router_b, l, batch, seq, mem_len)
        ys = _moe_forward(hn, order, counts, w_gate, w_up, w_down, l, n_tokens)
    return _final(h, ys, gate, norm_final).reshape(batch, seq, D_MODEL)
```
